```python
import math
import jax, jax.numpy as jnp
from jax import lax
import numpy as np

D_MODEL = 2048
BATCH = 4
SEQ = 2048
DEPTH = 2
DEC_BATCH = 128
DEC_SEQ = 4
PAST_LEN = 16384
PAGE_SIZE = 128

F32 = jnp.float32
N_EVEN = (DEPTH + 1) // 2
N_ODD = DEPTH // 2
N_MEM = 256
EPS = 1e-6

RW_HEADS = 16
RW_HD = 64
RW_DIM = RW_HEADS * RW_HD
RW_DECAY_LORA = 64
RW_A_LORA = 64
RW_GATE_LORA = 128
RW_PROJ = 3 * RW_DIM + RW_DECAY_LORA + RW_A_LORA + RW_GATE_LORA
RW_SPLITS = [RW_DIM, 2 * RW_DIM, 3 * RW_DIM, 3 * RW_DIM + RW_DECAY_LORA, 3 * RW_DIM + RW_DECAY_LORA + RW_A_LORA]
RW_GN_EPS = 6.4e-4

SSD_HEADS = 16
SSD_HD = 64
SSD_DIM = SSD_HEADS * SSD_HD
SSD_GROUPS = 2
SSD_HPG = SSD_HEADS // SSD_GROUPS
SSD_STATE = 128
SSD_CONV = 4
SSD_CONV_DIM = SSD_DIM + 2 * SSD_GROUPS * SSD_STATE
SSD_PROJ = SSD_DIM + SSD_CONV_DIM + SSD_HEADS
SSD_CHUNK = 64

AB_PROJ = RW_PROJ + SSD_PROJ
AB_OUT = RW_DIM + SSD_DIM

GDN_HEADS = 16
GDN_DK = 128
GDN_DV = 128
GDN_QK = GDN_HEADS * GDN_DK
GDN_V = GDN_HEADS * GDN_DV
GDN_CONV = 4
GDN_CONV_DIM = 2 * GDN_QK + GDN_V
GDN_PROJ = GDN_CONV_DIM + GDN_V + 2 * GDN_HEADS
GDN_CHUNK = 64

D_FF = 5632
FFN_CONV = 3

XA_HEADS = 4
XA_HD = 128
XA_DIM = XA_HEADS * XA_HD

kernel_name = 'hybrid_rwkv7_ssd_gdn_memxattn_convffn_step'


def rmsnorm(x, g, eps=EPS):
    xf = x.astype(F32)
    y = xf * lax.rsqrt(jnp.mean(xf * xf, axis=-1, keepdims=True) + eps)
    return (y * g.astype(F32)).astype(x.dtype)


def l2norm(x, eps=1e-6):
    xf = x.astype(F32)
    return xf * lax.rsqrt(jnp.sum(xf * xf, axis=-1, keepdims=True) + eps)


def causal_dwconv(x, buf, w, b):
    width, L = w.shape[0], x.shape[1]
    xp = jnp.concatenate([buf.astype(x.dtype), x], axis=1)
    y = xp[:, 0:L] * w[0]
    for i in range(1, width):
        y = y + xp[:, i:i + L] * w[i]
    if b is not None:
        y = y + b
    return y, xp[:, L:]


def rwkv7_mix(p, shift_buf, S0, mu, w0, w_up, a0, a_up, g_up, k_k, k_a, r_k, gn_w, gn_b):
    Bsz, L, _ = p.shape
    prev = jnp.concatenate([shift_buf[:, None, :].astype(p.dtype), p[:, :-1]], axis=1)
    ps = p + (prev - p) * mu
    r, k, v, wd, ad, gd = jnp.split(ps, RW_SPLITS, axis=-1)
    w = -jax.nn.softplus(-(w0 + jnp.tanh(wd) @ w_up).astype(F32)) - 0.5
    decay = jnp.exp(-jnp.exp(w))
    a = jax.nn.sigmoid((a0 + ad @ a_up).astype(F32))
    g = jax.nn.sigmoid(gd) @ g_up
    shp = (Bsz, L, RW_HEADS, RW_HD)
    r, k, v, decay, a = (t.astype(F32).reshape(shp) for t in (r, k, v, decay, a))
    kk = l2norm(k * k_k.astype(F32).reshape(RW_HEADS, RW_HD))
    k = k * (1.0 + (a - 1.0) * k_a.astype(F32).reshape(RW_HEADS, RW_HD))

    def step(S, inp):
        r_t, w_t, k_t, v_t, a_t, b_t = inp
        sa = jnp.einsum('bhvk,bhk->bhv', S, a_t)
        S = S * w_t[:, :, None, :] + sa[..., None] * b_t[:, :, None, :] + v_t[..., None] * k_t[:, :, None, :]
        return S, jnp.einsum('bhvk,bhk->bhv', S, r_t)

    seq = tuple(jnp.swapaxes(t, 0, 1) for t in (r, decay, k, v, -kk, kk * a))
    S_T, y = lax.scan(step, S0.astype(F32), seq)
    y = jnp.swapaxes(y, 0, 1)
    mean = jnp.mean(y, axis=-1, keepdims=True)
    var = jnp.mean(jnp.square(y - mean), axis=-1, keepdims=True)
    y = (y - mean) * lax.rsqrt(var + RW_GN_EPS) * gn_w.astype(F32) + gn_b.astype(F32)
    y = y + jnp.sum(r * k * r_k.astype(F32), axis=-1, keepdims=True) * v
    out = y.reshape(Bsz, L, RW_DIM).astype(p.dtype) * g
    return out, p[:, -1], S_T


def ssd_chunked(X, dA, Bm, Cm, h0):
    Bsz, L = X.shape[:2]
    Q = min(SSD_CHUNK, L)
    nc = L // Q
    X = X.reshape(Bsz, nc, Q, SSD_GROUPS, SSD_HPG, SSD_HD)
    dA = dA.reshape(Bsz, nc, Q, SSD_GROUPS, SSD_HPG)
    Bm = Bm.reshape(Bsz, nc, Q, SSD_GROUPS, SSD_STATE)
    Cm = Cm.reshape(Bsz, nc, Q, SSD_GROUPS, SSD_STATE)
    Acs = jnp.cumsum(dA, axis=2)
    incl = jnp.tril(jnp.ones((Q, Q), bool))
    Lmat = jnp.exp(jnp.where(incl[:, :, None, None], Acs[:, :, :, None] - Acs[:, :, None, :], -jnp.inf))
    CB = jnp.einsum('bclgn,bcsgn->bclsg', Cm, Bm)
    Y_diag = jnp.einsum('bclsgr,bcsgrp->bclgrp', CB[..., None] * Lmat, X)
    decay_states = jnp.exp(Acs[:, :, -1:] - Acs)
    states = jnp.einsum('bclgn,bclgrp->bcgrpn', Bm, X * decay_states[..., None])
    chunk_decay = jnp.exp(Acs[:, :, -1])

    def step(h, inp):
        st, cd = inp
        return h * cd[..., None, None] + st, h

    hT, h_in = lax.scan(step, h0, (jnp.moveaxis(states, 1, 0), jnp.moveaxis(chunk_decay, 1, 0)))
    h_in = jnp.moveaxis(h_in, 0, 1)
    Y_off = jnp.einsum('bclgn,bcgrpn->bclgrp', Cm, h_in) * jnp.exp(Acs)[..., None]
    return (Y_diag + Y_off).reshape(Bsz, L, SSD_GROUPS, SSD_HPG, SSD_HD), hT


def ssd_mix(p, conv_buf, h0, conv_w, conv_b, dt_bias, A_log, D_skip, norm_w):
    Bsz, L, _ = p.shape
    z, xBC, dt = jnp.split(p, [SSD_DIM, SSD_DIM + SSD_CONV_DIM], axis=-1)
    xBC, new_buf = causal_dwconv(xBC, conv_buf, conv_w, conv_b)
    xBC = jax.nn.silu(xBC).astype(F32)
    xs, Bm, Cm = jnp.split(xBC, [SSD_DIM, SSD_DIM + SSD_GROUPS * SSD_STATE], axis=-1)
    xs = xs.reshape(Bsz, L, SSD_GROUPS, SSD_HPG, SSD_HD)
    Bm = Bm.reshape(Bsz, L, SSD_GROUPS, SSD_STATE)
    Cm = Cm.reshape(Bsz, L, SSD_GROUPS, SSD_STATE)
    dt = jax.nn.softplus(dt.astype(F32) + dt_bias.astype(F32)).reshape(Bsz, L, SSD_GROUPS, SSD_HPG)
    dA = dt * (-jnp.exp(A_log.astype(F32))).reshape(SSD_GROUPS, SSD_HPG)
    y, hT = ssd_chunked(xs * dt[..., None], dA, Bm, Cm, h0.astype(F32))
    y = y + xs * D_skip.astype(F32).reshape(SSD_GROUPS, SSD_HPG, 1)
    yg = (y * jax.nn.silu(z.astype(F32)).reshape(Bsz, L, SSD_GROUPS, SSD_HPG, SSD_HD)).reshape(Bsz, L, SSD_GROUPS, SSD_HPG * SSD_HD)
    yg = yg * lax.rsqrt(jnp.mean(yg * yg, axis=-1, keepdims=True) + EPS)
    y = yg.reshape(Bsz, L, SSD_DIM) * norm_w.astype(F32)
    return y.astype(p.dtype), new_buf, hT


def gdn_chunked(q, k, v, g, beta, S0):
    Bsz, L = q.shape[:2]
    Q = min(GDN_CHUNK, L)
    nc = L // Q

    def blk(t):
        return jnp.moveaxis(t.reshape((Bsz, nc, Q, GDN_HEADS) + t.shape[3:]), 3, 2)

    q, k, v, g, beta = blk(q), blk(k), blk(v), blk(g), blk(beta)
    gc = jnp.cumsum(g, axis=-1)
    incl = jnp.tril(jnp.ones((Q, Q), bool))
    strict = incl & ~jnp.eye(Q, dtype=bool)
    dec = jnp.exp(jnp.where(incl, gc[..., :, None] - gc[..., None, :], -jnp.inf))
    kb = k * beta[..., None]
    A = jnp.where(strict, jnp.einsum('bchld,bchsd->bchls', kb, k) * dec, 0.0) + jnp.eye(Q, dtype=F32)
    rhs = jnp.concatenate([v * beta[..., None], kb * jnp.exp(gc)[..., None]], axis=-1)
    sol = lax.linalg.triangular_solve(A, rhs, left_side=True, lower=True, unit_diagonal=True)
    vw, kcd = jnp.split(sol, [GDN_DV], axis=-1)
    attn = jnp.einsum('bchld,bchsd->bchls', q, k) * dec
    qg = q * jnp.exp(gc)[..., None]
    kg = k * jnp.exp(gc[..., -1:] - gc)[..., None]
    glast = jnp.exp(gc[..., -1])

    def step(S, inp):
        qg_c, kg_c, vw_c, kcd_c, attn_c, gl_c = inp
        v_new = vw_c - jnp.einsum('bhld,bhdv->bhlv', kcd_c, S)
        o = jnp.einsum('bhld,bhdv->bhlv', qg_c, S) + jnp.einsum('bhls,bhsv->bhlv', attn_c, v_new)
        S = S * gl_c[..., None, None] + jnp.einsum('bhld,bhlv->bhdv', kg_c, v_new)
        return S, o

    xs = tuple(jnp.moveaxis(t, 1, 0) for t in (qg, kg, vw, kcd, attn, glast))
    S_T, o = lax.scan(step, S0, xs)
    o = jnp.transpose(o, (1, 0, 3, 2, 4)).reshape(Bsz, L, GDN_HEADS, GDN_DV)
    return o, S_T


def gdn_mix(p, conv_buf, S0, conv_w, A_log, dt_bias, norm_w):
    Bsz, L, _ = p.shape
    qkv, z, b, a = jnp.split(p, [GDN_CONV_DIM, GDN_CONV_DIM + GDN_V, GDN_CONV_DIM + GDN_V + GDN_HEADS], axis=-1)
    qkv, new_buf = causal_dwconv(qkv, conv_buf, conv_w, None)
    qkv = jax.nn.silu(qkv).astype(F32)
    q, k, v = jnp.split(qkv, [GDN_QK, 2 * GDN_QK], axis=-1)
    q = l2norm(q.reshape(Bsz, L, GDN_HEADS, GDN_DK)) * (GDN_DK ** -0.5)
    k = l2norm(k.reshape(Bsz, L, GDN_HEADS, GDN_DK))
    v = v.reshape(Bsz, L, GDN_HEADS, GDN_DV)
    beta = jax.nn.sigmoid(b.astype(F32))
    g = -jnp.exp(A_log.astype(F32)) * jax.nn.softplus(a.astype(F32) + dt_bias.astype(F32))
    o, S_T = gdn_chunked(q, k, v, g, beta, S0.astype(F32))
    o = o * lax.rsqrt(jnp.mean(o * o, axis=-1, keepdims=True) + EPS) * norm_w.astype(F32)
    o = o * jax.nn.silu(z.astype(F32)).reshape(Bsz, L, GDN_HEADS, GDN_DV)
    return o.reshape(Bsz, L, GDN_V).astype(p.dtype), new_buf, S_T


def memory_kv(mem, g, w_k, w_v):
    m = rmsnorm(mem, g)
    Bsz = mem.shape[0]
    return (m @ w_k).reshape(Bsz, N_MEM, XA_HEADS, XA_HD), (m @ w_v).reshape(Bsz, N_MEM, XA_HEADS, XA_HD)


def cross_attn(h, mk, mv, w_q, w_o):
    Bsz, L, _ = h.shape
    q = (h @ w_q).reshape(Bsz, L, XA_HEADS, XA_HD).astype(F32)
    s = jnp.einsum('blhd,bmhd->bhlm', q, mk.astype(F32)) * (XA_HD ** -0.5)
    pr = jax.nn.softmax(s, axis=-1)
    o = jnp.einsum('bhlm,bmhd->blhd', pr, mv.astype(F32))
    return o.reshape(Bsz, L, XA_DIM).astype(h.dtype) @ w_o


def conv_ffn(h, buf, w_in, conv_w, conv_b, w_out):
    gate, up = jnp.split(h @ w_in, [D_FF], axis=-1)
    gate, new_buf = causal_dwconv(gate, buf, conv_w, conv_b)
    return (jax.nn.silu(gate) * up) @ w_out, new_buf


def run_group(x, mem_k, mem_v, st_rw, st_sh, st_ssd, st_ssdc, st_gdn, st_gdnc, st_ffn, P):
    h = x
    n_rw, n_sh, n_ssd, n_ssdc, n_gdn, n_gdnc, n_ffn = [], [], [], [], [], [], []
    for l in range(DEPTH):
        i = l // 2
        hn = rmsnorm(h, P['norm_mix'][l])
        if l % 2 == 0:
            p_rw, p_ssd = jnp.split(hn @ P['w_in_ab'][i], [RW_PROJ], axis=-1)
            y_rw, sh, s_rw = rwkv7_mix(p_rw, st_sh[i], st_rw[i], P['rw_mu'][i], P['rw_w0'][i], P['rw_w_up'][i],
                                       P['rw_a0'][i], P['rw_a_up'][i], P['rw_g_up'][i], P['rw_k_k'][i],
                                       P['rw_k_a'][i], P['rw_r_k'][i], P['rw_gn_w'][i], P['rw_gn_b'][i])
            y_ssd, cb, s_ssd = ssd_mix(p_ssd, st_ssdc[i], st_ssd[i], P['ssd_conv_w'][i], P['ssd_conv_b'][i],
                                       P['ssd_dt_bias'][i], P['ssd_A_log'][i], P['ssd_D'][i], P['ssd_norm_w'][i])
            mix = jnp.concatenate([y_rw, y_ssd], axis=-1) @ P['w_out_ab'][i]
            n_rw.append(s_rw)
            n_sh.append(sh)
            n_ssd.append(s_ssd)
            n_ssdc.append(cb)
        else:
            y_c, cb, s_gdn = gdn_mix(hn @ P['w_in_c'][i], st_gdnc[i], st_gdn[i], P['gdn_conv_w'][i],
                                     P['gdn_A_log'][i], P['gdn_dt_bias'][i], P['gdn_norm_w'][i])
            mix = y_c @ P['w_out_c'][i]
            n_gdn.append(s_gdn)
            n_gdnc.append(cb)
        h = h + mix
        h = h + cross_attn(rmsnorm(h, P['norm_xa'][l]), mem_k[l], mem_v[l], P['w_xq'][l], P['w_xo'][l])
        f, fb = conv_ffn(rmsnorm(h, P['norm_ffn'][l]), st_ffn[l], P['ffn_w_in'][l], P['ffn_conv_w'][l],
                         P['ffn_conv_b'][l], P['ffn_w_out'][l])
        n_ffn.append(fb)
        h = h + f
    y = rmsnorm(h, P['norm_final'])
    return (y, jnp.stack(n_rw), jnp.stack(n_sh), jnp.stack(n_ssd), jnp.stack(n_ssdc),
            jnp.stack(n_gdn), jnp.stack(n_gdnc), jnp.stack(n_ffn))


def setup_inputs(seed: int = 0) -> dict:
    key = jax.random.key(seed)
    ks = iter(jax.random.split(key, 80))

    def nrm(shape, scale):
        return jax.random.normal(next(ks), shape, F32) * scale

    def unif(shape, lo, hi):
        return jax.random.uniform(next(ks), shape, F32, lo, hi)

    def gain(shape):
        return 1.0 + nrm(shape, 0.02)

    def dt_bias(shape):
        dt = jnp.exp(unif(shape, math.log(1e-3), math.log(1e-1)))
        return dt + jnp.log(-jnp.expm1(-dt))

    E, O = N_EVEN, N_ODD
    return {
        'x_prompt': nrm((BATCH, SEQ, D_MODEL), 1.0),
        'x_sample': nrm((DEC_BATCH, DEC_SEQ, D_MODEL), 1.0),
        'mem_prompt': nrm((BATCH, N_MEM, D_MODEL), 1.0),
        'state_rwkv': nrm((E, DEC_BATCH, RW_HEADS, RW_HD, RW_HD), 0.1),
        'state_rwkv_shift': nrm((E, DEC_BATCH, RW_PROJ), 1.0),
        'state_ssd': nrm((E, DEC_BATCH, SSD_GROUPS, SSD_HPG, SSD_HD, SSD_STATE), 0.1),
        'state_ssd_conv': nrm((E, DEC_BATCH, SSD_CONV - 1, SSD_CONV_DIM), 1.0),
        'state_gdn': nrm((O, DEC_BATCH, GDN_HEADS, GDN_DK, GDN_DV), 0.1),
        'state_gdn_conv': nrm((O, DEC_BATCH, GDN_CONV - 1, GDN_CONV_DIM), 1.0),
        'state_ffn_conv': nrm((DEPTH, DEC_BATCH, FFN_CONV - 1, D_FF), 1.0),
        'cache_mem_k': nrm((DEPTH, DEC_BATCH, N_MEM, XA_HEADS, XA_HD), 1.0),
        'cache_mem_v': nrm((DEPTH, DEC_BATCH, N_MEM, XA_HEADS, XA_HD), 1.0),
        'norm_mix': gain((DEPTH, D_MODEL)),
        'norm_xa': gain((DEPTH, D_MODEL)),
        'norm_mem': gain((DEPTH, D_MODEL)),
        'norm_ffn': gain((DEPTH, D_MODEL)),
        'norm_final': gain((D_MODEL,)),
        'w_in_ab': nrm((E, D_MODEL, AB_PROJ), D_MODEL ** -0.5),
        'rw_mu': unif((E, RW_PROJ), 0.0, 1.0),
        'rw_w0': unif((E, RW_DIM), -6.0, -0.5),
        'rw_w_up': nrm((E, RW_DECAY_LORA, RW_DIM), 0.1),
        'rw_a0': nrm((E, RW_DIM), 0.1),
        'rw_a_up': nrm((E, RW_A_LORA, RW_DIM), 0.1),
        'rw_g_up': nrm((E, RW_GATE_LORA, RW_DIM), RW_GATE_LORA ** -0.5),
        'rw_k_k': 0.85 + nrm((E, RW_DIM), 0.02),
        'rw_k_a': 1.0 + nrm((E, RW_DIM), 0.02),
        'rw_r_k': nrm((E, RW_HEADS, RW_HD), 0.1),
        'rw_gn_w': gain((E, RW_HEADS, RW_HD)),
        'rw_gn_b': nrm((E, RW_HEADS, RW_HD), 0.02),
        'ssd_conv_w': nrm((E, SSD_CONV, SSD_CONV_DIM), SSD_CONV ** -0.5),
        'ssd_conv_b': nrm((E, SSD_CONV_DIM), 0.02),
        'ssd_dt_bias': dt_bias((E, SSD_HEADS)),
        'ssd_A_log': jnp.log(unif((E, SSD_HEADS), 1.0, 16.0)),
        'ssd_D': gain((E, SSD_HEADS)),
        'ssd_norm_w': gain((E, SSD_DIM)),
        'w_out_ab': nrm((E, AB_OUT, D_MODEL), 0.5 * AB_OUT ** -0.5),
        'w_in_c': nrm((O, D_MODEL, GDN_PROJ), D_MODEL ** -0.5),
        'gdn_conv_w': nrm((O, GDN_CONV, GDN_CONV_DIM), GDN_CONV ** -0.5),
        'gdn_A_log': jnp.log(unif((O, GDN_HEADS), 1.0, 16.0)),
        'gdn_dt_bias': dt_bias((O, GDN_HEADS)),
        'gdn_norm_w': gain((O, GDN_DV)),
        'w_out_c': nrm((O, GDN_V, D_MODEL), 0.5 * GDN_V ** -0.5),
        'w_xq': nrm((DEPTH, D_MODEL, XA_DIM), D_MODEL ** -0.5),
        'w_xk': nrm((DEPTH, D_MODEL, XA_DIM), D_MODEL ** -0.5),
        'w_xv': nrm((DEPTH, D_MODEL, XA_DIM), D_MODEL ** -0.5),
        'w_xo': nrm((DEPTH, XA_DIM, D_MODEL), 0.5 * XA_DIM ** -0.5),
        'ffn_w_in': nrm((DEPTH, D_MODEL, 2 * D_FF), D_MODEL ** -0.5),
        'ffn_conv_w': nrm((DEPTH, FFN_CONV, D_FF), FFN_CONV ** -0.5),
        'ffn_conv_b': nrm((DEPTH, D_FF), 0.02),
        'ffn_w_out': nrm((DEPTH, D_FF, D_MODEL), 0.5 * D_FF ** -0.5),
    }


def reference(x_prompt, x_sample, mem_prompt, state_rwkv, state_rwkv_shift, state_ssd, state_ssd_conv,
              state_gdn, state_gdn_conv, state_ffn_conv, cache_mem_k, cache_mem_v,
              norm_mix, norm_xa, norm_mem, norm_ffn, norm_final,
              w_in_ab, rw_mu, rw_w0, rw_w_up, rw_a0, rw_a_up, rw_g_up, rw_k_k, rw_k_a, rw_r_k, rw_gn_w, rw_gn_b,
              ssd_conv_w, ssd_conv_b, ssd_dt_bias, ssd_A_log, ssd_D, ssd_norm_w, w_out_ab,
              w_in_c, gdn_conv_w, gdn_A_log, gdn_dt_bias, gdn_norm_w, w_out_c,
              w_xq, w_xk, w_xv, w_xo, ffn_w_in, ffn_conv_w, ffn_conv_b, ffn_w_out):
    P = dict(norm_mix=norm_mix, norm_xa=norm_xa, norm_ffn=norm_ffn, norm_final=norm_final,
             w_in_ab=w_in_ab, rw_mu=rw_mu, rw_w0=rw_w0, rw_w_up=rw_w_up, rw_a0=rw_a0, rw_a_up=rw_a_up,
             rw_g_up=rw_g_up, rw_k_k=rw_k_k, rw_k_a=rw_k_a, rw_r_k=rw_r_k, rw_gn_w=rw_gn_w, rw_gn_b=rw_gn_b,
             ssd_conv_w=ssd_conv_w, ssd_conv_b=ssd_conv_b, ssd_dt_bias=ssd_dt_bias, ssd_A_log=ssd_A_log,
             ssd_D=ssd_D, ssd_norm_w=ssd_norm_w, w_out_ab=w_out_ab,
             w_in_c=w_in_c, gdn_conv_w=gdn_conv_w, gdn_A_log=gdn_A_log, gdn_dt_bias=gdn_dt_bias,
             gdn_norm_w=gdn_norm_w, w_out_c=w_out_c, w_xq=w_xq, w_xo=w_xo,
             ffn_w_in=ffn_w_in, ffn_conv_w=ffn_conv_w, ffn_conv_b=ffn_conv_b, ffn_w_out=ffn_w_out)

    mk_l, mv_l = [], []
    for l in range(DEPTH):
        mk, mv = memory_kv(mem_prompt, norm_mem[l], w_xk[l], w_xv[l])
        mk_l.append(mk)
        mv_l.append(mv)
    mem_k_p = jnp.stack(mk_l)
    mem_v_p = jnp.stack(mv_l)
    dt = x_prompt.dtype
    y_p, rw_p, sh_p, ssd_p, ssdc_p, gdn_p, gdnc_p, ffn_p = run_group(
        x_prompt, mem_k_p, mem_v_p,
        jnp.zeros((N_EVEN, BATCH, RW_HEADS, RW_HD, RW_HD), F32),
        jnp.zeros((N_EVEN, BATCH, RW_PROJ), dt),
        jnp.zeros((N_EVEN, BATCH, SSD_GROUPS, SSD_HPG, SSD_HD, SSD_STATE), F32),
        jnp.zeros((N_EVEN, BATCH, SSD_CONV - 1, SSD_CONV_DIM), dt),
        jnp.zeros((N_ODD, BATCH, GDN_HEADS, GDN_DK, GDN_DV), F32),
        jnp.zeros((N_ODD, BATCH, GDN_CONV - 1, GDN_CONV_DIM), dt),
        jnp.zeros((DEPTH, BATCH, FFN_CONV - 1, D_FF), dt),
        P)

    y_s, rw_s, sh_s, ssd_s, ssdc_s, gdn_s, gdnc_s, ffn_s = run_group(
        x_sample, cache_mem_k, cache_mem_v, state_rwkv, state_rwkv_shift, state_ssd, state_ssd_conv,
        state_gdn, state_gdn_conv, state_ffn_conv, P)

    return (y_p, y_s, rw_p, rw_s, sh_p, sh_s, ssd_p, ssd_s, ssdc_p, ssdc_s, gdn_p, gdn_s, gdnc_p, gdnc_s,
            ffn_p, ffn_s, mem_k_p, mem_v_p)
```

```python
import functools
from typing import NamedTuple

import jax
import jax.numpy as jnp
from jax import lax
from jax.experimental import pallas as pl
from jax.experimental.pallas import tpu as pltpu

F32 = jnp.float32
BF16 = jnp.bfloat16
HI = lax.Precision.HIGHEST

D_MODEL = 2048
EPS = 1e-6
RW_HEADS, RW_HD = 16, 64
RW_DIM = RW_HEADS * RW_HD
RW_LORA = 256
RW_PROJ = 3 * RW_DIM + RW_LORA
RW_GN_EPS = 6.4e-4
SSD_HEADS, SSD_HD, SSD_GROUPS, SSD_STATE = 16, 64, 2, 128
SSD_HPG = SSD_HEADS // SSD_GROUPS
SSD_DIM = SSD_HEADS * SSD_HD
SSD_GDIM = SSD_HPG * SSD_HD
SSD_CONV_DIM = SSD_DIM + 2 * SSD_GROUPS * SSD_STATE
SSD_PROJ = SSD_DIM + SSD_CONV_DIM + SSD_HEADS
AB_PROJ = RW_PROJ + SSD_PROJ
GDN_HEADS, GDN_D = 16, 128
GDN_V = GDN_HEADS * GDN_D
GDN_CONV_DIM = 3 * GDN_V
GDN_PROJ = GDN_CONV_DIM + GDN_V + 2 * GDN_HEADS
D_FF = 5632
XA_HEADS, XA_HD, N_MEM = 4, 128, 256
XA_DIM = XA_HEADS * XA_HD
CONV_W = 4
FFN_CONV_W = 3

SUBLANES = 8
LANES = 128
VMEM_LIMIT = 56 * 1024 * 1024

MIX_ROWS = 64
INV_BLOCK = 16
MM_TM, MM_TN = 1024, 512
RW_HP = 2
GDN_G = 4
FFN_RT, FFN_CT = 256, 512
XA_TQ = 512
NORM_TM = 512


class Group(NamedTuple):
    B: int
    L: int
    first: int
    bb: int

    @property
    def qs(self):
        return MIX_ROWS // self.bb

    @property
    def nc(self):
        return self.L // self.qs

    @property
    def embedded(self):
        return self.first > 0


def _cp(*sem):
    return pltpu.CompilerParams(dimension_semantics=sem, vmem_limit_bytes=VMEM_LIMIT)


def _dot(a, b):
    return jnp.dot(a.astype(BF16), b.astype(BF16), preferred_element_type=F32)


def _dot_nt(a, b):
    return lax.dot_general(a.astype(BF16), b.astype(BF16), (((1,), (1,)), ((), ())), preferred_element_type=F32)


def _dot_tn(a, b):
    return lax.dot_general(a.astype(BF16), b.astype(BF16), (((0,), (0,)), ((), ())), preferred_element_type=F32)


def _dot_hi(a, b):
    return jnp.dot(a, b, preferred_element_type=F32, precision=HI)


def _dot_nt_hi(a, b):
    return lax.dot_general(a, b, (((1,), (1,)), ((), ())), preferred_element_type=F32, precision=HI)


def _sigmoid(x):
    return 1.0 / (1.0 + jnp.exp(-x))


def _silu(x):
    return x * _sigmoid(x)


def _softplus(x):
    return jnp.maximum(x, 0.0) + jnp.log(1.0 + jnp.exp(-jnp.abs(x)))


def _iota2(shape, axis):
    return lax.broadcasted_iota(jnp.int32, shape, axis)


class _Masks(NamedTuple):
    incl: jax.Array
    strict: jax.Array
    trilf: jax.Array
    lastsel: jax.Array
    eye: jax.Array
    valid: jax.Array


def _masks(grp):
    r = MIX_ROWS
    ri, ci = _iota2((r, r), 0), _iota2((r, r), 1)
    same = (ri // grp.qs) == (ci // grp.qs)
    incl = same & (ci <= ri)
    strict = same & (ci < ri)
    lastsel = (ci == (ri // grp.qs) * grp.qs + (grp.qs - 1)).astype(F32)
    t = _iota2((r, 1), 0) % grp.qs
    return _Masks(incl, strict, incl.astype(F32), lastsel, (ri == ci).astype(F32), t >= grp.first)


def _neumann(lm, eye, nil):
    t = eye + lm
    p = lm
    n = 2
    while n < nil:
        p = _dot_hi(p, p)
        t = t + _dot_hi(p, t)
        n *= 2
    return t


def _inv_unit_lower(lm, eye, qs):
    if qs <= INV_BLOCK:
        return _neumann(lm, eye, qs)
    r = lm.shape[0]
    ri, ci = _iota2((r, r), 0), _iota2((r, r), 1)
    diag = (ri // INV_BLOCK) == (ci // INV_BLOCK)
    d = jnp.where(diag, lm, 0.0)
    dinv = _neumann(d, eye, INV_BLOCK)
    n = _dot_hi(dinv, lm - d)
    return _dot_hi(_neumann(n, eye, qs // INV_BLOCK), dinv)


def _taps(grp, cur, halo, width):
    if grp.embedded:
        t = _iota2((cur.shape[0], 1), 0) % grp.qs
        full = jnp.where(t < grp.first, halo, cur)
        return [full] + [pltpu.roll(full, s, axis=0) for s in range(1, width)]
    full = jnp.concatenate([halo, cur], axis=0)
    return [cur] + [pltpu.roll(full, s, axis=0)[SUBLANES:] for s in range(1, width)]


def _conv(grp, cur, halo, w_ref, bias, width):
    taps = _taps(grp, cur, halo, width)
    y = taps[0] * w_ref[width - 1:width, :]
    for s in range(1, width):
        y = y + taps[s] * w_ref[width - 1 - s:width - s, :]
    return y if bias is None else y + bias


class _Tok(NamedTuple):
    arr: jax.Array
    width: int
    col: object
    hist: jax.Array = None
    hcol: object = None


def _mixer_specs(grp, toks):
    nc = grp.nc
    sub = MIX_ROWS // SUBLANES
    ops, specs = [], []
    for t in toks:
        ops.append(t.arr)
        specs.append(pl.BlockSpec((MIX_ROWS, t.width), lambda b, g, c, t=t: (b * nc + c, t.col(g))))
    for t in toks:
        if t.hist is None:
            continue
        if not grp.embedded:
            ops.append(t.arr)
            specs.append(pl.BlockSpec(
                (SUBLANES, t.width), lambda b, g, c, t=t: (jnp.maximum((b * nc + c) * sub - 1, 0), t.col(g))))
        ops.append(t.hist)
        hrows = MIX_ROWS if grp.embedded else SUBLANES
        specs.append(pl.BlockSpec((hrows, t.width), lambda b, g, c, t=t: (b, t.hcol(g))))
    return ops, specs


def _read_windows(grp, refs, n_plain, n_hist):
    c = pl.program_id(2)
    cur = [r[...] for r in refs[:n_plain]]
    rest = refs[n_plain:]
    halos = []
    for i in range(n_hist):
        if grp.embedded:
            halos.append(rest[i][...])
        else:
            halos.append(jnp.where(c == 0, rest[2 * i + 1][...], rest[2 * i][...]))
    used = n_hist if grp.embedded else 2 * n_hist
    return cur, halos, rest[used:]


def _param_spec(width, col):
    return lambda rows: pl.BlockSpec((rows, width), lambda b, g, c: (0, col(g)))


def _matmul_body(*refs, norm, has_res):
    it = iter(refs)
    a_ref = next(it)
    g_ref = next(it) if norm else None
    w_ref = next(it)
    res_ref = next(it) if has_res else None
    o_ref = next(it)
    if norm:
        an_ref = next(it)

        @pl.when(pl.program_id(1) == 0)
        def _():
            x = a_ref[...]
            y = x * lax.rsqrt(jnp.mean(x * x, axis=-1, keepdims=True) + EPS)
            an_ref[...] = (y * g_ref[...]).astype(BF16)

        a = an_ref[...]
    else:
        a = a_ref[...]
    acc = jnp.dot(a, w_ref[...], preferred_element_type=F32)
    if has_res:
        acc = acc + res_ref[...]
    o_ref[...] = acc.astype(o_ref.dtype)


def _matmul(a, w, *, gain=None, res=None, out_dtype=F32, tm=MM_TM, tn=MM_TN):
    m, k = a.shape
    n = w.shape[1]
    norm = gain is not None
    ops = [a]
    specs = [pl.BlockSpec((tm, k), lambda i, j: (i, 0))]
    if norm:
        ops.append(gain.reshape(1, k))
        specs.append(pl.BlockSpec((1, k), lambda i, j: (0, 0)))
    ops.append(w)
    specs.append(pl.BlockSpec((k, tn), lambda i, j: (0, j)))
    if res is not None:
        ops.append(res)
        specs.append(pl.BlockSpec((tm, tn), lambda i, j: (i, j)))
    return pl.pallas_call(
        functools.partial(_matmul_body, norm=norm, has_res=res is not None),
        grid=(m // tm, n // tn),
        in_specs=specs,
        out_specs=pl.BlockSpec((tm, tn), lambda i, j: (i, j)),
        out_shape=jax.ShapeDtypeStruct((m, n), out_dtype),
        scratch_shapes=[pltpu.VMEM((tm, k), BF16)] if norm else [],
        compiler_params=_cp("parallel", "arbitrary"),
        name="matmul",
    )(*ops)


def _rmsnorm_body(x_ref, g_ref, o_ref):
    x = x_ref[...]
    y = x * lax.rsqrt(jnp.mean(x * x, axis=-1, keepdims=True) + EPS)
    o_ref[...] = (y * g_ref[...]).astype(o_ref.dtype)


def _rmsnorm(x, gain, out_dtype=F32, tm=NORM_TM):
    m, k = x.shape
    return pl.pallas_call(
        _rmsnorm_body,
        grid=(m // tm,),
        in_specs=[pl.BlockSpec((tm, k), lambda i: (i, 0)), pl.BlockSpec((1, k), lambda i: (0, 0))],
        out_specs=pl.BlockSpec((tm, k), lambda i: (i, 0)),
        out_shape=jax.ShapeDtypeStruct((m, k), out_dtype),
        compiler_params=_cp("parallel"),
        name="rmsnorm",
    )(x, gain.reshape(1, k))


def _rwkv_body(*refs, grp):
    r_rows = MIX_ROWS
    qs, bb = grp.qs, grp.bb
    nh = 2 * RW_HP
    cur, halos, rest = _read_windows(grp, refs, 4, 4)
    (mu_r, mu_k, mu_v, mu_lo, w0_ref, wup_ref, a0_ref, aup_ref, gup_ref, kk_ref, ka_ref, rk_ref, gnw_ref, gnb_ref,
     s0_ref, y_ref, s_ref) = rest
    c = pl.program_id(2)

    @pl.when(c == 0)
    def _():
        s_ref[...] = s0_ref[...]

    m = _masks(grp)

    def shifted(i, mu_ref):
        x = cur[i]
        prev = _taps(grp, x, halos[i], 2)[1]
        return x + (prev - x) * mu_ref[...]

    r = shifted(0, mu_r)
    k = shifted(1, mu_k)
    v = shifted(2, mu_v)
    lo = shifted(3, mu_lo)
    wd, ad, gd = lo[:, :64], lo[:, 64:128], lo[:, 128:256]
    w = -_softplus(-(w0_ref[...] + _dot_hi(jnp.tanh(wd), wup_ref[...]))) - 0.5
    lw = -jnp.exp(w)
    a = _sigmoid(a0_ref[...] + _dot_hi(ad, aup_ref[...]))
    g = _dot_hi(_sigmoid(gd), gup_ref[...])

    width = nh * RW_HD
    bd = ((_iota2((width, width), 0) // RW_HD) == (_iota2((width, width), 1) // RW_HD)).astype(F32)
    kx = k * kk_ref[...]
    kkn = kx * lax.rsqrt(_dot_hi(kx * kx, bd) + 1e-6)
    kp = k * (1.0 + (a - 1.0) * ka_ref[...])
    lw = jnp.where(m.valid, lw, 0.0)
    at = jnp.where(m.valid, -kkn, 0.0)
    bt = jnp.where(m.valid, kkn * a, 0.0)
    kp = jnp.where(m.valid, kp, 0.0)
    bonus = _dot_hi(r * kp * rk_ref[...], bd) * v

    cum = _dot_hi(m.trilf, lw)
    cum_end = _dot_hi(m.lastsel, cum)
    e_neg = jnp.exp(-cum)
    a_til = at * jnp.exp(cum - lw)
    r_til = r * jnp.exp(cum)
    b_til = bt * e_neg
    k_til = kp * e_neg
    e_end = jnp.exp(cum_end - cum)
    b_hat = bt * e_end
    k_hat = kp * e_end
    p_end = jnp.exp(cum_end)

    ys = []
    for h in range(nh):
        sl = slice(h * RW_HD, (h + 1) * RW_HD)
        ar = jnp.concatenate([a_til[:, sl], r_til[:, sl]], axis=0)
        bk = jnp.concatenate([b_til[:, sl], k_til[:, sl]], axis=0)
        gm = _dot_nt(ar, bk)
        m_ab = jnp.where(m.strict, gm[:r_rows, :r_rows], 0.0)
        m_ak = jnp.where(m.strict, gm[:r_rows, r_rows:], 0.0)
        m_rb = jnp.where(m.incl, gm[r_rows:, :r_rows], 0.0)
        m_rk = jnp.where(m.incl, gm[r_rows:, r_rows:], 0.0)
        vh = v[:, sl]
        as_a, as_r = [], []
        for j in range(bb):
            rows = slice(j * qs, (j + 1) * qs)
            sj = s_ref[j, h]
            as_a.append(_dot_nt(a_til[rows, sl], sj))
            as_r.append(_dot_nt(r_til[rows, sl], sj))
        as_a = as_a[0] if bb == 1 else jnp.concatenate(as_a, axis=0)
        as_r = as_r[0] if bb == 1 else jnp.concatenate(as_r, axis=0)
        rhs = as_a + _dot(m_ak, vh)
        u = _dot_hi(_inv_unit_lower(m_ab, m.eye, qs), rhs)
        y = as_r + _dot(m_rb, u) + _dot(m_rk, vh)
        for j in range(bb):
            rows = slice(j * qs, (j + 1) * qs)
            upd = _dot_tn(u[rows], b_hat[rows, sl]) + _dot_tn(vh[rows], k_hat[rows, sl])
            s_ref[j, h] = s_ref[j, h] * p_end[j * qs:j * qs + 1, sl] + upd
        ys.append(y)
    y = jnp.concatenate(ys, axis=1)
    mean = _dot_hi(y, bd) * (1.0 / RW_HD)
    d = y - mean
    var = _dot_hi(d * d, bd) * (1.0 / RW_HD)
    out = d * lax.rsqrt(var + RW_GN_EPS) * gnw_ref[...] + gnb_ref[...] + bonus
    y_ref[...] = (out * g).astype(y_ref.dtype)


def _rwkv(grp, proj, hist, s0, prm):
    width = RW_HP * LANES
    ng = RW_DIM // width
    lora_blk = 3 * RW_DIM // RW_LORA
    toks = [
        _Tok(proj, width, lambda g: g, hist, lambda g: g),
        _Tok(proj, width, lambda g: ng + g, hist, lambda g: ng + g),
        _Tok(proj, width, lambda g: 2 * ng + g, hist, lambda g: 2 * ng + g),
        _Tok(proj, RW_LORA, lambda g: lora_blk, hist, lambda g: lora_blk),
    ]
    ops, specs = _mixer_specs(grp, toks)
    head = _param_spec(width, lambda g: g)
    mu = prm["mu"].reshape(1, RW_PROJ)
    flat = lambda x: x.reshape(1, RW_DIM)
    params = [
        (mu, pl.BlockSpec((1, width), lambda b, g, c: (0, g))),
        (mu, pl.BlockSpec((1, width), lambda b, g, c: (0, ng + g))),
        (mu, pl.BlockSpec((1, width), lambda b, g, c: (0, 2 * ng + g))),
        (mu, pl.BlockSpec((1, RW_LORA), lambda b, g, c: (0, lora_blk))),
        (flat(prm["w0"]), head(1)), (prm["w_up"], head(64)),
        (flat(prm["a0"]), head(1)), (prm["a_up"], head(64)), (prm["g_up"], head(128)),
        (flat(prm["k_k"]), head(1)), (flat(prm["k_a"]), head(1)), (flat(prm["r_k"]), head(1)),
        (flat(prm["gn_w"]), head(1)), (flat(prm["gn_b"]), head(1)),
    ]
    ops += [p for p, _ in params]
    specs += [s for _, s in params]
    st_spec = pl.BlockSpec((grp.bb, 2 * RW_HP, RW_HD, RW_HD), lambda b, g, c: (b, g, 0, 0))
    ops.append(s0)
    specs.append(st_spec)
    rows = grp.B * grp.L
    nc = grp.nc
    return pl.pallas_call(
        functools.partial(_rwkv_body, grp=grp),
        grid=(grp.B // grp.bb, ng, nc),
        in_specs=specs,
        out_specs=[pl.BlockSpec((MIX_ROWS, width), lambda b, g, c: (b * nc + c, g)), st_spec],
        out_shape=[jax.ShapeDtypeStruct((rows, RW_DIM), BF16), jax.ShapeDtypeStruct(s0.shape, F32)],
        compiler_params=_cp("parallel", "parallel", "arbitrary"),
        name="rwkv7",
    )(*ops)


def _ssd_body(*refs, grp):
    qs, bb = grp.qs, grp.bb
    cur, halos, rest = _read_windows(grp, refs, 7, 4)
    (cw_x0, cw_x1, cw_b, cw_c, cb_x0, cb_x1, cb_b, cb_c, dtb_ref, alog_ref, dskip_ref, nw_ref,
     h0_ref, y_ref, h_ref) = rest
    x0, x1, b_raw, c_raw, z0, z1, dt_blk = cur
    c = pl.program_id(2)
    grp_idx = pl.program_id(1)

    @pl.when(c == 0)
    def _():
        h_ref[...] = h0_ref[...]

    m = _masks(grp)
    xs = jnp.concatenate([
        _silu(_conv(grp, x0, halos[0], cw_x0, cb_x0[...], CONV_W)),
        _silu(_conv(grp, x1, halos[1], cw_x1, cb_x1[...], CONV_W))], axis=1)
    bm = _silu(_conv(grp, b_raw, halos[2], cw_b, cb_b[...], CONV_W))
    cm = _silu(_conv(grp, c_raw, halos[3], cw_c, cb_c[...], CONV_W))
    z = jnp.concatenate([z0, z1], axis=1)

    sel = (_iota2((SSD_HEADS, SSD_HPG), 0) == grp_idx * SSD_HPG + _iota2((SSD_HEADS, SSD_HPG), 1)).astype(F32)
    dt_all = _softplus(dt_blk[:, :SSD_HEADS] + dtb_ref[...])
    dt = jnp.where(m.valid, _dot_hi(dt_all, sel), 0.0)
    da = dt * (-jnp.exp(alog_ref[0]))
    acs = _dot_hi(m.trilf, da)
    acs_end = _dot_hi(m.lastsel, acs)
    eye8 = (_iota2((SSD_HPG, SSD_HPG), 0) == _iota2((SSD_HPG, SSD_HPG), 1)).astype(F32)
    acs_row = _dot_nt_hi(eye8, acs)
    expand = (_iota2((SSD_HPG, SSD_GDIM), 1) // SSD_HD == _iota2((SSD_HPG, SSD_GDIM), 0)).astype(F32)
    expand_t = (_iota2((SSD_GDIM, SSD_HPG), 0) // SSD_HD == _iota2((SSD_GDIM, SSD_HPG), 1)).astype(F32)
    xd = xs * _dot_hi(dt, expand)
    xd_dec = xd * jnp.exp(_dot_hi(acs_end - acs, expand))
    cb = _dot_nt(cm, bm)
    yd = []
    for r in range(SSD_HPG):
        diff = acs[:, r:r + 1] - acs_row[r:r + 1, :]
        lmat = jnp.where(m.incl, jnp.exp(jnp.where(m.incl, diff, 0.0)), 0.0)
        yd.append(_dot(cb * lmat, xd[:, r * SSD_HD:(r + 1) * SSD_HD]))
    y = jnp.concatenate(yd, axis=1)
    yoff = []
    for j in range(bb):
        rows = slice(j * qs, (j + 1) * qs)
        hj = h_ref[j, 0]
        yoff.append(_dot_nt(cm[rows], hj))
        end_col = jnp.exp(_dot_nt_hi(expand_t, acs_end[j * qs:j * qs + SUBLANES]))[:, :1]
        h_ref[j, 0] = hj * end_col + _dot_tn(xd_dec[rows], bm[rows])
    yoff = yoff[0] if bb == 1 else jnp.concatenate(yoff, axis=0)
    y = y + yoff * jnp.exp(_dot_hi(acs, expand)) + xs * dskip_ref[...]
    yg = y * _silu(z)
    yg = yg * lax.rsqrt(jnp.mean(yg * yg, axis=-1, keepdims=True) + EPS)
    y_ref[...] = (yg * nw_ref[...]).astype(y_ref.dtype)


def _ssd(grp, proj, hist, h0, prm):
    half = SSD_GDIM // 2
    z_blk = RW_PROJ // half
    x_blk = (RW_PROJ + SSD_DIM) // half
    b_blk = (RW_PROJ + 2 * SSD_DIM) // SSD_STATE
    c_blk = b_blk + SSD_GROUPS
    dt_blk = (RW_PROJ + SSD_DIM + SSD_CONV_DIM) // LANES
    hb_blk = SSD_DIM // SSD_STATE
    toks = [
        _Tok(proj, half, lambda g: x_blk + 2 * g, hist, lambda g: 2 * g),
        _Tok(proj, half, lambda g: x_blk + 2 * g + 1, hist, lambda g: 2 * g + 1),
        _Tok(proj, SSD_STATE, lambda g: b_blk + g, hist, lambda g: hb_blk + g),
        _Tok(proj, SSD_STATE, lambda g: c_blk + g, hist, lambda g: hb_blk + SSD_GROUPS + g),
        _Tok(proj, half, lambda g: z_blk + 2 * g),
        _Tok(proj, half, lambda g: z_blk + 2 * g + 1),
        _Tok(proj, LANES, lambda g: dt_blk),
    ]
    ops, specs = _mixer_specs(grp, toks)
    cw, cbias = prm["conv_w"], prm["conv_b"].reshape(1, SSD_CONV_DIM)
    conv_cols = [(half, lambda g: 2 * g), (half, lambda g: 2 * g + 1),
                 (SSD_STATE, lambda g: hb_blk + g), (SSD_STATE, lambda g: hb_blk + SSD_GROUPS + g)]
    params = [(cw, _param_spec(wd, col)(CONV_W)) for wd, col in conv_cols]
    params += [(cbias, _param_spec(wd, col)(1)) for wd, col in conv_cols]
    grouped = lambda x: x.reshape(SSD_GROUPS, 1, SSD_HPG)
    params += [
        (prm["dt_bias"].reshape(1, SSD_HEADS), pl.BlockSpec((1, SSD_HEADS), lambda b, g, c: (0, 0))),
        (grouped(prm["A_log"]), pl.BlockSpec((1, 1, SSD_HPG), lambda b, g, c: (g, 0, 0))),
        (jnp.repeat(prm["D"], SSD_HD).reshape(1, SSD_DIM), _param_spec(SSD_GDIM, lambda g: g)(1)),
        (prm["norm_w"].reshape(1, SSD_DIM), _param_spec(SSD_GDIM, lambda g: g)(1)),
    ]
    ops += [p for p, _ in params]
    specs += [s for _, s in params]
    st_spec = pl.BlockSpec((grp.bb, 1, SSD_GDIM, SSD_STATE), lambda b, g, c: (b, g, 0, 0))
    ops.append(h0)
    specs.append(st_spec)
    rows = grp.B * grp.L
    nc = grp.nc
    return pl.pallas_call(
        functools.partial(_ssd_body, grp=grp),
        grid=(grp.B // grp.bb, SSD_GROUPS, nc),
        in_specs=specs,
        out_specs=[pl.BlockSpec((MIX_ROWS, SSD_GDIM), lambda b, g, c: (b * nc + c, g)), st_spec],
        out_shape=[jax.ShapeDtypeStruct((rows, SSD_DIM), BF16), jax.ShapeDtypeStruct(h0.shape, F32)],
        compiler_params=_cp("parallel", "parallel", "arbitrary"),
        name="ssd",
    )(*ops)


def _gdn_body(*refs, grp):
    r_rows = MIX_ROWS
    qs, bb = grp.qs, grp.bb
    cur, halos, rest = _read_windows(grp, refs, 5, 3)
    cw_q, cw_k, cw_v, alog_ref, dtb_ref, nw_ref, s0_ref, o_ref, s_ref = rest
    q_raw, k_raw, v_raw, z, ba = cur
    c = pl.program_id(2)
    grp_idx = pl.program_id(1)

    @pl.when(c == 0)
    def _():
        s_ref[...] = s0_ref[...]

    m = _masks(grp)
    q = _silu(_conv(grp, q_raw, halos[0], cw_q, None, CONV_W))
    k = _silu(_conv(grp, k_raw, halos[1], cw_k, None, CONV_W))
    v = _silu(_conv(grp, v_raw, halos[2], cw_v, None, CONV_W))

    sel = (_iota2((GDN_HEADS, GDN_G), 0) == grp_idx * GDN_G + _iota2((GDN_HEADS, GDN_G), 1)).astype(F32)
    b_g = _dot_hi(ba[:, :GDN_HEADS], sel)
    a_g = _dot_hi(ba[:, GDN_HEADS:2 * GDN_HEADS], sel)
    beta = jnp.where(m.valid, _sigmoid(b_g), 0.0)
    gg = jnp.where(m.valid, -jnp.exp(alog_ref[0]) * _softplus(a_g + dtb_ref[0]), 0.0)
    gc = _dot_hi(m.trilf, gg)
    gc_end = _dot_hi(m.lastsel, gc)
    eye_g = (_iota2((SUBLANES, GDN_G), 0) == _iota2((SUBLANES, GDN_G), 1)).astype(F32)
    gc_row = _dot_nt_hi(eye_g, gc)

    outs = []
    for h in range(GDN_G):
        sl = slice(h * GDN_D, (h + 1) * GDN_D)
        qh, kh, vh = q[:, sl], k[:, sl], v[:, sl]
        qh = qh * lax.rsqrt(jnp.sum(qh * qh, axis=-1, keepdims=True) + 1e-6) * (GDN_D ** -0.5)
        kh = kh * lax.rsqrt(jnp.sum(kh * kh, axis=-1, keepdims=True) + 1e-6)
        bcol, gcol, gend = beta[:, h:h + 1], gc[:, h:h + 1], gc_end[:, h:h + 1]
        diff = gcol - gc_row[h:h + 1, :]
        dec = jnp.where(m.incl, jnp.exp(jnp.where(m.incl, diff, 0.0)), 0.0)
        kb = kh * bcol
        kq = _dot_nt(jnp.concatenate([kb, qh], axis=0), kh)
        lm = -jnp.where(m.strict, kq[:r_rows] * dec, 0.0)
        attn = jnp.where(m.incl, kq[r_rows:] * dec, 0.0)
        eg = jnp.exp(gcol)
        sol = _dot_hi(_inv_unit_lower(lm, m.eye, qs), jnp.concatenate([vh * bcol, kb * eg], axis=1))
        vw, kcd = sol[:, :GDN_D], sol[:, GDN_D:]
        qg = qh * eg
        kg = kh * jnp.exp(gend - gcol)
        ks, qsv = [], []
        for j in range(bb):
            rows = slice(j * qs, (j + 1) * qs)
            sj = s_ref[j, h]
            ks.append(_dot(kcd[rows], sj))
            qsv.append(_dot(qg[rows], sj))
        ks = ks[0] if bb == 1 else jnp.concatenate(ks, axis=0)
        qsv = qsv[0] if bb == 1 else jnp.concatenate(qsv, axis=0)
        v_new = vw - ks
        o = qsv + _dot(attn, v_new)
        for j in range(bb):
            rows = slice(j * qs, (j + 1) * qs)
            g_last = jnp.exp(gend[j * qs:j * qs + 1, :])
            s_ref[j, h] = s_ref[j, h] * g_last + _dot_tn(kg[rows], v_new[rows])
        o = o * lax.rsqrt(jnp.mean(o * o, axis=-1, keepdims=True) + EPS) * nw_ref[...]
        outs.append(o * _silu(z[:, sl]))
    o_ref[...] = jnp.concatenate(outs, axis=1).astype(o_ref.dtype)


def _gdn(grp, proj, hist, s0, prm):
    width = GDN_G * GDN_D
    ng = GDN_HEADS // GDN_G
    ba_blk = (GDN_CONV_DIM + GDN_V) // LANES
    toks = [
        _Tok(proj, width, lambda g: g, hist, lambda g: g),
        _Tok(proj, width, lambda g: ng + g, hist, lambda g: ng + g),
        _Tok(proj, width, lambda g: 2 * ng + g, hist, lambda g: 2 * ng + g),
        _Tok(proj, width, lambda g: 3 * ng + g),
        _Tok(proj, LANES, lambda g: ba_blk),
    ]
    ops, specs = _mixer_specs(grp, toks)
    cw = prm["conv_w"]
    grouped = lambda x: x.reshape(ng, 1, GDN_G)
    params = [
        (cw, _param_spec(width, lambda g: g)(CONV_W)),
        (cw, _param_spec(width, lambda g: ng + g)(CONV_W)),
        (cw, _param_spec(width, lambda g: 2 * ng + g)(CONV_W)),
        (grouped(prm["A_log"]), pl.BlockSpec((1, 1, GDN_G), lambda b, g, c: (g, 0, 0))),
        (grouped(prm["dt_bias"]), pl.BlockSpec((1, 1, GDN_G), lambda b, g, c: (g, 0, 0))),
        (prm["norm_w"].reshape(1, GDN_D), pl.BlockSpec((1, GDN_D), lambda b, g, c: (0, 0))),
    ]
    ops += [p for p, _ in params]
    specs += [s for _, s in params]
    st_spec = pl.BlockSpec((grp.bb, GDN_G, GDN_D, GDN_D), lambda b, g, c: (b, g, 0, 0))
    ops.append(s0)
    specs.append(st_spec)
    rows = grp.B * grp.L
    nc = grp.nc
    return pl.pallas_call(
        functools.partial(_gdn_body, grp=grp),
        grid=(grp.B // grp.bb, ng, nc),
        in_specs=specs,
        out_specs=[pl.BlockSpec((MIX_ROWS, width), lambda b, g, c: (b * nc + c, g)), st_spec],
        out_shape=[jax.ShapeDtypeStruct((rows, GDN_V), BF16), jax.ShapeDtypeStruct(s0.shape, F32)],
        compiler_params=_cp("parallel", "parallel", "arbitrary"),
        name="gdn",
    )(*ops)


def _xattn_body(q_ref, k_ref, v_ref, o_ref, *, nseq, lq):
    scale = XA_HD ** -0.5
    for j in range(nseq):
        qrows = slice(j * lq, (j + 1) * lq)
        mrows = slice(j * N_MEM, (j + 1) * N_MEM)
        outs = []
        for h in range(XA_HEADS):
            sl = slice(h * XA_HD, (h + 1) * XA_HD)
            s = _dot_nt(q_ref[qrows, sl], k_ref[mrows, sl]) * scale
            s = s - jnp.max(s, axis=-1, keepdims=True)
            p = jnp.exp(s)
            p = p / jnp.sum(p, axis=-1, keepdims=True)
            outs.append(_dot(p, v_ref[mrows, sl]))
        o_ref[qrows, :] = jnp.concatenate(outs, axis=1).astype(o_ref.dtype)


def _xattn(grp, q, mem_k, mem_v, mem_row0):
    rows = grp.B * grp.L
    if grp.L >= XA_TQ:
        nseq, lq = 1, XA_TQ
        per = grp.L // XA_TQ
        mem_idx = lambda i: mem_row0 // N_MEM + i // per
    else:
        nseq, lq = MIX_ROWS // grp.L, grp.L
        mem_idx = lambda i: mem_row0 // (nseq * N_MEM) + i
    mem_spec = pl.BlockSpec((nseq * N_MEM, XA_DIM), lambda i: (mem_idx(i), 0))
    q_spec = pl.BlockSpec((nseq * lq, XA_DIM), lambda i: (i, 0))
    return pl.pallas_call(
        functools.partial(_xattn_body, nseq=nseq, lq=lq),
        grid=(rows // (nseq * lq),),
        in_specs=[q_spec, mem_spec, mem_spec],
        out_specs=q_spec,
        out_shape=jax.ShapeDtypeStruct((rows, XA_DIM), BF16),
        compiler_params=_cp("parallel"),
        name="xattn",
    )(q, mem_k, mem_v)


def _ffn_gate_body(*refs, grp, tiles_per_seq):
    if grp.embedded:
        gate_ref, up_ref, hist_ref, cw_ref, cb_ref, o_ref = refs
        halo = hist_ref[...]
    else:
        gate_ref, up_ref, prev_ref, hist_ref, cw_ref, cb_ref, o_ref = refs
        halo = jnp.where(pl.program_id(0) % tiles_per_seq == 0, hist_ref[...], prev_ref[...])
    gate = _conv(grp, gate_ref[...], halo, cw_ref, cb_ref[...], FFN_CONV_W)
    o_ref[...] = (_silu(gate) * up_ref[...]).astype(o_ref.dtype)


def _ffn_gate(grp, gu, hist, conv_w, conv_b):
    rows = grp.B * grp.L
    rt, ct = FFN_RT, FFN_CT
    nct = D_FF // ct
    tiles_per_seq = max(grp.L // rt, 1)
    ops = [gu, gu]
    specs = [pl.BlockSpec((rt, ct), lambda i, j: (i, j)), pl.BlockSpec((rt, ct), lambda i, j: (i, nct + j))]
    if grp.embedded:
        ops.append(hist)
        specs.append(pl.BlockSpec((rt, ct), lambda i, j: (i, j)))
    else:
        ops += [gu, hist]
        specs += [pl.BlockSpec((SUBLANES, ct), lambda i, j: (jnp.maximum(i * (rt // SUBLANES) - 1, 0), j)),
                  pl.BlockSpec((SUBLANES, ct), lambda i, j: (i // tiles_per_seq, j))]
    ops += [conv_w, conv_b.reshape(1, D_FF)]
    specs += [pl.BlockSpec((FFN_CONV_W, ct), lambda i, j: (0, j)), pl.BlockSpec((1, ct), lambda i, j: (0, j))]
    return pl.pallas_call(
        functools.partial(_ffn_gate_body, grp=grp, tiles_per_seq=tiles_per_seq),
        grid=(rows // rt, nct),
        in_specs=specs,
        out_specs=pl.BlockSpec((rt, ct), lambda i, j: (i, j)),
        out_shape=jax.ShapeDtypeStruct((rows, D_FF), BF16),
        compiler_params=_cp("parallel", "parallel"),
        name="ffn_gate",
    )(*ops)


def _history(grp, buf, width):
    b, k, c = buf.shape
    if grp.embedded:
        h = jnp.pad(buf, ((0, 0), (grp.first - k, grp.L - grp.first), (0, 0)))
        return h.reshape(b * grp.L, c)
    return jnp.pad(buf, ((0, 0), (SUBLANES - k, 0), (0, 0))).reshape(b * SUBLANES, c)


def _tail(grp, x2d, k, cols):
    x = x2d.reshape(grp.B, grp.L, x2d.shape[1])
    return x[:, grp.L - k:, cols]


def _cast_pad(w, mult):
    n = w.shape[1]
    return jnp.pad(w.astype(BF16), ((0, 0), (0, -n % mult)))


def _run_group(grp, x2d, mem_k, mem_v, mem_row0, st, wts, prm):
    h = x2d
    out = {}
    proj = _matmul(h, wts["in_ab"], gain=prm["norm_mix"][0])
    y_rw, s_rw = _rwkv(grp, proj, _history(grp, st["rw_shift"][:, None, :], RW_PROJ), st["rwkv"], prm["rw"])
    y_ssd, s_ssd = _ssd(grp, proj, _history(grp, st["ssd_conv"], SSD_CONV_DIM),
                        st["ssd"].reshape(grp.B, SSD_GROUPS, SSD_GDIM, SSD_STATE), prm["ssd"])
    out["rwkv"] = s_rw
    out["rw_shift"] = _tail(grp, proj, 1, slice(0, RW_PROJ))[:, 0]
    out["ssd"] = s_ssd.reshape(st["ssd"].shape)
    out["ssd_conv"] = _tail(grp, proj, CONV_W - 1, slice(RW_PROJ + SSD_DIM, RW_PROJ + SSD_DIM + SSD_CONV_DIM))
    h = _matmul(jnp.concatenate([y_rw, y_ssd], axis=1), wts["out_ab"], res=h)
    ffn_bufs = []
    for l in range(2):
        if l == 1:
            proj = _matmul(h, wts["in_c"], gain=prm["norm_mix"][1])
            y_c, s_gdn = _gdn(grp, proj, _history(grp, st["gdn_conv"], GDN_CONV_DIM), st["gdn"], prm["gdn"])
            out["gdn"] = s_gdn
            out["gdn_conv"] = _tail(grp, proj, CONV_W - 1, slice(0, GDN_CONV_DIM))
            h = _matmul(y_c, wts["out_c"], res=h)
        q = _matmul(h, wts["xq"][l], gain=prm["norm_xa"][l], out_dtype=BF16)
        o = _xattn(grp, q, mem_k, mem_v, mem_row0[l])
        h = _matmul(o, wts["xo"][l], res=h)
        gu = _matmul(h, wts["ffn_in"][l], gain=prm["norm_ffn"][l])
        ffn_bufs.append(_tail(grp, gu, FFN_CONV_W - 1, slice(0, D_FF)))
        act = _ffn_gate(grp, gu, _history(grp, st["ffn_conv"][l], D_FF), prm["ffn_conv_w"][l], prm["ffn_conv_b"][l])
        h = _matmul(act, wts["ffn_out"][l], res=h)
    out["ffn_conv"] = jnp.stack(ffn_bufs)
    out["y"] = _rmsnorm(h, prm["norm_final"])
    return out


def kernel(x_prompt, x_sample, mem_prompt, state_rwkv, state_rwkv_shift, state_ssd, state_ssd_conv, state_gdn,
           state_gdn_conv, state_ffn_conv, cache_mem_k, cache_mem_v, norm_mix, norm_xa, norm_mem, norm_ffn,
           norm_final, w_in_ab, rw_mu, rw_w0, rw_w_up, rw_a0, rw_a_up, rw_g_up, rw_k_k, rw_k_a, rw_r_k, rw_gn_w,
           rw_gn_b, ssd_conv_w, ssd_conv_b, ssd_dt_bias, ssd_A_log, ssd_D, ssd_norm_w, w_out_ab, w_in_c,
           gdn_conv_w, gdn_A_log, gdn_dt_bias, gdn_norm_w, w_out_c, w_xq, w_xk, w_xv, w_xo, ffn_w_in, ffn_conv_w,
           ffn_conv_b, ffn_w_out):
    bp, lp, _ = x_prompt.shape
    bs, ls, _ = x_sample.shape
    depth = norm_mix.shape[0]
    assert depth == 2 and w_in_ab.shape[0] == 1 and w_in_c.shape[0] == 1
    assert lp % MIX_ROWS == 0 and lp % XA_TQ == 0 and lp % FFN_RT == 0
    pad_rows = SUBLANES - ls
    assert CONV_W - 1 <= pad_rows and CONV_W - 1 <= ls and MIX_ROWS % SUBLANES == 0
    gp = Group(B=bp, L=lp, first=0, bb=1)
    gs = Group(B=bs, L=SUBLANES, first=pad_rows, bb=MIX_ROWS // SUBLANES)
    assert bs % gs.bb == 0 and (bs * SUBLANES) % MM_TM == 0 and (bp * lp) % MM_TM == 0

    wts = dict(
        in_ab=_cast_pad(w_in_ab[0], MM_TN), out_ab=w_out_ab[0].astype(BF16),
        in_c=_cast_pad(w_in_c[0], MM_TN), out_c=w_out_c[0].astype(BF16),
        xq=w_xq.astype(BF16), xo=w_xo.astype(BF16), ffn_in=ffn_w_in.astype(BF16), ffn_out=ffn_w_out.astype(BF16),
    )
    prm = dict(
        norm_mix=norm_mix, norm_xa=norm_xa, norm_ffn=norm_ffn, norm_final=norm_final,
        ffn_conv_w=ffn_conv_w, ffn_conv_b=ffn_conv_b,
        rw=dict(mu=rw_mu[0], w0=rw_w0[0], w_up=rw_w_up[0], a0=rw_a0[0], a_up=rw_a_up[0], g_up=rw_g_up[0],
                k_k=rw_k_k[0], k_a=rw_k_a[0], r_k=rw_r_k[0], gn_w=rw_gn_w[0], gn_b=rw_gn_b[0]),
        ssd=dict(conv_w=ssd_conv_w[0], conv_b=ssd_conv_b[0], dt_bias=ssd_dt_bias[0], A_log=ssd_A_log[0],
                 D=ssd_D[0], norm_w=ssd_norm_w[0]),
        gdn=dict(conv_w=gdn_conv_w[0], A_log=gdn_A_log[0], dt_bias=gdn_dt_bias[0], norm_w=gdn_norm_w[0]),
    )

    mem2d = mem_prompt.reshape(bp * N_MEM, D_MODEL)
    mem_tm = min(MM_TM, bp * N_MEM)
    mk, mv = [], []
    for l in range(depth):
        mk.append(_matmul(mem2d, w_xk[l].astype(BF16), gain=norm_mem[l], tm=mem_tm))
        mv.append(_matmul(mem2d, w_xv[l].astype(BF16), gain=norm_mem[l], tm=mem_tm))
    mem_k_p = jnp.stack(mk)
    mem_v_p = jnp.stack(mv)

    zeros = lambda *s: jnp.zeros(s, F32)
    st_p = dict(
        rwkv=zeros(bp, RW_HEADS, RW_HD, RW_HD), rw_shift=zeros(bp, RW_PROJ),
        ssd=zeros(bp, SSD_GROUPS, SSD_HPG, SSD_HD, SSD_STATE), ssd_conv=zeros(bp, CONV_W - 1, SSD_CONV_DIM),
        gdn=zeros(bp, GDN_HEADS, GDN_D, GDN_D), gdn_conv=zeros(bp, CONV_W - 1, GDN_CONV_DIM),
        ffn_conv=zeros(depth, bp, FFN_CONV_W - 1, D_FF),
    )
    rp = _run_group(gp, x_prompt.reshape(bp * lp, D_MODEL), mem_k_p.reshape(depth * bp * N_MEM, XA_DIM),
                    mem_v_p.reshape(depth * bp * N_MEM, XA_DIM), [l * bp * N_MEM for l in range(depth)],
                    st_p, wts, prm)

    st_s = dict(rwkv=state_rwkv[0], rw_shift=state_rwkv_shift[0], ssd=state_ssd[0], ssd_conv=state_ssd_conv[0],
                gdn=state_gdn[0], gdn_conv=state_gdn_conv[0], ffn_conv=state_ffn_conv)
    xs = jnp.pad(x_sample, ((0, 0), (pad_rows, 0), (0, 0))).reshape(bs * SUBLANES, D_MODEL)
    rs = _run_group(gs, xs, cache_mem_k.reshape(depth * bs * N_MEM, XA_DIM),
                    cache_mem_v.reshape(depth * bs * N_MEM, XA_DIM), [l * bs * N_MEM for l in range(depth)],
                    st_s, wts, prm)

    y_p = rp["y"].reshape(bp, lp, D_MODEL)
    y_s = rs["y"].reshape(bs, SUBLANES, D_MODEL)[:, pad_rows:]
    lead = lambda x: x[None]
    mem_shape = (depth, bp, N_MEM, XA_HEADS, XA_HD)
    return (y_p, y_s, lead(rp["rwkv"]), lead(rs["rwkv"]), lead(rp["rw_shift"]), lead(rs["rw_shift"]),
            lead(rp["ssd"]), lead(rs["ssd"]), lead(rp["ssd_conv"]), lead(rs["ssd_conv"]),
            lead(rp["gdn"]), lead(rs["gdn"]), lead(rp["gdn_conv"]), lead(rs["gdn_conv"]),
            rp["ffn_conv"], rs["ffn_conv"], mem_k_p.reshape(mem_shape), mem_v_p.reshape(mem_shape))
```

```python
import functools
from typing import NamedTuple

import jax
import jax.numpy as jnp
from jax import lax
from jax.experimental import pallas as pl
from jax.experimental.pallas import tpu as pltpu

F32 = jnp.float32
BF16 = jnp.bfloat16
HI = lax.Precision.HIGHEST

D_MODEL = 2048
EPS = 1e-6
RW_HEADS, RW_HD = 16, 64
RW_DIM = RW_HEADS * RW_HD
RW_LORA = 256
RW_PROJ = 3 * RW_DIM + RW_LORA
RW_GN_EPS = 6.4e-4
SSD_HEADS, SSD_HD, SSD_GROUPS, SSD_STATE = 16, 64, 2, 128
SSD_HPG = SSD_HEADS // SSD_GROUPS
SSD_DIM = SSD_HEADS * SSD_HD
SSD_GDIM = SSD_HPG * SSD_HD
SSD_CONV_DIM = SSD_DIM + 2 * SSD_GROUPS * SSD_STATE
SSD_PROJ = SSD_DIM + SSD_CONV_DIM + SSD_HEADS
AB_PROJ = RW_PROJ + SSD_PROJ
GDN_HEADS, GDN_D = 16, 128
GDN_V = GDN_HEADS * GDN_D
GDN_CONV_DIM = 3 * GDN_V
GDN_PROJ = GDN_CONV_DIM + GDN_V + 2 * GDN_HEADS
D_FF = 5632
XA_HEADS, XA_HD, N_MEM = 4, 128, 256
XA_DIM = XA_HEADS * XA_HD
CONV_W = 4
FFN_CONV_W = 3

SUBLANES = 8
LANES = 128
VMEM_LIMIT = 56 * 1024 * 1024

MIX_ROWS = 64
INV_BLOCK = 16
MM_TM, MM_TN = 1024, 512
RW_HP = 2
GDN_G = 4
FFN_RT, FFN_CT = 256, 512
XA_TQ = 512
NORM_TM = 512


class Group(NamedTuple):
    B: int
    L: int
    first: int
    bb: int

    @property
    def qs(self):
        return MIX_ROWS // self.bb

    @property
    def nc(self):
        return self.L // self.qs

    @property
    def embedded(self):
        return self.first > 0


def _cp(*sem):
    return pltpu.CompilerParams(dimension_semantics=sem, vmem_limit_bytes=VMEM_LIMIT)


def _dot(a, b):
    return jnp.dot(a.astype(BF16), b.astype(BF16), preferred_element_type=F32)


def _dot_nt(a, b):
    return lax.dot_general(a.astype(BF16), b.astype(BF16), (((1,), (1,)), ((), ())), preferred_element_type=F32)


def _dot_tn(a, b):
    return lax.dot_general(a.astype(BF16), b.astype(BF16), (((0,), (0,)), ((), ())), preferred_element_type=F32)


def _dot_hi(a, b):
    return jnp.dot(a, b, preferred_element_type=F32, precision=HI)


def _split_bf16(x):
    hi = x.astype(BF16)
    return hi, (x - hi.astype(F32)).astype(BF16)


def _dot_x3(a, b):
    ah, al = _split_bf16(a)
    bh, bl = _split_bf16(b)
    return jnp.dot(jnp.concatenate([ah, ah, al], axis=1), jnp.concatenate([bh, bl, bh], axis=0),
                   preferred_element_type=F32)


_dot_inv = _dot_x3


def _dot_nt_hi(a, b):
    return lax.dot_general(a, b, (((1,), (1,)), ((), ())), preferred_element_type=F32, precision=HI)


def _sigmoid(x):
    return 1.0 / (1.0 + jnp.exp(-x))


def _silu(x):
    return x * _sigmoid(x)


def _softplus(x):
    return jnp.maximum(x, 0.0) + jnp.log(1.0 + jnp.exp(-jnp.abs(x)))


def _iota2(shape, axis):
    return lax.broadcasted_iota(jnp.int32, shape, axis)


class _Masks(NamedTuple):
    incl: jax.Array
    strict: jax.Array
    eye: jax.Array
    cumsum: jax.Array
    valid: jax.Array


def _masks(grp, heads=1):
    r = MIX_ROWS
    n = heads * r
    ri, ci = _iota2((n, n), 0), _iota2((n, n), 1)
    same = (ri // grp.qs) == (ci // grp.qs)
    incl = same & (ci <= ri)
    strict = same & (ci < ri)
    si, sj = _iota2((2 * r, r), 0), _iota2((2 * r, r), 1)
    same_seq = ((si % r) // grp.qs) == (sj // grp.qs)
    cumsum = (same_seq & ((sj <= si) | (si >= r))).astype(F32)
    t = _iota2((r, 1), 0) % grp.qs
    return _Masks(incl, strict, (ri == ci).astype(F32), cumsum, t >= grp.first)


def _neumann(lm, rhs, nil, eye=None):
    n = lm.shape[0]
    p = None
    if rhs is None:
        t = eye + lm
        if nil > 2:
            p = _dot_inv(lm, lm)
    elif nil > 2:
        both = _dot_inv(lm, jnp.concatenate([lm, rhs], axis=1))
        p, t = both[:, :n], rhs + both[:, n:]
    else:
        t = rhs + _dot_inv(lm, rhs)
    k = 2
    while k < nil:
        if 2 * k < nil:
            both = _dot_inv(p, jnp.concatenate([p, t], axis=1))
            p, t = both[:, :n], t + both[:, n:]
        else:
            t = t + _dot_inv(p, t)
        k *= 2
    return t


def _inv_unit_lower(lm, eye, qs):
    if qs <= INV_BLOCK:
        return _neumann(lm, None, qs, eye)
    n = lm.shape[0]
    diag = (_iota2((n, n), 0) // INV_BLOCK) == (_iota2((n, n), 1) // INV_BLOCK)
    d = jnp.where(diag, lm, 0.0)
    dinv = _neumann(d, None, INV_BLOCK, eye)
    nm = _dot_inv(dinv, lm - d)
    return _neumann(nm, dinv, qs // INV_BLOCK)


def _stack_heads(x, heads, width):
    return jnp.concatenate([x[:, h * width:(h + 1) * width] for h in range(heads)], axis=0)


def _unstack_heads(x, heads):
    r = x.shape[0] // heads
    return jnp.concatenate([x[h * r:(h + 1) * r] for h in range(heads)], axis=1)


def _seq_rows(grp, x, j, heads):
    if grp.bb == 1:
        return x
    qs = grp.qs
    return jnp.concatenate([x[h * MIX_ROWS + j * qs:h * MIX_ROWS + (j + 1) * qs] for h in range(heads)], axis=0)


def _from_seq_rows(grp, parts, heads):
    if grp.bb == 1:
        return parts[0]
    qs = grp.qs
    return jnp.concatenate([p[h * qs:(h + 1) * qs] for h in range(heads) for p in parts], axis=0)


def _expand_rows(x, heads, rows_per_head, width):
    m = x.shape[0]
    if x.shape[1] == width:
        x = jnp.concatenate([x] * heads, axis=1)
    keep = ((_iota2((m, heads * width), 0) // rows_per_head) % heads) == (_iota2((m, heads * width), 1) // width)
    return jnp.where(keep, x, 0.0)


def _taps(grp, cur, halo, width):
    if grp.embedded:
        t = _iota2((cur.shape[0], 1), 0) % grp.qs
        full = jnp.where(t < grp.first, halo, cur)
        return [full] + [pltpu.roll(full, s, axis=0) for s in range(1, width)]
    full = jnp.concatenate([halo, cur], axis=0)
    return [cur] + [pltpu.roll(full, s, axis=0)[SUBLANES:] for s in range(1, width)]


def _conv(grp, cur, halo, w_ref, bias, width):
    taps = _taps(grp, cur, halo, width)
    y = taps[0] * w_ref[width - 1:width, :]
    for s in range(1, width):
        y = y + taps[s] * w_ref[width - 1 - s:width - s, :]
    return y if bias is None else y + bias


class _Tok(NamedTuple):
    arr: jax.Array
    width: int
    col: object
    hist: jax.Array = None
    hcol: object = None


def _mixer_specs(grp, toks):
    nc = grp.nc
    sub = MIX_ROWS // SUBLANES
    ops, specs = [], []
    for t in toks:
        ops.append(t.arr)
        specs.append(pl.BlockSpec((MIX_ROWS, t.width), lambda b, g, c, t=t: (b * nc + c, t.col(g))))
    for t in toks:
        if t.hist is None:
            continue
        if not grp.embedded:
            ops.append(t.arr)
            specs.append(pl.BlockSpec(
                (SUBLANES, t.width), lambda b, g, c, t=t: (jnp.maximum((b * nc + c) * sub - 1, 0), t.col(g))))
        ops.append(t.hist)
        hrows = MIX_ROWS if grp.embedded else SUBLANES
        specs.append(pl.BlockSpec((hrows, t.width), lambda b, g, c, t=t: (b, t.hcol(g))))
    return ops, specs


def _read_windows(grp, refs, n_plain, n_hist):
    c = pl.program_id(2)
    cur = [r[...] for r in refs[:n_plain]]
    rest = refs[n_plain:]
    halos = []
    for i in range(n_hist):
        if grp.embedded:
            halos.append(rest[i][...])
        else:
            halos.append(jnp.where(c == 0, rest[2 * i + 1][...], rest[2 * i][...]))
    used = n_hist if grp.embedded else 2 * n_hist
    return cur, halos, rest[used:]


def _param_spec(width, col):
    return lambda rows: pl.BlockSpec((rows, width), lambda b, g, c: (0, col(g)))


def _matmul_body(*refs, norm, has_res):
    it = iter(refs)
    a_ref = next(it)
    g_ref = next(it) if norm else None
    w_ref = next(it)
    res_ref = next(it) if has_res else None
    o_ref = next(it)
    if norm:
        an_ref = next(it)

        @pl.when(pl.program_id(1) == 0)
        def _():
            x = a_ref[...]
            y = x * lax.rsqrt(jnp.mean(x * x, axis=-1, keepdims=True) + EPS)
            an_ref[...] = (y * g_ref[...]).astype(BF16)

        a = an_ref[...]
    else:
        a = a_ref[...]
    acc = jnp.dot(a, w_ref[...], preferred_element_type=F32)
    if has_res:
        acc = acc + res_ref[...]
    o_ref[...] = acc.astype(o_ref.dtype)


def _matmul(a, w, *, gain=None, res=None, out_dtype=F32, tm=MM_TM, tn=MM_TN):
    m, k = a.shape
    n = w.shape[1]
    norm = gain is not None
    ops = [a]
    specs = [pl.BlockSpec((tm, k), lambda i, j: (i, 0))]
    if norm:
        ops.append(gain.reshape(1, k))
        specs.append(pl.BlockSpec((1, k), lambda i, j: (0, 0)))
    ops.append(w)
    specs.append(pl.BlockSpec((k, tn), lambda i, j: (0, j)))
    if res is not None:
        ops.append(res)
        specs.append(pl.BlockSpec((tm, tn), lambda i, j: (i, j)))
    return pl.pallas_call(
        functools.partial(_matmul_body, norm=norm, has_res=res is not None),
        grid=(m // tm, n // tn),
        in_specs=specs,
        out_specs=pl.BlockSpec((tm, tn), lambda i, j: (i, j)),
        out_shape=jax.ShapeDtypeStruct((m, n), out_dtype),
        scratch_shapes=[pltpu.VMEM((tm, k), BF16)] if norm else [],
        compiler_params=_cp("parallel", "arbitrary"),
        name="matmul",
    )(*ops)


def _rmsnorm_body(x_ref, g_ref, o_ref):
    x = x_ref[...]
    y = x * lax.rsqrt(jnp.mean(x * x, axis=-1, keepdims=True) + EPS)
    o_ref[...] = (y * g_ref[...]).astype(o_ref.dtype)


def _rmsnorm(x, gain, out_dtype=F32, tm=NORM_TM):
    m, k = x.shape
    return pl.pallas_call(
        _rmsnorm_body,
        grid=(m // tm,),
        in_specs=[pl.BlockSpec((tm, k), lambda i: (i, 0)), pl.BlockSpec((1, k), lambda i: (0, 0))],
        out_specs=pl.BlockSpec((tm, k), lambda i: (i, 0)),
        out_shape=jax.ShapeDtypeStruct((m, k), out_dtype),
        compiler_params=_cp("parallel"),
        name="rmsnorm",
    )(x, gain.reshape(1, k))


def _rwkv_body(*refs, grp):
    r_rows = MIX_ROWS
    qs, bb, nc = grp.qs, grp.bb, grp.nc
    nh = 2 * RW_HP
    width = nh * RW_HD
    n = nh * r_rows
    cur, halos, rest = _read_windows(grp, refs, 4, 4)
    (mu_r, mu_k, mu_v, mu_lo, w0_ref, wup_ref, a0_ref, aup_ref, gup_ref, kk_ref, ka_ref, rk_ref, gnw_ref, gnb_ref,
     s0_ref, y_ref, s_ref, sbig_ref) = rest
    c = pl.program_id(2)

    @pl.when(c == 0)
    def _():
        for j in range(bb):
            st = jnp.concatenate([s0_ref[j, h] for h in range(nh)], axis=0)
            sbig_ref[j] = _expand_rows(st, nh, RW_HD, RW_HD)

    m = _masks(grp, nh)

    def shifted(i, mu_ref):
        x = cur[i]
        prev = _taps(grp, x, halos[i], 2)[1]
        return x + (prev - x) * mu_ref[...]

    r = shifted(0, mu_r)
    k = shifted(1, mu_k)
    v = shifted(2, mu_v)
    lo = shifted(3, mu_lo)
    wd, ad, gd = lo[:, :64], lo[:, 64:128], lo[:, 128:256]
    w = -_softplus(-(w0_ref[...] + _dot_hi(jnp.tanh(wd), wup_ref[...]))) - 0.5
    lw = jnp.where(m.valid, -jnp.exp(w), 0.0)
    a = _sigmoid(a0_ref[...] + _dot_hi(ad, aup_ref[...]))
    g = _dot_hi(_sigmoid(gd), gup_ref[...])

    bd = ((_iota2((width, width), 0) // RW_HD) == (_iota2((width, width), 1) // RW_HD)).astype(F32)
    kx = k * kk_ref[...]
    kp = jnp.where(m.valid, k * (1.0 + (a - 1.0) * ka_ref[...]), 0.0)
    sums = _dot_hi(jnp.concatenate([kx * kx, r * kp * rk_ref[...]], axis=0), bd)
    kkn = kx * lax.rsqrt(sums[:r_rows] + 1e-6)
    bonus = sums[r_rows:] * v
    at = jnp.where(m.valid, -kkn, 0.0)
    bt = jnp.where(m.valid, kkn * a, 0.0)

    cums = _dot_hi(m.cumsum, lw)
    cum, cum_end = cums[:r_rows], cums[r_rows:]
    e_neg = jnp.exp(-cum)
    e_end = jnp.exp(cum_end - cum)
    p_end = jnp.exp(cum_end)

    def tile(x):
        return _expand_rows(jnp.concatenate([x] * nh, axis=0), nh, r_rows, RW_HD)

    ar = jnp.concatenate([tile(at * jnp.exp(cum - lw)), tile(r * jnp.exp(cum))], axis=0)
    bk = jnp.concatenate([tile(bt * e_neg), tile(kp * e_neg)], axis=0)
    bk_end = jnp.concatenate([tile(bt * e_end), tile(kp * e_end)], axis=0)
    v_exp = tile(v)
    gm = _dot_nt(ar, bk)
    m_ab = jnp.where(m.strict, gm[:n, :n], 0.0)
    m_ak = jnp.where(m.strict, gm[:n, n:], 0.0)
    m_r = jnp.concatenate([jnp.where(m.incl, gm[n:, :n], 0.0), jnp.where(m.incl, gm[n:, n:], 0.0)], axis=1)

    def seq2(x, j):
        if bb == 1:
            return x
        return jnp.concatenate([_seq_rows(grp, x[:n], j, nh), _seq_rows(grp, x[n:], j, nh)], axis=0)

    parts = [_dot_nt(seq2(ar, j), sbig_ref[j]) for j in range(bb)]
    half = nh * qs
    as_a = _from_seq_rows(grp, [p[:half] for p in parts], nh)
    as_r = _from_seq_rows(grp, [p[half:] for p in parts], nh)
    rhs = as_a + _dot(m_ak, v_exp)
    u = _dot_inv(_inv_unit_lower(m_ab, m.eye, qs), rhs)
    uv = jnp.concatenate([u, v_exp], axis=0)
    y_exp = as_r + _dot(m_r, uv)
    y = y_exp[:r_rows]
    for h in range(1, nh):
        y = y + y_exp[h * r_rows:(h + 1) * r_rows]
    for j in range(bb):
        sbig_ref[j] = sbig_ref[j] * p_end[j * qs:j * qs + 1, :] + _dot_tn(seq2(uv, j), seq2(bk_end, j))

    @pl.when(c == nc - 1)
    def _():
        for j in range(bb):
            sb = sbig_ref[j]
            for h in range(nh):
                s_ref[j, h] = sb[h * RW_HD:(h + 1) * RW_HD, h * RW_HD:(h + 1) * RW_HD]

    mean = _dot_hi(y, bd) * (1.0 / RW_HD)
    d = y - mean
    var = _dot_hi(d * d, bd) * (1.0 / RW_HD)
    out = d * lax.rsqrt(var + RW_GN_EPS) * gnw_ref[...] + gnb_ref[...] + bonus
    y_ref[...] = (out * g).astype(y_ref.dtype)


def _rwkv(grp, proj, hist, s0, prm):
    width = RW_HP * LANES
    ng = RW_DIM // width
    lora_blk = 3 * RW_DIM // RW_LORA
    toks = [
        _Tok(proj, width, lambda g: g, hist, lambda g: g),
        _Tok(proj, width, lambda g: ng + g, hist, lambda g: ng + g),
        _Tok(proj, width, lambda g: 2 * ng + g, hist, lambda g: 2 * ng + g),
        _Tok(proj, RW_LORA, lambda g: lora_blk, hist, lambda g: lora_blk),
    ]
    ops, specs = _mixer_specs(grp, toks)
    head = _param_spec(width, lambda g: g)
    mu = prm["mu"].reshape(1, RW_PROJ)
    flat = lambda x: x.reshape(1, RW_DIM)
    params = [
        (mu, pl.BlockSpec((1, width), lambda b, g, c: (0, g))),
        (mu, pl.BlockSpec((1, width), lambda b, g, c: (0, ng + g))),
        (mu, pl.BlockSpec((1, width), lambda b, g, c: (0, 2 * ng + g))),
        (mu, pl.BlockSpec((1, RW_LORA), lambda b, g, c: (0, lora_blk))),
        (flat(prm["w0"]), head(1)), (prm["w_up"], head(64)),
        (flat(prm["a0"]), head(1)), (prm["a_up"], head(64)), (prm["g_up"], head(128)),
        (flat(prm["k_k"]), head(1)), (flat(prm["k_a"]), head(1)), (flat(prm["r_k"]), head(1)),
        (flat(prm["gn_w"]), head(1)), (flat(prm["gn_b"]), head(1)),
    ]
    ops += [p for p, _ in params]
    specs += [s for _, s in params]
    st_spec = pl.BlockSpec((grp.bb, 2 * RW_HP, RW_HD, RW_HD), lambda b, g, c: (b, g, 0, 0))
    ops.append(s0)
    specs.append(st_spec)
    rows = grp.B * grp.L
    nc = grp.nc
    return pl.pallas_call(
        functools.partial(_rwkv_body, grp=grp),
        grid=(grp.B // grp.bb, ng, nc),
        in_specs=specs,
        out_specs=[pl.BlockSpec((MIX_ROWS, width), lambda b, g, c: (b * nc + c, g)), st_spec],
        out_shape=[jax.ShapeDtypeStruct((rows, RW_DIM), BF16), jax.ShapeDtypeStruct(s0.shape, F32)],
        scratch_shapes=[pltpu.VMEM((grp.bb, width, width), F32)],
        compiler_params=_cp("parallel", "parallel", "arbitrary"),
        name="rwkv7",
    )(*ops)


def _ssd_body(*refs, grp):
    qs, bb = grp.qs, grp.bb
    cur, halos, rest = _read_windows(grp, refs, 7, 4)
    (cw_x0, cw_x1, cw_b, cw_c, cb_x0, cb_x1, cb_b, cb_c, dtb_ref, alog_ref, dskip_ref, nw_ref,
     h0_ref, y_ref, h_ref) = rest
    x0, x1, b_raw, c_raw, z0, z1, dt_blk = cur
    c = pl.program_id(2)
    grp_idx = pl.program_id(1)

    @pl.when(c == 0)
    def _():
        h_ref[...] = h0_ref[...]

    m = _masks(grp)
    xs = jnp.concatenate([
        _silu(_conv(grp, x0, halos[0], cw_x0, cb_x0[...], CONV_W)),
        _silu(_conv(grp, x1, halos[1], cw_x1, cb_x1[...], CONV_W))], axis=1)
    bm = _silu(_conv(grp, b_raw, halos[2], cw_b, cb_b[...], CONV_W))
    cm = _silu(_conv(grp, c_raw, halos[3], cw_c, cb_c[...], CONV_W))
    z = jnp.concatenate([z0, z1], axis=1)

    sel = (_iota2((SSD_HEADS, SSD_HPG), 0) == grp_idx * SSD_HPG + _iota2((SSD_HEADS, SSD_HPG), 1)).astype(F32)
    dt_all = _softplus(dt_blk[:, :SSD_HEADS] + dtb_ref[...])
    dt = jnp.where(m.valid, _dot_hi(dt_all, sel), 0.0)
    da = dt * (-jnp.exp(alog_ref[0]))
    acs_both = _dot_hi(m.cumsum, da)
    acs, acs_end = acs_both[:MIX_ROWS], acs_both[MIX_ROWS:]
    eye8 = (_iota2((SSD_HPG, SSD_HPG), 0) == _iota2((SSD_HPG, SSD_HPG), 1)).astype(F32)
    acs_row = _dot_nt_hi(eye8, acs)
    expand = (_iota2((SSD_HPG, SSD_GDIM), 1) // SSD_HD == _iota2((SSD_HPG, SSD_GDIM), 0)).astype(F32)
    expand_t = (_iota2((SSD_GDIM, SSD_HPG), 0) // SSD_HD == _iota2((SSD_GDIM, SSD_HPG), 1)).astype(F32)
    xd = xs * _dot_hi(dt, expand)
    xd_dec = xd * jnp.exp(_dot_hi(acs_end - acs, expand))
    cb = _dot_nt(cm, bm)
    yd = []
    for r in range(SSD_HPG):
        diff = acs[:, r:r + 1] - acs_row[r:r + 1, :]
        lmat = jnp.where(m.incl, jnp.exp(jnp.where(m.incl, diff, 0.0)), 0.0)
        yd.append(_dot(cb * lmat, xd[:, r * SSD_HD:(r + 1) * SSD_HD]))
    y = jnp.concatenate(yd, axis=1)
    yoff = []
    for j in range(bb):
        rows = slice(j * qs, (j + 1) * qs)
        hj = h_ref[j, 0]
        yoff.append(_dot_nt(cm[rows], hj))
        end_col = jnp.exp(_dot_nt_hi(expand_t, acs_end[j * qs:j * qs + SUBLANES]))[:, :1]
        h_ref[j, 0] = hj * end_col + _dot_tn(xd_dec[rows], bm[rows])
    yoff = yoff[0] if bb == 1 else jnp.concatenate(yoff, axis=0)
    y = y + yoff * jnp.exp(_dot_hi(acs, expand)) + xs * dskip_ref[...]
    yg = y * _silu(z)
    yg = yg * lax.rsqrt(jnp.mean(yg * yg, axis=-1, keepdims=True) + EPS)
    y_ref[...] = (yg * nw_ref[...]).astype(y_ref.dtype)


def _ssd(grp, proj, hist, h0, prm):
    half = SSD_GDIM // 2
    z_blk = RW_PROJ // half
    x_blk = (RW_PROJ + SSD_DIM) // half
    b_blk = (RW_PROJ + 2 * SSD_DIM) // SSD_STATE
    c_blk = b_blk + SSD_GROUPS
    dt_blk = (RW_PROJ + SSD_DIM + SSD_CONV_DIM) // LANES
    hb_blk = SSD_DIM // SSD_STATE
    toks = [
        _Tok(proj, half, lambda g: x_blk + 2 * g, hist, lambda g: 2 * g),
        _Tok(proj, half, lambda g: x_blk + 2 * g + 1, hist, lambda g: 2 * g + 1),
        _Tok(proj, SSD_STATE, lambda g: b_blk + g, hist, lambda g: hb_blk + g),
        _Tok(proj, SSD_STATE, lambda g: c_blk + g, hist, lambda g: hb_blk + SSD_GROUPS + g),
        _Tok(proj, half, lambda g: z_blk + 2 * g),
        _Tok(proj, half, lambda g: z_blk + 2 * g + 1),
        _Tok(proj, LANES, lambda g: dt_blk),
    ]
    ops, specs = _mixer_specs(grp, toks)
    cw, cbias = prm["conv_w"], prm["conv_b"].reshape(1, SSD_CONV_DIM)
    conv_cols = [(half, lambda g: 2 * g), (half, lambda g: 2 * g + 1),
                 (SSD_STATE, lambda g: hb_blk + g), (SSD_STATE, lambda g: hb_blk + SSD_GROUPS + g)]
    params = [(cw, _param_spec(wd, col)(CONV_W)) for wd, col in conv_cols]
    params += [(cbias, _param_spec(wd, col)(1)) for wd, col in conv_cols]
    grouped = lambda x: x.reshape(SSD_GROUPS, 1, SSD_HPG)
    params += [
        (prm["dt_bias"].reshape(1, SSD_HEADS), pl.BlockSpec((1, SSD_HEADS), lambda b, g, c: (0, 0))),
        (grouped(prm["A_log"]), pl.BlockSpec((1, 1, SSD_HPG), lambda b, g, c: (g, 0, 0))),
        (jnp.repeat(prm["D"], SSD_HD).reshape(1, SSD_DIM), _param_spec(SSD_GDIM, lambda g: g)(1)),
        (prm["norm_w"].reshape(1, SSD_DIM), _param_spec(SSD_GDIM, lambda g: g)(1)),
    ]
    ops += [p for p, _ in params]
    specs += [s for _, s in params]
    st_spec = pl.BlockSpec((grp.bb, 1, SSD_GDIM, SSD_STATE), lambda b, g, c: (b, g, 0, 0))
    ops.append(h0)
    specs.append(st_spec)
    rows = grp.B * grp.L
    nc = grp.nc
    return pl.pallas_call(
        functools.partial(_ssd_body, grp=grp),
        grid=(grp.B // grp.bb, SSD_GROUPS, nc),
        in_specs=specs,
        out_specs=[pl.BlockSpec((MIX_ROWS, SSD_GDIM), lambda b, g, c: (b * nc + c, g)), st_spec],
        out_shape=[jax.ShapeDtypeStruct((rows, SSD_DIM), BF16), jax.ShapeDtypeStruct(h0.shape, F32)],
        compiler_params=_cp("parallel", "parallel", "arbitrary"),
        name="ssd",
    )(*ops)


def _gdn_body(*refs, grp):
    r_rows = MIX_ROWS
    qs, bb = grp.qs, grp.bb
    nh = GDN_G
    n = nh * r_rows
    cur, halos, rest = _read_windows(grp, refs, 5, 3)
    cw_q, cw_k, cw_v, alog_ref, dtb_ref, nw_ref, s0_ref, o_ref, s_ref = rest
    q_raw, k_raw, v_raw, z, ba = cur
    c = pl.program_id(2)
    grp_idx = pl.program_id(1)

    @pl.when(c == 0)
    def _():
        s_ref[...] = s0_ref[...]

    m = _masks(grp, nh)
    q = _stack_heads(_silu(_conv(grp, q_raw, halos[0], cw_q, None, CONV_W)), nh, GDN_D)
    k = _stack_heads(_silu(_conv(grp, k_raw, halos[1], cw_k, None, CONV_W)), nh, GDN_D)
    v = _stack_heads(_silu(_conv(grp, v_raw, halos[2], cw_v, None, CONV_W)), nh, GDN_D)
    q = q * lax.rsqrt(jnp.sum(q * q, axis=-1, keepdims=True) + 1e-6) * (GDN_D ** -0.5)
    k = k * lax.rsqrt(jnp.sum(k * k, axis=-1, keepdims=True) + 1e-6)

    si, sj = _iota2((2 * GDN_HEADS, 2 * nh), 0), _iota2((2 * GDN_HEADS, 2 * nh), 1)
    sel = (si == (sj // nh) * GDN_HEADS + grp_idx * nh + sj % nh).astype(F32)
    bag = _dot_hi(ba[:, :2 * GDN_HEADS], sel)
    beta = jnp.where(m.valid, _sigmoid(bag[:, :nh]), 0.0)
    gg = jnp.where(m.valid, -jnp.exp(alog_ref[0]) * _softplus(bag[:, nh:] + dtb_ref[0]), 0.0)
    gcs = _dot_hi(m.cumsum, gg)
    gc, gc_end = gcs[:r_rows], gcs[r_rows:]
    eye_g = (_iota2((SUBLANES, nh), 0) == _iota2((SUBLANES, nh), 1)).astype(F32)
    gc_row = _dot_nt_hi(eye_g, gc)

    col = lambda x: jnp.concatenate([x[:, h:h + 1] for h in range(nh)], axis=0)
    bcol, gcol, gend = col(beta), col(gc), col(gc_end)
    grow = jnp.concatenate([gc_row[h:h + 1, :] for h in range(nh)], axis=1)
    dec = jnp.where(m.incl, jnp.exp(jnp.where(m.incl, gcol - grow, 0.0)), 0.0)
    kb = k * bcol
    kq = _dot_nt(jnp.concatenate([kb, q], axis=0), k)
    lm = -jnp.where(m.strict, kq[:n] * dec, 0.0)
    attn = jnp.where(m.incl, kq[n:] * dec, 0.0)
    eg = jnp.exp(gcol)
    sol = _dot_inv(_inv_unit_lower(lm, m.eye, qs), jnp.concatenate([v * bcol, kb * eg], axis=1))
    vw, kcd = sol[:, :GDN_D], sol[:, GDN_D:]
    qg = q * eg
    kg = k * jnp.exp(gend - gcol)

    def state(j):
        return jnp.concatenate([s_ref[j, h] for h in range(nh)], axis=0)

    parts = []
    for j in range(bb):
        lhs = jnp.concatenate([_seq_rows(grp, kcd, j, nh), _seq_rows(grp, qg, j, nh)], axis=0)
        parts.append(_dot(_expand_rows(lhs, nh, qs, GDN_D), state(j)))
    half = nh * qs
    ks = _from_seq_rows(grp, [p[:half] for p in parts], nh)
    qsv = _from_seq_rows(grp, [p[half:] for p in parts], nh)
    v_new = vw - ks
    o = qsv + _dot(attn, v_new)
    for j in range(bb):
        kgj = _expand_rows(_seq_rows(grp, kg, j, nh), nh, qs, GDN_D)
        g_last = jnp.exp(gc_end[j * qs:j * qs + 1, :])
        g_last = jnp.concatenate([jnp.broadcast_to(g_last[:, h:h + 1], (GDN_D, 1)) for h in range(nh)], axis=0)
        s_new = state(j) * g_last + _dot_tn(kgj, _seq_rows(grp, v_new, j, nh))
        for h in range(nh):
            s_ref[j, h] = s_new[h * GDN_D:(h + 1) * GDN_D]
    o = o * lax.rsqrt(jnp.mean(o * o, axis=-1, keepdims=True) + EPS) * nw_ref[...]
    o_ref[...] = (_unstack_heads(o, nh) * _silu(z)).astype(o_ref.dtype)


def _gdn(grp, proj, hist, s0, prm):
    width = GDN_G * GDN_D
    ng = GDN_HEADS // GDN_G
    ba_blk = (GDN_CONV_DIM + GDN_V) // LANES
    toks = [
        _Tok(proj, width, lambda g: g, hist, lambda g: g),
        _Tok(proj, width, lambda g: ng + g, hist, lambda g: ng + g),
        _Tok(proj, width, lambda g: 2 * ng + g, hist, lambda g: 2 * ng + g),
        _Tok(proj, width, lambda g: 3 * ng + g),
        _Tok(proj, LANES, lambda g: ba_blk),
    ]
    ops, specs = _mixer_specs(grp, toks)
    cw = prm["conv_w"]
    grouped = lambda x: x.reshape(ng, 1, GDN_G)
    params = [
        (cw, _param_spec(width, lambda g: g)(CONV_W)),
        (cw, _param_spec(width, lambda g: ng + g)(CONV_W)),
        (cw, _param_spec(width, lambda g: 2 * ng + g)(CONV_W)),
        (grouped(prm["A_log"]), pl.BlockSpec((1, 1, GDN_G), lambda b, g, c: (g, 0, 0))),
        (grouped(prm["dt_bias"]), pl.BlockSpec((1, 1, GDN_G), lambda b, g, c: (g, 0, 0))),
        (prm["norm_w"].reshape(1, GDN_D), pl.BlockSpec((1, GDN_D), lambda b, g, c: (0, 0))),
    ]
    ops += [p for p, _ in params]
    specs += [s for _, s in params]
    st_spec = pl.BlockSpec((grp.bb, GDN_G, GDN_D, GDN_D), lambda b, g, c: (b, g, 0, 0))
    ops.append(s0)
    specs.append(st_spec)
    rows = grp.B * grp.L
    nc = grp.nc
    return pl.pallas_call(
        functools.partial(_gdn_body, grp=grp),
        grid=(grp.B // grp.bb, ng, nc),
        in_specs=specs,
        out_specs=[pl.BlockSpec((MIX_ROWS, width), lambda b, g, c: (b * nc + c, g)), st_spec],
        out_shape=[jax.ShapeDtypeStruct((rows, GDN_V), BF16), jax.ShapeDtypeStruct(s0.shape, F32)],
        compiler_params=_cp("parallel", "parallel", "arbitrary"),
        name="gdn",
    )(*ops)


def _xattn_body(q_ref, k_ref, v_ref, o_ref, *, nseq, lq):
    scale = XA_HD ** -0.5
    for j in range(nseq):
        qrows = slice(j * lq, (j + 1) * lq)
        mrows = slice(j * N_MEM, (j + 1) * N_MEM)
        outs = []
        for h in range(XA_HEADS):
            sl = slice(h * XA_HD, (h + 1) * XA_HD)
            s = _dot_nt(q_ref[qrows, sl], k_ref[mrows, sl]) * scale
            s = s - jnp.max(s, axis=-1, keepdims=True)
            p = jnp.exp(s)
            p = p / jnp.sum(p, axis=-1, keepdims=True)
            outs.append(_dot(p, v_ref[mrows, sl]))
        o_ref[qrows, :] = jnp.concatenate(outs, axis=1).astype(o_ref.dtype)


def _xattn(grp, q, mem_k, mem_v, mem_row0):
    rows = grp.B * grp.L
    if grp.L >= XA_TQ:
        nseq, lq = 1, XA_TQ
        per = grp.L // XA_TQ
        mem_idx = lambda i: mem_row0 // N_MEM + i // per
    else:
        nseq, lq = MIX_ROWS // grp.L, grp.L
        mem_idx = lambda i: mem_row0 // (nseq * N_MEM) + i
    mem_spec = pl.BlockSpec((nseq * N_MEM, XA_DIM), lambda i: (mem_idx(i), 0))
    q_spec = pl.BlockSpec((nseq * lq, XA_DIM), lambda i: (i, 0))
    return pl.pallas_call(
        functools.partial(_xattn_body, nseq=nseq, lq=lq),
        grid=(rows // (nseq * lq),),
        in_specs=[q_spec, mem_spec, mem_spec],
        out_specs=q_spec,
        out_shape=jax.ShapeDtypeStruct((rows, XA_DIM), BF16),
        compiler_params=_cp("parallel"),
        name="xattn",
    )(q, mem_k, mem_v)


def _ffn_gate_body(*refs, grp, tiles_per_seq):
    if grp.embedded:
        gate_ref, up_ref, hist_ref, cw_ref, cb_ref, o_ref = refs
        halo = hist_ref[...]
    else:
        gate_ref, up_ref, prev_ref, hist_ref, cw_ref, cb_ref, o_ref = refs
        halo = jnp.where(pl.program_id(0) % tiles_per_seq == 0, hist_ref[...], prev_ref[...])
    gate = _conv(grp, gate_ref[...], halo, cw_ref, cb_ref[...], FFN_CONV_W)
    o_ref[...] = (_silu(gate) * up_ref[...]).astype(o_ref.dtype)


def _ffn_gate(grp, gu, hist, conv_w, conv_b):
    rows = grp.B * grp.L
    rt, ct = FFN_RT, FFN_CT
    nct = D_FF // ct
    tiles_per_seq = max(grp.L // rt, 1)
    ops = [gu, gu]
    specs = [pl.BlockSpec((rt, ct), lambda i, j: (i, j)), pl.BlockSpec((rt, ct), lambda i, j: (i, nct + j))]
    if grp.embedded:
        ops.append(hist)
        specs.append(pl.BlockSpec((rt, ct), lambda i, j: (i, j)))
    else:
        ops += [gu, hist]
        specs += [pl.BlockSpec((SUBLANES, ct), lambda i, j: (jnp.maximum(i * (rt // SUBLANES) - 1, 0), j)),
                  pl.BlockSpec((SUBLANES, ct), lambda i, j: (i // tiles_per_seq, j))]
    ops += [conv_w, conv_b.reshape(1, D_FF)]
    specs += [pl.BlockSpec((FFN_CONV_W, ct), lambda i, j: (0, j)), pl.BlockSpec((1, ct), lambda i, j: (0, j))]
    return pl.pallas_call(
        functools.partial(_ffn_gate_body, grp=grp, tiles_per_seq=tiles_per_seq),
        grid=(rows // rt, nct),
        in_specs=specs,
        out_specs=pl.BlockSpec((rt, ct), lambda i, j: (i, j)),
        out_shape=jax.ShapeDtypeStruct((rows, D_FF), BF16),
        compiler_params=_cp("parallel", "parallel"),
        name="ffn_gate",
    )(*ops)


def _history(grp, buf, width):
    b, k, c = buf.shape
    if grp.embedded:
        h = jnp.pad(buf, ((0, 0), (grp.first - k, grp.L - grp.first), (0, 0)))
        return h.reshape(b * grp.L, c)
    return jnp.pad(buf, ((0, 0), (SUBLANES - k, 0), (0, 0))).reshape(b * SUBLANES, c)


def _tail(grp, x2d, k, cols):
    x = x2d.reshape(grp.B, grp.L, x2d.shape[1])
    return x[:, grp.L - k:, cols]


def _cast_pad(w, mult):
    n = w.shape[1]
    return jnp.pad(w.astype(BF16), ((0, 0), (0, -n % mult)))


def _run_group(grp, x2d, mem_k, mem_v, mem_row0, st, wts, prm):
    h = x2d
    out = {}
    proj = _matmul(h, wts["in_ab"], gain=prm["norm_mix"][0])
    y_rw, s_rw = _rwkv(grp, proj, _history(grp, st["rw_shift"][:, None, :], RW_PROJ), st["rwkv"], prm["rw"])
    y_ssd, s_ssd = _ssd(grp, proj, _history(grp, st["ssd_conv"], SSD_CONV_DIM),
                        st["ssd"].reshape(grp.B, SSD_GROUPS, SSD_GDIM, SSD_STATE), prm["ssd"])
    out["rwkv"] = s_rw
    out["rw_shift"] = _tail(grp, proj, 1, slice(0, RW_PROJ))[:, 0]
    out["ssd"] = s_ssd.reshape(st["ssd"].shape)
    out["ssd_conv"] = _tail(grp, proj, CONV_W - 1, slice(RW_PROJ + SSD_DIM, RW_PROJ + SSD_DIM + SSD_CONV_DIM))
    h = _matmul(jnp.concatenate([y_rw, y_ssd], axis=1), wts["out_ab"], res=h)
    ffn_bufs = []
    for l in range(2):
        if l == 1:
            proj = _matmul(h, wts["in_c"], gain=prm["norm_mix"][1])
            y_c, s_gdn = _gdn(grp, proj, _history(grp, st["gdn_conv"], GDN_CONV_DIM), st["gdn"], prm["gdn"])
            out["gdn"] = s_gdn
            out["gdn_conv"] = _tail(grp, proj, CONV_W - 1, slice(0, GDN_CONV_DIM))
            h = _matmul(y_c, wts["out_c"], res=h)
        q = _matmul(h, wts["xq"][l], gain=prm["norm_xa"][l], out_dtype=BF16)
        o = _xattn(grp, q, mem_k, mem_v, mem_row0[l])
        h = _matmul(o, wts["xo"][l], res=h)
        gu = _matmul(h, wts["ffn_in"][l], gain=prm["norm_ffn"][l])
        ffn_bufs.append(_tail(grp, gu, FFN_CONV_W - 1, slice(0, D_FF)))
        act = _ffn_gate(grp, gu, _history(grp, st["ffn_conv"][l], D_FF), prm["ffn_conv_w"][l], prm["ffn_conv_b"][l])
        h = _matmul(act, wts["ffn_out"][l], res=h)
    out["ffn_conv"] = jnp.stack(ffn_bufs)
    out["y"] = _rmsnorm(h, prm["norm_final"])
    return out


def kernel(x_prompt, x_sample, mem_prompt, state_rwkv, state_rwkv_shift, state_ssd, state_ssd_conv, state_gdn,
           state_gdn_conv, state_ffn_conv, cache_mem_k, cache_mem_v, norm_mix, norm_xa, norm_mem, norm_ffn,
           norm_final, w_in_ab, rw_mu, rw_w0, rw_w_up, rw_a0, rw_a_up, rw_g_up, rw_k_k, rw_k_a, rw_r_k, rw_gn_w,
           rw_gn_b, ssd_conv_w, ssd_conv_b, ssd_dt_bias, ssd_A_log, ssd_D, ssd_norm_w, w_out_ab, w_in_c,
           gdn_conv_w, gdn_A_log, gdn_dt_bias, gdn_norm_w, w_out_c, w_xq, w_xk, w_xv, w_xo, ffn_w_in, ffn_conv_w,
           ffn_conv_b, ffn_w_out):
    bp, lp, _ = x_prompt.shape
    bs, ls, _ = x_sample.shape
    depth = norm_mix.shape[0]
    assert depth == 2 and w_in_ab.shape[0] == 1 and w_in_c.shape[0] == 1
    assert lp % MIX_ROWS == 0 and lp % XA_TQ == 0 and lp % FFN_RT == 0
    pad_rows = SUBLANES - ls
    assert CONV_W - 1 <= pad_rows and CONV_W - 1 <= ls and MIX_ROWS % SUBLANES == 0
    gp = Group(B=bp, L=lp, first=0, bb=1)
    gs = Group(B=bs, L=SUBLANES, first=pad_rows, bb=MIX_ROWS // SUBLANES)
    assert bs % gs.bb == 0 and (bs * SUBLANES) % MM_TM == 0 and (bp * lp) % MM_TM == 0

    wts = dict(
        in_ab=_cast_pad(w_in_ab[0], MM_TN), out_ab=w_out_ab[0].astype(BF16),
        in_c=_cast_pad(w_in_c[0], MM_TN), out_c=w_out_c[0].astype(BF16),
        xq=w_xq.astype(BF16), xo=w_xo.astype(BF16), ffn_in=ffn_w_in.astype(BF16), ffn_out=ffn_w_out.astype(BF16),
    )
    prm = dict(
        norm_mix=norm_mix, norm_xa=norm_xa, norm_ffn=norm_ffn, norm_final=norm_final,
        ffn_conv_w=ffn_conv_w, ffn_conv_b=ffn_conv_b,
        rw=dict(mu=rw_mu[0], w0=rw_w0[0], w_up=rw_w_up[0], a0=rw_a0[0], a_up=rw_a_up[0], g_up=rw_g_up[0],
                k_k=rw_k_k[0], k_a=rw_k_a[0], r_k=rw_r_k[0], gn_w=rw_gn_w[0], gn_b=rw_gn_b[0]),
        ssd=dict(conv_w=ssd_conv_w[0], conv_b=ssd_conv_b[0], dt_bias=ssd_dt_bias[0], A_log=ssd_A_log[0],
                 D=ssd_D[0], norm_w=ssd_norm_w[0]),
        gdn=dict(conv_w=gdn_conv_w[0], A_log=gdn_A_log[0], dt_bias=gdn_dt_bias[0], norm_w=gdn_norm_w[0]),
    )

    mem2d = mem_prompt.reshape(bp * N_MEM, D_MODEL)
    mem_tm = min(MM_TM, bp * N_MEM)
    mk, mv = [], []
    for l in range(depth):
        mk.append(_matmul(mem2d, w_xk[l].astype(BF16), gain=norm_mem[l], tm=mem_tm))
        mv.append(_matmul(mem2d, w_xv[l].astype(BF16), gain=norm_mem[l], tm=mem_tm))
    mem_k_p = jnp.stack(mk)
    mem_v_p = jnp.stack(mv)

    zeros = lambda *s: jnp.zeros(s, F32)
    st_p = dict(
        rwkv=zeros(bp, RW_HEADS, RW_HD, RW_HD), rw_shift=zeros(bp, RW_PROJ),
        ssd=zeros(bp, SSD_GROUPS, SSD_HPG, SSD_HD, SSD_STATE), ssd_conv=zeros(bp, CONV_W - 1, SSD_CONV_DIM),
        gdn=zeros(bp, GDN_HEADS, GDN_D, GDN_D), gdn_conv=zeros(bp, CONV_W - 1, GDN_CONV_DIM),
        ffn_conv=zeros(depth, bp, FFN_CONV_W - 1, D_FF),
    )
    rp = _run_group(gp, x_prompt.reshape(bp * lp, D_MODEL), mem_k_p.reshape(depth * bp * N_MEM, XA_DIM),
                    mem_v_p.reshape(depth * bp * N_MEM, XA_DIM), [l * bp * N_MEM for l in range(depth)],
                    st_p, wts, prm)

    st_s = dict(rwkv=state_rwkv[0], rw_shift=state_rwkv_shift[0], ssd=state_ssd[0], ssd_conv=state_ssd_conv[0],
                gdn=state_gdn[0], gdn_conv=state_gdn_conv[0], ffn_conv=state_ffn_conv)
    xs = jnp.pad(x_sample, ((0, 0), (pad_rows, 0), (0, 0))).reshape(bs * SUBLANES, D_MODEL)
    rs = _run_group(gs, xs, cache_mem_k.reshape(depth * bs * N_MEM, XA_DIM),
                    cache_mem_v.reshape(depth * bs * N_MEM, XA_DIM), [l * bs * N_MEM for l in range(depth)],
                    st_s, wts, prm)

    y_p = rp["y"].reshape(bp, lp, D_MODEL)
    y_s = rs["y"].reshape(bs, SUBLANES, D_MODEL)[:, pad_rows:]
    lead = lambda x: x[None]
    mem_shape = (depth, bp, N_MEM, XA_HEADS, XA_HD)
    return (y_p, y_s, lead(rp["rwkv"]), lead(rs["rwkv"]), lead(rp["rw_shift"]), lead(rs["rw_shift"]),
            lead(rp["ssd"]), lead(rs["ssd"]), lead(rp["ssd_conv"]), lead(rs["ssd_conv"]),
            lead(rp["gdn"]), lead(rs["gdn"]), lead(rp["gdn_conv"]), lead(rs["gdn_conv"]),
            rp["ffn_conv"], rs["ffn_conv"], mem_k_p.reshape(mem_shape), mem_v_p.reshape(mem_shape))
```

```python
import functools
from typing import NamedTuple

import jax
import jax.numpy as jnp
from jax import lax
from jax.experimental import pallas as pl
from jax.experimental.pallas import tpu as pltpu

F32 = jnp.float32
BF16 = jnp.bfloat16
HI = lax.Precision.HIGHEST

D_MODEL = 2048
EPS = 1e-6
RW_HEADS, RW_HD = 16, 64
RW_DIM = RW_HEADS * RW_HD
RW_LORA = 256
RW_PROJ = 3 * RW_DIM + RW_LORA
RW_GN_EPS = 6.4e-4
SSD_HEADS, SSD_HD, SSD_GROUPS, SSD_STATE = 16, 64, 2, 128
SSD_HPG = SSD_HEADS // SSD_GROUPS
SSD_DIM = SSD_HEADS * SSD_HD
SSD_GDIM = SSD_HPG * SSD_HD
SSD_CONV_DIM = SSD_DIM + 2 * SSD_GROUPS * SSD_STATE
SSD_PROJ = SSD_DIM + SSD_CONV_DIM + SSD_HEADS
AB_PROJ = RW_PROJ + SSD_PROJ
GDN_HEADS, GDN_D = 16, 128
GDN_V = GDN_HEADS * GDN_D
GDN_CONV_DIM = 3 * GDN_V
GDN_PROJ = GDN_CONV_DIM + GDN_V + 2 * GDN_HEADS
D_FF = 5632
XA_HEADS, XA_HD, N_MEM = 4, 128, 256
XA_DIM = XA_HEADS * XA_HD
CONV_W = 4
FFN_CONV_W = 3

SUBLANES = 8
LANES = 128
VMEM_LIMIT = 56 * 1024 * 1024

MIX_ROWS = 64
INV_BLOCK = 16
MM_TM, MM_TN = 1024, 512
RW_HP = 2
RW_SUB = 2
GDN_G = 4
GDN_SUB = 2
XA_TQ = 512
NORM_TM = 512


class Group(NamedTuple):
    B: int
    L: int
    first: int
    bb: int

    @property
    def qs(self):
        return MIX_ROWS // self.bb

    @property
    def nc(self):
        return self.L // self.qs

    @property
    def embedded(self):
        return self.first > 0


MIXER_FLAGS = None


def _cp(*sem, flags=None):
    return pltpu.CompilerParams(dimension_semantics=sem, vmem_limit_bytes=VMEM_LIMIT, flags=flags)


def _dot(a, b):
    return jnp.dot(a.astype(BF16), b.astype(BF16), preferred_element_type=F32)


def _dot_nt(a, b):
    return lax.dot_general(a.astype(BF16), b.astype(BF16), (((1,), (1,)), ((), ())), preferred_element_type=F32)


def _dot_tn(a, b):
    return lax.dot_general(a.astype(BF16), b.astype(BF16), (((0,), (0,)), ((), ())), preferred_element_type=F32)


def _dot_hi(a, b):
    return jnp.dot(a, b, preferred_element_type=F32, precision=HI)


def _split_bf16(x):
    hi = x.astype(BF16)
    return hi, (x - hi.astype(F32)).astype(BF16)


def _dot_x3(a, b):
    ah, al = _split_bf16(a)
    bh, bl = _split_bf16(b)
    return jnp.dot(jnp.concatenate([ah, ah, al], axis=1), jnp.concatenate([bh, bl, bh], axis=0),
                   preferred_element_type=F32)


def _dot_nt_hi(a, b):
    return lax.dot_general(a, b, (((1,), (1,)), ((), ())), preferred_element_type=F32, precision=HI)


def _sigmoid(x):
    return 1.0 / (1.0 + jnp.exp(-x))


def _silu(x):
    return x * _sigmoid(x)


def _softplus(x):
    return jnp.maximum(x, 0.0) + jnp.log(1.0 + jnp.exp(-jnp.abs(x)))


def _iota2(shape, axis):
    return lax.broadcasted_iota(jnp.int32, shape, axis)


class _Masks(NamedTuple):
    incl: jax.Array
    strict: jax.Array
    eye: jax.Array
    cumsum: jax.Array
    valid: jax.Array


def _masks(grp, heads=1):
    r = MIX_ROWS
    n = heads * r
    ri, ci = _iota2((n, n), 0), _iota2((n, n), 1)
    same = (ri // grp.qs) == (ci // grp.qs)
    incl = same & (ci <= ri)
    strict = same & (ci < ri)
    si, sj = _iota2((2 * r, r), 0), _iota2((2 * r, r), 1)
    same_seq = ((si % r) // grp.qs) == (sj // grp.qs)
    cumsum = (same_seq & ((sj <= si) | (si >= r))).astype(F32)
    t = _iota2((r, 1), 0) % grp.qs
    return _Masks(incl, strict, (ri == ci).astype(F32), cumsum, t >= grp.first)


def _neumann(lm, rhs, nil, eye=None):
    n = lm.shape[0]
    p = None
    if rhs is None:
        t = eye + lm
        if nil > 2:
            p = _dot(lm, lm)
    elif nil > 2:
        both = _dot(lm, jnp.concatenate([lm, rhs], axis=1))
        p, t = both[:, :n], rhs + both[:, n:]
    else:
        t = rhs + _dot(lm, rhs)
    k = 2
    while k < nil:
        if 2 * k < nil:
            both = _dot(p, jnp.concatenate([p, t], axis=1))
            p, t = both[:, :n], t + both[:, n:]
        else:
            t = t + _dot(p, t)
        k *= 2
    return t


def _inv_unit_lower(lm, eye, qs):
    if qs <= INV_BLOCK:
        return _neumann(lm, None, qs, eye)
    n = lm.shape[0]
    diag = (_iota2((n, n), 0) // INV_BLOCK) == (_iota2((n, n), 1) // INV_BLOCK)
    d = jnp.where(diag, lm, 0.0)
    dinv = _neumann(d, None, INV_BLOCK, eye)
    nm = _dot(dinv, lm - d)
    return _neumann(nm, dinv, qs // INV_BLOCK)


def _solve_unit_lower(lm, rhs, eye, qs):
    t = _inv_unit_lower(lm, eye, qs)
    x = _dot(t, rhs)
    return x + _dot(t, rhs - x + _dot_x3(lm, x))


def _stack_heads(x, heads, width):
    return jnp.concatenate([x[:, h * width:(h + 1) * width] for h in range(heads)], axis=0)


def _unstack_heads(x, heads):
    r = x.shape[0] // heads
    return jnp.concatenate([x[h * r:(h + 1) * r] for h in range(heads)], axis=1)


def _seq_rows(grp, x, j, heads):
    if grp.bb == 1:
        return x
    qs = grp.qs
    return jnp.concatenate([x[h * MIX_ROWS + j * qs:h * MIX_ROWS + (j + 1) * qs] for h in range(heads)], axis=0)


def _from_seq_rows(grp, parts, heads):
    if grp.bb == 1:
        return parts[0]
    qs = grp.qs
    return jnp.concatenate([p[h * qs:(h + 1) * qs] for h in range(heads) for p in parts], axis=0)


def _expand_rows(x, heads, rows_per_head, width):
    m = x.shape[0]
    if x.shape[1] == width:
        x = jnp.concatenate([x] * heads, axis=1)
    keep = ((_iota2((m, heads * width), 0) // rows_per_head) % heads) == (_iota2((m, heads * width), 1) // width)
    return jnp.where(keep, x, 0.0)


def _taps(grp, cur, halo, width):
    if grp.embedded:
        t = _iota2((cur.shape[0], 1), 0) % grp.qs
        full = jnp.where(t < grp.first, halo, cur)
        return [full] + [pltpu.roll(full, s, axis=0) for s in range(1, width)]
    full = jnp.concatenate([halo, cur], axis=0)
    return [cur] + [pltpu.roll(full, s, axis=0)[SUBLANES:] for s in range(1, width)]


def _conv(grp, cur, halo, w_ref, bias, width):
    taps = _taps(grp, cur, halo, width)
    y = taps[0] * w_ref[width - 1:width, :]
    for s in range(1, width):
        y = y + taps[s] * w_ref[width - 1 - s:width - s, :]
    return y if bias is None else y + bias


class _Tok(NamedTuple):
    arr: jax.Array
    width: int
    col: object
    hist: jax.Array = None
    hcol: object = None


def _mixer_specs(grp, toks):
    nc = grp.nc
    sub = MIX_ROWS // SUBLANES
    ops, specs = [], []
    for t in toks:
        ops.append(t.arr)
        specs.append(pl.BlockSpec((MIX_ROWS, t.width), lambda b, g, c, t=t: (b * nc + c, t.col(g))))
    for t in toks:
        if t.hist is None:
            continue
        if not grp.embedded:
            ops.append(t.arr)
            specs.append(pl.BlockSpec(
                (SUBLANES, t.width), lambda b, g, c, t=t: (jnp.maximum((b * nc + c) * sub - 1, 0), t.col(g))))
        ops.append(t.hist)
        hrows = MIX_ROWS if grp.embedded else SUBLANES
        specs.append(pl.BlockSpec((hrows, t.width), lambda b, g, c, t=t: (b, t.hcol(g))))
    return ops, specs


def _read_windows(grp, refs, n_plain, n_hist):
    c = pl.program_id(2)
    cur = [r[...] for r in refs[:n_plain]]
    rest = refs[n_plain:]
    halos = []
    for i in range(n_hist):
        if grp.embedded:
            halos.append(rest[i][...])
        else:
            halos.append(jnp.where(c == 0, rest[2 * i + 1][...], rest[2 * i][...]))
    used = n_hist if grp.embedded else 2 * n_hist
    return cur, halos, rest[used:]


def _param_spec(width, col):
    return lambda rows: pl.BlockSpec((rows, width), lambda b, g, c: (0, col(g)))


def _matmul_body(*refs, norm, has_res):
    it = iter(refs)
    a_ref = next(it)
    g_ref = next(it) if norm else None
    w_ref = next(it)
    res_ref = next(it) if has_res else None
    o_ref = next(it)
    if norm:
        an_ref = next(it)

        @pl.when(pl.program_id(1) == 0)
        def _():
            x = a_ref[...]
            y = x * lax.rsqrt(jnp.mean(x * x, axis=-1, keepdims=True) + EPS)
            an_ref[...] = (y * g_ref[...]).astype(BF16)

        a = an_ref[...]
    else:
        a = a_ref[...]
    acc = jnp.dot(a, w_ref[...], preferred_element_type=F32)
    if has_res:
        acc = acc + res_ref[...]
    o_ref[...] = acc.astype(o_ref.dtype)


def _matmul(a, w, *, gain=None, res=None, out_dtype=F32, tm=MM_TM, tn=MM_TN):
    m, k = a.shape
    n = w.shape[1]
    norm = gain is not None
    ops = [a]
    specs = [pl.BlockSpec((tm, k), lambda i, j: (i, 0))]
    if norm:
        ops.append(gain.reshape(1, k))
        specs.append(pl.BlockSpec((1, k), lambda i, j: (0, 0)))
    ops.append(w)
    specs.append(pl.BlockSpec((k, tn), lambda i, j: (0, j)))
    if res is not None:
        ops.append(res)
        specs.append(pl.BlockSpec((tm, tn), lambda i, j: (i, j)))
    return pl.pallas_call(
        functools.partial(_matmul_body, norm=norm, has_res=res is not None),
        grid=(m // tm, n // tn),
        in_specs=specs,
        out_specs=pl.BlockSpec((tm, tn), lambda i, j: (i, j)),
        out_shape=jax.ShapeDtypeStruct((m, n), out_dtype),
        scratch_shapes=[pltpu.VMEM((tm, k), BF16)] if norm else [],
        compiler_params=_cp("parallel", "arbitrary"),
        name="matmul",
    )(*ops)


def _rmsnorm_body(x_ref, g_ref, o_ref):
    x = x_ref[...]
    y = x * lax.rsqrt(jnp.mean(x * x, axis=-1, keepdims=True) + EPS)
    o_ref[...] = (y * g_ref[...]).astype(o_ref.dtype)


def _rmsnorm(x, gain, out_dtype=F32, tm=NORM_TM):
    m, k = x.shape
    return pl.pallas_call(
        _rmsnorm_body,
        grid=(m // tm,),
        in_specs=[pl.BlockSpec((tm, k), lambda i: (i, 0)), pl.BlockSpec((1, k), lambda i: (0, 0))],
        out_specs=pl.BlockSpec((tm, k), lambda i: (i, 0)),
        out_shape=jax.ShapeDtypeStruct((m, k), out_dtype),
        compiler_params=_cp("parallel"),
        name="rmsnorm",
    )(x, gain.reshape(1, k))


def _rwkv_body(*refs, grp):
    r_rows = MIX_ROWS
    qs, bb, nc = grp.qs, grp.bb, grp.nc
    nh = 2 * RW_HP
    sw = nh * RW_HD
    width = RW_SUB * sw
    n = nh * r_rows
    cur, halos, rest = _read_windows(grp, refs, 4, 4)
    (mu_r, mu_k, mu_v, mu_lo, w0_ref, a0_ref, lora_ref, kk_ref, ka_ref, rk_ref, gnw_ref, gnb_ref,
     s0_ref, y_ref, s_ref, sbig_ref) = rest
    c = pl.program_id(2)

    @pl.when(c == 0)
    def _():
        for j in range(bb):
            for sub in range(RW_SUB):
                st = jnp.concatenate([s0_ref[j, sub * nh + h] for h in range(nh)], axis=0)
                sbig_ref[j, sub] = _expand_rows(st, nh, RW_HD, RW_HD)

    m = _masks(grp, nh)

    def shifted(i, mu_ref):
        x = cur[i]
        prev = _taps(grp, x, halos[i], 2)[1]
        return x + (prev - x) * mu_ref[...]

    r = shifted(0, mu_r)
    k = shifted(1, mu_k)
    v = shifted(2, mu_v)
    lo = shifted(3, mu_lo)
    lora_in = jnp.concatenate([jnp.tanh(lo[:, :64]), lo[:, 64:128], _sigmoid(lo[:, 128:256])], axis=1)
    lora = _dot_x3(lora_in, lora_ref[0])
    w = -_softplus(-(w0_ref[...] + lora[:, :width])) - 0.5
    lw = jnp.where(m.valid, -jnp.exp(w), 0.0)
    a = _sigmoid(a0_ref[...] + lora[:, width:2 * width])
    g = lora[:, 2 * width:]

    bd = ((_iota2((width, width), 0) // RW_HD) == (_iota2((width, width), 1) // RW_HD)).astype(F32)
    kx = k * kk_ref[...]
    kp = jnp.where(m.valid, k * (1.0 + (a - 1.0) * ka_ref[...]), 0.0)
    sums = _dot_hi(jnp.concatenate([kx * kx, r * kp * rk_ref[...]], axis=0), bd)
    kkn = kx * lax.rsqrt(sums[:r_rows] + 1e-6)
    bonus = sums[r_rows:] * v
    at = jnp.where(m.valid, -kkn, 0.0)
    bt = jnp.where(m.valid, kkn * a, 0.0)

    cums = _dot_hi(m.cumsum, lw)
    cum, cum_end = cums[:r_rows], cums[r_rows:]
    e_neg = jnp.exp(-cum)
    e_end = jnp.exp(cum_end - cum)
    p_end = jnp.exp(cum_end)
    a_til, r_til = at * jnp.exp(cum - lw), r * jnp.exp(cum)
    b_til, k_til = bt * e_neg, kp * e_neg
    b_hat, k_hat = bt * e_end, kp * e_end

    def stack(sub):
        lanes = slice(sub * sw, (sub + 1) * sw)

        def tile(x):
            return _expand_rows(jnp.concatenate([x[:, lanes]] * nh, axis=0), nh, r_rows, RW_HD)

        ar = jnp.concatenate([tile(a_til), tile(r_til)], axis=0)
        bk = jnp.concatenate([tile(b_til), tile(k_til)], axis=0)
        bk_end = jnp.concatenate([tile(b_hat), tile(k_hat)], axis=0)
        v_exp = tile(v)
        gm = _dot_nt(ar, bk)
        m_ab = jnp.where(m.strict, gm[:n, :n], 0.0)
        m_ak = jnp.where(m.strict, gm[:n, n:], 0.0)
        m_r = jnp.concatenate([jnp.where(m.incl, gm[n:, :n], 0.0), jnp.where(m.incl, gm[n:, n:], 0.0)], axis=1)

        def seq2(x, j):
            if bb == 1:
                return x
            return jnp.concatenate([_seq_rows(grp, x[:n], j, nh), _seq_rows(grp, x[n:], j, nh)], axis=0)

        parts = [_dot_nt(seq2(ar, j), sbig_ref[j, sub]) for j in range(bb)]
        half = nh * qs
        as_a = _from_seq_rows(grp, [p[:half] for p in parts], nh)
        as_r = _from_seq_rows(grp, [p[half:] for p in parts], nh)
        rhs = as_a + _dot(m_ak, v_exp)
        u = _solve_unit_lower(m_ab, rhs, m.eye, qs)
        uv = jnp.concatenate([u, v_exp], axis=0)
        y_exp = as_r + _dot(m_r, uv)
        y = y_exp[:r_rows]
        for h in range(1, nh):
            y = y + y_exp[h * r_rows:(h + 1) * r_rows]
        for j in range(bb):
            sbig_ref[j, sub] = (sbig_ref[j, sub] * p_end[j * qs:j * qs + 1, lanes]
                                + _dot_tn(seq2(uv, j), seq2(bk_end, j)))
        return y

    y = jnp.concatenate([stack(sub) for sub in range(RW_SUB)], axis=1)

    @pl.when(c == nc - 1)
    def _():
        for j in range(bb):
            for sub in range(RW_SUB):
                sb = sbig_ref[j, sub]
                for h in range(nh):
                    s_ref[j, sub * nh + h] = sb[h * RW_HD:(h + 1) * RW_HD, h * RW_HD:(h + 1) * RW_HD]

    mean = _dot_hi(y, bd) * (1.0 / RW_HD)
    d = y - mean
    var = _dot_hi(d * d, bd) * (1.0 / RW_HD)
    out = d * lax.rsqrt(var + RW_GN_EPS) * gnw_ref[...] + gnb_ref[...] + bonus
    y_ref[...] = (out * g).astype(y_ref.dtype)


def _rwkv(grp, proj, hist, s0, prm):
    width = RW_SUB * RW_HP * LANES
    ng = RW_DIM // width
    lora_blk = 3 * RW_DIM // RW_LORA
    toks = [
        _Tok(proj, width, lambda g: g, hist, lambda g: g),
        _Tok(proj, width, lambda g: ng + g, hist, lambda g: ng + g),
        _Tok(proj, width, lambda g: 2 * ng + g, hist, lambda g: 2 * ng + g),
        _Tok(proj, RW_LORA, lambda g: lora_blk, hist, lambda g: lora_blk),
    ]
    ops, specs = _mixer_specs(grp, toks)
    head = _param_spec(width, lambda g: g)
    lora_w = jnp.zeros((ng, RW_LORA, 3 * width), F32)
    for i, (name, r0, r1) in enumerate((("w_up", 0, 64), ("a_up", 64, 128), ("g_up", 128, 256))):
        blk = prm[name].reshape(r1 - r0, ng, width).transpose(1, 0, 2)
        lora_w = lora_w.at[:, r0:r1, i * width:(i + 1) * width].set(blk)
    mu = prm["mu"].reshape(1, RW_PROJ)
    flat = lambda x: x.reshape(1, RW_DIM)
    params = [
        (mu, pl.BlockSpec((1, width), lambda b, g, c: (0, g))),
        (mu, pl.BlockSpec((1, width), lambda b, g, c: (0, ng + g))),
        (mu, pl.BlockSpec((1, width), lambda b, g, c: (0, 2 * ng + g))),
        (mu, pl.BlockSpec((1, RW_LORA), lambda b, g, c: (0, lora_blk))),
        (flat(prm["w0"]), head(1)), (flat(prm["a0"]), head(1)),
        (lora_w, pl.BlockSpec((1, RW_LORA, 3 * width), lambda b, g, c: (g, 0, 0))),
        (flat(prm["k_k"]), head(1)), (flat(prm["k_a"]), head(1)), (flat(prm["r_k"]), head(1)),
        (flat(prm["gn_w"]), head(1)), (flat(prm["gn_b"]), head(1)),
    ]
    ops += [p for p, _ in params]
    specs += [s for _, s in params]
    st_spec = pl.BlockSpec((grp.bb, RW_SUB * 2 * RW_HP, RW_HD, RW_HD), lambda b, g, c: (b, g, 0, 0))
    ops.append(s0)
    specs.append(st_spec)
    rows = grp.B * grp.L
    nc = grp.nc
    return pl.pallas_call(
        functools.partial(_rwkv_body, grp=grp),
        grid=(grp.B // grp.bb, ng, nc),
        in_specs=specs,
        out_specs=[pl.BlockSpec((MIX_ROWS, width), lambda b, g, c: (b * nc + c, g)), st_spec],
        out_shape=[jax.ShapeDtypeStruct((rows, RW_DIM), BF16), jax.ShapeDtypeStruct(s0.shape, F32)],
        scratch_shapes=[pltpu.VMEM((grp.bb, RW_SUB, RW_HP * LANES, RW_HP * LANES), F32)],
        compiler_params=_cp("parallel", "parallel", "arbitrary", flags=MIXER_FLAGS),
        name="rwkv7",
    )(*ops)


def _ssd_body(*refs, grp):
    qs, bb = grp.qs, grp.bb
    cur, halos, rest = _read_windows(grp, refs, 7, 4)
    (cw_x0, cw_x1, cw_b, cw_c, cb_x0, cb_x1, cb_b, cb_c, dtb_ref, alog_ref, dskip_ref, nw_ref,
     h0_ref, y_ref, h_ref) = rest
    x0, x1, b_raw, c_raw, z0, z1, dt_blk = cur
    c = pl.program_id(2)
    grp_idx = pl.program_id(1)

    @pl.when(c == 0)
    def _():
        h_ref[...] = h0_ref[...]

    m = _masks(grp)
    xs = jnp.concatenate([
        _silu(_conv(grp, x0, halos[0], cw_x0, cb_x0[...], CONV_W)),
        _silu(_conv(grp, x1, halos[1], cw_x1, cb_x1[...], CONV_W))], axis=1)
    bm = _silu(_conv(grp, b_raw, halos[2], cw_b, cb_b[...], CONV_W))
    cm = _silu(_conv(grp, c_raw, halos[3], cw_c, cb_c[...], CONV_W))
    z = jnp.concatenate([z0, z1], axis=1)

    sel = (_iota2((SSD_HEADS, SSD_HPG), 0) == grp_idx * SSD_HPG + _iota2((SSD_HEADS, SSD_HPG), 1)).astype(F32)
    dt_all = _softplus(dt_blk[:, :SSD_HEADS] + dtb_ref[...])
    dt = jnp.where(m.valid, _dot_hi(dt_all, sel), 0.0)
    da = dt * (-jnp.exp(alog_ref[0]))
    acs_both = _dot_hi(m.cumsum, da)
    acs, acs_end = acs_both[:MIX_ROWS], acs_both[MIX_ROWS:]
    eye8 = (_iota2((SSD_HPG, SSD_HPG), 0) == _iota2((SSD_HPG, SSD_HPG), 1)).astype(F32)
    acs_row = _dot_nt_hi(eye8, acs)
    expand = (_iota2((SSD_HPG, SSD_GDIM), 1) // SSD_HD == _iota2((SSD_HPG, SSD_GDIM), 0)).astype(F32)
    expand_t = (_iota2((SSD_GDIM, SSD_HPG), 0) // SSD_HD == _iota2((SSD_GDIM, SSD_HPG), 1)).astype(F32)
    xd = xs * _dot_hi(dt, expand)
    xd_dec = xd * jnp.exp(_dot_hi(acs_end - acs, expand))
    cb = _dot_nt(cm, bm)
    yd = []
    for r in range(SSD_HPG):
        diff = acs[:, r:r + 1] - acs_row[r:r + 1, :]
        lmat = jnp.where(m.incl, jnp.exp(jnp.where(m.incl, diff, 0.0)), 0.0)
        yd.append(_dot(cb * lmat, xd[:, r * SSD_HD:(r + 1) * SSD_HD]))
    y = jnp.concatenate(yd, axis=1)
    yoff = []
    for j in range(bb):
        rows = slice(j * qs, (j + 1) * qs)
        hj = h_ref[j, 0]
        yoff.append(_dot_nt(cm[rows], hj))
        end_col = jnp.exp(_dot_nt_hi(expand_t, acs_end[j * qs:j * qs + SUBLANES]))[:, :1]
        h_ref[j, 0] = hj * end_col + _dot_tn(xd_dec[rows], bm[rows])
    yoff = yoff[0] if bb == 1 else jnp.concatenate(yoff, axis=0)
    y = y + yoff * jnp.exp(_dot_hi(acs, expand)) + xs * dskip_ref[...]
    yg = y * _silu(z)
    yg = yg * lax.rsqrt(jnp.mean(yg * yg, axis=-1, keepdims=True) + EPS)
    y_ref[...] = (yg * nw_ref[...]).astype(y_ref.dtype)


def _ssd(grp, proj, hist, h0, prm):
    half = SSD_GDIM // 2
    z_blk = RW_PROJ // half
    x_blk = (RW_PROJ + SSD_DIM) // half
    b_blk = (RW_PROJ + 2 * SSD_DIM) // SSD_STATE
    c_blk = b_blk + SSD_GROUPS
    dt_blk = (RW_PROJ + SSD_DIM + SSD_CONV_DIM) // LANES
    hb_blk = SSD_DIM // SSD_STATE
    toks = [
        _Tok(proj, half, lambda g: x_blk + 2 * g, hist, lambda g: 2 * g),
        _Tok(proj, half, lambda g: x_blk + 2 * g + 1, hist, lambda g: 2 * g + 1),
        _Tok(proj, SSD_STATE, lambda g: b_blk + g, hist, lambda g: hb_blk + g),
        _Tok(proj, SSD_STATE, lambda g: c_blk + g, hist, lambda g: hb_blk + SSD_GROUPS + g),
        _Tok(proj, half, lambda g: z_blk + 2 * g),
        _Tok(proj, half, lambda g: z_blk + 2 * g + 1),
        _Tok(proj, LANES, lambda g: dt_blk),
    ]
    ops, specs = _mixer_specs(grp, toks)
    cw, cbias = prm["conv_w"], prm["conv_b"].reshape(1, SSD_CONV_DIM)
    conv_cols = [(half, lambda g: 2 * g), (half, lambda g: 2 * g + 1),
                 (SSD_STATE, lambda g: hb_blk + g), (SSD_STATE, lambda g: hb_blk + SSD_GROUPS + g)]
    params = [(cw, _param_spec(wd, col)(CONV_W)) for wd, col in conv_cols]
    params += [(cbias, _param_spec(wd, col)(1)) for wd, col in conv_cols]
    grouped = lambda x: x.reshape(SSD_GROUPS, 1, SSD_HPG)
    params += [
        (prm["dt_bias"].reshape(1, SSD_HEADS), pl.BlockSpec((1, SSD_HEADS), lambda b, g, c: (0, 0))),
        (grouped(prm["A_log"]), pl.BlockSpec((1, 1, SSD_HPG), lambda b, g, c: (g, 0, 0))),
        (jnp.repeat(prm["D"], SSD_HD).reshape(1, SSD_DIM), _param_spec(SSD_GDIM, lambda g: g)(1)),
        (prm["norm_w"].reshape(1, SSD_DIM), _param_spec(SSD_GDIM, lambda g: g)(1)),
    ]
    ops += [p for p, _ in params]
    specs += [s for _, s in params]
    st_spec = pl.BlockSpec((grp.bb, 1, SSD_GDIM, SSD_STATE), lambda b, g, c: (b, g, 0, 0))
    ops.append(h0)
    specs.append(st_spec)
    rows = grp.B * grp.L
    nc = grp.nc
    return pl.pallas_call(
        functools.partial(_ssd_body, grp=grp),
        grid=(grp.B // grp.bb, SSD_GROUPS, nc),
        in_specs=specs,
        out_specs=[pl.BlockSpec((MIX_ROWS, SSD_GDIM), lambda b, g, c: (b * nc + c, g)), st_spec],
        out_shape=[jax.ShapeDtypeStruct((rows, SSD_DIM), BF16), jax.ShapeDtypeStruct(h0.shape, F32)],
        compiler_params=_cp("parallel", "parallel", "arbitrary"),
        name="ssd",
    )(*ops)


def _gdn_body(*refs, grp):
    r_rows = MIX_ROWS
    qs, bb = grp.qs, grp.bb
    nh = GDN_G
    n = nh * r_rows
    cur, halos, rest = _read_windows(grp, refs, 5, 3)
    cw_q, cw_k, cw_v, alog_ref, dtb_ref, nw_ref, s0_ref, o_ref, s_ref = rest
    q_raw, k_raw, v_raw, z, ba = cur
    c = pl.program_id(2)

    @pl.when(c == 0)
    def _():
        s_ref[...] = s0_ref[...]

    m = _masks(grp, nh)
    q_all = _silu(_conv(grp, q_raw, halos[0], cw_q, None, CONV_W))
    k_all = _silu(_conv(grp, k_raw, halos[1], cw_k, None, CONV_W))
    v_all = _silu(_conv(grp, v_raw, halos[2], cw_v, None, CONV_W))
    per_step = nh * GDN_SUB
    eye_g = (_iota2((SUBLANES, nh), 0) == _iota2((SUBLANES, nh), 1)).astype(F32)
    col = lambda x: jnp.concatenate([x[:, h:h + 1] for h in range(nh)], axis=0)

    def stack(sub):
        lanes = slice(sub * nh * GDN_D, (sub + 1) * nh * GDN_D)
        heads = slice(sub * nh, (sub + 1) * nh)
        q = _stack_heads(q_all[:, lanes], nh, GDN_D)
        k = _stack_heads(k_all[:, lanes], nh, GDN_D)
        v = _stack_heads(v_all[:, lanes], nh, GDN_D)
        q = q * lax.rsqrt(jnp.sum(q * q, axis=-1, keepdims=True) + 1e-6) * (GDN_D ** -0.5)
        k = k * lax.rsqrt(jnp.sum(k * k, axis=-1, keepdims=True) + 1e-6)

        beta = jnp.where(m.valid, _sigmoid(ba[:, heads]), 0.0)
        a_raw = ba[:, per_step + sub * nh:per_step + (sub + 1) * nh]
        gg = jnp.where(m.valid, -jnp.exp(alog_ref[0][:, heads]) * _softplus(a_raw + dtb_ref[0][:, heads]), 0.0)
        gcs = _dot_hi(m.cumsum, gg)
        gc, gc_end = gcs[:r_rows], gcs[r_rows:]
        gc_row = _dot_nt_hi(eye_g, gc)

        bcol, gcol, gend = col(beta), col(gc), col(gc_end)
        grow = jnp.concatenate([gc_row[h:h + 1, :] for h in range(nh)], axis=1)
        dec = jnp.where(m.incl, jnp.exp(jnp.where(m.incl, gcol - grow, 0.0)), 0.0)
        kb = k * bcol
        kq = _dot_nt(jnp.concatenate([kb, q], axis=0), k)
        lm = -jnp.where(m.strict, kq[:n] * dec, 0.0)
        attn = jnp.where(m.incl, kq[n:] * dec, 0.0)
        eg = jnp.exp(gcol)
        sol = _solve_unit_lower(lm, jnp.concatenate([v * bcol, kb * eg], axis=1), m.eye, qs)
        vw, kcd = sol[:, :GDN_D], sol[:, GDN_D:]
        qg = q * eg
        kg = k * jnp.exp(gend - gcol)

        def state(j):
            return jnp.concatenate([s_ref[j, sub * nh + h] for h in range(nh)], axis=0)

        parts = []
        for j in range(bb):
            lhs = jnp.concatenate([_seq_rows(grp, kcd, j, nh), _seq_rows(grp, qg, j, nh)], axis=0)
            parts.append(_dot(_expand_rows(lhs, nh, qs, GDN_D), state(j)))
        half = nh * qs
        ks = _from_seq_rows(grp, [p[:half] for p in parts], nh)
        qsv = _from_seq_rows(grp, [p[half:] for p in parts], nh)
        v_new = vw - ks
        o = qsv + _dot(attn, v_new)
        for j in range(bb):
            kgj = _expand_rows(_seq_rows(grp, kg, j, nh), nh, qs, GDN_D)
            g_last = jnp.exp(gc_end[j * qs:j * qs + 1, :])
            g_last = jnp.concatenate([jnp.broadcast_to(g_last[:, h:h + 1], (GDN_D, 1)) for h in range(nh)], axis=0)
            s_new = state(j) * g_last + _dot_tn(kgj, _seq_rows(grp, v_new, j, nh))
            for h in range(nh):
                s_ref[j, sub * nh + h] = s_new[h * GDN_D:(h + 1) * GDN_D]
        o = o * lax.rsqrt(jnp.mean(o * o, axis=-1, keepdims=True) + EPS) * nw_ref[...]
        return _unstack_heads(o, nh)

    o = jnp.concatenate([stack(sub) for sub in range(GDN_SUB)], axis=1)
    o_ref[...] = (o * _silu(z)).astype(o_ref.dtype)


def _gdn_layout(x):
    per_step = GDN_G * GDN_SUB
    main = GDN_CONV_DIM + GDN_V
    b, a = x[..., main:main + GDN_HEADS], x[..., main + GDN_HEADS:main + 2 * GDN_HEADS]
    parts = [x[..., :main]]
    for g in range(GDN_HEADS // per_step):
        hs = slice(g * per_step, (g + 1) * per_step)
        parts += [b[..., hs], a[..., hs], jnp.zeros(x.shape[:-1] + (LANES - 2 * per_step,), x.dtype)]
    y = jnp.concatenate(parts, axis=-1)
    return jnp.pad(y, [(0, 0)] * (x.ndim - 1) + [(0, -y.shape[-1] % MM_TN)])


def _gdn(grp, proj, hist, s0, prm):
    per_step = GDN_G * GDN_SUB
    width = per_step * GDN_D
    ng = GDN_HEADS // per_step
    ba_blk = (GDN_CONV_DIM + GDN_V) // LANES
    toks = [
        _Tok(proj, width, lambda g: g, hist, lambda g: g),
        _Tok(proj, width, lambda g: ng + g, hist, lambda g: ng + g),
        _Tok(proj, width, lambda g: 2 * ng + g, hist, lambda g: 2 * ng + g),
        _Tok(proj, width, lambda g: 3 * ng + g),
        _Tok(proj, LANES, lambda g: ba_blk + g),
    ]
    ops, specs = _mixer_specs(grp, toks)
    cw = prm["conv_w"]
    grouped = lambda x: x.reshape(ng, 1, per_step)
    params = [
        (cw, _param_spec(width, lambda g: g)(CONV_W)),
        (cw, _param_spec(width, lambda g: ng + g)(CONV_W)),
        (cw, _param_spec(width, lambda g: 2 * ng + g)(CONV_W)),
        (grouped(prm["A_log"]), pl.BlockSpec((1, 1, per_step), lambda b, g, c: (g, 0, 0))),
        (grouped(prm["dt_bias"]), pl.BlockSpec((1, 1, per_step), lambda b, g, c: (g, 0, 0))),
        (prm["norm_w"].reshape(1, GDN_D), pl.BlockSpec((1, GDN_D), lambda b, g, c: (0, 0))),
    ]
    ops += [p for p, _ in params]
    specs += [s for _, s in params]
    st_spec = pl.BlockSpec((grp.bb, per_step, GDN_D, GDN_D), lambda b, g, c: (b, g, 0, 0))
    ops.append(s0)
    specs.append(st_spec)
    rows = grp.B * grp.L
    nc = grp.nc
    return pl.pallas_call(
        functools.partial(_gdn_body, grp=grp),
        grid=(grp.B // grp.bb, ng, nc),
        in_specs=specs,
        out_specs=[pl.BlockSpec((MIX_ROWS, width), lambda b, g, c: (b * nc + c, g)), st_spec],
        out_shape=[jax.ShapeDtypeStruct((rows, GDN_V), BF16), jax.ShapeDtypeStruct(s0.shape, F32)],
        compiler_params=_cp("parallel", "parallel", "arbitrary", flags=MIXER_FLAGS),
        name="gdn",
    )(*ops)


def _xattn_body(q_ref, k_ref, v_ref, o_ref, *, nseq, lq):
    scale = XA_HD ** -0.5
    for j in range(nseq):
        qrows = slice(j * lq, (j + 1) * lq)
        mrows = slice(j * N_MEM, (j + 1) * N_MEM)
        outs = []
        for h in range(XA_HEADS):
            sl = slice(h * XA_HD, (h + 1) * XA_HD)
            s = _dot_nt(q_ref[qrows, sl], k_ref[mrows, sl]) * scale
            s = s - jnp.max(s, axis=-1, keepdims=True)
            p = jnp.exp(s)
            p = p / jnp.sum(p, axis=-1, keepdims=True)
            outs.append(_dot(p, v_ref[mrows, sl]))
        o_ref[qrows, :] = jnp.concatenate(outs, axis=1).astype(o_ref.dtype)


def _xattn(grp, q, mem_k, mem_v, mem_row0):
    rows = grp.B * grp.L
    if grp.L >= XA_TQ:
        nseq, lq = 1, XA_TQ
        per = grp.L // XA_TQ
        mem_idx = lambda i: mem_row0 // N_MEM + i // per
    else:
        nseq, lq = MIX_ROWS // grp.L, grp.L
        mem_idx = lambda i: mem_row0 // (nseq * N_MEM) + i
    mem_spec = pl.BlockSpec((nseq * N_MEM, XA_DIM), lambda i: (mem_idx(i), 0))
    q_spec = pl.BlockSpec((nseq * lq, XA_DIM), lambda i: (i, 0))
    return pl.pallas_call(
        functools.partial(_xattn_body, nseq=nseq, lq=lq),
        grid=(rows // (nseq * lq),),
        in_specs=[q_spec, mem_spec, mem_spec],
        out_specs=q_spec,
        out_shape=jax.ShapeDtypeStruct((rows, XA_DIM), BF16),
        compiler_params=_cp("parallel"),
        name="xattn",
    )(q, mem_k, mem_v)


def _ffn_in_body(*refs, grp, tiles_per_seq):
    it = iter(refs)
    a_ref, g_ref, wg_ref, wu_ref, hist_ref, cw_ref, cb_ref, act_ref, tail_ref, an_ref = (next(it) for _ in range(10))
    carry_ref = None if grp.embedded else next(it)
    i, j = pl.program_id(0), pl.program_id(1)

    @pl.when(j == 0)
    def _():
        x = a_ref[...]
        y = x * lax.rsqrt(jnp.mean(x * x, axis=-1, keepdims=True) + EPS)
        an_ref[...] = (y * g_ref[...]).astype(BF16)

    a = an_ref[...]
    gate = jnp.dot(a, wg_ref[...], preferred_element_type=F32)
    up = jnp.dot(a, wu_ref[...], preferred_element_type=F32)
    if grp.embedded:
        halo = hist_ref[...]
        tail_ref[...] = gate
    else:
        halo = jnp.where(i % tiles_per_seq == 0, hist_ref[...], carry_ref[j])
        last = gate[gate.shape[0] - SUBLANES:]
        carry_ref[j] = last
        tail_ref[...] = last
    conv = _conv(grp, gate, halo, cw_ref, cb_ref[...], FFN_CONV_W)
    act_ref[...] = (_silu(conv) * up).astype(act_ref.dtype)


def _ffn_in(grp, h, gain, w_in, hist, conv_w, conv_b, tm=MM_TM, tn=MM_TN):
    assert grp.embedded or grp.L % tm == 0
    rows, k = h.shape
    nct = D_FF // tn
    tiles_per_seq = max(grp.L // tm, 1)
    hrows = tm if grp.embedded else SUBLANES
    tile_idx = lambda i, j: (i, j)
    hist_idx = tile_idx if grp.embedded else (lambda i, j: (i // tiles_per_seq, j))
    specs = [
        pl.BlockSpec((tm, k), lambda i, j: (i, 0)),
        pl.BlockSpec((1, k), lambda i, j: (0, 0)),
        pl.BlockSpec((k, tn), lambda i, j: (0, j)),
        pl.BlockSpec((k, tn), lambda i, j: (0, nct + j)),
        pl.BlockSpec((hrows, tn), hist_idx),
        pl.BlockSpec((FFN_CONV_W, tn), lambda i, j: (0, j)),
        pl.BlockSpec((1, tn), lambda i, j: (0, j)),
    ]
    scratch = [pltpu.VMEM((tm, k), BF16)]
    if not grp.embedded:
        scratch.append(pltpu.VMEM((nct, SUBLANES, tn), F32))
    act, tails = pl.pallas_call(
        functools.partial(_ffn_in_body, grp=grp, tiles_per_seq=tiles_per_seq),
        grid=(rows // tm, nct),
        in_specs=specs,
        out_specs=[pl.BlockSpec((tm, tn), tile_idx), pl.BlockSpec((hrows, tn), tile_idx)],
        out_shape=[jax.ShapeDtypeStruct((rows, D_FF), BF16), jax.ShapeDtypeStruct((rows // tm * hrows, D_FF), F32)],
        scratch_shapes=scratch,
        compiler_params=_cp("arbitrary", "arbitrary"),
        name="ffn_in",
    )(h, gain.reshape(1, k), w_in, w_in, hist, conv_w, conv_b.reshape(1, D_FF))
    if not grp.embedded:
        tails = tails.reshape(grp.B, tiles_per_seq, SUBLANES, D_FF)[:, -1].reshape(grp.B * SUBLANES, D_FF)
    return act, tails


def _history(grp, buf, width):
    b, k, c = buf.shape
    if grp.embedded:
        h = jnp.pad(buf, ((0, 0), (grp.first - k, grp.L - grp.first), (0, 0)))
        return h.reshape(b * grp.L, c)
    return jnp.pad(buf, ((0, 0), (SUBLANES - k, 0), (0, 0))).reshape(b * SUBLANES, c)


def _tail(grp, x2d, k, cols):
    x = x2d.reshape(grp.B, grp.L, x2d.shape[1])
    return x[:, grp.L - k:, cols]


def _cast_pad(w, mult):
    n = w.shape[1]
    return jnp.pad(w.astype(BF16), ((0, 0), (0, -n % mult)))


def _run_group(grp, x2d, mem_k, mem_v, mem_row0, st, wts, prm):
    h = x2d
    out = {}
    proj = _matmul(h, wts["in_ab"], gain=prm["norm_mix"][0])
    y_rw, s_rw = _rwkv(grp, proj, _history(grp, st["rw_shift"][:, None, :], RW_PROJ), st["rwkv"], prm["rw"])
    y_ssd, s_ssd = _ssd(grp, proj, _history(grp, st["ssd_conv"], SSD_CONV_DIM),
                        st["ssd"].reshape(grp.B, SSD_GROUPS, SSD_GDIM, SSD_STATE), prm["ssd"])
    out["rwkv"] = s_rw
    out["rw_shift"] = _tail(grp, proj, 1, slice(0, RW_PROJ))[:, 0]
    out["ssd"] = s_ssd.reshape(st["ssd"].shape)
    out["ssd_conv"] = _tail(grp, proj, CONV_W - 1, slice(RW_PROJ + SSD_DIM, RW_PROJ + SSD_DIM + SSD_CONV_DIM))
    h = _matmul(jnp.concatenate([y_rw, y_ssd], axis=1), wts["out_ab"], res=h)
    ffn_bufs = []
    for l in range(2):
        if l == 1:
            proj = _matmul(h, wts["in_c"], gain=prm["norm_mix"][1])
            y_c, s_gdn = _gdn(grp, proj, _history(grp, st["gdn_conv"], GDN_CONV_DIM), st["gdn"], prm["gdn"])
            out["gdn"] = s_gdn
            out["gdn_conv"] = _tail(grp, proj, CONV_W - 1, slice(0, GDN_CONV_DIM))
            h = _matmul(y_c, wts["out_c"], res=h)
        q = _matmul(h, wts["xq"][l], gain=prm["norm_xa"][l], out_dtype=BF16)
        o = _xattn(grp, q, mem_k, mem_v, mem_row0[l])
        h = _matmul(o, wts["xo"][l], res=h)
        act, gate_tail = _ffn_in(grp, h, prm["norm_ffn"][l], wts["ffn_in"][l], _history(grp, st["ffn_conv"][l], D_FF),
                                 prm["ffn_conv_w"][l], prm["ffn_conv_b"][l])
        ffn_bufs.append(gate_tail.reshape(grp.B, SUBLANES, D_FF)[:, SUBLANES - (FFN_CONV_W - 1):])
        h = _matmul(act, wts["ffn_out"][l], res=h)
    out["ffn_conv"] = jnp.stack(ffn_bufs)
    out["y"] = _rmsnorm(h, prm["norm_final"])
    return out


def kernel(x_prompt, x_sample, mem_prompt, state_rwkv, state_rwkv_shift, state_ssd, state_ssd_conv, state_gdn,
           state_gdn_conv, state_ffn_conv, cache_mem_k, cache_mem_v, norm_mix, norm_xa, norm_mem, norm_ffn,
           norm_final, w_in_ab, rw_mu, rw_w0, rw_w_up, rw_a0, rw_a_up, rw_g_up, rw_k_k, rw_k_a, rw_r_k, rw_gn_w,
           rw_gn_b, ssd_conv_w, ssd_conv_b, ssd_dt_bias, ssd_A_log, ssd_D, ssd_norm_w, w_out_ab, w_in_c,
           gdn_conv_w, gdn_A_log, gdn_dt_bias, gdn_norm_w, w_out_c, w_xq, w_xk, w_xv, w_xo, ffn_w_in, ffn_conv_w,
           ffn_conv_b, ffn_w_out):
    bp, lp, _ = x_prompt.shape
    bs, ls, _ = x_sample.shape
    depth = norm_mix.shape[0]
    assert depth == 2 and w_in_ab.shape[0] == 1 and w_in_c.shape[0] == 1
    assert lp % MIX_ROWS == 0 and lp % XA_TQ == 0 and lp % MM_TM == 0
    pad_rows = SUBLANES - ls
    assert CONV_W - 1 <= pad_rows and CONV_W - 1 <= ls and MIX_ROWS % SUBLANES == 0
    gp = Group(B=bp, L=lp, first=0, bb=1)
    gs = Group(B=bs, L=SUBLANES, first=pad_rows, bb=MIX_ROWS // SUBLANES)
    assert bs % gs.bb == 0 and (bs * SUBLANES) % MM_TM == 0 and (bp * lp) % MM_TM == 0

    wts = dict(
        in_ab=_cast_pad(w_in_ab[0], MM_TN), out_ab=w_out_ab[0].astype(BF16),
        in_c=_gdn_layout(w_in_c[0].astype(BF16)), out_c=w_out_c[0].astype(BF16),
        xq=w_xq.astype(BF16), xo=w_xo.astype(BF16), ffn_in=ffn_w_in.astype(BF16), ffn_out=ffn_w_out.astype(BF16),
    )
    prm = dict(
        norm_mix=norm_mix, norm_xa=norm_xa, norm_ffn=norm_ffn, norm_final=norm_final,
        ffn_conv_w=ffn_conv_w, ffn_conv_b=ffn_conv_b,
        rw=dict(mu=rw_mu[0], w0=rw_w0[0], w_up=rw_w_up[0], a0=rw_a0[0], a_up=rw_a_up[0], g_up=rw_g_up[0],
                k_k=rw_k_k[0], k_a=rw_k_a[0], r_k=rw_r_k[0], gn_w=rw_gn_w[0], gn_b=rw_gn_b[0]),
        ssd=dict(conv_w=ssd_conv_w[0], conv_b=ssd_conv_b[0], dt_bias=ssd_dt_bias[0], A_log=ssd_A_log[0],
                 D=ssd_D[0], norm_w=ssd_norm_w[0]),
        gdn=dict(conv_w=gdn_conv_w[0], A_log=gdn_A_log[0], dt_bias=gdn_dt_bias[0], norm_w=gdn_norm_w[0]),
    )

    mem2d = mem_prompt.reshape(bp * N_MEM, D_MODEL)
    mem_tm = min(MM_TM, bp * N_MEM)
    mk, mv = [], []
    for l in range(depth):
        mk.append(_matmul(mem2d, w_xk[l].astype(BF16), gain=norm_mem[l], tm=mem_tm))
        mv.append(_matmul(mem2d, w_xv[l].astype(BF16), gain=norm_mem[l], tm=mem_tm))
    mem_k_p = jnp.stack(mk)
    mem_v_p = jnp.stack(mv)

    zeros = lambda *s: jnp.zeros(s, F32)
    st_p = dict(
        rwkv=zeros(bp, RW_HEADS, RW_HD, RW_HD), rw_shift=zeros(bp, RW_PROJ),
        ssd=zeros(bp, SSD_GROUPS, SSD_HPG, SSD_HD, SSD_STATE), ssd_conv=zeros(bp, CONV_W - 1, SSD_CONV_DIM),
        gdn=zeros(bp, GDN_HEADS, GDN_D, GDN_D), gdn_conv=zeros(bp, CONV_W - 1, GDN_CONV_DIM),
        ffn_conv=zeros(depth, bp, FFN_CONV_W - 1, D_FF),
    )
    rp = _run_group(gp, x_prompt.reshape(bp * lp, D_MODEL), mem_k_p.reshape(depth * bp * N_MEM, XA_DIM),
                    mem_v_p.reshape(depth * bp * N_MEM, XA_DIM), [l * bp * N_MEM for l in range(depth)],
                    st_p, wts, prm)

    st_s = dict(rwkv=state_rwkv[0], rw_shift=state_rwkv_shift[0], ssd=state_ssd[0], ssd_conv=state_ssd_conv[0],
                gdn=state_gdn[0], gdn_conv=state_gdn_conv[0], ffn_conv=state_ffn_conv)
    xs = jnp.pad(x_sample, ((0, 0), (pad_rows, 0), (0, 0))).reshape(bs * SUBLANES, D_MODEL)
    rs = _run_group(gs, xs, cache_mem_k.reshape(depth * bs * N_MEM, XA_DIM),
                    cache_mem_v.reshape(depth * bs * N_MEM, XA_DIM), [l * bs * N_MEM for l in range(depth)],
                    st_s, wts, prm)

    y_p = rp["y"].reshape(bp, lp, D_MODEL)
    y_s = rs["y"].reshape(bs, SUBLANES, D_MODEL)[:, pad_rows:]
    lead = lambda x: x[None]
    mem_shape = (depth, bp, N_MEM, XA_HEADS, XA_HD)
    return (y_p, y_s, lead(rp["rwkv"]), lead(rs["rwkv"]), lead(rp["rw_shift"]), lead(rs["rw_shift"]),
            lead(rp["ssd"]), lead(rs["ssd"]), lead(rp["ssd_conv"]), lead(rs["ssd_conv"]),
            lead(rp["gdn"]), lead(rs["gdn"]), lead(rp["gdn_conv"]), lead(rs["gdn_conv"]),
            rp["ffn_conv"], rs["ffn_conv"], mem_k_p.reshape(mem_shape), mem_v_p.reshape(mem_shape))
```

```python
import functools
from typing import NamedTuple

import jax
import jax.numpy as jnp
from jax import lax
from jax.experimental import pallas as pl
from jax.experimental.pallas import tpu as pltpu

F32 = jnp.float32
BF16 = jnp.bfloat16
HI = lax.Precision.HIGHEST

D_MODEL = 2048
EPS = 1e-6
RW_HEADS, RW_HD = 16, 64
RW_DIM = RW_HEADS * RW_HD
RW_LORA = 256
RW_PROJ = 3 * RW_DIM + RW_LORA
RW_GN_EPS = 6.4e-4
SSD_HEADS, SSD_HD, SSD_GROUPS, SSD_STATE = 16, 64, 2, 128
SSD_HPG = SSD_HEADS // SSD_GROUPS
SSD_DIM = SSD_HEADS * SSD_HD
SSD_GDIM = SSD_HPG * SSD_HD
SSD_CONV_DIM = SSD_DIM + 2 * SSD_GROUPS * SSD_STATE
SSD_PROJ = SSD_DIM + SSD_CONV_DIM + SSD_HEADS
AB_PROJ = RW_PROJ + SSD_PROJ
GDN_HEADS, GDN_D = 16, 128
GDN_V = GDN_HEADS * GDN_D
GDN_CONV_DIM = 3 * GDN_V
GDN_PROJ = GDN_CONV_DIM + GDN_V + 2 * GDN_HEADS
D_FF = 5632
XA_HEADS, XA_HD, N_MEM = 4, 128, 256
XA_DIM = XA_HEADS * XA_HD
CONV_W = 4
FFN_CONV_W = 3

SUBLANES = 8
LANES = 128
VMEM_LIMIT = 56 * 1024 * 1024

MIX_ROWS = 64
INV_BLOCK = 16
MM_TM, MM_TN = 1024, 512
RW_HP = 2
RW_SUB = 4
GDN_G = 4
GDN_SUB = 4
XA_TQ = 512
NORM_TM = 512


class Group(NamedTuple):
    B: int
    L: int
    first: int
    bb: int

    @property
    def qs(self):
        return MIX_ROWS // self.bb

    @property
    def nc(self):
        return self.L // self.qs

    @property
    def embedded(self):
        return self.first > 0


MIXER_FLAGS = None


def _cp(*sem, flags=None):
    return pltpu.CompilerParams(dimension_semantics=sem, vmem_limit_bytes=VMEM_LIMIT, flags=flags)


def _dot(a, b):
    return jnp.dot(a.astype(BF16), b.astype(BF16), preferred_element_type=F32)


def _dot_nt(a, b):
    return lax.dot_general(a.astype(BF16), b.astype(BF16), (((1,), (1,)), ((), ())), preferred_element_type=F32)


def _dot_tn(a, b):
    return lax.dot_general(a.astype(BF16), b.astype(BF16), (((0,), (0,)), ((), ())), preferred_element_type=F32)


def _dot_hi(a, b):
    return jnp.dot(a, b, preferred_element_type=F32, precision=HI)


def _split_bf16(x):
    hi = x.astype(BF16)
    return hi, (x - hi.astype(F32)).astype(BF16)


def _dot_x3(a, b):
    ah, al = _split_bf16(a)
    bh, bl = _split_bf16(b)
    return jnp.dot(jnp.concatenate([ah, ah, al], axis=1), jnp.concatenate([bh, bl, bh], axis=0),
                   preferred_element_type=F32)


def _dot_nt_hi(a, b):
    return lax.dot_general(a, b, (((1,), (1,)), ((), ())), preferred_element_type=F32, precision=HI)


def _sigmoid(x):
    return 1.0 / (1.0 + jnp.exp(-x))


def _silu(x):
    return x * _sigmoid(x)


def _softplus(x):
    return jnp.maximum(x, 0.0) + jnp.log(1.0 + jnp.exp(-jnp.abs(x)))


def _iota2(shape, axis):
    return lax.broadcasted_iota(jnp.int32, shape, axis)


class _Masks(NamedTuple):
    incl: jax.Array
    strict: jax.Array
    eye: jax.Array
    cumsum: jax.Array
    valid: jax.Array


def _masks(grp, heads=1):
    r = MIX_ROWS
    n = heads * r
    ri, ci = _iota2((n, n), 0), _iota2((n, n), 1)
    same = (ri // grp.qs) == (ci // grp.qs)
    incl = same & (ci <= ri)
    strict = same & (ci < ri)
    si, sj = _iota2((2 * r, r), 0), _iota2((2 * r, r), 1)
    same_seq = ((si % r) // grp.qs) == (sj // grp.qs)
    cumsum = (same_seq & ((sj <= si) | (si >= r))).astype(F32)
    t = _iota2((r, 1), 0) % grp.qs
    return _Masks(incl, strict, (ri == ci).astype(F32), cumsum, t >= grp.first)


def _neumann(lm, rhs, nil, eye=None):
    n = lm.shape[0]
    p = None
    if rhs is None:
        t = eye + lm
        if nil > 2:
            p = _dot(lm, lm)
            yield
    elif nil > 2:
        both = _dot(lm, jnp.concatenate([lm, rhs], axis=1))
        yield
        p, t = both[:, :n], rhs + both[:, n:]
    else:
        t = rhs + _dot(lm, rhs)
        yield
    k = 2
    while k < nil:
        if 2 * k < nil:
            both = _dot(p, jnp.concatenate([p, t], axis=1))
            p, t = both[:, :n], t + both[:, n:]
        else:
            t = t + _dot(p, t)
        yield
        k *= 2
    return t


def _inv_unit_lower(lm, eye, qs):
    if qs <= INV_BLOCK:
        return (yield from _neumann(lm, None, qs, eye))
    n = lm.shape[0]
    diag = (_iota2((n, n), 0) // INV_BLOCK) == (_iota2((n, n), 1) // INV_BLOCK)
    d = jnp.where(diag, lm, 0.0)
    dinv = yield from _neumann(d, None, INV_BLOCK, eye)
    nm = _dot(dinv, lm - d)
    yield
    return (yield from _neumann(nm, dinv, qs // INV_BLOCK))


def _solve_unit_lower(lm, rhs, eye, qs):
    t = yield from _inv_unit_lower(lm, eye, qs)
    x = _dot(t, rhs)
    yield
    resid = rhs - x + _dot_x3(lm, x)
    yield
    x = x + _dot(t, resid)
    yield
    return x


def _interleave(gens):
    results = [None] * len(gens)
    active = list(range(len(gens)))
    while active:
        for i in list(active):
            try:
                next(gens[i])
            except StopIteration as stop:
                results[i] = stop.value
                active.remove(i)
    return results


def _stack_heads(x, heads, width):
    return jnp.concatenate([x[:, h * width:(h + 1) * width] for h in range(heads)], axis=0)


def _unstack_heads(x, heads):
    r = x.shape[0] // heads
    return jnp.concatenate([x[h * r:(h + 1) * r] for h in range(heads)], axis=1)


def _seq_rows(grp, x, j, heads):
    if grp.bb == 1:
        return x
    qs = grp.qs
    return jnp.concatenate([x[h * MIX_ROWS + j * qs:h * MIX_ROWS + (j + 1) * qs] for h in range(heads)], axis=0)


def _from_seq_rows(grp, parts, heads):
    if grp.bb == 1:
        return parts[0]
    qs = grp.qs
    return jnp.concatenate([p[h * qs:(h + 1) * qs] for h in range(heads) for p in parts], axis=0)


def _expand_rows(x, heads, rows_per_head, width):
    m = x.shape[0]
    if x.shape[1] == width:
        x = jnp.concatenate([x] * heads, axis=1)
    keep = ((_iota2((m, heads * width), 0) // rows_per_head) % heads) == (_iota2((m, heads * width), 1) // width)
    return jnp.where(keep, x, 0.0)


def _taps(grp, cur, halo, width):
    if grp.embedded:
        t = _iota2((cur.shape[0], 1), 0) % grp.qs
        full = jnp.where(t < grp.first, halo, cur)
        return [full] + [pltpu.roll(full, s, axis=0) for s in range(1, width)]
    full = jnp.concatenate([halo, cur], axis=0)
    return [cur] + [pltpu.roll(full, s, axis=0)[SUBLANES:] for s in range(1, width)]


def _conv(grp, cur, halo, w_ref, bias, width):
    taps = _taps(grp, cur, halo, width)
    y = taps[0] * w_ref[width - 1:width, :]
    for s in range(1, width):
        y = y + taps[s] * w_ref[width - 1 - s:width - s, :]
    return y if bias is None else y + bias


class _Tok(NamedTuple):
    arr: jax.Array
    width: int
    col: object
    hist: jax.Array = None
    hcol: object = None


def _mixer_specs(grp, toks):
    nc = grp.nc
    sub = MIX_ROWS // SUBLANES
    ops, specs = [], []
    for t in toks:
        ops.append(t.arr)
        specs.append(pl.BlockSpec((MIX_ROWS, t.width), lambda b, g, c, t=t: (b * nc + c, t.col(g))))
    for t in toks:
        if t.hist is None:
            continue
        if not grp.embedded:
            ops.append(t.arr)
            specs.append(pl.BlockSpec(
                (SUBLANES, t.width), lambda b, g, c, t=t: (jnp.maximum((b * nc + c) * sub - 1, 0), t.col(g))))
        ops.append(t.hist)
        hrows = MIX_ROWS if grp.embedded else SUBLANES
        specs.append(pl.BlockSpec((hrows, t.width), lambda b, g, c, t=t: (b, t.hcol(g))))
    return ops, specs


def _read_windows(grp, refs, n_plain, n_hist):
    c = pl.program_id(2)
    cur = [r[...] for r in refs[:n_plain]]
    rest = refs[n_plain:]
    halos = []
    for i in range(n_hist):
        if grp.embedded:
            halos.append(rest[i][...])
        else:
            halos.append(jnp.where(c == 0, rest[2 * i + 1][...], rest[2 * i][...]))
    used = n_hist if grp.embedded else 2 * n_hist
    return cur, halos, rest[used:]


def _param_spec(width, col):
    return lambda rows: pl.BlockSpec((rows, width), lambda b, g, c: (0, col(g)))


def _matmul_body(*refs, norm, has_res):
    it = iter(refs)
    a_ref = next(it)
    g_ref = next(it) if norm else None
    w_ref = next(it)
    res_ref = next(it) if has_res else None
    o_ref = next(it)
    if norm:
        an_ref = next(it)

        @pl.when(pl.program_id(1) == 0)
        def _():
            x = a_ref[...]
            y = x * lax.rsqrt(jnp.mean(x * x, axis=-1, keepdims=True) + EPS)
            an_ref[...] = (y * g_ref[...]).astype(BF16)

        a = an_ref[...]
    else:
        a = a_ref[...]
    acc = jnp.dot(a, w_ref[...], preferred_element_type=F32)
    if has_res:
        acc = acc + res_ref[...]
    o_ref[...] = acc.astype(o_ref.dtype)


def _matmul(a, w, *, gain=None, res=None, out_dtype=F32, tm=MM_TM, tn=MM_TN):
    m, k = a.shape
    n = w.shape[1]
    norm = gain is not None
    ops = [a]
    specs = [pl.BlockSpec((tm, k), lambda i, j: (i, 0))]
    if norm:
        ops.append(gain.reshape(1, k))
        specs.append(pl.BlockSpec((1, k), lambda i, j: (0, 0)))
    ops.append(w)
    specs.append(pl.BlockSpec((k, tn), lambda i, j: (0, j)))
    if res is not None:
        ops.append(res)
        specs.append(pl.BlockSpec((tm, tn), lambda i, j: (i, j)))
    return pl.pallas_call(
        functools.partial(_matmul_body, norm=norm, has_res=res is not None),
        grid=(m // tm, n // tn),
        in_specs=specs,
        out_specs=pl.BlockSpec((tm, tn), lambda i, j: (i, j)),
        out_shape=jax.ShapeDtypeStruct((m, n), out_dtype),
        scratch_shapes=[pltpu.VMEM((tm, k), BF16)] if norm else [],
        compiler_params=_cp("parallel", "arbitrary"),
        name="matmul",
    )(*ops)


def _rmsnorm_body(x_ref, g_ref, o_ref):
    x = x_ref[...]
    y = x * lax.rsqrt(jnp.mean(x * x, axis=-1, keepdims=True) + EPS)
    o_ref[...] = (y * g_ref[...]).astype(o_ref.dtype)


def _rmsnorm(x, gain, out_dtype=F32, tm=NORM_TM):
    m, k = x.shape
    return pl.pallas_call(
        _rmsnorm_body,
        grid=(m // tm,),
        in_specs=[pl.BlockSpec((tm, k), lambda i: (i, 0)), pl.BlockSpec((1, k), lambda i: (0, 0))],
        out_specs=pl.BlockSpec((tm, k), lambda i: (i, 0)),
        out_shape=jax.ShapeDtypeStruct((m, k), out_dtype),
        compiler_params=_cp("parallel"),
        name="rmsnorm",
    )(x, gain.reshape(1, k))


def _rwkv_body(*refs, grp):
    r_rows = MIX_ROWS
    qs, bb, nc = grp.qs, grp.bb, grp.nc
    nh = 2 * RW_HP
    sw = nh * RW_HD
    width = RW_SUB * sw
    n = nh * r_rows
    cur, halos, rest = _read_windows(grp, refs, 4, 4)
    (mu_r, mu_k, mu_v, mu_lo, w0_ref, a0_ref, lora_ref, kk_ref, ka_ref, rk_ref, gnw_ref, gnb_ref,
     s0_ref, y_ref, s_ref, sbig_ref) = rest
    c = pl.program_id(2)

    @pl.when(c == 0)
    def _():
        for j in range(bb):
            for sub in range(RW_SUB):
                st = jnp.concatenate([s0_ref[j, sub * nh + h] for h in range(nh)], axis=0)
                sbig_ref[j, sub] = _expand_rows(st, nh, RW_HD, RW_HD)

    m = _masks(grp, nh)

    def shifted(i, mu_ref):
        x = cur[i]
        prev = _taps(grp, x, halos[i], 2)[1]
        return x + (prev - x) * mu_ref[...]

    r = shifted(0, mu_r)
    k = shifted(1, mu_k)
    v = shifted(2, mu_v)
    lo = shifted(3, mu_lo)
    lora_in = jnp.concatenate([jnp.tanh(lo[:, :64]), lo[:, 64:128], _sigmoid(lo[:, 128:256])], axis=1)
    lora = _dot_x3(lora_in, lora_ref[0])
    w = -_softplus(-(w0_ref[...] + lora[:, :width])) - 0.5
    lw = jnp.where(m.valid, -jnp.exp(w), 0.0)
    a = _sigmoid(a0_ref[...] + lora[:, width:2 * width])
    g = lora[:, 2 * width:]

    bd = ((_iota2((sw, sw), 0) // RW_HD) == (_iota2((sw, sw), 1) // RW_HD)).astype(F32)

    def head_sums(x):
        return _unstack_heads(_dot_hi(_stack_heads(x, RW_SUB, sw), bd), RW_SUB)

    kx = k * kk_ref[...]
    kp = jnp.where(m.valid, k * (1.0 + (a - 1.0) * ka_ref[...]), 0.0)
    sums = head_sums(jnp.concatenate([kx * kx, r * kp * rk_ref[...]], axis=0))
    kkn = kx * lax.rsqrt(sums[:r_rows] + 1e-6)
    bonus = sums[r_rows:] * v
    at = jnp.where(m.valid, -kkn, 0.0)
    bt = jnp.where(m.valid, kkn * a, 0.0)

    cums = _dot_hi(m.cumsum, lw)
    cum, cum_end = cums[:r_rows], cums[r_rows:]
    e_neg = jnp.exp(-cum)
    e_end = jnp.exp(cum_end - cum)
    p_end = jnp.exp(cum_end)
    a_til, r_til = at * jnp.exp(cum - lw), r * jnp.exp(cum)
    b_til, k_til = bt * e_neg, kp * e_neg
    b_hat, k_hat = bt * e_end, kp * e_end

    def stack(sub):
        lanes = slice(sub * sw, (sub + 1) * sw)

        def tile(x):
            return _expand_rows(jnp.concatenate([x[:, lanes]] * nh, axis=0), nh, r_rows, RW_HD)

        ar = jnp.concatenate([tile(a_til), tile(r_til)], axis=0)
        bk = jnp.concatenate([tile(b_til), tile(k_til)], axis=0)
        bk_end = jnp.concatenate([tile(b_hat), tile(k_hat)], axis=0)
        v_exp = tile(v)
        gm = _dot_nt(ar, bk)
        yield
        m_ab = jnp.where(m.strict, gm[:n, :n], 0.0)
        m_ak = jnp.where(m.strict, gm[:n, n:], 0.0)
        m_r = jnp.concatenate([jnp.where(m.incl, gm[n:, :n], 0.0), jnp.where(m.incl, gm[n:, n:], 0.0)], axis=1)

        def seq2(x, j):
            if bb == 1:
                return x
            return jnp.concatenate([_seq_rows(grp, x[:n], j, nh), _seq_rows(grp, x[n:], j, nh)], axis=0)

        parts = []
        for j in range(bb):
            parts.append(_dot_nt(seq2(ar, j), sbig_ref[j, sub]))
            yield
        half = nh * qs
        as_a = _from_seq_rows(grp, [p[:half] for p in parts], nh)
        as_r = _from_seq_rows(grp, [p[half:] for p in parts], nh)
        rhs = as_a + _dot(m_ak, v_exp)
        yield
        u = yield from _solve_unit_lower(m_ab, rhs, m.eye, qs)
        uv = jnp.concatenate([u, v_exp], axis=0)
        y_exp = as_r + _dot(m_r, uv)
        yield
        y = y_exp[:r_rows]
        for h in range(1, nh):
            y = y + y_exp[h * r_rows:(h + 1) * r_rows]
        for j in range(bb):
            sbig_ref[j, sub] = (sbig_ref[j, sub] * p_end[j * qs:j * qs + 1, lanes]
                                + _dot_tn(seq2(uv, j), seq2(bk_end, j)))
            yield
        return y

    y = jnp.concatenate(_interleave([stack(sub) for sub in range(RW_SUB)]), axis=1)

    @pl.when(c == nc - 1)
    def _():
        for j in range(bb):
            for sub in range(RW_SUB):
                sb = sbig_ref[j, sub]
                for h in range(nh):
                    s_ref[j, sub * nh + h] = sb[h * RW_HD:(h + 1) * RW_HD, h * RW_HD:(h + 1) * RW_HD]

    mean = head_sums(y) * (1.0 / RW_HD)
    d = y - mean
    var = head_sums(d * d) * (1.0 / RW_HD)
    out = d * lax.rsqrt(var + RW_GN_EPS) * gnw_ref[...] + gnb_ref[...] + bonus
    y_ref[...] = (out * g).astype(y_ref.dtype)


def _rwkv(grp, proj, hist, s0, prm):
    width = RW_SUB * RW_HP * LANES
    ng = RW_DIM // width
    lora_blk = 3 * RW_DIM // RW_LORA
    toks = [
        _Tok(proj, width, lambda g: g, hist, lambda g: g),
        _Tok(proj, width, lambda g: ng + g, hist, lambda g: ng + g),
        _Tok(proj, width, lambda g: 2 * ng + g, hist, lambda g: 2 * ng + g),
        _Tok(proj, RW_LORA, lambda g: lora_blk, hist, lambda g: lora_blk),
    ]
    ops, specs = _mixer_specs(grp, toks)
    head = _param_spec(width, lambda g: g)
    lora_w = jnp.zeros((ng, RW_LORA, 3 * width), F32)
    for i, (name, r0, r1) in enumerate((("w_up", 0, 64), ("a_up", 64, 128), ("g_up", 128, 256))):
        blk = prm[name].reshape(r1 - r0, ng, width).transpose(1, 0, 2)
        lora_w = lora_w.at[:, r0:r1, i * width:(i + 1) * width].set(blk)
    mu = prm["mu"].reshape(1, RW_PROJ)
    flat = lambda x: x.reshape(1, RW_DIM)
    params = [
        (mu, pl.BlockSpec((1, width), lambda b, g, c: (0, g))),
        (mu, pl.BlockSpec((1, width), lambda b, g, c: (0, ng + g))),
        (mu, pl.BlockSpec((1, width), lambda b, g, c: (0, 2 * ng + g))),
        (mu, pl.BlockSpec((1, RW_LORA), lambda b, g, c: (0, lora_blk))),
        (flat(prm["w0"]), head(1)), (flat(prm["a0"]), head(1)),
        (lora_w, pl.BlockSpec((1, RW_LORA, 3 * width), lambda b, g, c: (g, 0, 0))),
        (flat(prm["k_k"]), head(1)), (flat(prm["k_a"]), head(1)), (flat(prm["r_k"]), head(1)),
        (flat(prm["gn_w"]), head(1)), (flat(prm["gn_b"]), head(1)),
    ]
    ops += [p for p, _ in params]
    specs += [s for _, s in params]
    st_spec = pl.BlockSpec((grp.bb, RW_SUB * 2 * RW_HP, RW_HD, RW_HD), lambda b, g, c: (b, g, 0, 0))
    ops.append(s0)
    specs.append(st_spec)
    rows = grp.B * grp.L
    nc = grp.nc
    return pl.pallas_call(
        functools.partial(_rwkv_body, grp=grp),
        grid=(grp.B // grp.bb, ng, nc),
        in_specs=specs,
        out_specs=[pl.BlockSpec((MIX_ROWS, width), lambda b, g, c: (b * nc + c, g)), st_spec],
        out_shape=[jax.ShapeDtypeStruct((rows, RW_DIM), BF16), jax.ShapeDtypeStruct(s0.shape, F32)],
        scratch_shapes=[pltpu.VMEM((grp.bb, RW_SUB, RW_HP * LANES, RW_HP * LANES), F32)],
        compiler_params=_cp("parallel", "parallel", "arbitrary", flags=MIXER_FLAGS),
        name="rwkv7",
    )(*ops)


def _ssd_body(*refs, grp):
    qs, bb = grp.qs, grp.bb
    cur, halos, rest = _read_windows(grp, refs, 7, 4)
    (cw_x0, cw_x1, cw_b, cw_c, cb_x0, cb_x1, cb_b, cb_c, dtb_ref, alog_ref, dskip_ref, nw_ref,
     h0_ref, y_ref, h_ref) = rest
    x0, x1, b_raw, c_raw, z0, z1, dt_blk = cur
    c = pl.program_id(2)
    grp_idx = pl.program_id(1)

    @pl.when(c == 0)
    def _():
        h_ref[...] = h0_ref[...]

    m = _masks(grp)
    xs = jnp.concatenate([
        _silu(_conv(grp, x0, halos[0], cw_x0, cb_x0[...], CONV_W)),
        _silu(_conv(grp, x1, halos[1], cw_x1, cb_x1[...], CONV_W))], axis=1)
    bm = _silu(_conv(grp, b_raw, halos[2], cw_b, cb_b[...], CONV_W))
    cm = _silu(_conv(grp, c_raw, halos[3], cw_c, cb_c[...], CONV_W))
    z = jnp.concatenate([z0, z1], axis=1)

    sel = (_iota2((SSD_HEADS, SSD_HPG), 0) == grp_idx * SSD_HPG + _iota2((SSD_HEADS, SSD_HPG), 1)).astype(F32)
    dt_all = _softplus(dt_blk[:, :SSD_HEADS] + dtb_ref[...])
    dt = jnp.where(m.valid, _dot_hi(dt_all, sel), 0.0)
    da = dt * (-jnp.exp(alog_ref[0]))
    acs_both = _dot_hi(m.cumsum, da)
    acs, acs_end = acs_both[:MIX_ROWS], acs_both[MIX_ROWS:]
    eye8 = (_iota2((SSD_HPG, SSD_HPG), 0) == _iota2((SSD_HPG, SSD_HPG), 1)).astype(F32)
    acs_row = _dot_nt_hi(eye8, acs)
    expand = (_iota2((SSD_HPG, SSD_GDIM), 1) // SSD_HD == _iota2((SSD_HPG, SSD_GDIM), 0)).astype(F32)
    expand_t = (_iota2((SSD_GDIM, SSD_HPG), 0) // SSD_HD == _iota2((SSD_GDIM, SSD_HPG), 1)).astype(F32)
    xd = xs * _dot_hi(dt, expand)
    xd_dec = xd * jnp.exp(_dot_hi(acs_end - acs, expand))
    cb = _dot_nt(cm, bm)
    yd = []
    for r in range(SSD_HPG):
        diff = acs[:, r:r + 1] - acs_row[r:r + 1, :]
        lmat = jnp.where(m.incl, jnp.exp(jnp.where(m.incl, diff, 0.0)), 0.0)
        yd.append(_dot(cb * lmat, xd[:, r * SSD_HD:(r + 1) * SSD_HD]))
    y = jnp.concatenate(yd, axis=1)
    yoff = []
    for j in range(bb):
        rows = slice(j * qs, (j + 1) * qs)
        hj = h_ref[j, 0]
        yoff.append(_dot_nt(cm[rows], hj))
        end_col = jnp.exp(_dot_nt_hi(expand_t, acs_end[j * qs:j * qs + SUBLANES]))[:, :1]
        h_ref[j, 0] = hj * end_col + _dot_tn(xd_dec[rows], bm[rows])
    yoff = yoff[0] if bb == 1 else jnp.concatenate(yoff, axis=0)
    y = y + yoff * jnp.exp(_dot_hi(acs, expand)) + xs * dskip_ref[...]
    yg = y * _silu(z)
    yg = yg * lax.rsqrt(jnp.mean(yg * yg, axis=-1, keepdims=True) + EPS)
    y_ref[...] = (yg * nw_ref[...]).astype(y_ref.dtype)


def _ssd(grp, proj, hist, h0, prm):
    half = SSD_GDIM // 2
    z_blk = RW_PROJ // half
    x_blk = (RW_PROJ + SSD_DIM) // half
    b_blk = (RW_PROJ + 2 * SSD_DIM) // SSD_STATE
    c_blk = b_blk + SSD_GROUPS
    dt_blk = (RW_PROJ + SSD_DIM + SSD_CONV_DIM) // LANES
    hb_blk = SSD_DIM // SSD_STATE
    toks = [
        _Tok(proj, half, lambda g: x_blk + 2 * g, hist, lambda g: 2 * g),
        _Tok(proj, half, lambda g: x_blk + 2 * g + 1, hist, lambda g: 2 * g + 1),
        _Tok(proj, SSD_STATE, lambda g: b_blk + g, hist, lambda g: hb_blk + g),
        _Tok(proj, SSD_STATE, lambda g: c_blk + g, hist, lambda g: hb_blk + SSD_GROUPS + g),
        _Tok(proj, half, lambda g: z_blk + 2 * g),
        _Tok(proj, half, lambda g: z_blk + 2 * g + 1),
        _Tok(proj, LANES, lambda g: dt_blk),
    ]
    ops, specs = _mixer_specs(grp, toks)
    cw, cbias = prm["conv_w"], prm["conv_b"].reshape(1, SSD_CONV_DIM)
    conv_cols = [(half, lambda g: 2 * g), (half, lambda g: 2 * g + 1),
                 (SSD_STATE, lambda g: hb_blk + g), (SSD_STATE, lambda g: hb_blk + SSD_GROUPS + g)]
    params = [(cw, _param_spec(wd, col)(CONV_W)) for wd, col in conv_cols]
    params += [(cbias, _param_spec(wd, col)(1)) for wd, col in conv_cols]
    grouped = lambda x: x.reshape(SSD_GROUPS, 1, SSD_HPG)
    params += [
        (prm["dt_bias"].reshape(1, SSD_HEADS), pl.BlockSpec((1, SSD_HEADS), lambda b, g, c: (0, 0))),
        (grouped(prm["A_log"]), pl.BlockSpec((1, 1, SSD_HPG), lambda b, g, c: (g, 0, 0))),
        (jnp.repeat(prm["D"], SSD_HD).reshape(1, SSD_DIM), _param_spec(SSD_GDIM, lambda g: g)(1)),
        (prm["norm_w"].reshape(1, SSD_DIM), _param_spec(SSD_GDIM, lambda g: g)(1)),
    ]
    ops += [p for p, _ in params]
    specs += [s for _, s in params]
    st_spec = pl.BlockSpec((grp.bb, 1, SSD_GDIM, SSD_STATE), lambda b, g, c: (b, g, 0, 0))
    ops.append(h0)
    specs.append(st_spec)
    rows = grp.B * grp.L
    nc = grp.nc
    return pl.pallas_call(
        functools.partial(_ssd_body, grp=grp),
        grid=(grp.B // grp.bb, SSD_GROUPS, nc),
        in_specs=specs,
        out_specs=[pl.BlockSpec((MIX_ROWS, SSD_GDIM), lambda b, g, c: (b * nc + c, g)), st_spec],
        out_shape=[jax.ShapeDtypeStruct((rows, SSD_DIM), BF16), jax.ShapeDtypeStruct(h0.shape, F32)],
        compiler_params=_cp("parallel", "parallel", "arbitrary"),
        name="ssd",
    )(*ops)


def _gdn_body(*refs, grp):
    r_rows = MIX_ROWS
    qs, bb = grp.qs, grp.bb
    nh = GDN_G
    n = nh * r_rows
    cur, halos, rest = _read_windows(grp, refs, 5, 3)
    cw_q, cw_k, cw_v, alog_ref, dtb_ref, nw_ref, s0_ref, o_ref, s_ref = rest
    q_raw, k_raw, v_raw, z, ba = cur
    c = pl.program_id(2)

    @pl.when(c == 0)
    def _():
        s_ref[...] = s0_ref[...]

    m = _masks(grp, nh)
    q_all = _silu(_conv(grp, q_raw, halos[0], cw_q, None, CONV_W))
    k_all = _silu(_conv(grp, k_raw, halos[1], cw_k, None, CONV_W))
    v_all = _silu(_conv(grp, v_raw, halos[2], cw_v, None, CONV_W))
    per_step = nh * GDN_SUB
    eye_g = (_iota2((SUBLANES, nh), 0) == _iota2((SUBLANES, nh), 1)).astype(F32)
    col = lambda x: jnp.concatenate([x[:, h:h + 1] for h in range(nh)], axis=0)

    def stack(sub):
        lanes = slice(sub * nh * GDN_D, (sub + 1) * nh * GDN_D)
        heads = slice(sub * nh, (sub + 1) * nh)
        q = _stack_heads(q_all[:, lanes], nh, GDN_D)
        k = _stack_heads(k_all[:, lanes], nh, GDN_D)
        v = _stack_heads(v_all[:, lanes], nh, GDN_D)
        q = q * lax.rsqrt(jnp.sum(q * q, axis=-1, keepdims=True) + 1e-6) * (GDN_D ** -0.5)
        k = k * lax.rsqrt(jnp.sum(k * k, axis=-1, keepdims=True) + 1e-6)

        beta = jnp.where(m.valid, _sigmoid(ba[:, heads]), 0.0)
        a_raw = ba[:, per_step + sub * nh:per_step + (sub + 1) * nh]
        gg = jnp.where(m.valid, -jnp.exp(alog_ref[0][:, heads]) * _softplus(a_raw + dtb_ref[0][:, heads]), 0.0)
        gcs = _dot_hi(m.cumsum, gg)
        yield
        gc, gc_end = gcs[:r_rows], gcs[r_rows:]
        gc_row = _dot_nt_hi(eye_g, gc)
        yield

        bcol, gcol, gend = col(beta), col(gc), col(gc_end)
        grow = jnp.concatenate([gc_row[h:h + 1, :] for h in range(nh)], axis=1)
        dec = jnp.where(m.incl, jnp.exp(jnp.where(m.incl, gcol - grow, 0.0)), 0.0)
        kb = k * bcol
        kq = _dot_nt(jnp.concatenate([kb, q], axis=0), k)
        yield
        lm = -jnp.where(m.strict, kq[:n] * dec, 0.0)
        attn = jnp.where(m.incl, kq[n:] * dec, 0.0)
        eg = jnp.exp(gcol)
        sol = yield from _solve_unit_lower(lm, jnp.concatenate([v * bcol, kb * eg], axis=1), m.eye, qs)
        vw, kcd = sol[:, :GDN_D], sol[:, GDN_D:]
        qg = q * eg
        kg = k * jnp.exp(gend - gcol)

        def state(j):
            return jnp.concatenate([s_ref[j, sub * nh + h] for h in range(nh)], axis=0)

        parts = []
        for j in range(bb):
            lhs = jnp.concatenate([_seq_rows(grp, kcd, j, nh), _seq_rows(grp, qg, j, nh)], axis=0)
            parts.append(_dot(_expand_rows(lhs, nh, qs, GDN_D), state(j)))
            yield
        half = nh * qs
        ks = _from_seq_rows(grp, [p[:half] for p in parts], nh)
        qsv = _from_seq_rows(grp, [p[half:] for p in parts], nh)
        v_new = vw - ks
        o = qsv + _dot(attn, v_new)
        yield
        for j in range(bb):
            kgj = _expand_rows(_seq_rows(grp, kg, j, nh), nh, qs, GDN_D)
            g_last = jnp.exp(gc_end[j * qs:j * qs + 1, :])
            g_last = jnp.concatenate([jnp.broadcast_to(g_last[:, h:h + 1], (GDN_D, 1)) for h in range(nh)], axis=0)
            s_new = state(j) * g_last + _dot_tn(kgj, _seq_rows(grp, v_new, j, nh))
            yield
            for h in range(nh):
                s_ref[j, sub * nh + h] = s_new[h * GDN_D:(h + 1) * GDN_D]
        o = o * lax.rsqrt(jnp.mean(o * o, axis=-1, keepdims=True) + EPS) * nw_ref[...]
        return _unstack_heads(o, nh)

    o = jnp.concatenate(_interleave([stack(sub) for sub in range(GDN_SUB)]), axis=1)
    o_ref[...] = (o * _silu(z)).astype(o_ref.dtype)


def _gdn_layout(x):
    per_step = GDN_G * GDN_SUB
    main = GDN_CONV_DIM + GDN_V
    b, a = x[..., main:main + GDN_HEADS], x[..., main + GDN_HEADS:main + 2 * GDN_HEADS]
    parts = [x[..., :main]]
    for g in range(GDN_HEADS // per_step):
        hs = slice(g * per_step, (g + 1) * per_step)
        parts += [b[..., hs], a[..., hs], jnp.zeros(x.shape[:-1] + (LANES - 2 * per_step,), x.dtype)]
    y = jnp.concatenate(parts, axis=-1)
    return jnp.pad(y, [(0, 0)] * (x.ndim - 1) + [(0, -y.shape[-1] % MM_TN)])


def _gdn(grp, proj, hist, s0, prm):
    per_step = GDN_G * GDN_SUB
    width = per_step * GDN_D
    ng = GDN_HEADS // per_step
    ba_blk = (GDN_CONV_DIM + GDN_V) // LANES
    toks = [
        _Tok(proj, width, lambda g: g, hist, lambda g: g),
        _Tok(proj, width, lambda g: ng + g, hist, lambda g: ng + g),
        _Tok(proj, width, lambda g: 2 * ng + g, hist, lambda g: 2 * ng + g),
        _Tok(proj, width, lambda g: 3 * ng + g),
        _Tok(proj, LANES, lambda g: ba_blk + g),
    ]
    ops, specs = _mixer_specs(grp, toks)
    cw = prm["conv_w"]
    grouped = lambda x: x.reshape(ng, 1, per_step)
    params = [
        (cw, _param_spec(width, lambda g: g)(CONV_W)),
        (cw, _param_spec(width, lambda g: ng + g)(CONV_W)),
        (cw, _param_spec(width, lambda g: 2 * ng + g)(CONV_W)),
        (grouped(prm["A_log"]), pl.BlockSpec((1, 1, per_step), lambda b, g, c: (g, 0, 0))),
        (grouped(prm["dt_bias"]), pl.BlockSpec((1, 1, per_step), lambda b, g, c: (g, 0, 0))),
        (prm["norm_w"].reshape(1, GDN_D), pl.BlockSpec((1, GDN_D), lambda b, g, c: (0, 0))),
    ]
    ops += [p for p, _ in params]
    specs += [s for _, s in params]
    st_spec = pl.BlockSpec((grp.bb, per_step, GDN_D, GDN_D), lambda b, g, c: (b, g, 0, 0))
    ops.append(s0)
    specs.append(st_spec)
    rows = grp.B * grp.L
    nc = grp.nc
    return pl.pallas_call(
        functools.partial(_gdn_body, grp=grp),
        grid=(grp.B // grp.bb, ng, nc),
        in_specs=specs,
        out_specs=[pl.BlockSpec((MIX_ROWS, width), lambda b, g, c: (b * nc + c, g)), st_spec],
        out_shape=[jax.ShapeDtypeStruct((rows, GDN_V), BF16), jax.ShapeDtypeStruct(s0.shape, F32)],
        compiler_params=_cp("parallel", "parallel", "arbitrary", flags=MIXER_FLAGS),
        name="gdn",
    )(*ops)


def _xattn_body(q_ref, k_ref, v_ref, o_ref, *, nseq, lq):
    scale = XA_HD ** -0.5
    for j in range(nseq):
        qrows = slice(j * lq, (j + 1) * lq)
        mrows = slice(j * N_MEM, (j + 1) * N_MEM)
        outs = []
        for h in range(XA_HEADS):
            sl = slice(h * XA_HD, (h + 1) * XA_HD)
            s = _dot_nt(q_ref[qrows, sl], k_ref[mrows, sl]) * scale
            s = s - jnp.max(s, axis=-1, keepdims=True)
            p = jnp.exp(s)
            p = p / jnp.sum(p, axis=-1, keepdims=True)
            outs.append(_dot(p, v_ref[mrows, sl]))
        o_ref[qrows, :] = jnp.concatenate(outs, axis=1).astype(o_ref.dtype)


def _xattn_packed_body(q_ref, k_ref, v_ref, o_ref, *, nseq, lq):
    scale = XA_HD ** -0.5
    rows, cols = XA_HEADS * lq, N_MEM * XA_HEADS
    own_head = (_iota2((rows, cols), 0) // lq) == (_iota2((rows, cols), 1) % XA_HEADS)
    for j in range(nseq):
        q = q_ref[j * lq:(j + 1) * lq, :].astype(F32)
        qh = _stack_heads(q, XA_HEADS, XA_HD)
        mrows = slice(j * cols, (j + 1) * cols)
        s = jnp.where(own_head, _dot_nt(qh, k_ref[mrows, :]) * scale, -1e30)
        p = jnp.where(own_head, jnp.exp(s - jnp.max(s, axis=-1, keepdims=True)), 0.0)
        o = _dot(p, v_ref[mrows, :]) / jnp.sum(p, axis=-1, keepdims=True)
        o_ref[j * lq:(j + 1) * lq, :] = _unstack_heads(o, XA_HEADS).astype(o_ref.dtype)


def _xattn(grp, q, mem_k, mem_v, mem_row0):
    rows = grp.B * grp.L
    if grp.L >= XA_TQ:
        nseq, lq = 1, XA_TQ
        per = grp.L // XA_TQ
        body = _xattn_body
        mem_spec = pl.BlockSpec((N_MEM, XA_DIM), lambda i: (mem_row0 // N_MEM + i // per, 0))
    else:
        nseq, lq = MIX_ROWS // grp.L, grp.L
        body = _xattn_packed_body
        blk = nseq * N_MEM * XA_HEADS
        mem_spec = pl.BlockSpec((blk, XA_HD), lambda i: (mem_row0 * XA_HEADS // blk + i, 0))
    q_spec = pl.BlockSpec((nseq * lq, XA_DIM), lambda i: (i, 0))
    return pl.pallas_call(
        functools.partial(body, nseq=nseq, lq=lq),
        grid=(rows // (nseq * lq),),
        in_specs=[q_spec, mem_spec, mem_spec],
        out_specs=q_spec,
        out_shape=jax.ShapeDtypeStruct((rows, XA_DIM), BF16),
        compiler_params=_cp("parallel"),
        name="xattn",
    )(q, mem_k, mem_v)


def _ffn_in_body(*refs, grp, tiles_per_seq):
    it = iter(refs)
    a_ref, g_ref, wg_ref, wu_ref, hist_ref, cw_ref, cb_ref, act_ref, tail_ref, an_ref = (next(it) for _ in range(10))
    carry_ref = None if grp.embedded else next(it)
    i, j = pl.program_id(0), pl.program_id(1)

    @pl.when(j == 0)
    def _():
        x = a_ref[...]
        y = x * lax.rsqrt(jnp.mean(x * x, axis=-1, keepdims=True) + EPS)
        an_ref[...] = (y * g_ref[...]).astype(BF16)

    a = an_ref[...]
    gate = jnp.dot(a, wg_ref[...], preferred_element_type=F32)
    up = jnp.dot(a, wu_ref[...], preferred_element_type=F32)
    if grp.embedded:
        halo = hist_ref[...]
        tail_ref[...] = gate
    else:
        halo = jnp.where(i % tiles_per_seq == 0, hist_ref[...], carry_ref[j])
        last = gate[gate.shape[0] - SUBLANES:]
        carry_ref[j] = last
        tail_ref[...] = last
    conv = _conv(grp, gate, halo, cw_ref, cb_ref[...], FFN_CONV_W)
    act_ref[...] = (_silu(conv) * up).astype(act_ref.dtype)


def _ffn_in(grp, h, gain, w_in, hist, conv_w, conv_b, tm=MM_TM, tn=MM_TN):
    assert grp.embedded or grp.L % tm == 0
    rows, k = h.shape
    nct = D_FF // tn
    tiles_per_seq = max(grp.L // tm, 1)
    hrows = tm if grp.embedded else SUBLANES
    tile_idx = lambda i, j: (i, j)
    hist_idx = tile_idx if grp.embedded else (lambda i, j: (i // tiles_per_seq, j))
    specs = [
        pl.BlockSpec((tm, k), lambda i, j: (i, 0)),
        pl.BlockSpec((1, k), lambda i, j: (0, 0)),
        pl.BlockSpec((k, tn), lambda i, j: (0, j)),
        pl.BlockSpec((k, tn), lambda i, j: (0, nct + j)),
        pl.BlockSpec((hrows, tn), hist_idx),
        pl.BlockSpec((FFN_CONV_W, tn), lambda i, j: (0, j)),
        pl.BlockSpec((1, tn), lambda i, j: (0, j)),
    ]
    scratch = [pltpu.VMEM((tm, k), BF16)]
    if not grp.embedded:
        scratch.append(pltpu.VMEM((nct, SUBLANES, tn), F32))
    act, tails = pl.pallas_call(
        functools.partial(_ffn_in_body, grp=grp, tiles_per_seq=tiles_per_seq),
        grid=(rows // tm, nct),
        in_specs=specs,
        out_specs=[pl.BlockSpec((tm, tn), tile_idx), pl.BlockSpec((hrows, tn), tile_idx)],
        out_shape=[jax.ShapeDtypeStruct((rows, D_FF), BF16), jax.ShapeDtypeStruct((rows // tm * hrows, D_FF), F32)],
        scratch_shapes=scratch,
        compiler_params=_cp("arbitrary", "arbitrary"),
        name="ffn_in",
    )(h, gain.reshape(1, k), w_in, w_in, hist, conv_w, conv_b.reshape(1, D_FF))
    if not grp.embedded:
        tails = tails.reshape(grp.B, tiles_per_seq, SUBLANES, D_FF)[:, -1].reshape(grp.B * SUBLANES, D_FF)
    return act, tails


def _history(grp, buf, width):
    b, k, c = buf.shape
    if grp.embedded:
        h = jnp.pad(buf, ((0, 0), (grp.first - k, grp.L - grp.first), (0, 0)))
        return h.reshape(b * grp.L, c)
    return jnp.pad(buf, ((0, 0), (SUBLANES - k, 0), (0, 0))).reshape(b * SUBLANES, c)


def _tail(grp, x2d, k, cols):
    x = x2d.reshape(grp.B, grp.L, x2d.shape[1])
    return x[:, grp.L - k:, cols]


def _cast_pad(w, mult):
    n = w.shape[1]
    return jnp.pad(w.astype(BF16), ((0, 0), (0, -n % mult)))


def _run_group(grp, x2d, mem_k, mem_v, mem_row0, st, wts, prm):
    h = x2d
    out = {}
    proj = _matmul(h, wts["in_ab"], gain=prm["norm_mix"][0])
    y_rw, s_rw = _rwkv(grp, proj, _history(grp, st["rw_shift"][:, None, :], RW_PROJ), st["rwkv"], prm["rw"])
    y_ssd, s_ssd = _ssd(grp, proj, _history(grp, st["ssd_conv"], SSD_CONV_DIM),
                        st["ssd"].reshape(grp.B, SSD_GROUPS, SSD_GDIM, SSD_STATE), prm["ssd"])
    out["rwkv"] = s_rw
    out["rw_shift"] = _tail(grp, proj, 1, slice(0, RW_PROJ))[:, 0]
    out["ssd"] = s_ssd.reshape(st["ssd"].shape)
    out["ssd_conv"] = _tail(grp, proj, CONV_W - 1, slice(RW_PROJ + SSD_DIM, RW_PROJ + SSD_DIM + SSD_CONV_DIM))
    h = _matmul(jnp.concatenate([y_rw, y_ssd], axis=1), wts["out_ab"], res=h)
    ffn_bufs = []
    for l in range(2):
        if l == 1:
            proj = _matmul(h, wts["in_c"], gain=prm["norm_mix"][1])
            y_c, s_gdn = _gdn(grp, proj, _history(grp, st["gdn_conv"], GDN_CONV_DIM), st["gdn"], prm["gdn"])
            out["gdn"] = s_gdn
            out["gdn_conv"] = _tail(grp, proj, CONV_W - 1, slice(0, GDN_CONV_DIM))
            h = _matmul(y_c, wts["out_c"], res=h)
        q = _matmul(h, wts["xq"][l], gain=prm["norm_xa"][l], out_dtype=BF16)
        o = _xattn(grp, q, mem_k, mem_v, mem_row0[l])
        h = _matmul(o, wts["xo"][l], res=h)
        act, gate_tail = _ffn_in(grp, h, prm["norm_ffn"][l], wts["ffn_in"][l], _history(grp, st["ffn_conv"][l], D_FF),
                                 prm["ffn_conv_w"][l], prm["ffn_conv_b"][l])
        ffn_bufs.append(gate_tail.reshape(grp.B, SUBLANES, D_FF)[:, SUBLANES - (FFN_CONV_W - 1):])
        h = _matmul(act, wts["ffn_out"][l], res=h)
    out["ffn_conv"] = jnp.stack(ffn_bufs)
    out["y"] = _rmsnorm(h, prm["norm_final"])
    return out


def kernel(x_prompt, x_sample, mem_prompt, state_rwkv, state_rwkv_shift, state_ssd, state_ssd_conv, state_gdn,
           state_gdn_conv, state_ffn_conv, cache_mem_k, cache_mem_v, norm_mix, norm_xa, norm_mem, norm_ffn,
           norm_final, w_in_ab, rw_mu, rw_w0, rw_w_up, rw_a0, rw_a_up, rw_g_up, rw_k_k, rw_k_a, rw_r_k, rw_gn_w,
           rw_gn_b, ssd_conv_w, ssd_conv_b, ssd_dt_bias, ssd_A_log, ssd_D, ssd_norm_w, w_out_ab, w_in_c,
           gdn_conv_w, gdn_A_log, gdn_dt_bias, gdn_norm_w, w_out_c, w_xq, w_xk, w_xv, w_xo, ffn_w_in, ffn_conv_w,
           ffn_conv_b, ffn_w_out):
    bp, lp, _ = x_prompt.shape
    bs, ls, _ = x_sample.shape
    depth = norm_mix.shape[0]
    assert depth == 2 and w_in_ab.shape[0] == 1 and w_in_c.shape[0] == 1
    assert lp % MIX_ROWS == 0 and lp % XA_TQ == 0 and lp % MM_TM == 0
    pad_rows = SUBLANES - ls
    assert CONV_W - 1 <= pad_rows and CONV_W - 1 <= ls and MIX_ROWS % SUBLANES == 0
    gp = Group(B=bp, L=lp, first=0, bb=1)
    gs = Group(B=bs, L=SUBLANES, first=pad_rows, bb=MIX_ROWS // SUBLANES)
    assert bs % gs.bb == 0 and (bs * SUBLANES) % MM_TM == 0 and (bp * lp) % MM_TM == 0

    per_layer = lambda w: [w[l].astype(BF16) for l in range(depth)]
    wts = dict(
        in_ab=_cast_pad(w_in_ab[0], MM_TN), out_ab=w_out_ab[0].astype(BF16),
        in_c=_gdn_layout(w_in_c[0]).astype(BF16), out_c=w_out_c[0].astype(BF16),
        xq=per_layer(w_xq), xo=per_layer(w_xo), ffn_in=per_layer(ffn_w_in), ffn_out=per_layer(ffn_w_out),
    )
    prm = dict(
        norm_mix=norm_mix, norm_xa=norm_xa, norm_ffn=norm_ffn, norm_final=norm_final,
        ffn_conv_w=ffn_conv_w, ffn_conv_b=ffn_conv_b,
        rw=dict(mu=rw_mu[0], w0=rw_w0[0], w_up=rw_w_up[0], a0=rw_a0[0], a_up=rw_a_up[0], g_up=rw_g_up[0],
                k_k=rw_k_k[0], k_a=rw_k_a[0], r_k=rw_r_k[0], gn_w=rw_gn_w[0], gn_b=rw_gn_b[0]),
        ssd=dict(conv_w=ssd_conv_w[0], conv_b=ssd_conv_b[0], dt_bias=ssd_dt_bias[0], A_log=ssd_A_log[0],
                 D=ssd_D[0], norm_w=ssd_norm_w[0]),
        gdn=dict(conv_w=gdn_conv_w[0], A_log=gdn_A_log[0], dt_bias=gdn_dt_bias[0], norm_w=gdn_norm_w[0]),
    )

    mem2d = mem_prompt.reshape(bp * N_MEM, D_MODEL)
    mem_tm = min(MM_TM, bp * N_MEM)
    mk, mv = [], []
    for l in range(depth):
        mk.append(_matmul(mem2d, w_xk[l].astype(BF16), gain=norm_mem[l], tm=mem_tm))
        mv.append(_matmul(mem2d, w_xv[l].astype(BF16), gain=norm_mem[l], tm=mem_tm))
    mem_k_p = jnp.stack(mk)
    mem_v_p = jnp.stack(mv)

    zeros = lambda *s: jnp.zeros(s, F32)
    st_p = dict(
        rwkv=zeros(bp, RW_HEADS, RW_HD, RW_HD), rw_shift=zeros(bp, RW_PROJ),
        ssd=zeros(bp, SSD_GROUPS, SSD_HPG, SSD_HD, SSD_STATE), ssd_conv=zeros(bp, CONV_W - 1, SSD_CONV_DIM),
        gdn=zeros(bp, GDN_HEADS, GDN_D, GDN_D), gdn_conv=zeros(bp, CONV_W - 1, GDN_CONV_DIM),
        ffn_conv=zeros(depth, bp, FFN_CONV_W - 1, D_FF),
    )
    rp = _run_group(gp, x_prompt.reshape(bp * lp, D_MODEL), mem_k_p.reshape(depth * bp * N_MEM, XA_DIM),
                    mem_v_p.reshape(depth * bp * N_MEM, XA_DIM), [l * bp * N_MEM for l in range(depth)],
                    st_p, wts, prm)

    st_s = dict(rwkv=state_rwkv[0], rw_shift=state_rwkv_shift[0], ssd=state_ssd[0], ssd_conv=state_ssd_conv[0],
                gdn=state_gdn[0], gdn_conv=state_gdn_conv[0], ffn_conv=state_ffn_conv)
    xs = jnp.pad(x_sample, ((0, 0), (pad_rows, 0), (0, 0))).reshape(bs * SUBLANES, D_MODEL)
    rs = _run_group(gs, xs, cache_mem_k.reshape(depth * bs * N_MEM * XA_HEADS, XA_HD),
                    cache_mem_v.reshape(depth * bs * N_MEM * XA_HEADS, XA_HD), [l * bs * N_MEM for l in range(depth)],
                    st_s, wts, prm)

    y_p = rp["y"].reshape(bp, lp, D_MODEL)
    y_s = rs["y"].reshape(bs, SUBLANES, D_MODEL)[:, pad_rows:]
    lead = lambda x: x[None]
    mem_shape = (depth, bp, N_MEM, XA_HEADS, XA_HD)
    return (y_p, y_s, lead(rp["rwkv"]), lead(rs["rwkv"]), lead(rp["rw_shift"]), lead(rs["rw_shift"]),
            lead(rp["ssd"]), lead(rs["ssd"]), lead(rp["ssd_conv"]), lead(rs["ssd_conv"]),
            lead(rp["gdn"]), lead(rs["gdn"]), lead(rp["gdn_conv"]), lead(rs["gdn_conv"]),
            rp["ffn_conv"], rs["ffn_conv"], mem_k_p.reshape(mem_shape), mem_v_p.reshape(mem_shape))
```

```python
import functools
from typing import NamedTuple

import jax
import jax.numpy as jnp
from jax import lax
from jax.experimental import pallas as pl
from jax.experimental.pallas import tpu as pltpu

F32 = jnp.float32
BF16 = jnp.bfloat16
HI = lax.Precision.HIGHEST

D_MODEL = 2048
EPS = 1e-6
RW_HEADS, RW_HD = 16, 64
RW_DIM = RW_HEADS * RW_HD
RW_LORA = 256
RW_PROJ = 3 * RW_DIM + RW_LORA
RW_GN_EPS = 6.4e-4
SSD_HEADS, SSD_HD, SSD_GROUPS, SSD_STATE = 16, 64, 2, 128
SSD_HPG = SSD_HEADS // SSD_GROUPS
SSD_DIM = SSD_HEADS * SSD_HD
SSD_GDIM = SSD_HPG * SSD_HD
SSD_CONV_DIM = SSD_DIM + 2 * SSD_GROUPS * SSD_STATE
SSD_PROJ = SSD_DIM + SSD_CONV_DIM + SSD_HEADS
AB_PROJ = RW_PROJ + SSD_PROJ
GDN_HEADS, GDN_D = 16, 128
GDN_V = GDN_HEADS * GDN_D
GDN_CONV_DIM = 3 * GDN_V
GDN_PROJ = GDN_CONV_DIM + GDN_V + 2 * GDN_HEADS
D_FF = 5632
XA_HEADS, XA_HD, N_MEM = 4, 128, 256
XA_DIM = XA_HEADS * XA_HD
CONV_W = 4
FFN_CONV_W = 3

AB_Z0 = 0
AB_X0 = SSD_DIM
AB_B0 = 2 * SSD_DIM
AB_C0 = AB_B0 + SSD_GROUPS * SSD_STATE
AB_DT0 = AB_C0 + SSD_GROUPS * SSD_STATE
AB_RW0 = 3 * RW_DIM
AB_COLS = 6656
assert AB_DT0 + 128 <= AB_RW0 and AB_RW0 + RW_PROJ <= AB_COLS and AB_RW0 % RW_DIM == 0

SUBLANES = 8
LANES = 128
VMEM_LIMIT = 56 * 1024 * 1024

MIX_ROWS = 64
INV_BLOCK = 16
MM_TM, MM_TN = 1024, 512
RW_HP = 2
RW_SUB = 4
GDN_G = 4
GDN_SUB = 4
XA_TQ = 512
NORM_TM = 512


class Group(NamedTuple):
    B: int
    L: int
    first: int
    bb: int

    @property
    def qs(self):
        return MIX_ROWS // self.bb

    @property
    def nc(self):
        return self.L // self.qs

    @property
    def embedded(self):
        return self.first > 0


MIXER_FLAGS = None


def _cp(*sem, flags=None):
    return pltpu.CompilerParams(dimension_semantics=sem, vmem_limit_bytes=VMEM_LIMIT, flags=flags)


def _dot(a, b):
    return jnp.dot(a.astype(BF16), b.astype(BF16), preferred_element_type=F32)


def _dot_nt(a, b):
    return lax.dot_general(a.astype(BF16), b.astype(BF16), (((1,), (1,)), ((), ())), preferred_element_type=F32)


def _dot_tn(a, b):
    return lax.dot_general(a.astype(BF16), b.astype(BF16), (((0,), (0,)), ((), ())), preferred_element_type=F32)


def _dot_hi(a, b):
    return jnp.dot(a, b, preferred_element_type=F32, precision=HI)


def _split_bf16(x):
    hi = x.astype(BF16)
    return hi, (x - hi.astype(F32)).astype(BF16)


def _dot_x3(a, b):
    ah, al = _split_bf16(a)
    bh, bl = _split_bf16(b)
    return jnp.dot(jnp.concatenate([ah, ah, al], axis=1), jnp.concatenate([bh, bl, bh], axis=0),
                   preferred_element_type=F32)


def _dot_nt_hi(a, b):
    return lax.dot_general(a, b, (((1,), (1,)), ((), ())), preferred_element_type=F32, precision=HI)


def _sigmoid(x):
    return 1.0 / (1.0 + jnp.exp(-x))


def _silu(x):
    return x * _sigmoid(x)


def _softplus(x):
    return jnp.maximum(x, 0.0) + jnp.log(1.0 + jnp.exp(-jnp.abs(x)))


def _iota2(shape, axis):
    return lax.broadcasted_iota(jnp.int32, shape, axis)


class _Masks(NamedTuple):
    incl: jax.Array
    strict: jax.Array
    eye: jax.Array
    cumsum: jax.Array
    valid: jax.Array


def _masks(grp, heads=1):
    r = MIX_ROWS
    n = heads * r
    ri, ci = _iota2((n, n), 0), _iota2((n, n), 1)
    same = (ri // grp.qs) == (ci // grp.qs)
    incl = same & (ci <= ri)
    strict = same & (ci < ri)
    si, sj = _iota2((2 * r, r), 0), _iota2((2 * r, r), 1)
    same_seq = ((si % r) // grp.qs) == (sj // grp.qs)
    cumsum = (same_seq & ((sj <= si) | (si >= r))).astype(F32)
    t = _iota2((r, 1), 0) % grp.qs
    return _Masks(incl, strict, (ri == ci).astype(F32), cumsum, t >= grp.first)


def _neumann(lm, rhs, nil, eye=None):
    n = lm.shape[0]
    p = None
    if rhs is None:
        t = eye + lm
        if nil > 2:
            p = _dot(lm, lm)
            yield
    elif nil > 2:
        both = _dot(lm, jnp.concatenate([lm, rhs], axis=1))
        yield
        p, t = both[:, :n], rhs + both[:, n:]
    else:
        t = rhs + _dot(lm, rhs)
        yield
    k = 2
    while k < nil:
        if 2 * k < nil:
            both = _dot(p, jnp.concatenate([p, t], axis=1))
            p, t = both[:, :n], t + both[:, n:]
        else:
            t = t + _dot(p, t)
        yield
        k *= 2
    return t


def _inv_unit_lower(lm, eye, qs):
    if qs <= INV_BLOCK:
        return (yield from _neumann(lm, None, qs, eye))
    n = lm.shape[0]
    diag = (_iota2((n, n), 0) // INV_BLOCK) == (_iota2((n, n), 1) // INV_BLOCK)
    d = jnp.where(diag, lm, 0.0)
    dinv = yield from _neumann(d, None, INV_BLOCK, eye)
    nm = _dot(dinv, lm - d)
    yield
    return (yield from _neumann(nm, dinv, qs // INV_BLOCK))


def _solve_unit_lower(lm, rhs, eye, qs):
    t = yield from _inv_unit_lower(lm, eye, qs)
    x = _dot(t, rhs)
    yield
    return x


def _interleave(gens):
    results = [None] * len(gens)
    active = list(range(len(gens)))
    while active:
        for i in list(active):
            try:
                next(gens[i])
            except StopIteration as stop:
                results[i] = stop.value
                active.remove(i)
    return results


def _stack_heads(x, heads, width):
    return jnp.concatenate([x[:, h * width:(h + 1) * width] for h in range(heads)], axis=0)


def _unstack_heads(x, heads):
    r = x.shape[0] // heads
    return jnp.concatenate([x[h * r:(h + 1) * r] for h in range(heads)], axis=1)


def _seq_rows(grp, x, j, heads):
    if grp.bb == 1:
        return x
    qs = grp.qs
    return jnp.concatenate([x[h * MIX_ROWS + j * qs:h * MIX_ROWS + (j + 1) * qs] for h in range(heads)], axis=0)


def _from_seq_rows(grp, parts, heads):
    if grp.bb == 1:
        return parts[0]
    qs = grp.qs
    return jnp.concatenate([p[h * qs:(h + 1) * qs] for h in range(heads) for p in parts], axis=0)


def _expand_rows(x, heads, rows_per_head, width):
    m = x.shape[0]
    if x.shape[1] == width:
        x = jnp.concatenate([x] * heads, axis=1)
    keep = ((_iota2((m, heads * width), 0) // rows_per_head) % heads) == (_iota2((m, heads * width), 1) // width)
    return jnp.where(keep, x, 0.0)


def _taps(grp, cur, halo, width):
    if grp.embedded:
        t = _iota2((cur.shape[0], 1), 0) % grp.qs
        full = jnp.where(t < grp.first, halo, cur)
        return [full] + [pltpu.roll(full, s, axis=0) for s in range(1, width)]
    full = jnp.concatenate([halo, cur], axis=0)
    return [cur] + [pltpu.roll(full, s, axis=0)[SUBLANES:] for s in range(1, width)]


def _conv(grp, cur, halo, w_ref, bias, width):
    taps = _taps(grp, cur, halo, width)
    y = taps[0] * w_ref[width - 1:width, :]
    for s in range(1, width):
        y = y + taps[s] * w_ref[width - 1 - s:width - s, :]
    return y if bias is None else y + bias


class _Tok(NamedTuple):
    arr: jax.Array
    width: int
    col: object
    hist: jax.Array = None
    hcol: object = None


def _mixer_specs(grp, toks):
    nc = grp.nc
    ops, specs = [], []
    for t in toks:
        ops.append(t.arr)
        specs.append(pl.BlockSpec((MIX_ROWS, t.width), lambda b, g, c, t=t: (b * nc + c, t.col(g))))
    for t in toks:
        if t.hist is None:
            continue
        if not grp.embedded:
            tile = _sublane_tile(t.arr.dtype)
            sub = MIX_ROWS // tile
            ops.append(t.arr)
            specs.append(pl.BlockSpec(
                (tile, t.width), lambda b, g, c, t=t, sub=sub: (jnp.maximum((b * nc + c) * sub - 1, 0), t.col(g))))
        ops.append(t.hist)
        hrows = MIX_ROWS if grp.embedded else SUBLANES
        specs.append(pl.BlockSpec((hrows, t.width), lambda b, g, c, t=t: (b, t.hcol(g))))
    return ops, specs


def _read_windows(grp, refs, n_plain, n_hist):
    c = pl.program_id(2)
    cur = [r[...].astype(F32) for r in refs[:n_plain]]
    rest = refs[n_plain:]
    halos = []
    for i in range(n_hist):
        if grp.embedded:
            halos.append(rest[i][...])
        else:
            prev = rest[2 * i][...]
            prev = prev[prev.shape[0] - SUBLANES:].astype(F32)
            halos.append(jnp.where(c == 0, rest[2 * i + 1][...], prev))
    used = n_hist if grp.embedded else 2 * n_hist
    return cur, halos, rest[used:]


def _sublane_tile(dtype):
    return SUBLANES * 4 // jnp.dtype(dtype).itemsize


def _param_spec(width, col):
    return lambda rows: pl.BlockSpec((rows, width), lambda b, g, c: (0, col(g)))


def _matmul_body(*refs, norm, has_res):
    it = iter(refs)
    a_ref = next(it)
    g_ref = next(it) if norm else None
    w_ref = next(it)
    res_ref = next(it) if has_res else None
    o_ref = next(it)
    if norm:
        an_ref = next(it)

        @pl.when(pl.program_id(1) == 0)
        def _():
            x = a_ref[...]
            y = x * lax.rsqrt(jnp.mean(x * x, axis=-1, keepdims=True) + EPS)
            an_ref[...] = (y * g_ref[...]).astype(BF16)

        a = an_ref[...]
    else:
        a = a_ref[...]
    acc = jnp.dot(a, w_ref[...], preferred_element_type=F32)
    if has_res:
        acc = acc + res_ref[...]
    o_ref[...] = acc.astype(o_ref.dtype)


def _matmul(a, w, *, gain=None, res=None, out_dtype=F32, tm=MM_TM, tn=MM_TN):
    m, k = a.shape
    n = w.shape[1]
    norm = gain is not None
    ops = [a]
    specs = [pl.BlockSpec((tm, k), lambda i, j: (i, 0))]
    if norm:
        ops.append(gain.reshape(1, k))
        specs.append(pl.BlockSpec((1, k), lambda i, j: (0, 0)))
    ops.append(w)
    specs.append(pl.BlockSpec((k, tn), lambda i, j: (0, j)))
    if res is not None:
        ops.append(res)
        specs.append(pl.BlockSpec((tm, tn), lambda i, j: (i, j)))
    return pl.pallas_call(
        functools.partial(_matmul_body, norm=norm, has_res=res is not None),
        grid=(m // tm, n // tn),
        in_specs=specs,
        out_specs=pl.BlockSpec((tm, tn), lambda i, j: (i, j)),
        out_shape=jax.ShapeDtypeStruct((m, n), out_dtype),
        scratch_shapes=[pltpu.VMEM((tm, k), BF16)] if norm else [],
        compiler_params=_cp("parallel", "arbitrary"),
        name="matmul",
    )(*ops)


def _rmsnorm_body(x_ref, g_ref, o_ref):
    x = x_ref[...]
    y = x * lax.rsqrt(jnp.mean(x * x, axis=-1, keepdims=True) + EPS)
    o_ref[...] = (y * g_ref[...]).astype(o_ref.dtype)


def _rmsnorm(x, gain, out_dtype=F32, tm=NORM_TM):
    m, k = x.shape
    return pl.pallas_call(
        _rmsnorm_body,
        grid=(m // tm,),
        in_specs=[pl.BlockSpec((tm, k), lambda i: (i, 0)), pl.BlockSpec((1, k), lambda i: (0, 0))],
        out_specs=pl.BlockSpec((tm, k), lambda i: (i, 0)),
        out_shape=jax.ShapeDtypeStruct((m, k), out_dtype),
        compiler_params=_cp("parallel"),
        name="rmsnorm",
    )(x, gain.reshape(1, k))


def _rwkv_body(*refs, grp):
    r_rows = MIX_ROWS
    qs, bb, nc = grp.qs, grp.bb, grp.nc
    nh = 2 * RW_HP
    sw = nh * RW_HD
    width = RW_SUB * sw
    n = nh * r_rows
    cur, halos, rest = _read_windows(grp, refs, 4, 4)
    (mu_r, mu_k, mu_v, mu_lo, w0_ref, a0_ref, lora_ref, kk_ref, ka_ref, rk_ref, gnw_ref, gnb_ref,
     s0_ref, y_ref, s_ref, sbig_ref) = rest
    c = pl.program_id(2)

    @pl.when(c == 0)
    def _():
        for j in range(bb):
            for sub in range(RW_SUB):
                st = jnp.concatenate([s0_ref[j, sub * nh + h] for h in range(nh)], axis=0)
                sbig_ref[j, sub] = _expand_rows(st, nh, RW_HD, RW_HD)

    m = _masks(grp, nh)

    def shifted(i, mu_ref):
        x = cur[i]
        prev = _taps(grp, x, halos[i], 2)[1]
        return x + (prev - x) * mu_ref[...]

    r = shifted(0, mu_r)
    k = shifted(1, mu_k)
    v = shifted(2, mu_v)
    lo = shifted(3, mu_lo)
    lora_in = jnp.concatenate([jnp.tanh(lo[:, :64]), lo[:, 64:128], _sigmoid(lo[:, 128:256])], axis=1)
    lora = _dot_x3(lora_in, lora_ref[0])
    w = -_softplus(-(w0_ref[...] + lora[:, :width])) - 0.5
    lw = jnp.where(m.valid, -jnp.exp(w), 0.0)
    a = _sigmoid(a0_ref[...] + lora[:, width:2 * width])
    g = lora[:, 2 * width:]

    bd = ((_iota2((sw, sw), 0) // RW_HD) == (_iota2((sw, sw), 1) // RW_HD)).astype(F32)

    def head_sums(x):
        return _unstack_heads(_dot_x3(_stack_heads(x, RW_SUB, sw), bd), RW_SUB)

    kx = k * kk_ref[...]
    kp = jnp.where(m.valid, k * (1.0 + (a - 1.0) * ka_ref[...]), 0.0)
    sums = head_sums(jnp.concatenate([kx * kx, r * kp * rk_ref[...]], axis=0))
    kkn = kx * lax.rsqrt(sums[:r_rows] + 1e-6)
    bonus = sums[r_rows:] * v
    at = jnp.where(m.valid, -kkn, 0.0)
    bt = jnp.where(m.valid, kkn * a, 0.0)

    cums = _dot_hi(m.cumsum, lw)
    cum, cum_end = cums[:r_rows], cums[r_rows:]
    e_neg = jnp.exp(-cum)
    e_end = jnp.exp(cum_end - cum)
    p_end = jnp.exp(cum_end)
    a_til, r_til = at * jnp.exp(cum - lw), r * jnp.exp(cum)
    b_til, k_til = bt * e_neg, kp * e_neg
    b_hat, k_hat = bt * e_end, kp * e_end

    def stack(sub):
        lanes = slice(sub * sw, (sub + 1) * sw)

        def tile(x):
            return _expand_rows(jnp.concatenate([x[:, lanes]] * nh, axis=0), nh, r_rows, RW_HD)

        ar = jnp.concatenate([tile(a_til), tile(r_til)], axis=0)
        bk = jnp.concatenate([tile(b_til), tile(k_til)], axis=0)
        bk_end = jnp.concatenate([tile(b_hat), tile(k_hat)], axis=0)
        v_exp = tile(v)
        gm = _dot_nt(ar, bk)
        yield
        m_ab = jnp.where(m.strict, gm[:n, :n], 0.0)
        m_ak = jnp.where(m.strict, gm[:n, n:], 0.0)
        m_r = jnp.concatenate([jnp.where(m.incl, gm[n:, :n], 0.0), jnp.where(m.incl, gm[n:, n:], 0.0)], axis=1)

        def seq2(x, j):
            if bb == 1:
                return x
            return jnp.concatenate([_seq_rows(grp, x[:n], j, nh), _seq_rows(grp, x[n:], j, nh)], axis=0)

        parts = []
        for j in range(bb):
            parts.append(_dot_nt(seq2(ar, j), sbig_ref[j, sub]))
            yield
        half = nh * qs
        as_a = _from_seq_rows(grp, [p[:half] for p in parts], nh)
        as_r = _from_seq_rows(grp, [p[half:] for p in parts], nh)
        rhs = as_a + _dot(m_ak, v_exp)
        yield
        u = yield from _solve_unit_lower(m_ab, rhs, m.eye, qs)
        uv = jnp.concatenate([u, v_exp], axis=0)
        y_exp = as_r + _dot(m_r, uv)
        yield
        y = y_exp[:r_rows]
        for h in range(1, nh):
            y = y + y_exp[h * r_rows:(h + 1) * r_rows]
        for j in range(bb):
            sbig_ref[j, sub] = (sbig_ref[j, sub] * p_end[j * qs:j * qs + 1, lanes]
                                + _dot_tn(seq2(uv, j), seq2(bk_end, j)))
            yield
        return y

    y = jnp.concatenate(_interleave([stack(sub) for sub in range(RW_SUB)]), axis=1)

    @pl.when(c == nc - 1)
    def _():
        for j in range(bb):
            for sub in range(RW_SUB):
                sb = sbig_ref[j, sub]
                for h in range(nh):
                    s_ref[j, sub * nh + h] = sb[h * RW_HD:(h + 1) * RW_HD, h * RW_HD:(h + 1) * RW_HD]

    mean = head_sums(y) * (1.0 / RW_HD)
    d = y - mean
    var = head_sums(d * d) * (1.0 / RW_HD)
    out = d * lax.rsqrt(var + RW_GN_EPS) * gnw_ref[...] + gnb_ref[...] + bonus
    y_ref[...] = (out * g).astype(y_ref.dtype)


def _rwkv(grp, proj, hist, s0, prm):
    width = RW_SUB * RW_HP * LANES
    ng = RW_DIM // width
    lora_blk = 3 * RW_DIM // RW_LORA
    p0 = AB_RW0 // width
    toks = [
        _Tok(proj, width, lambda g: p0 + g, hist, lambda g: g),
        _Tok(proj, width, lambda g: p0 + ng + g, hist, lambda g: ng + g),
        _Tok(proj, width, lambda g: p0 + 2 * ng + g, hist, lambda g: 2 * ng + g),
        _Tok(proj, RW_LORA, lambda g: AB_RW0 // RW_LORA + lora_blk, hist, lambda g: lora_blk),
    ]
    ops, specs = _mixer_specs(grp, toks)
    head = _param_spec(width, lambda g: g)
    lora_w = jnp.zeros((ng, RW_LORA, 3 * width), F32)
    for i, (name, r0, r1) in enumerate((("w_up", 0, 64), ("a_up", 64, 128), ("g_up", 128, 256))):
        blk = prm[name].reshape(r1 - r0, ng, width).transpose(1, 0, 2)
        lora_w = lora_w.at[:, r0:r1, i * width:(i + 1) * width].set(blk)
    mu = prm["mu"].reshape(1, RW_PROJ)
    flat = lambda x: x.reshape(1, RW_DIM)
    params = [
        (mu, pl.BlockSpec((1, width), lambda b, g, c: (0, g))),
        (mu, pl.BlockSpec((1, width), lambda b, g, c: (0, ng + g))),
        (mu, pl.BlockSpec((1, width), lambda b, g, c: (0, 2 * ng + g))),
        (mu, pl.BlockSpec((1, RW_LORA), lambda b, g, c: (0, lora_blk))),
        (flat(prm["w0"]), head(1)), (flat(prm["a0"]), head(1)),
        (lora_w, pl.BlockSpec((1, RW_LORA, 3 * width), lambda b, g, c: (g, 0, 0))),
        (flat(prm["k_k"]), head(1)), (flat(prm["k_a"]), head(1)), (flat(prm["r_k"]), head(1)),
        (flat(prm["gn_w"]), head(1)), (flat(prm["gn_b"]), head(1)),
    ]
    ops += [p for p, _ in params]
    specs += [s for _, s in params]
    st_spec = pl.BlockSpec((grp.bb, RW_SUB * 2 * RW_HP, RW_HD, RW_HD), lambda b, g, c: (b, g, 0, 0))
    ops.append(s0)
    specs.append(st_spec)
    rows = grp.B * grp.L
    nc = grp.nc
    return pl.pallas_call(
        functools.partial(_rwkv_body, grp=grp),
        grid=(grp.B // grp.bb, ng, nc),
        in_specs=specs,
        out_specs=[pl.BlockSpec((MIX_ROWS, width), lambda b, g, c: (b * nc + c, g)), st_spec],
        out_shape=[jax.ShapeDtypeStruct((rows, RW_DIM), BF16), jax.ShapeDtypeStruct(s0.shape, F32)],
        scratch_shapes=[pltpu.VMEM((grp.bb, RW_SUB, RW_HP * LANES, RW_HP * LANES), F32)],
        compiler_params=_cp("parallel", "parallel", "arbitrary", flags=MIXER_FLAGS),
        name="rwkv7",
    )(*ops)


def _ssd_body(*refs, grp):
    r_rows = MIX_ROWS
    qs, bb = grp.qs, grp.bb
    nh, hd, gdim = SSD_HPG, SSD_HD, SSD_GDIM
    cur, halos, rest = _read_windows(grp, refs, 5, 3)
    cw_x, cw_b, cw_c, cb_x, cb_b, cb_c, dtb_ref, alog_ref, dskip_ref, nw_ref, h0_ref, y_ref, h_ref = rest
    x_raw, b_raw, c_raw, z, dt_blk = cur
    c = pl.program_id(2)

    @pl.when(c == 0)
    def _():
        h_ref[...] = h0_ref[...]

    m = _masks(grp)
    xs = _silu(_conv(grp, x_raw, halos[0], cw_x, cb_x[...], CONV_W))
    bm = _silu(_conv(grp, b_raw, halos[1], cw_b, cb_b[...], CONV_W))
    cm = _silu(_conv(grp, c_raw, halos[2], cw_c, cb_c[...], CONV_W))

    dt = jnp.where(m.valid, _softplus(dt_blk[:, :SSD_HEADS] + dtb_ref[...]), 0.0)
    cums = _dot_hi(m.cumsum, dt * (-jnp.exp(alog_ref[...])))
    acs, acs_end = cums[:r_rows], cums[r_rows:]
    eye_h = (_iota2((SSD_HEADS, SSD_HEADS), 0) == _iota2((SSD_HEADS, SSD_HEADS), 1)).astype(F32)
    acs_row = _dot_nt_hi(eye_h, acs)
    ei, ej = _iota2((3 * SSD_HEADS, 3 * SSD_DIM), 0), _iota2((3 * SSD_HEADS, 3 * SSD_DIM), 1)
    spread = ((ei // SSD_HEADS == ej // SSD_DIM) & (ei % SSD_HEADS == (ej % SSD_DIM) // hd)).astype(F32)
    lanes = _dot_hi(jnp.concatenate([dt, acs_end - acs, acs], axis=1), spread)
    xd = xs * lanes[:, :SSD_DIM]
    xd_dec = xd * jnp.exp(lanes[:, SSD_DIM:2 * SSD_DIM])
    acs_lanes = lanes[:, 2 * SSD_DIM:]

    def group(g):
        gl = slice(g * gdim, (g + 1) * gdim)
        bmg, cmg = bm[:, g * SSD_STATE:(g + 1) * SSD_STATE], cm[:, g * SSD_STATE:(g + 1) * SSD_STATE]
        cb = _dot_nt(cmg, bmg)
        yield
        yd = []
        for r in range(nh):
            h = g * nh + r
            diff = acs[:, h:h + 1] - acs_row[h:h + 1, :]
            lmat = jnp.where(m.incl, jnp.exp(jnp.where(m.incl, diff, 0.0)), 0.0)
            yd.append(_dot(cb * lmat, xd[:, h * hd:(h + 1) * hd]))
            yield
        y = jnp.concatenate(yd, axis=1)
        head_rows = (_iota2((gdim, SSD_HEADS), 0) // hd + g * nh == _iota2((gdim, SSD_HEADS), 1)).astype(F32)
        yoff = []
        for j in range(bb):
            rows = slice(j * qs, (j + 1) * qs)
            hj = h_ref[j, g]
            yoff.append(_dot_nt(cmg[rows], hj))
            yield
            end_col = jnp.exp(_dot_nt_hi(head_rows, acs_end[j * qs:j * qs + SUBLANES]))[:, :1]
            yield
            h_ref[j, g] = hj * end_col + _dot_tn(xd_dec[rows, gl], bmg[rows])
            yield
        yoff = yoff[0] if bb == 1 else jnp.concatenate(yoff, axis=0)
        y = y + yoff * jnp.exp(acs_lanes[:, gl]) + xs[:, gl] * dskip_ref[:, gl]
        yg = y * _silu(z[:, gl])
        yg = yg * lax.rsqrt(jnp.mean(yg * yg, axis=-1, keepdims=True) + EPS)
        return yg * nw_ref[:, gl]

    y = jnp.concatenate(_interleave([group(g) for g in range(SSD_GROUPS)]), axis=1)
    y_ref[...] = y.astype(y_ref.dtype)


def _ssd(grp, proj, hist, h0, prm):
    bc = SSD_GROUPS * SSD_STATE
    toks = [
        _Tok(proj, SSD_DIM, lambda g: AB_X0 // SSD_DIM, hist, lambda g: 0),
        _Tok(proj, bc, lambda g: AB_B0 // bc, hist, lambda g: SSD_DIM // bc),
        _Tok(proj, bc, lambda g: AB_C0 // bc, hist, lambda g: SSD_DIM // bc + 1),
        _Tok(proj, SSD_DIM, lambda g: AB_Z0 // SSD_DIM),
        _Tok(proj, LANES, lambda g: AB_DT0 // LANES),
    ]
    ops, specs = _mixer_specs(grp, toks)
    cw, cbias = prm["conv_w"], prm["conv_b"].reshape(1, SSD_CONV_DIM)
    conv_cols = [(SSD_DIM, lambda g: 0), (bc, lambda g: SSD_DIM // bc), (bc, lambda g: SSD_DIM // bc + 1)]
    params = [(cw, _param_spec(wd, col)(CONV_W)) for wd, col in conv_cols]
    params += [(cbias, _param_spec(wd, col)(1)) for wd, col in conv_cols]
    full = lambda n: pl.BlockSpec((1, n), lambda b, g, c: (0, 0))
    params += [
        (prm["dt_bias"].reshape(1, SSD_HEADS), full(SSD_HEADS)),
        (prm["A_log"].reshape(1, SSD_HEADS), full(SSD_HEADS)),
        (jnp.repeat(prm["D"], SSD_HD).reshape(1, SSD_DIM), full(SSD_DIM)),
        (prm["norm_w"].reshape(1, SSD_DIM), full(SSD_DIM)),
    ]
    ops += [p for p, _ in params]
    specs += [s for _, s in params]
    st_spec = pl.BlockSpec((grp.bb, SSD_GROUPS, SSD_GDIM, SSD_STATE), lambda b, g, c: (b, 0, 0, 0))
    ops.append(h0)
    specs.append(st_spec)
    rows = grp.B * grp.L
    nc = grp.nc
    return pl.pallas_call(
        functools.partial(_ssd_body, grp=grp),
        grid=(grp.B // grp.bb, 1, nc),
        in_specs=specs,
        out_specs=[pl.BlockSpec((MIX_ROWS, SSD_DIM), lambda b, g, c: (b * nc + c, 0)), st_spec],
        out_shape=[jax.ShapeDtypeStruct((rows, SSD_DIM), BF16), jax.ShapeDtypeStruct(h0.shape, F32)],
        compiler_params=_cp("parallel", "arbitrary", "arbitrary"),
        name="ssd",
    )(*ops)


def _gdn_body(*refs, grp):
    r_rows = MIX_ROWS
    qs, bb = grp.qs, grp.bb
    nh = GDN_G
    n = nh * r_rows
    cur, halos, rest = _read_windows(grp, refs, 5, 3)
    cw_q, cw_k, cw_v, alog_ref, dtb_ref, nw_ref, s0_ref, o_ref, s_ref = rest
    q_raw, k_raw, v_raw, z, ba = cur
    c = pl.program_id(2)

    @pl.when(c == 0)
    def _():
        s_ref[...] = s0_ref[...]

    m = _masks(grp, nh)
    q_all = _silu(_conv(grp, q_raw, halos[0], cw_q, None, CONV_W))
    k_all = _silu(_conv(grp, k_raw, halos[1], cw_k, None, CONV_W))
    v_all = _silu(_conv(grp, v_raw, halos[2], cw_v, None, CONV_W))
    per_step = nh * GDN_SUB
    eye_g = (_iota2((SUBLANES, nh), 0) == _iota2((SUBLANES, nh), 1)).astype(F32)
    col = lambda x: jnp.concatenate([x[:, h:h + 1] for h in range(nh)], axis=0)

    def stack(sub):
        lanes = slice(sub * nh * GDN_D, (sub + 1) * nh * GDN_D)
        heads = slice(sub * nh, (sub + 1) * nh)
        q = _stack_heads(q_all[:, lanes], nh, GDN_D)
        k = _stack_heads(k_all[:, lanes], nh, GDN_D)
        v = _stack_heads(v_all[:, lanes], nh, GDN_D)
        q = q * lax.rsqrt(jnp.sum(q * q, axis=-1, keepdims=True) + 1e-6) * (GDN_D ** -0.5)
        k = k * lax.rsqrt(jnp.sum(k * k, axis=-1, keepdims=True) + 1e-6)

        beta = jnp.where(m.valid, _sigmoid(ba[:, heads]), 0.0)
        a_raw = ba[:, per_step + sub * nh:per_step + (sub + 1) * nh]
        gg = jnp.where(m.valid, -jnp.exp(alog_ref[0][:, heads]) * _softplus(a_raw + dtb_ref[0][:, heads]), 0.0)
        gcs = _dot_hi(m.cumsum, gg)
        yield
        gc, gc_end = gcs[:r_rows], gcs[r_rows:]
        gc_row = _dot_nt_hi(eye_g, gc)
        yield

        bcol, gcol, gend = col(beta), col(gc), col(gc_end)
        grow = jnp.concatenate([gc_row[h:h + 1, :] for h in range(nh)], axis=1)
        dec = jnp.where(m.incl, jnp.exp(jnp.where(m.incl, gcol - grow, 0.0)), 0.0)
        kb = k * bcol
        kq = _dot_nt(jnp.concatenate([kb, q], axis=0), k)
        yield
        lm = -jnp.where(m.strict, kq[:n] * dec, 0.0)
        attn = jnp.where(m.incl, kq[n:] * dec, 0.0)
        eg = jnp.exp(gcol)
        sol = yield from _solve_unit_lower(lm, jnp.concatenate([v * bcol, kb * eg], axis=1), m.eye, qs)
        vw, kcd = sol[:, :GDN_D], sol[:, GDN_D:]
        qg = q * eg
        kg = k * jnp.exp(gend - gcol)

        def state(j):
            return jnp.concatenate([s_ref[j, sub * nh + h] for h in range(nh)], axis=0)

        parts = []
        for j in range(bb):
            lhs = jnp.concatenate([_seq_rows(grp, kcd, j, nh), _seq_rows(grp, qg, j, nh)], axis=0)
            parts.append(_dot(_expand_rows(lhs, nh, qs, GDN_D), state(j)))
            yield
        half = nh * qs
        ks = _from_seq_rows(grp, [p[:half] for p in parts], nh)
        qsv = _from_seq_rows(grp, [p[half:] for p in parts], nh)
        v_new = vw - ks
        o = qsv + _dot(attn, v_new)
        yield
        for j in range(bb):
            kgj = _expand_rows(_seq_rows(grp, kg, j, nh), nh, qs, GDN_D)
            g_last = jnp.exp(gc_end[j * qs:j * qs + 1, :])
            g_last = jnp.concatenate([jnp.broadcast_to(g_last[:, h:h + 1], (GDN_D, 1)) for h in range(nh)], axis=0)
            s_new = state(j) * g_last + _dot_tn(kgj, _seq_rows(grp, v_new, j, nh))
            yield
            for h in range(nh):
                s_ref[j, sub * nh + h] = s_new[h * GDN_D:(h + 1) * GDN_D]
        o = o * lax.rsqrt(jnp.mean(o * o, axis=-1, keepdims=True) + EPS) * nw_ref[...]
        return _unstack_heads(o, nh)

    o = jnp.concatenate(_interleave([stack(sub) for sub in range(GDN_SUB)]), axis=1)
    o_ref[...] = (o * _silu(z)).astype(o_ref.dtype)


def _ab_layout(x):
    rw, z = x[..., :RW_PROJ], x[..., RW_PROJ:RW_PROJ + SSD_DIM]
    xbc = x[..., RW_PROJ + SSD_DIM:RW_PROJ + SSD_DIM + SSD_CONV_DIM]
    dt = x[..., RW_PROJ + SSD_DIM + SSD_CONV_DIM:]
    zeros = lambda n: jnp.zeros(x.shape[:-1] + (n,), x.dtype)
    return jnp.concatenate([z, xbc, dt, zeros(AB_RW0 - AB_DT0 - SSD_HEADS), rw, zeros(AB_COLS - AB_RW0 - RW_PROJ)],
                           axis=-1)


def _gdn_layout(x):
    per_step = GDN_G * GDN_SUB
    main = GDN_CONV_DIM + GDN_V
    b, a = x[..., main:main + GDN_HEADS], x[..., main + GDN_HEADS:main + 2 * GDN_HEADS]
    parts = [x[..., :main]]
    for g in range(GDN_HEADS // per_step):
        hs = slice(g * per_step, (g + 1) * per_step)
        parts += [b[..., hs], a[..., hs], jnp.zeros(x.shape[:-1] + (LANES - 2 * per_step,), x.dtype)]
    y = jnp.concatenate(parts, axis=-1)
    return jnp.pad(y, [(0, 0)] * (x.ndim - 1) + [(0, -y.shape[-1] % MM_TN)])


def _gdn(grp, proj, hist, s0, prm):
    per_step = GDN_G * GDN_SUB
    width = per_step * GDN_D
    ng = GDN_HEADS // per_step
    ba_blk = (GDN_CONV_DIM + GDN_V) // LANES
    toks = [
        _Tok(proj, width, lambda g: g, hist, lambda g: g),
        _Tok(proj, width, lambda g: ng + g, hist, lambda g: ng + g),
        _Tok(proj, width, lambda g: 2 * ng + g, hist, lambda g: 2 * ng + g),
        _Tok(proj, width, lambda g: 3 * ng + g),
        _Tok(proj, LANES, lambda g: ba_blk + g),
    ]
    ops, specs = _mixer_specs(grp, toks)
    cw = prm["conv_w"]
    grouped = lambda x: x.reshape(ng, 1, per_step)
    params = [
        (cw, _param_spec(width, lambda g: g)(CONV_W)),
        (cw, _param_spec(width, lambda g: ng + g)(CONV_W)),
        (cw, _param_spec(width, lambda g: 2 * ng + g)(CONV_W)),
        (grouped(prm["A_log"]), pl.BlockSpec((1, 1, per_step), lambda b, g, c: (g, 0, 0))),
        (grouped(prm["dt_bias"]), pl.BlockSpec((1, 1, per_step), lambda b, g, c: (g, 0, 0))),
        (prm["norm_w"].reshape(1, GDN_D), pl.BlockSpec((1, GDN_D), lambda b, g, c: (0, 0))),
    ]
    ops += [p for p, _ in params]
    specs += [s for _, s in params]
    st_spec = pl.BlockSpec((grp.bb, per_step, GDN_D, GDN_D), lambda b, g, c: (b, g, 0, 0))
    ops.append(s0)
    specs.append(st_spec)
    rows = grp.B * grp.L
    nc = grp.nc
    return pl.pallas_call(
        functools.partial(_gdn_body, grp=grp),
        grid=(grp.B // grp.bb, ng, nc),
        in_specs=specs,
        out_specs=[pl.BlockSpec((MIX_ROWS, width), lambda b, g, c: (b * nc + c, g)), st_spec],
        out_shape=[jax.ShapeDtypeStruct((rows, GDN_V), BF16), jax.ShapeDtypeStruct(s0.shape, F32)],
        compiler_params=_cp("parallel", "parallel", "arbitrary", flags=MIXER_FLAGS),
        name="gdn",
    )(*ops)


def _xattn_body(q_ref, k_ref, v_ref, o_ref, *, nseq, lq):
    scale = XA_HD ** -0.5
    for j in range(nseq):
        qrows = slice(j * lq, (j + 1) * lq)
        mrows = slice(j * N_MEM, (j + 1) * N_MEM)
        outs = []
        for h in range(XA_HEADS):
            sl = slice(h * XA_HD, (h + 1) * XA_HD)
            s = _dot_nt(q_ref[qrows, sl], k_ref[mrows, sl]) * scale
            s = s - jnp.max(s, axis=-1, keepdims=True)
            p = jnp.exp(s)
            p = p / jnp.sum(p, axis=-1, keepdims=True)
            outs.append(_dot(p, v_ref[mrows, sl]))
        o_ref[qrows, :] = jnp.concatenate(outs, axis=1).astype(o_ref.dtype)


def _xattn_packed_body(q_ref, k_ref, v_ref, o_ref, *, nseq, lq):
    scale = XA_HD ** -0.5
    rows, cols = XA_HEADS * lq, N_MEM * XA_HEADS
    own_head = (_iota2((rows, cols), 0) // lq) == (_iota2((rows, cols), 1) % XA_HEADS)
    for j in range(nseq):
        q = q_ref[j * lq:(j + 1) * lq, :].astype(F32)
        qh = _stack_heads(q, XA_HEADS, XA_HD)
        mrows = slice(j * cols, (j + 1) * cols)
        s = jnp.where(own_head, _dot_nt(qh, k_ref[mrows, :]) * scale, -1e30)
        p = jnp.where(own_head, jnp.exp(s - jnp.max(s, axis=-1, keepdims=True)), 0.0)
        o = _dot(p, v_ref[mrows, :]) / jnp.sum(p, axis=-1, keepdims=True)
        o_ref[j * lq:(j + 1) * lq, :] = _unstack_heads(o, XA_HEADS).astype(o_ref.dtype)


def _xattn(grp, q, mem_k, mem_v, mem_row0):
    rows = grp.B * grp.L
    if grp.L >= XA_TQ:
        nseq, lq = 1, XA_TQ
        per = grp.L // XA_TQ
        body = _xattn_body
        mem_spec = pl.BlockSpec((N_MEM, XA_DIM), lambda i: (mem_row0 // N_MEM + i // per, 0))
    else:
        nseq, lq = MIX_ROWS // grp.L, grp.L
        body = _xattn_packed_body
        blk = nseq * N_MEM * XA_HEADS
        mem_spec = pl.BlockSpec((blk, XA_HD), lambda i: (mem_row0 * XA_HEADS // blk + i, 0))
    q_spec = pl.BlockSpec((nseq * lq, XA_DIM), lambda i: (i, 0))
    return pl.pallas_call(
        functools.partial(body, nseq=nseq, lq=lq),
        grid=(rows // (nseq * lq),),
        in_specs=[q_spec, mem_spec, mem_spec],
        out_specs=q_spec,
        out_shape=jax.ShapeDtypeStruct((rows, XA_DIM), BF16),
        compiler_params=_cp("parallel"),
        name="xattn",
    )(q, mem_k, mem_v)


def _ffn_in_body(*refs, grp, tiles_per_seq):
    it = iter(refs)
    a_ref, g_ref, wg_ref, wu_ref, hist_ref, cw_ref, cb_ref, act_ref, tail_ref, an_ref = (next(it) for _ in range(10))
    carry_ref = None if grp.embedded else next(it)
    i, j = pl.program_id(0), pl.program_id(1)

    @pl.when(j == 0)
    def _():
        x = a_ref[...]
        y = x * lax.rsqrt(jnp.mean(x * x, axis=-1, keepdims=True) + EPS)
        an_ref[...] = (y * g_ref[...]).astype(BF16)

    a = an_ref[...]
    gate = jnp.dot(a, wg_ref[...], preferred_element_type=F32)
    up = jnp.dot(a, wu_ref[...], preferred_element_type=F32)
    if grp.embedded:
        halo = hist_ref[...]
        tail_ref[...] = gate
    else:
        halo = jnp.where(i % tiles_per_seq == 0, hist_ref[...], carry_ref[j])
        last = gate[gate.shape[0] - SUBLANES:]
        carry_ref[j] = last
        tail_ref[...] = last
    conv = _conv(grp, gate, halo, cw_ref, cb_ref[...], FFN_CONV_W)
    act_ref[...] = (_silu(conv) * up).astype(act_ref.dtype)


def _ffn_in(grp, h, gain, w_in, hist, conv_w, conv_b, tm=MM_TM, tn=MM_TN):
    assert grp.embedded or grp.L % tm == 0
    rows, k = h.shape
    nct = D_FF // tn
    tiles_per_seq = max(grp.L // tm, 1)
    hrows = tm if grp.embedded else SUBLANES
    tile_idx = lambda i, j: (i, j)
    hist_idx = tile_idx if grp.embedded else (lambda i, j: (i // tiles_per_seq, j))
    specs = [
        pl.BlockSpec((tm, k), lambda i, j: (i, 0)),
        pl.BlockSpec((1, k), lambda i, j: (0, 0)),
        pl.BlockSpec((k, tn), lambda i, j: (0, j)),
        pl.BlockSpec((k, tn), lambda i, j: (0, nct + j)),
        pl.BlockSpec((hrows, tn), hist_idx),
        pl.BlockSpec((FFN_CONV_W, tn), lambda i, j: (0, j)),
        pl.BlockSpec((1, tn), lambda i, j: (0, j)),
    ]
    scratch = [pltpu.VMEM((tm, k), BF16)]
    if not grp.embedded:
        scratch.append(pltpu.VMEM((nct, SUBLANES, tn), F32))
    act, tails = pl.pallas_call(
        functools.partial(_ffn_in_body, grp=grp, tiles_per_seq=tiles_per_seq),
        grid=(rows // tm, nct),
        in_specs=specs,
        out_specs=[pl.BlockSpec((tm, tn), tile_idx), pl.BlockSpec((hrows, tn), tile_idx)],
        out_shape=[jax.ShapeDtypeStruct((rows, D_FF), BF16), jax.ShapeDtypeStruct((rows // tm * hrows, D_FF), F32)],
        scratch_shapes=scratch,
        compiler_params=_cp("arbitrary", "arbitrary"),
        name="ffn_in",
    )(h, gain.reshape(1, k), w_in, w_in, hist, conv_w, conv_b.reshape(1, D_FF))
    if not grp.embedded:
        tails = tails.reshape(grp.B, tiles_per_seq, SUBLANES, D_FF)[:, -1].reshape(grp.B * SUBLANES, D_FF)
    return act, tails


def _history(grp, buf, width):
    b, k, c = buf.shape
    if grp.embedded:
        h = jnp.pad(buf, ((0, 0), (grp.first - k, grp.L - grp.first), (0, 0)))
        return h.reshape(b * grp.L, c)
    return jnp.pad(buf, ((0, 0), (SUBLANES - k, 0), (0, 0))).reshape(b * SUBLANES, c)


def _tail(grp, x2d, k, cols):
    x = x2d.reshape(grp.B, grp.L, x2d.shape[1])
    return x[:, grp.L - k:, cols].astype(F32)


def _run_group(grp, x2d, mem_k, mem_v, mem_row0, st, wts, prm):
    h = x2d
    out = {}
    proj = _matmul(h, wts["in_ab"], gain=prm["norm_mix"][0], out_dtype=BF16)
    y_rw, s_rw = _rwkv(grp, proj, _history(grp, st["rw_shift"][:, None, :], RW_PROJ), st["rwkv"], prm["rw"])
    y_ssd, s_ssd = _ssd(grp, proj, _history(grp, st["ssd_conv"], SSD_CONV_DIM),
                        st["ssd"].reshape(grp.B, SSD_GROUPS, SSD_GDIM, SSD_STATE), prm["ssd"])
    out["rwkv"] = s_rw
    out["rw_shift"] = _tail(grp, proj, 1, slice(AB_RW0, AB_RW0 + RW_PROJ))[:, 0]
    out["ssd"] = s_ssd.reshape(st["ssd"].shape)
    out["ssd_conv"] = _tail(grp, proj, CONV_W - 1, slice(AB_X0, AB_X0 + SSD_CONV_DIM))
    h = _matmul(jnp.concatenate([y_rw, y_ssd], axis=1), wts["out_ab"], res=h)
    ffn_bufs = []
    for l in range(2):
        if l == 1:
            proj = _matmul(h, wts["in_c"], gain=prm["norm_mix"][1], out_dtype=BF16)
            y_c, s_gdn = _gdn(grp, proj, _history(grp, st["gdn_conv"], GDN_CONV_DIM), st["gdn"], prm["gdn"])
            out["gdn"] = s_gdn
            out["gdn_conv"] = _tail(grp, proj, CONV_W - 1, slice(0, GDN_CONV_DIM))
            h = _matmul(y_c, wts["out_c"], res=h)
        q = _matmul(h, wts["xq"][l], gain=prm["norm_xa"][l], out_dtype=BF16)
        o = _xattn(grp, q, mem_k, mem_v, mem_row0[l])
        h = _matmul(o, wts["xo"][l], res=h)
        act, gate_tail = _ffn_in(grp, h, prm["norm_ffn"][l], wts["ffn_in"][l], _history(grp, st["ffn_conv"][l], D_FF),
                                 prm["ffn_conv_w"][l], prm["ffn_conv_b"][l])
        ffn_bufs.append(gate_tail.reshape(grp.B, SUBLANES, D_FF)[:, SUBLANES - (FFN_CONV_W - 1):])
        h = _matmul(act, wts["ffn_out"][l], res=h)
    out["ffn_conv"] = jnp.stack(ffn_bufs)
    out["y"] = _rmsnorm(h, prm["norm_final"])
    return out


def kernel(x_prompt, x_sample, mem_prompt, state_rwkv, state_rwkv_shift, state_ssd, state_ssd_conv, state_gdn,
           state_gdn_conv, state_ffn_conv, cache_mem_k, cache_mem_v, norm_mix, norm_xa, norm_mem, norm_ffn,
           norm_final, w_in_ab, rw_mu, rw_w0, rw_w_up, rw_a0, rw_a_up, rw_g_up, rw_k_k, rw_k_a, rw_r_k, rw_gn_w,
           rw_gn_b, ssd_conv_w, ssd_conv_b, ssd_dt_bias, ssd_A_log, ssd_D, ssd_norm_w, w_out_ab, w_in_c,
           gdn_conv_w, gdn_A_log, gdn_dt_bias, gdn_norm_w, w_out_c, w_xq, w_xk, w_xv, w_xo, ffn_w_in, ffn_conv_w,
           ffn_conv_b, ffn_w_out):
    bp, lp, _ = x_prompt.shape
    bs, ls, _ = x_sample.shape
    depth = norm_mix.shape[0]
    assert depth == 2 and w_in_ab.shape[0] == 1 and w_in_c.shape[0] == 1
    assert lp % MIX_ROWS == 0 and lp % XA_TQ == 0 and lp % MM_TM == 0
    pad_rows = SUBLANES - ls
    assert CONV_W - 1 <= pad_rows and CONV_W - 1 <= ls and MIX_ROWS % SUBLANES == 0
    gp = Group(B=bp, L=lp, first=0, bb=1)
    gs = Group(B=bs, L=SUBLANES, first=pad_rows, bb=MIX_ROWS // SUBLANES)
    assert bs % gs.bb == 0 and (bs * SUBLANES) % MM_TM == 0 and (bp * lp) % MM_TM == 0

    per_layer = lambda w: [w[l].astype(BF16) for l in range(depth)]
    wts = dict(
        in_ab=_ab_layout(w_in_ab[0]).astype(BF16), out_ab=w_out_ab[0].astype(BF16),
        in_c=_gdn_layout(w_in_c[0]).astype(BF16), out_c=w_out_c[0].astype(BF16),
        xq=per_layer(w_xq), xo=per_layer(w_xo), ffn_in=per_layer(ffn_w_in), ffn_out=per_layer(ffn_w_out),
    )
    prm = dict(
        norm_mix=norm_mix, norm_xa=norm_xa, norm_ffn=norm_ffn, norm_final=norm_final,
        ffn_conv_w=ffn_conv_w, ffn_conv_b=ffn_conv_b,
        rw=dict(mu=rw_mu[0], w0=rw_w0[0], w_up=rw_w_up[0], a0=rw_a0[0], a_up=rw_a_up[0], g_up=rw_g_up[0],
                k_k=rw_k_k[0], k_a=rw_k_a[0], r_k=rw_r_k[0], gn_w=rw_gn_w[0], gn_b=rw_gn_b[0]),
        ssd=dict(conv_w=ssd_conv_w[0], conv_b=ssd_conv_b[0], dt_bias=ssd_dt_bias[0], A_log=ssd_A_log[0],
                 D=ssd_D[0], norm_w=ssd_norm_w[0]),
        gdn=dict(conv_w=gdn_conv_w[0], A_log=gdn_A_log[0], dt_bias=gdn_dt_bias[0], norm_w=gdn_norm_w[0]),
    )

    mem2d = mem_prompt.reshape(bp * N_MEM, D_MODEL)
    mem_tm = min(MM_TM, bp * N_MEM)
    mk, mv = [], []
    for l in range(depth):
        mk.append(_matmul(mem2d, w_xk[l].astype(BF16), gain=norm_mem[l], tm=mem_tm))
        mv.append(_matmul(mem2d, w_xv[l].astype(BF16), gain=norm_mem[l], tm=mem_tm))
    mem_k_p = jnp.stack(mk)
    mem_v_p = jnp.stack(mv)

    zeros = lambda *s: jnp.zeros(s, F32)
    st_p = dict(
        rwkv=zeros(bp, RW_HEADS, RW_HD, RW_HD), rw_shift=zeros(bp, RW_PROJ),
        ssd=zeros(bp, SSD_GROUPS, SSD_HPG, SSD_HD, SSD_STATE), ssd_conv=zeros(bp, CONV_W - 1, SSD_CONV_DIM),
        gdn=zeros(bp, GDN_HEADS, GDN_D, GDN_D), gdn_conv=zeros(bp, CONV_W - 1, GDN_CONV_DIM),
        ffn_conv=zeros(depth, bp, FFN_CONV_W - 1, D_FF),
    )
    rp = _run_group(gp, x_prompt.reshape(bp * lp, D_MODEL), mem_k_p.reshape(depth * bp * N_MEM, XA_DIM),
                    mem_v_p.reshape(depth * bp * N_MEM, XA_DIM), [l * bp * N_MEM for l in range(depth)],
                    st_p, wts, prm)

    st_s = dict(rwkv=state_rwkv[0], rw_shift=state_rwkv_shift[0], ssd=state_ssd[0], ssd_conv=state_ssd_conv[0],
                gdn=state_gdn[0], gdn_conv=state_gdn_conv[0], ffn_conv=state_ffn_conv)
    xs = jnp.pad(x_sample, ((0, 0), (pad_rows, 0), (0, 0))).reshape(bs * SUBLANES, D_MODEL)
    rs = _run_group(gs, xs, cache_mem_k.reshape(depth * bs * N_MEM * XA_HEADS, XA_HD),
                    cache_mem_v.reshape(depth * bs * N_MEM * XA_HEADS, XA_HD), [l * bs * N_MEM for l in range(depth)],
                    st_s, wts, prm)

    y_p = rp["y"].reshape(bp, lp, D_MODEL)
    y_s = rs["y"].reshape(bs, SUBLANES, D_MODEL)[:, pad_rows:]
    lead = lambda x: x[None]
    mem_shape = (depth, bp, N_MEM, XA_HEADS, XA_HD)
    return (y_p, y_s, lead(rp["rwkv"]), lead(rs["rwkv"]), lead(rp["rw_shift"]), lead(rs["rw_shift"]),
            lead(rp["ssd"]), lead(rs["ssd"]), lead(rp["ssd_conv"]), lead(rs["ssd_conv"]),
            lead(rp["gdn"]), lead(rs["gdn"]), lead(rp["gdn_conv"]), lead(rs["gdn_conv"]),
            rp["ffn_conv"], rs["ffn_conv"], mem_k_p.reshape(mem_shape), mem_v_p.reshape(mem_shape))
```

```python
import functools
from typing import NamedTuple

import jax
import jax.numpy as jnp
from jax import lax
from jax.experimental import pallas as pl
from jax.experimental.pallas import tpu as pltpu

F32 = jnp.float32
BF16 = jnp.bfloat16
HI = lax.Precision.HIGHEST

D_MODEL = 2048
EPS = 1e-6
RW_HEADS, RW_HD = 16, 64
RW_DIM = RW_HEADS * RW_HD
RW_LORA = 256
RW_PROJ = 3 * RW_DIM + RW_LORA
RW_GN_EPS = 6.4e-4
SSD_HEADS, SSD_HD, SSD_GROUPS, SSD_STATE = 16, 64, 2, 128
SSD_HPG = SSD_HEADS // SSD_GROUPS
SSD_DIM = SSD_HEADS * SSD_HD
SSD_GDIM = SSD_HPG * SSD_HD
SSD_CONV_DIM = SSD_DIM + 2 * SSD_GROUPS * SSD_STATE
SSD_PROJ = SSD_DIM + SSD_CONV_DIM + SSD_HEADS
AB_PROJ = RW_PROJ + SSD_PROJ
GDN_HEADS, GDN_D = 16, 128
GDN_V = GDN_HEADS * GDN_D
GDN_CONV_DIM = 3 * GDN_V
GDN_PROJ = GDN_CONV_DIM + GDN_V + 2 * GDN_HEADS
D_FF = 5632
XA_HEADS, XA_HD, N_MEM = 4, 128, 256
XA_DIM = XA_HEADS * XA_HD
CONV_W = 4
FFN_CONV_W = 3

AB_Z0 = 0
AB_X0 = SSD_DIM
AB_B0 = 2 * SSD_DIM
AB_C0 = AB_B0 + SSD_GROUPS * SSD_STATE
AB_DT0 = AB_C0 + SSD_GROUPS * SSD_STATE
AB_RW0 = 3 * RW_DIM
AB_COLS = 6656
assert AB_DT0 + 128 <= AB_RW0 and AB_RW0 + RW_PROJ <= AB_COLS and AB_RW0 % RW_DIM == 0

SUBLANES = 8
LANES = 128
VMEM_LIMIT = 56 * 1024 * 1024

MIX_ROWS = 64
INV_BLOCK = 16
INV_SPLIT = 128
MM_TM, MM_TN = 1024, 512
RW_HP = 2
RW_SUB = 4
GDN_G = 4
GDN_SUB = 4
XA_TQ = 512
NORM_TM = 512


class Group(NamedTuple):
    B: int
    L: int
    first: int
    bb: int

    @property
    def qs(self):
        return MIX_ROWS // self.bb

    @property
    def nc(self):
        return self.L // self.qs

    @property
    def embedded(self):
        return self.first > 0


MIXER_FLAGS = None


def _cp(*sem, flags=None):
    return pltpu.CompilerParams(dimension_semantics=sem, vmem_limit_bytes=VMEM_LIMIT, flags=flags)


def _dot(a, b):
    return jnp.dot(a.astype(BF16), b.astype(BF16), preferred_element_type=F32)


def _dot_nt(a, b):
    return lax.dot_general(a.astype(BF16), b.astype(BF16), (((1,), (1,)), ((), ())), preferred_element_type=F32)


def _dot_tn(a, b):
    return lax.dot_general(a.astype(BF16), b.astype(BF16), (((0,), (0,)), ((), ())), preferred_element_type=F32)


def _dot_hi(a, b):
    return jnp.dot(a, b, preferred_element_type=F32, precision=HI)


def _split_bf16(x):
    hi = x.astype(BF16)
    return hi, (x - hi.astype(F32)).astype(BF16)


def _dot_x3(a, b):
    ah, al = _split_bf16(a)
    bh, bl = _split_bf16(b)
    return jnp.dot(jnp.concatenate([ah, ah, al], axis=1), jnp.concatenate([bh, bl, bh], axis=0),
                   preferred_element_type=F32)


def _dot_nt_hi(a, b):
    return lax.dot_general(a, b, (((1,), (1,)), ((), ())), preferred_element_type=F32, precision=HI)


def _sigmoid(x):
    return 1.0 / (1.0 + jnp.exp(-x))


def _silu(x):
    return x * _sigmoid(x)


def _softplus(x):
    return jnp.maximum(x, 0.0) + jnp.log(1.0 + jnp.exp(-jnp.abs(x)))


def _iota2(shape, axis):
    return lax.broadcasted_iota(jnp.int32, shape, axis)


class _Masks(NamedTuple):
    incl: jax.Array
    strict: jax.Array
    eye: jax.Array
    cumsum: jax.Array
    valid: jax.Array


def _masks(grp, heads=1):
    r = MIX_ROWS
    n = heads * r
    ri, ci = _iota2((n, n), 0), _iota2((n, n), 1)
    same = (ri // grp.qs) == (ci // grp.qs)
    incl = same & (ci <= ri)
    strict = same & (ci < ri)
    si, sj = _iota2((2 * r, r), 0), _iota2((2 * r, r), 1)
    same_seq = ((si % r) // grp.qs) == (sj // grp.qs)
    cumsum = (same_seq & ((sj <= si) | (si >= r))).astype(F32)
    t = _iota2((r, 1), 0) % grp.qs
    return _Masks(incl, strict, (ri == ci).astype(F32), cumsum, t >= grp.first)


def _neumann(lm, rhs, nil, eye=None):
    n = lm.shape[0]
    p = None
    if rhs is None:
        t = eye + lm
        if nil > 2:
            p = _dot(lm, lm)
            yield
    elif nil > 2:
        both = _dot(lm, jnp.concatenate([lm, rhs], axis=1))
        yield
        p, t = both[:, :n], rhs + both[:, n:]
    else:
        t = rhs + _dot(lm, rhs)
        yield
    k = 2
    while k < nil:
        if 2 * k < nil:
            both = _dot(p, jnp.concatenate([p, t], axis=1))
            p, t = both[:, :n], t + both[:, n:]
        else:
            t = t + _dot(p, t)
        yield
        k *= 2
    return t


def _inv_unit_lower(lm, eye, qs):
    if qs <= INV_BLOCK:
        return (yield from _neumann(lm, None, qs, eye))
    n = lm.shape[0]
    diag = (_iota2((n, n), 0) // INV_BLOCK) == (_iota2((n, n), 1) // INV_BLOCK)
    d = jnp.where(diag, lm, 0.0)
    dinv = yield from _neumann(d, None, INV_BLOCK, eye)
    nm = _dot(dinv, lm - d)
    yield
    return (yield from _neumann(nm, dinv, qs // INV_BLOCK))


def _lockstep(gens):
    results = [None] * len(gens)
    active = list(range(len(gens)))
    while active:
        for i in list(active):
            try:
                next(gens[i])
            except StopIteration as stop:
                results[i] = stop.value
                active.remove(i)
        if active:
            yield
    return results


def _solve_unit_lower(lm, rhs, eye, qs):
    n = lm.shape[0]
    if n > INV_SPLIT:
        blocks = [slice(i, i + INV_SPLIT) for i in range(0, n, INV_SPLIT)]
        inv = yield from _lockstep([_inv_unit_lower(lm[b, b], eye[b, b], qs) for b in blocks])
        zero = jnp.zeros((INV_SPLIT, INV_SPLIT), F32)
        t = jnp.concatenate([jnp.concatenate([inv[i] if i == j else zero for j in range(len(blocks))], axis=1)
                             for i in range(len(blocks))], axis=0)
    else:
        t = yield from _inv_unit_lower(lm, eye, qs)
    x = _dot(t, rhs)
    yield
    return x


def _interleave(gens):
    results = [None] * len(gens)
    active = list(range(len(gens)))
    while active:
        for i in list(active):
            try:
                next(gens[i])
            except StopIteration as stop:
                results[i] = stop.value
                active.remove(i)
    return results


def _stack_heads(x, heads, width):
    return jnp.concatenate([x[:, h * width:(h + 1) * width] for h in range(heads)], axis=0)


def _unstack_heads(x, heads):
    r = x.shape[0] // heads
    return jnp.concatenate([x[h * r:(h + 1) * r] for h in range(heads)], axis=1)


def _seq_rows(grp, x, j, heads):
    if grp.bb == 1:
        return x
    qs = grp.qs
    return jnp.concatenate([x[h * MIX_ROWS + j * qs:h * MIX_ROWS + (j + 1) * qs] for h in range(heads)], axis=0)


def _from_seq_rows(grp, parts, heads):
    if grp.bb == 1:
        return parts[0]
    qs = grp.qs
    return jnp.concatenate([p[h * qs:(h + 1) * qs] for h in range(heads) for p in parts], axis=0)


def _expand_rows(x, heads, rows_per_head, width):
    m = x.shape[0]
    if x.shape[1] == width:
        x = jnp.concatenate([x] * heads, axis=1)
    keep = ((_iota2((m, heads * width), 0) // rows_per_head) % heads) == (_iota2((m, heads * width), 1) // width)
    return jnp.where(keep, x, 0.0)


def _taps(grp, cur, halo, width):
    if grp.embedded:
        t = _iota2((cur.shape[0], 1), 0) % grp.qs
        full = jnp.where(t < grp.first, halo, cur)
        return [full] + [pltpu.roll(full, s, axis=0) for s in range(1, width)]
    full = jnp.concatenate([halo, cur], axis=0)
    return [cur] + [pltpu.roll(full, s, axis=0)[SUBLANES:] for s in range(1, width)]


def _conv(grp, cur, halo, w_ref, bias, width):
    taps = _taps(grp, cur, halo, width)
    y = taps[0] * w_ref[width - 1:width, :]
    for s in range(1, width):
        y = y + taps[s] * w_ref[width - 1 - s:width - s, :]
    return y if bias is None else y + bias


class _Tok(NamedTuple):
    arr: jax.Array
    width: int
    col: object
    hist: jax.Array = None
    hcol: object = None


def _mixer_specs(grp, toks):
    nc = grp.nc
    ops, specs = [], []
    for t in toks:
        ops.append(t.arr)
        specs.append(pl.BlockSpec((MIX_ROWS, t.width), lambda b, g, c, t=t: (b * nc + c, t.col(g))))
    for t in toks:
        if t.hist is None:
            continue
        if not grp.embedded:
            tile = _sublane_tile(t.arr.dtype)
            sub = MIX_ROWS // tile
            ops.append(t.arr)
            specs.append(pl.BlockSpec(
                (tile, t.width), lambda b, g, c, t=t, sub=sub: (jnp.maximum((b * nc + c) * sub - 1, 0), t.col(g))))
        ops.append(t.hist)
        hrows = MIX_ROWS if grp.embedded else SUBLANES
        specs.append(pl.BlockSpec((hrows, t.width), lambda b, g, c, t=t: (b, t.hcol(g))))
    return ops, specs


def _read_windows(grp, refs, n_plain, n_hist):
    c = pl.program_id(2)
    cur = [r[...].astype(F32) for r in refs[:n_plain]]
    rest = refs[n_plain:]
    halos = []
    for i in range(n_hist):
        if grp.embedded:
            halos.append(rest[i][...])
        else:
            prev = rest[2 * i][...]
            prev = prev[prev.shape[0] - SUBLANES:].astype(F32)
            halos.append(jnp.where(c == 0, rest[2 * i + 1][...], prev))
    used = n_hist if grp.embedded else 2 * n_hist
    return cur, halos, rest[used:]


def _sublane_tile(dtype):
    return SUBLANES * 4 // jnp.dtype(dtype).itemsize


def _param_spec(width, col):
    return lambda rows: pl.BlockSpec((rows, width), lambda b, g, c: (0, col(g)))


def _matmul_body(*refs, norm, has_res, n_a):
    it = iter(refs)
    a_refs = [next(it) for _ in range(n_a)]
    g_ref = next(it) if norm else None
    w_ref = next(it)
    res_ref = next(it) if has_res else None
    o_ref = next(it)
    if norm:
        an_ref = next(it)

        @pl.when(pl.program_id(1) == 0)
        def _():
            x = a_refs[0][...]
            y = x * lax.rsqrt(jnp.mean(x * x, axis=-1, keepdims=True) + EPS)
            an_ref[...] = (y * g_ref[...]).astype(BF16)

        acc = jnp.dot(an_ref[...], w_ref[...], preferred_element_type=F32)
    else:
        acc, k0 = None, 0
        for a_ref in a_refs:
            k1 = k0 + a_ref.shape[1]
            part = jnp.dot(a_ref[...], w_ref[k0:k1, :], preferred_element_type=F32)
            acc = part if acc is None else acc + part
            k0 = k1
    if has_res:
        acc = acc + res_ref[...]
    o_ref[...] = acc.astype(o_ref.dtype)


def _matmul(a, w, *, layer=None, gain=None, res=None, out_dtype=F32, tm=MM_TM, tn=MM_TN):
    a_list = list(a) if isinstance(a, (list, tuple)) else [a]
    m = a_list[0].shape[0]
    k, n = w.shape[-2:]
    assert sum(x.shape[1] for x in a_list) == k
    norm = gain is not None
    assert not norm or len(a_list) == 1
    ops = list(a_list)
    specs = [pl.BlockSpec((tm, x.shape[1]), lambda i, j: (i, 0)) for x in a_list]
    if norm:
        ops.append(gain.reshape(1, k))
        specs.append(pl.BlockSpec((1, k), lambda i, j: (0, 0)))
    ops.append(w)
    specs.append(_weight_spec(w, layer, k, tn, lambda j: j))
    if res is not None:
        ops.append(res)
        specs.append(pl.BlockSpec((tm, tn), lambda i, j: (i, j)))
    return pl.pallas_call(
        functools.partial(_matmul_body, norm=norm, has_res=res is not None, n_a=len(a_list)),
        grid=(m // tm, n // tn),
        in_specs=specs,
        out_specs=pl.BlockSpec((tm, tn), lambda i, j: (i, j)),
        out_shape=jax.ShapeDtypeStruct((m, n), out_dtype),
        scratch_shapes=[pltpu.VMEM((tm, k), BF16)] if norm else [],
        compiler_params=_cp("parallel", "arbitrary"),
        name="matmul",
    )(*ops)


def _weight_spec(w, layer, k, tn, col):
    if w.ndim == 2:
        return pl.BlockSpec((k, tn), lambda i, j: (0, col(j)))
    return pl.BlockSpec((None, k, tn), lambda i, j: (layer, 0, col(j)))


def _rmsnorm_body(x_ref, g_ref, o_ref):
    x = x_ref[...]
    y = x * lax.rsqrt(jnp.mean(x * x, axis=-1, keepdims=True) + EPS)
    o_ref[...] = (y * g_ref[...]).astype(o_ref.dtype)


def _rmsnorm(x, gain, out_dtype=F32, tm=NORM_TM):
    m, k = x.shape
    return pl.pallas_call(
        _rmsnorm_body,
        grid=(m // tm,),
        in_specs=[pl.BlockSpec((tm, k), lambda i: (i, 0)), pl.BlockSpec((1, k), lambda i: (0, 0))],
        out_specs=pl.BlockSpec((tm, k), lambda i: (i, 0)),
        out_shape=jax.ShapeDtypeStruct((m, k), out_dtype),
        compiler_params=_cp("parallel"),
        name="rmsnorm",
    )(x, gain.reshape(1, k))


def _rwkv_body(*refs, grp):
    r_rows = MIX_ROWS
    qs, bb, nc = grp.qs, grp.bb, grp.nc
    nh = 2 * RW_HP
    sw = nh * RW_HD
    width = RW_SUB * sw
    n = nh * r_rows
    cur, halos, rest = _read_windows(grp, refs, 4, 4)
    (mu_r, mu_k, mu_v, mu_lo, w0_ref, a0_ref, lora_ref, kk_ref, ka_ref, rk_ref, gnw_ref, gnb_ref,
     s0_ref, y_ref, s_ref, sbig_ref) = rest
    c = pl.program_id(2)

    @pl.when(c == 0)
    def _():
        for j in range(bb):
            for sub in range(RW_SUB):
                st = jnp.concatenate([s0_ref[j, sub * nh + h] for h in range(nh)], axis=0)
                sbig_ref[j, sub] = _expand_rows(st, nh, RW_HD, RW_HD)

    m = _masks(grp, nh)

    def shifted(i, mu_ref):
        x = cur[i]
        prev = _taps(grp, x, halos[i], 2)[1]
        return x + (prev - x) * mu_ref[...]

    r = shifted(0, mu_r)
    k = shifted(1, mu_k)
    v = shifted(2, mu_v)
    lo = shifted(3, mu_lo)
    lora_in = jnp.concatenate([jnp.tanh(lo[:, :64]), lo[:, 64:128], _sigmoid(lo[:, 128:256])], axis=1)
    lora = _dot_x3(lora_in, lora_ref[0])
    w = -_softplus(-(w0_ref[...] + lora[:, :width])) - 0.5
    lw = jnp.where(m.valid, -jnp.exp(w), 0.0)
    a = _sigmoid(a0_ref[...] + lora[:, width:2 * width])
    g = lora[:, 2 * width:]

    bd = ((_iota2((sw, sw), 0) // RW_HD) == (_iota2((sw, sw), 1) // RW_HD)).astype(F32)

    def head_sums(x):
        return _unstack_heads(_dot_x3(_stack_heads(x, RW_SUB, sw), bd), RW_SUB)

    kx = k * kk_ref[...]
    kp = jnp.where(m.valid, k * (1.0 + (a - 1.0) * ka_ref[...]), 0.0)
    sums = head_sums(jnp.concatenate([kx * kx, r * kp * rk_ref[...]], axis=0))
    kkn = kx * lax.rsqrt(sums[:r_rows] + 1e-6)
    bonus = sums[r_rows:] * v
    at = jnp.where(m.valid, -kkn, 0.0)
    bt = jnp.where(m.valid, kkn * a, 0.0)

    cums = _dot_hi(m.cumsum, lw)
    cum, cum_end = cums[:r_rows], cums[r_rows:]
    e_neg = jnp.exp(-cum)
    e_end = jnp.exp(cum_end - cum)
    p_end = jnp.exp(cum_end)
    a_til, r_til = at * jnp.exp(cum - lw), r * jnp.exp(cum)
    b_til, k_til = bt * e_neg, kp * e_neg
    b_hat, k_hat = bt * e_end, kp * e_end

    def stack(sub):
        lanes = slice(sub * sw, (sub + 1) * sw)

        def tile(x):
            return _expand_rows(jnp.concatenate([x[:, lanes]] * nh, axis=0), nh, r_rows, RW_HD)

        ar = jnp.concatenate([tile(a_til), tile(r_til)], axis=0)
        bk = jnp.concatenate([tile(b_til), tile(k_til)], axis=0)
        bk_end = jnp.concatenate([tile(b_hat), tile(k_hat)], axis=0)
        v_exp = tile(v)
        gm = _dot_nt(ar, bk)
        yield
        m_ab = jnp.where(m.strict, gm[:n, :n], 0.0)
        m_ak = jnp.where(m.strict, gm[:n, n:], 0.0)
        m_r = jnp.concatenate([jnp.where(m.incl, gm[n:, :n], 0.0), jnp.where(m.incl, gm[n:, n:], 0.0)], axis=1)

        def seq2(x, j):
            if bb == 1:
                return x
            return jnp.concatenate([_seq_rows(grp, x[:n], j, nh), _seq_rows(grp, x[n:], j, nh)], axis=0)

        parts = []
        for j in range(bb):
            parts.append(_dot_nt(seq2(ar, j), sbig_ref[j, sub]))
            yield
        half = nh * qs
        as_a = _from_seq_rows(grp, [p[:half] for p in parts], nh)
        as_r = _from_seq_rows(grp, [p[half:] for p in parts], nh)
        rhs = as_a + _dot(m_ak, v_exp)
        yield
        u = yield from _solve_unit_lower(m_ab, rhs, m.eye, qs)
        uv = jnp.concatenate([u, v_exp], axis=0)
        y_exp = as_r + _dot(m_r, uv)
        yield
        y = y_exp[:r_rows]
        for h in range(1, nh):
            y = y + y_exp[h * r_rows:(h + 1) * r_rows]
        for j in range(bb):
            sbig_ref[j, sub] = (sbig_ref[j, sub] * p_end[j * qs:j * qs + 1, lanes]
                                + _dot_tn(seq2(uv, j), seq2(bk_end, j)))
            yield
        return y

    y = jnp.concatenate(_interleave([stack(sub) for sub in range(RW_SUB)]), axis=1)

    @pl.when(c == nc - 1)
    def _():
        for j in range(bb):
            for sub in range(RW_SUB):
                sb = sbig_ref[j, sub]
                for h in range(nh):
                    s_ref[j, sub * nh + h] = sb[h * RW_HD:(h + 1) * RW_HD, h * RW_HD:(h + 1) * RW_HD]

    mean = head_sums(y) * (1.0 / RW_HD)
    d = y - mean
    var = head_sums(d * d) * (1.0 / RW_HD)
    out = d * lax.rsqrt(var + RW_GN_EPS) * gnw_ref[...] + gnb_ref[...] + bonus
    y_ref[...] = (out * g).astype(y_ref.dtype)


def _rwkv(grp, proj, hist, s0, prm):
    width = RW_SUB * RW_HP * LANES
    ng = RW_DIM // width
    lora_blk = 3 * RW_DIM // RW_LORA
    p0 = AB_RW0 // width
    toks = [
        _Tok(proj, width, lambda g: p0 + g, hist, lambda g: g),
        _Tok(proj, width, lambda g: p0 + ng + g, hist, lambda g: ng + g),
        _Tok(proj, width, lambda g: p0 + 2 * ng + g, hist, lambda g: 2 * ng + g),
        _Tok(proj, RW_LORA, lambda g: AB_RW0 // RW_LORA + lora_blk, hist, lambda g: lora_blk),
    ]
    ops, specs = _mixer_specs(grp, toks)
    head = _param_spec(width, lambda g: g)
    lora_w = jnp.zeros((ng, RW_LORA, 3 * width), F32)
    for i, (name, r0, r1) in enumerate((("w_up", 0, 64), ("a_up", 64, 128), ("g_up", 128, 256))):
        blk = prm[name].reshape(r1 - r0, ng, width).transpose(1, 0, 2)
        lora_w = lora_w.at[:, r0:r1, i * width:(i + 1) * width].set(blk)
    mu = prm["mu"].reshape(1, RW_PROJ)
    flat = lambda x: x.reshape(1, RW_DIM)
    params = [
        (mu, pl.BlockSpec((1, width), lambda b, g, c: (0, g))),
        (mu, pl.BlockSpec((1, width), lambda b, g, c: (0, ng + g))),
        (mu, pl.BlockSpec((1, width), lambda b, g, c: (0, 2 * ng + g))),
        (mu, pl.BlockSpec((1, RW_LORA), lambda b, g, c: (0, lora_blk))),
        (flat(prm["w0"]), head(1)), (flat(prm["a0"]), head(1)),
        (lora_w, pl.BlockSpec((1, RW_LORA, 3 * width), lambda b, g, c: (g, 0, 0))),
        (flat(prm["k_k"]), head(1)), (flat(prm["k_a"]), head(1)), (flat(prm["r_k"]), head(1)),
        (flat(prm["gn_w"]), head(1)), (flat(prm["gn_b"]), head(1)),
    ]
    ops += [p for p, _ in params]
    specs += [s for _, s in params]
    st_spec = pl.BlockSpec((grp.bb, RW_SUB * 2 * RW_HP, RW_HD, RW_HD), lambda b, g, c: (b, g, 0, 0))
    ops.append(s0)
    specs.append(st_spec)
    rows = grp.B * grp.L
    nc = grp.nc
    return pl.pallas_call(
        functools.partial(_rwkv_body, grp=grp),
        grid=(grp.B // grp.bb, ng, nc),
        in_specs=specs,
        out_specs=[pl.BlockSpec((MIX_ROWS, width), lambda b, g, c: (b * nc + c, g)), st_spec],
        out_shape=[jax.ShapeDtypeStruct((rows, RW_DIM), BF16), jax.ShapeDtypeStruct(s0.shape, F32)],
        scratch_shapes=[pltpu.VMEM((grp.bb, RW_SUB, RW_HP * LANES, RW_HP * LANES), F32)],
        compiler_params=_cp("parallel", "parallel", "arbitrary", flags=MIXER_FLAGS),
        name="rwkv7",
    )(*ops)


def _ssd_body(*refs, grp):
    r_rows = MIX_ROWS
    qs, bb = grp.qs, grp.bb
    nh, hd, gdim = SSD_HPG, SSD_HD, SSD_GDIM
    cur, halos, rest = _read_windows(grp, refs, 5, 3)
    cw_x, cw_b, cw_c, cb_x, cb_b, cb_c, dtb_ref, alog_ref, dskip_ref, nw_ref, h0_ref, y_ref, h_ref = rest
    x_raw, b_raw, c_raw, z, dt_blk = cur
    c = pl.program_id(2)

    @pl.when(c == 0)
    def _():
        h_ref[...] = h0_ref[...]

    m = _masks(grp)
    xs = _silu(_conv(grp, x_raw, halos[0], cw_x, cb_x[...], CONV_W))
    bm = _silu(_conv(grp, b_raw, halos[1], cw_b, cb_b[...], CONV_W))
    cm = _silu(_conv(grp, c_raw, halos[2], cw_c, cb_c[...], CONV_W))

    dt = jnp.where(m.valid, _softplus(dt_blk[:, :SSD_HEADS] + dtb_ref[...]), 0.0)
    cums = _dot_hi(m.cumsum, dt * (-jnp.exp(alog_ref[...])))
    acs, acs_end = cums[:r_rows], cums[r_rows:]
    eye_h = (_iota2((SSD_HEADS, SSD_HEADS), 0) == _iota2((SSD_HEADS, SSD_HEADS), 1)).astype(F32)
    acs_row = _dot_nt_hi(eye_h, acs)
    ei, ej = _iota2((3 * SSD_HEADS, 3 * SSD_DIM), 0), _iota2((3 * SSD_HEADS, 3 * SSD_DIM), 1)
    spread = ((ei // SSD_HEADS == ej // SSD_DIM) & (ei % SSD_HEADS == (ej % SSD_DIM) // hd)).astype(F32)
    lanes = _dot_hi(jnp.concatenate([dt, acs_end - acs, acs], axis=1), spread)
    xd = xs * lanes[:, :SSD_DIM]
    xd_dec = xd * jnp.exp(lanes[:, SSD_DIM:2 * SSD_DIM])
    acs_lanes = lanes[:, 2 * SSD_DIM:]

    def group(g):
        gl = slice(g * gdim, (g + 1) * gdim)
        bmg, cmg = bm[:, g * SSD_STATE:(g + 1) * SSD_STATE], cm[:, g * SSD_STATE:(g + 1) * SSD_STATE]
        cb = _dot_nt(cmg, bmg)
        yield
        yd = []
        for r in range(nh):
            h = g * nh + r
            diff = acs[:, h:h + 1] - acs_row[h:h + 1, :]
            lmat = jnp.where(m.incl, jnp.exp(jnp.where(m.incl, diff, 0.0)), 0.0)
            yd.append(_dot(cb * lmat, xd[:, h * hd:(h + 1) * hd]))
            yield
        y = jnp.concatenate(yd, axis=1)
        head_rows = (_iota2((gdim, SSD_HEADS), 0) // hd + g * nh == _iota2((gdim, SSD_HEADS), 1)).astype(F32)
        yoff = []
        for j in range(bb):
            rows = slice(j * qs, (j + 1) * qs)
            hj = h_ref[j, g]
            yoff.append(_dot_nt(cmg[rows], hj))
            yield
            end_col = jnp.exp(_dot_nt_hi(head_rows, acs_end[j * qs:j * qs + SUBLANES]))[:, :1]
            yield
            h_ref[j, g] = hj * end_col + _dot_tn(xd_dec[rows, gl], bmg[rows])
            yield
        yoff = yoff[0] if bb == 1 else jnp.concatenate(yoff, axis=0)
        y = y + yoff * jnp.exp(acs_lanes[:, gl]) + xs[:, gl] * dskip_ref[:, gl]
        yg = y * _silu(z[:, gl])
        yg = yg * lax.rsqrt(jnp.mean(yg * yg, axis=-1, keepdims=True) + EPS)
        return yg * nw_ref[:, gl]

    y = jnp.concatenate(_interleave([group(g) for g in range(SSD_GROUPS)]), axis=1)
    y_ref[...] = y.astype(y_ref.dtype)


def _ssd(grp, proj, hist, h0, prm):
    bc = SSD_GROUPS * SSD_STATE
    toks = [
        _Tok(proj, SSD_DIM, lambda g: AB_X0 // SSD_DIM, hist, lambda g: 0),
        _Tok(proj, bc, lambda g: AB_B0 // bc, hist, lambda g: SSD_DIM // bc),
        _Tok(proj, bc, lambda g: AB_C0 // bc, hist, lambda g: SSD_DIM // bc + 1),
        _Tok(proj, SSD_DIM, lambda g: AB_Z0 // SSD_DIM),
        _Tok(proj, LANES, lambda g: AB_DT0 // LANES),
    ]
    ops, specs = _mixer_specs(grp, toks)
    cw, cbias = prm["conv_w"], prm["conv_b"].reshape(1, SSD_CONV_DIM)
    conv_cols = [(SSD_DIM, lambda g: 0), (bc, lambda g: SSD_DIM // bc), (bc, lambda g: SSD_DIM // bc + 1)]
    params = [(cw, _param_spec(wd, col)(CONV_W)) for wd, col in conv_cols]
    params += [(cbias, _param_spec(wd, col)(1)) for wd, col in conv_cols]
    full = lambda n: pl.BlockSpec((1, n), lambda b, g, c: (0, 0))
    params += [
        (prm["dt_bias"].reshape(1, SSD_HEADS), full(SSD_HEADS)),
        (prm["A_log"].reshape(1, SSD_HEADS), full(SSD_HEADS)),
        (jnp.repeat(prm["D"], SSD_HD).reshape(1, SSD_DIM), full(SSD_DIM)),
        (prm["norm_w"].reshape(1, SSD_DIM), full(SSD_DIM)),
    ]
    ops += [p for p, _ in params]
    specs += [s for _, s in params]
    st_spec = pl.BlockSpec((grp.bb, SSD_GROUPS, SSD_GDIM, SSD_STATE), lambda b, g, c: (b, 0, 0, 0))
    ops.append(h0)
    specs.append(st_spec)
    rows = grp.B * grp.L
    nc = grp.nc
    return pl.pallas_call(
        functools.partial(_ssd_body, grp=grp),
        grid=(grp.B // grp.bb, 1, nc),
        in_specs=specs,
        out_specs=[pl.BlockSpec((MIX_ROWS, SSD_DIM), lambda b, g, c: (b * nc + c, 0)), st_spec],
        out_shape=[jax.ShapeDtypeStruct((rows, SSD_DIM), BF16), jax.ShapeDtypeStruct(h0.shape, F32)],
        compiler_params=_cp("parallel", "arbitrary", "arbitrary"),
        name="ssd",
    )(*ops)


def _gdn_body(*refs, grp):
    r_rows = MIX_ROWS
    qs, bb = grp.qs, grp.bb
    nh = GDN_G
    n = nh * r_rows
    cur, halos, rest = _read_windows(grp, refs, 5, 3)
    cw_q, cw_k, cw_v, alog_ref, dtb_ref, nw_ref, s0_ref, o_ref, s_ref = rest
    q_raw, k_raw, v_raw, z, ba = cur
    c = pl.program_id(2)

    @pl.when(c == 0)
    def _():
        s_ref[...] = s0_ref[...]

    m = _masks(grp, nh)
    q_all = _silu(_conv(grp, q_raw, halos[0], cw_q, None, CONV_W))
    k_all = _silu(_conv(grp, k_raw, halos[1], cw_k, None, CONV_W))
    v_all = _silu(_conv(grp, v_raw, halos[2], cw_v, None, CONV_W))
    per_step = nh * GDN_SUB
    eye_g = (_iota2((SUBLANES, nh), 0) == _iota2((SUBLANES, nh), 1)).astype(F32)
    col = lambda x: jnp.concatenate([x[:, h:h + 1] for h in range(nh)], axis=0)

    def stack(sub):
        lanes = slice(sub * nh * GDN_D, (sub + 1) * nh * GDN_D)
        heads = slice(sub * nh, (sub + 1) * nh)
        q = _stack_heads(q_all[:, lanes], nh, GDN_D)
        k = _stack_heads(k_all[:, lanes], nh, GDN_D)
        v = _stack_heads(v_all[:, lanes], nh, GDN_D)
        q = q * lax.rsqrt(jnp.sum(q * q, axis=-1, keepdims=True) + 1e-6) * (GDN_D ** -0.5)
        k = k * lax.rsqrt(jnp.sum(k * k, axis=-1, keepdims=True) + 1e-6)

        beta = jnp.where(m.valid, _sigmoid(ba[:, heads]), 0.0)
        a_raw = ba[:, per_step + sub * nh:per_step + (sub + 1) * nh]
        gg = jnp.where(m.valid, -jnp.exp(alog_ref[0][:, heads]) * _softplus(a_raw + dtb_ref[0][:, heads]), 0.0)
        gcs = _dot_hi(m.cumsum, gg)
        yield
        gc, gc_end = gcs[:r_rows], gcs[r_rows:]
        gc_row = _dot_nt_hi(eye_g, gc)
        yield

        bcol, gcol, gend = col(beta), col(gc), col(gc_end)
        grow = jnp.concatenate([gc_row[h:h + 1, :] for h in range(nh)], axis=1)
        dec = jnp.where(m.incl, jnp.exp(jnp.where(m.incl, gcol - grow, 0.0)), 0.0)
        kb = k * bcol
        kq = _dot_nt(jnp.concatenate([kb, q], axis=0), k)
        yield
        lm = -jnp.where(m.strict, kq[:n] * dec, 0.0)
        attn = jnp.where(m.incl, kq[n:] * dec, 0.0)
        eg = jnp.exp(gcol)
        sol = yield from _solve_unit_lower(lm, jnp.concatenate([v * bcol, kb * eg], axis=1), m.eye, qs)
        vw, kcd = sol[:, :GDN_D], sol[:, GDN_D:]
        qg = q * eg
        kg = k * jnp.exp(gend - gcol)

        def state(j):
            return jnp.concatenate([s_ref[j, sub * nh + h] for h in range(nh)], axis=0)

        parts = []
        for j in range(bb):
            lhs = jnp.concatenate([_seq_rows(grp, kcd, j, nh), _seq_rows(grp, qg, j, nh)], axis=0)
            parts.append(_dot(_expand_rows(lhs, nh, qs, GDN_D), state(j)))
            yield
        half = nh * qs
        ks = _from_seq_rows(grp, [p[:half] for p in parts], nh)
        qsv = _from_seq_rows(grp, [p[half:] for p in parts], nh)
        v_new = vw - ks
        o = qsv + _dot(attn, v_new)
        yield
        for j in range(bb):
            kgj = _expand_rows(_seq_rows(grp, kg, j, nh), nh, qs, GDN_D)
            g_last = jnp.exp(gc_end[j * qs:j * qs + 1, :])
            g_last = jnp.concatenate([jnp.broadcast_to(g_last[:, h:h + 1], (GDN_D, 1)) for h in range(nh)], axis=0)
            s_new = state(j) * g_last + _dot_tn(kgj, _seq_rows(grp, v_new, j, nh))
            yield
            for h in range(nh):
                s_ref[j, sub * nh + h] = s_new[h * GDN_D:(h + 1) * GDN_D]
        o = o * lax.rsqrt(jnp.mean(o * o, axis=-1, keepdims=True) + EPS) * nw_ref[...]
        return _unstack_heads(o, nh)

    o = jnp.concatenate(_interleave([stack(sub) for sub in range(GDN_SUB)]), axis=1)
    o_ref[...] = (o * _silu(z)).astype(o_ref.dtype)


def _ab_layout(x):
    rw, z = x[..., :RW_PROJ], x[..., RW_PROJ:RW_PROJ + SSD_DIM]
    xbc = x[..., RW_PROJ + SSD_DIM:RW_PROJ + SSD_DIM + SSD_CONV_DIM]
    dt = x[..., RW_PROJ + SSD_DIM + SSD_CONV_DIM:]
    zeros = lambda n: jnp.zeros(x.shape[:-1] + (n,), x.dtype)
    return jnp.concatenate([z, xbc, dt, zeros(AB_RW0 - AB_DT0 - SSD_HEADS), rw, zeros(AB_COLS - AB_RW0 - RW_PROJ)],
                           axis=-1)


def _gdn_layout(x):
    per_step = GDN_G * GDN_SUB
    main = GDN_CONV_DIM + GDN_V
    b, a = x[..., main:main + GDN_HEADS], x[..., main + GDN_HEADS:main + 2 * GDN_HEADS]
    parts = [x[..., :main]]
    for g in range(GDN_HEADS // per_step):
        hs = slice(g * per_step, (g + 1) * per_step)
        parts += [b[..., hs], a[..., hs], jnp.zeros(x.shape[:-1] + (LANES - 2 * per_step,), x.dtype)]
    y = jnp.concatenate(parts, axis=-1)
    return jnp.pad(y, [(0, 0)] * (x.ndim - 1) + [(0, -y.shape[-1] % MM_TN)])


def _gdn(grp, proj, hist, s0, prm):
    per_step = GDN_G * GDN_SUB
    width = per_step * GDN_D
    ng = GDN_HEADS // per_step
    ba_blk = (GDN_CONV_DIM + GDN_V) // LANES
    toks = [
        _Tok(proj, width, lambda g: g, hist, lambda g: g),
        _Tok(proj, width, lambda g: ng + g, hist, lambda g: ng + g),
        _Tok(proj, width, lambda g: 2 * ng + g, hist, lambda g: 2 * ng + g),
        _Tok(proj, width, lambda g: 3 * ng + g),
        _Tok(proj, LANES, lambda g: ba_blk + g),
    ]
    ops, specs = _mixer_specs(grp, toks)
    cw = prm["conv_w"]
    grouped = lambda x: x.reshape(ng, 1, per_step)
    params = [
        (cw, _param_spec(width, lambda g: g)(CONV_W)),
        (cw, _param_spec(width, lambda g: ng + g)(CONV_W)),
        (cw, _param_spec(width, lambda g: 2 * ng + g)(CONV_W)),
        (grouped(prm["A_log"]), pl.BlockSpec((1, 1, per_step), lambda b, g, c: (g, 0, 0))),
        (grouped(prm["dt_bias"]), pl.BlockSpec((1, 1, per_step), lambda b, g, c: (g, 0, 0))),
        (prm["norm_w"].reshape(1, GDN_D), pl.BlockSpec((1, GDN_D), lambda b, g, c: (0, 0))),
    ]
    ops += [p for p, _ in params]
    specs += [s for _, s in params]
    st_spec = pl.BlockSpec((grp.bb, per_step, GDN_D, GDN_D), lambda b, g, c: (b, g, 0, 0))
    ops.append(s0)
    specs.append(st_spec)
    rows = grp.B * grp.L
    nc = grp.nc
    return pl.pallas_call(
        functools.partial(_gdn_body, grp=grp),
        grid=(grp.B // grp.bb, ng, nc),
        in_specs=specs,
        out_specs=[pl.BlockSpec((MIX_ROWS, width), lambda b, g, c: (b * nc + c, g)), st_spec],
        out_shape=[jax.ShapeDtypeStruct((rows, GDN_V), BF16), jax.ShapeDtypeStruct(s0.shape, F32)],
        compiler_params=_cp("parallel", "parallel", "arbitrary", flags=MIXER_FLAGS),
        name="gdn",
    )(*ops)


def _xattn_body(q_ref, k_ref, v_ref, o_ref, *, nseq, lq):
    scale = XA_HD ** -0.5
    for j in range(nseq):
        qrows = slice(j * lq, (j + 1) * lq)
        mrows = slice(j * N_MEM, (j + 1) * N_MEM)
        outs = []
        for h in range(XA_HEADS):
            sl = slice(h * XA_HD, (h + 1) * XA_HD)
            s = _dot_nt(q_ref[qrows, sl], k_ref[mrows, sl]) * scale
            s = s - jnp.max(s, axis=-1, keepdims=True)
            p = jnp.exp(s)
            p = p / jnp.sum(p, axis=-1, keepdims=True)
            outs.append(_dot(p, v_ref[mrows, sl]))
        o_ref[qrows, :] = jnp.concatenate(outs, axis=1).astype(o_ref.dtype)


def _xattn_packed_body(q_ref, k_ref, v_ref, o_ref, *, nseq, lq):
    scale = XA_HD ** -0.5
    rows, cols = XA_HEADS * lq, N_MEM * XA_HEADS
    own_head = (_iota2((rows, cols), 0) // lq) == (_iota2((rows, cols), 1) % XA_HEADS)
    for j in range(nseq):
        q = q_ref[j * lq:(j + 1) * lq, :].astype(F32)
        qh = _stack_heads(q, XA_HEADS, XA_HD)
        mrows = slice(j * cols, (j + 1) * cols)
        s = jnp.where(own_head, _dot_nt(qh, k_ref[mrows, :]) * scale, -1e30)
        p = jnp.where(own_head, jnp.exp(s - jnp.max(s, axis=-1, keepdims=True)), 0.0)
        o = _dot(p, v_ref[mrows, :]) / jnp.sum(p, axis=-1, keepdims=True)
        o_ref[j * lq:(j + 1) * lq, :] = _unstack_heads(o, XA_HEADS).astype(o_ref.dtype)


def _xattn(grp, q, mem_k, mem_v, mem_row0):
    rows = grp.B * grp.L
    if grp.L >= XA_TQ:
        nseq, lq = 1, XA_TQ
        per = grp.L // XA_TQ
        body = _xattn_body
        mem_spec = pl.BlockSpec((N_MEM, XA_DIM), lambda i: (mem_row0 // N_MEM + i // per, 0))
    else:
        nseq, lq = MIX_ROWS // grp.L, grp.L
        body = _xattn_packed_body
        blk = nseq * N_MEM * XA_HEADS
        mem_spec = pl.BlockSpec((blk, XA_HD), lambda i: (mem_row0 * XA_HEADS // blk + i, 0))
    q_spec = pl.BlockSpec((nseq * lq, XA_DIM), lambda i: (i, 0))
    return pl.pallas_call(
        functools.partial(body, nseq=nseq, lq=lq),
        grid=(rows // (nseq * lq),),
        in_specs=[q_spec, mem_spec, mem_spec],
        out_specs=q_spec,
        out_shape=jax.ShapeDtypeStruct((rows, XA_DIM), BF16),
        compiler_params=_cp("parallel"),
        name="xattn",
    )(q, mem_k, mem_v)


def _ffn_in_body(*refs, grp, tiles_per_seq):
    it = iter(refs)
    a_ref, g_ref, wg_ref, wu_ref, hist_ref, cw_ref, cb_ref, act_ref, tail_ref, an_ref = (next(it) for _ in range(10))
    carry_ref = None if grp.embedded else next(it)
    i, j = pl.program_id(0), pl.program_id(1)

    @pl.when(j == 0)
    def _():
        x = a_ref[...]
        y = x * lax.rsqrt(jnp.mean(x * x, axis=-1, keepdims=True) + EPS)
        an_ref[...] = (y * g_ref[...]).astype(BF16)

    a = an_ref[...]
    gate = jnp.dot(a, wg_ref[...], preferred_element_type=F32)
    up = jnp.dot(a, wu_ref[...], preferred_element_type=F32)
    if grp.embedded:
        halo = hist_ref[...]
        tail_ref[...] = gate
    else:
        halo = jnp.where(i % tiles_per_seq == 0, hist_ref[...], carry_ref[j])
        last = gate[gate.shape[0] - SUBLANES:]
        carry_ref[j] = last
        tail_ref[...] = last
    conv = _conv(grp, gate, halo, cw_ref, cb_ref[...], FFN_CONV_W)
    act_ref[...] = (_silu(conv) * up).astype(act_ref.dtype)


def _ffn_in(grp, h, gain, w_in, layer, hist, conv_w, conv_b, tm=MM_TM, tn=MM_TN):
    assert grp.embedded or grp.L % tm == 0
    rows, k = h.shape
    nct = D_FF // tn
    tiles_per_seq = max(grp.L // tm, 1)
    hrows = tm if grp.embedded else SUBLANES
    tile_idx = lambda i, j: (i, j)
    hist_idx = tile_idx if grp.embedded else (lambda i, j: (i // tiles_per_seq, j))
    specs = [
        pl.BlockSpec((tm, k), lambda i, j: (i, 0)),
        pl.BlockSpec((1, k), lambda i, j: (0, 0)),
        _weight_spec(w_in, layer, k, tn, lambda j: j),
        _weight_spec(w_in, layer, k, tn, lambda j: nct + j),
        pl.BlockSpec((hrows, tn), hist_idx),
        pl.BlockSpec((FFN_CONV_W, tn), lambda i, j: (0, j)),
        pl.BlockSpec((1, tn), lambda i, j: (0, j)),
    ]
    scratch = [pltpu.VMEM((tm, k), BF16)]
    if not grp.embedded:
        scratch.append(pltpu.VMEM((nct, SUBLANES, tn), F32))
    act, tails = pl.pallas_call(
        functools.partial(_ffn_in_body, grp=grp, tiles_per_seq=tiles_per_seq),
        grid=(rows // tm, nct),
        in_specs=specs,
        out_specs=[pl.BlockSpec((tm, tn), tile_idx), pl.BlockSpec((hrows, tn), tile_idx)],
        out_shape=[jax.ShapeDtypeStruct((rows, D_FF), BF16), jax.ShapeDtypeStruct((rows // tm * hrows, D_FF), F32)],
        scratch_shapes=scratch,
        compiler_params=_cp("arbitrary", "arbitrary"),
        name="ffn_in",
    )(h, gain.reshape(1, k), w_in, w_in, hist, conv_w, conv_b.reshape(1, D_FF))
    if not grp.embedded:
        tails = tails.reshape(grp.B, tiles_per_seq, SUBLANES, D_FF)[:, -1].reshape(grp.B * SUBLANES, D_FF)
    return act, tails


def _history(grp, buf, width):
    b, k, c = buf.shape
    if grp.embedded:
        h = jnp.pad(buf, ((0, 0), (grp.first - k, grp.L - grp.first), (0, 0)))
        return h.reshape(b * grp.L, c)
    return jnp.pad(buf, ((0, 0), (SUBLANES - k, 0), (0, 0))).reshape(b * SUBLANES, c)


def _tail(grp, x2d, k, cols):
    x = x2d.reshape(grp.B, grp.L, x2d.shape[1])
    return x[:, grp.L - k:, cols].astype(F32)


def _run_group(grp, x2d, mem_k, mem_v, mem_row0, st, wts, prm):
    h = x2d
    out = {}
    proj = _matmul(h, wts["in_ab"], gain=prm["norm_mix"][0], out_dtype=BF16)
    y_rw, s_rw = _rwkv(grp, proj, _history(grp, st["rw_shift"][:, None, :], RW_PROJ), st["rwkv"], prm["rw"])
    y_ssd, s_ssd = _ssd(grp, proj, _history(grp, st["ssd_conv"], SSD_CONV_DIM),
                        st["ssd"].reshape(grp.B, SSD_GROUPS, SSD_GDIM, SSD_STATE), prm["ssd"])
    out["rwkv"] = s_rw
    out["rw_shift"] = _tail(grp, proj, 1, slice(AB_RW0, AB_RW0 + RW_PROJ))[:, 0]
    out["ssd"] = s_ssd.reshape(st["ssd"].shape)
    out["ssd_conv"] = _tail(grp, proj, CONV_W - 1, slice(AB_X0, AB_X0 + SSD_CONV_DIM))
    h = _matmul([y_rw, y_ssd], wts["out_ab"], res=h)
    ffn_bufs = []
    for l in range(2):
        if l == 1:
            proj = _matmul(h, wts["in_c"], gain=prm["norm_mix"][1], out_dtype=BF16)
            y_c, s_gdn = _gdn(grp, proj, _history(grp, st["gdn_conv"], GDN_CONV_DIM), st["gdn"], prm["gdn"])
            out["gdn"] = s_gdn
            out["gdn_conv"] = _tail(grp, proj, CONV_W - 1, slice(0, GDN_CONV_DIM))
            h = _matmul(y_c, wts["out_c"], res=h)
        q = _matmul(h, wts["xq"], layer=l, gain=prm["norm_xa"][l], out_dtype=BF16)
        o = _xattn(grp, q, mem_k, mem_v, mem_row0[l])
        h = _matmul(o, wts["xo"], layer=l, res=h)
        act, gate_tail = _ffn_in(grp, h, prm["norm_ffn"][l], wts["ffn_in"], l, _history(grp, st["ffn_conv"][l], D_FF),
                                 prm["ffn_conv_w"][l], prm["ffn_conv_b"][l])
        ffn_bufs.append(gate_tail.reshape(grp.B, SUBLANES, D_FF)[:, SUBLANES - (FFN_CONV_W - 1):])
        h = _matmul(act, wts["ffn_out"], layer=l, res=h)
    out["ffn_conv"] = jnp.stack(ffn_bufs)
    out["y"] = _rmsnorm(h, prm["norm_final"])
    return out


def kernel(x_prompt, x_sample, mem_prompt, state_rwkv, state_rwkv_shift, state_ssd, state_ssd_conv, state_gdn,
           state_gdn_conv, state_ffn_conv, cache_mem_k, cache_mem_v, norm_mix, norm_xa, norm_mem, norm_ffn,
           norm_final, w_in_ab, rw_mu, rw_w0, rw_w_up, rw_a0, rw_a_up, rw_g_up, rw_k_k, rw_k_a, rw_r_k, rw_gn_w,
           rw_gn_b, ssd_conv_w, ssd_conv_b, ssd_dt_bias, ssd_A_log, ssd_D, ssd_norm_w, w_out_ab, w_in_c,
           gdn_conv_w, gdn_A_log, gdn_dt_bias, gdn_norm_w, w_out_c, w_xq, w_xk, w_xv, w_xo, ffn_w_in, ffn_conv_w,
           ffn_conv_b, ffn_w_out):
    bp, lp, _ = x_prompt.shape
    bs, ls, _ = x_sample.shape
    depth = norm_mix.shape[0]
    assert depth == 2 and w_in_ab.shape[0] == 1 and w_in_c.shape[0] == 1
    assert lp % MIX_ROWS == 0 and lp % XA_TQ == 0 and lp % MM_TM == 0
    pad_rows = SUBLANES - ls
    assert CONV_W - 1 <= pad_rows and CONV_W - 1 <= ls and MIX_ROWS % SUBLANES == 0
    gp = Group(B=bp, L=lp, first=0, bb=1)
    gs = Group(B=bs, L=SUBLANES, first=pad_rows, bb=MIX_ROWS // SUBLANES)
    assert bs % gs.bb == 0 and (bs * SUBLANES) % MM_TM == 0 and (bp * lp) % MM_TM == 0

    wts = dict(
        in_ab=_ab_layout(w_in_ab[0]).astype(BF16), out_ab=w_out_ab[0].astype(BF16),
        in_c=_gdn_layout(w_in_c[0]).astype(BF16), out_c=w_out_c[0].astype(BF16),
        xq=w_xq.astype(BF16), xo=w_xo.astype(BF16), ffn_in=ffn_w_in.astype(BF16), ffn_out=ffn_w_out.astype(BF16),
    )
    prm = dict(
        norm_mix=norm_mix, norm_xa=norm_xa, norm_ffn=norm_ffn, norm_final=norm_final,
        ffn_conv_w=ffn_conv_w, ffn_conv_b=ffn_conv_b,
        rw=dict(mu=rw_mu[0], w0=rw_w0[0], w_up=rw_w_up[0], a0=rw_a0[0], a_up=rw_a_up[0], g_up=rw_g_up[0],
                k_k=rw_k_k[0], k_a=rw_k_a[0], r_k=rw_r_k[0], gn_w=rw_gn_w[0], gn_b=rw_gn_b[0]),
        ssd=dict(conv_w=ssd_conv_w[0], conv_b=ssd_conv_b[0], dt_bias=ssd_dt_bias[0], A_log=ssd_A_log[0],
                 D=ssd_D[0], norm_w=ssd_norm_w[0]),
        gdn=dict(conv_w=gdn_conv_w[0], A_log=gdn_A_log[0], dt_bias=gdn_dt_bias[0], norm_w=gdn_norm_w[0]),
    )

    mem2d = mem_prompt.reshape(bp * N_MEM, D_MODEL)
    mem_tm = min(MM_TM, bp * N_MEM)
    mk, mv = [], []
    for l in range(depth):
        mk.append(_matmul(mem2d, w_xk[l].astype(BF16), gain=norm_mem[l], tm=mem_tm))
        mv.append(_matmul(mem2d, w_xv[l].astype(BF16), gain=norm_mem[l], tm=mem_tm))
    mem_k_p = jnp.stack(mk)
    mem_v_p = jnp.stack(mv)

    zeros = lambda *s: jnp.zeros(s, F32)
    st_p = dict(
        rwkv=zeros(bp, RW_HEADS, RW_HD, RW_HD), rw_shift=zeros(bp, RW_PROJ),
        ssd=zeros(bp, SSD_GROUPS, SSD_HPG, SSD_HD, SSD_STATE), ssd_conv=zeros(bp, CONV_W - 1, SSD_CONV_DIM),
        gdn=zeros(bp, GDN_HEADS, GDN_D, GDN_D), gdn_conv=zeros(bp, CONV_W - 1, GDN_CONV_DIM),
        ffn_conv=zeros(depth, bp, FFN_CONV_W - 1, D_FF),
    )
    rp = _run_group(gp, x_prompt.reshape(bp * lp, D_MODEL), mem_k_p.reshape(depth * bp * N_MEM, XA_DIM),
                    mem_v_p.reshape(depth * bp * N_MEM, XA_DIM), [l * bp * N_MEM for l in range(depth)],
                    st_p, wts, prm)

    st_s = dict(rwkv=state_rwkv[0], rw_shift=state_rwkv_shift[0], ssd=state_ssd[0], ssd_conv=state_ssd_conv[0],
                gdn=state_gdn[0], gdn_conv=state_gdn_conv[0], ffn_conv=state_ffn_conv)
    xs = jnp.pad(x_sample, ((0, 0), (pad_rows, 0), (0, 0))).reshape(bs * SUBLANES, D_MODEL)
    rs = _run_group(gs, xs, cache_mem_k.reshape(depth * bs * N_MEM * XA_HEADS, XA_HD),
                    cache_mem_v.reshape(depth * bs * N_MEM * XA_HEADS, XA_HD), [l * bs * N_MEM for l in range(depth)],
                    st_s, wts, prm)

    y_p = rp["y"].reshape(bp, lp, D_MODEL)
    y_s = rs["y"].reshape(bs, SUBLANES, D_MODEL)[:, pad_rows:]
    lead = lambda x: x[None]
    mem_shape = (depth, bp, N_MEM, XA_HEADS, XA_HD)
    return (y_p, y_s, lead(rp["rwkv"]), lead(rs["rwkv"]), lead(rp["rw_shift"]), lead(rs["rw_shift"]),
            lead(rp["ssd"]), lead(rs["ssd"]), lead(rp["ssd_conv"]), lead(rs["ssd_conv"]),
            lead(rp["gdn"]), lead(rs["gdn"]), lead(rp["gdn_conv"]), lead(rs["gdn_conv"]),
            rp["ffn_conv"], rs["ffn_conv"], mem_k_p.reshape(mem_shape), mem_v_p.reshape(mem_shape))
```

```python
import functools
from typing import NamedTuple

import jax
import jax.numpy as jnp
from jax import lax
from jax.experimental import pallas as pl
from jax.experimental.pallas import tpu as pltpu

F32 = jnp.float32
BF16 = jnp.bfloat16
HI = lax.Precision.HIGHEST

D_MODEL = 2048
EPS = 1e-6
RW_HEADS, RW_HD = 16, 64
RW_DIM = RW_HEADS * RW_HD
RW_LORA = 256
RW_PROJ = 3 * RW_DIM + RW_LORA
RW_GN_EPS = 6.4e-4
SSD_HEADS, SSD_HD, SSD_GROUPS, SSD_STATE = 16, 64, 2, 128
SSD_HPG = SSD_HEADS // SSD_GROUPS
SSD_DIM = SSD_HEADS * SSD_HD
SSD_GDIM = SSD_HPG * SSD_HD
SSD_CONV_DIM = SSD_DIM + 2 * SSD_GROUPS * SSD_STATE
SSD_PROJ = SSD_DIM + SSD_CONV_DIM + SSD_HEADS
AB_PROJ = RW_PROJ + SSD_PROJ
GDN_HEADS, GDN_D = 16, 128
GDN_V = GDN_HEADS * GDN_D
GDN_CONV_DIM = 3 * GDN_V
GDN_PROJ = GDN_CONV_DIM + GDN_V + 2 * GDN_HEADS
D_FF = 5632
XA_HEADS, XA_HD, N_MEM = 4, 128, 256
XA_DIM = XA_HEADS * XA_HD
CONV_W = 4
FFN_CONV_W = 3

AB_Z0 = 0
AB_X0 = SSD_DIM
AB_B0 = 2 * SSD_DIM
AB_C0 = AB_B0 + SSD_GROUPS * SSD_STATE
AB_DT0 = AB_C0 + SSD_GROUPS * SSD_STATE
AB_RW0 = 3 * RW_DIM
AB_COLS = 6656
assert AB_DT0 + 128 <= AB_RW0 and AB_RW0 + RW_PROJ <= AB_COLS and AB_RW0 % RW_DIM == 0

SUBLANES = 8
LANES = 128
VMEM_LIMIT = 56 * 1024 * 1024

MIX_ROWS = 64
INV_BLOCK = 16
INV_SPLIT = 128
MM_TM, MM_TN = 1024, 512
RW_HP = 2
RW_SUB = 4
GDN_G = 4
GDN_SUB = 4
XA_TQ = 512
NORM_TM = 512


class Group(NamedTuple):
    B: int
    L: int
    first: int
    bb: int

    @property
    def qs(self):
        return MIX_ROWS // self.bb

    @property
    def nc(self):
        return self.L // self.qs

    @property
    def embedded(self):
        return self.first > 0


MIXER_FLAGS = None


def _cp(*sem, flags=None):
    return pltpu.CompilerParams(dimension_semantics=sem, vmem_limit_bytes=VMEM_LIMIT, flags=flags)


def _dot(a, b):
    return jnp.dot(a.astype(BF16), b.astype(BF16), preferred_element_type=F32)


def _dot_nt(a, b):
    return lax.dot_general(a.astype(BF16), b.astype(BF16), (((1,), (1,)), ((), ())), preferred_element_type=F32)


def _dot_tn(a, b):
    return lax.dot_general(a.astype(BF16), b.astype(BF16), (((0,), (0,)), ((), ())), preferred_element_type=F32)


def _dot_hi(a, b):
    return jnp.dot(a, b, preferred_element_type=F32, precision=HI)


def _split_bf16(x):
    hi = x.astype(BF16)
    return hi, (x - hi.astype(F32)).astype(BF16)


def _dot_x3(a, b):
    ah, al = _split_bf16(a)
    bh, bl = _split_bf16(b)
    return jnp.dot(jnp.concatenate([ah, ah, al], axis=1), jnp.concatenate([bh, bl, bh], axis=0),
                   preferred_element_type=F32)


def _dot_nt_hi(a, b):
    return lax.dot_general(a, b, (((1,), (1,)), ((), ())), preferred_element_type=F32, precision=HI)


def _sigmoid(x):
    return 1.0 / (1.0 + jnp.exp(-x))


def _silu(x):
    return x * _sigmoid(x)


def _softplus(x):
    return jnp.maximum(x, 0.0) + jnp.log(1.0 + jnp.exp(-jnp.abs(x)))


def _iota2(shape, axis):
    return lax.broadcasted_iota(jnp.int32, shape, axis)


class _Masks(NamedTuple):
    incl: jax.Array
    strict: jax.Array
    eye: jax.Array
    cumsum: jax.Array
    valid: jax.Array


def _masks(grp, heads=1):
    r = MIX_ROWS
    n = heads * r
    ri, ci = _iota2((n, n), 0), _iota2((n, n), 1)
    same = (ri // grp.qs) == (ci // grp.qs)
    incl = same & (ci <= ri)
    strict = same & (ci < ri)
    si, sj = _iota2((2 * r, r), 0), _iota2((2 * r, r), 1)
    same_seq = ((si % r) // grp.qs) == (sj // grp.qs)
    cumsum = (same_seq & ((sj <= si) | (si >= r))).astype(F32)
    t = _iota2((r, 1), 0) % grp.qs
    return _Masks(incl, strict, (ri == ci).astype(F32), cumsum, t >= grp.first)


def _neumann(lm, rhs, nil, eye=None):
    n = lm.shape[0]
    p = None
    if rhs is None:
        t = eye + lm
        if nil > 2:
            p = _dot(lm, lm)
            yield
    elif nil > 2:
        both = _dot(lm, jnp.concatenate([lm, rhs], axis=1))
        yield
        p, t = both[:, :n], rhs + both[:, n:]
    else:
        t = rhs + _dot(lm, rhs)
        yield
    k = 2
    while k < nil:
        if 2 * k < nil:
            both = _dot(p, jnp.concatenate([p, t], axis=1))
            p, t = both[:, :n], t + both[:, n:]
        else:
            t = t + _dot(p, t)
        yield
        k *= 2
    return t


def _inv_unit_lower(lm, eye, qs):
    if qs <= INV_BLOCK:
        return (yield from _neumann(lm, None, qs, eye))
    n = lm.shape[0]
    diag = (_iota2((n, n), 0) // INV_BLOCK) == (_iota2((n, n), 1) // INV_BLOCK)
    d = jnp.where(diag, lm, 0.0)
    dinv = yield from _neumann(d, None, INV_BLOCK, eye)
    nm = _dot(dinv, lm - d)
    yield
    return (yield from _neumann(nm, dinv, qs // INV_BLOCK))


def _lockstep(gens):
    results = [None] * len(gens)
    active = list(range(len(gens)))
    while active:
        for i in list(active):
            try:
                next(gens[i])
            except StopIteration as stop:
                results[i] = stop.value
                active.remove(i)
        if active:
            yield
    return results


def _solve_unit_lower(lm, rhs, eye, qs):
    n = lm.shape[0]
    if n > INV_SPLIT:
        blocks = [slice(i, i + INV_SPLIT) for i in range(0, n, INV_SPLIT)]
        inv = yield from _lockstep([_inv_unit_lower(lm[b, b], eye[b, b], qs) for b in blocks])
        zero = jnp.zeros((INV_SPLIT, INV_SPLIT), F32)
        t = jnp.concatenate([jnp.concatenate([inv[i] if i == j else zero for j in range(len(blocks))], axis=1)
                             for i in range(len(blocks))], axis=0)
    else:
        t = yield from _inv_unit_lower(lm, eye, qs)
    x = _dot(t, rhs)
    yield
    return x


def _interleave(gens):
    results = [None] * len(gens)
    active = list(range(len(gens)))
    while active:
        for i in list(active):
            try:
                next(gens[i])
            except StopIteration as stop:
                results[i] = stop.value
                active.remove(i)
    return results


def _stack_heads(x, heads, width):
    return jnp.concatenate([x[:, h * width:(h + 1) * width] for h in range(heads)], axis=0)


def _unstack_heads(x, heads):
    r = x.shape[0] // heads
    return jnp.concatenate([x[h * r:(h + 1) * r] for h in range(heads)], axis=1)


def _seq_rows(grp, x, j, heads):
    if grp.bb == 1:
        return x
    qs = grp.qs
    return jnp.concatenate([x[h * MIX_ROWS + j * qs:h * MIX_ROWS + (j + 1) * qs] for h in range(heads)], axis=0)


def _from_seq_rows(grp, parts, heads):
    if grp.bb == 1:
        return parts[0]
    qs = grp.qs
    return jnp.concatenate([p[h * qs:(h + 1) * qs] for h in range(heads) for p in parts], axis=0)


def _expand_rows(x, heads, rows_per_head, width):
    m = x.shape[0]
    if x.shape[1] == width:
        x = jnp.concatenate([x] * heads, axis=1)
    keep = ((_iota2((m, heads * width), 0) // rows_per_head) % heads) == (_iota2((m, heads * width), 1) // width)
    return jnp.where(keep, x, 0.0)


def _taps(grp, cur, halo, width):
    if grp.embedded:
        t = _iota2((cur.shape[0], 1), 0) % grp.qs
        full = jnp.where(t < grp.first, halo, cur)
        return [full] + [pltpu.roll(full, s, axis=0) for s in range(1, width)]
    full = jnp.concatenate([halo, cur], axis=0)
    return [cur] + [pltpu.roll(full, s, axis=0)[SUBLANES:] for s in range(1, width)]


def _conv(grp, cur, halo, w_ref, bias, width):
    taps = _taps(grp, cur, halo, width)
    y = taps[0] * w_ref[width - 1:width, :]
    for s in range(1, width):
        y = y + taps[s] * w_ref[width - 1 - s:width - s, :]
    return y if bias is None else y + bias


class _Tok(NamedTuple):
    arr: jax.Array
    width: int
    col: object
    hist: jax.Array = None
    hcol: object = None


def _mixer_specs(grp, toks):
    nc = grp.nc
    ops, specs = [], []
    for t in toks:
        ops.append(t.arr)
        specs.append(pl.BlockSpec((MIX_ROWS, t.width), lambda b, g, c, t=t: (b * nc + c, t.col(g))))
    for t in toks:
        if t.hist is None:
            continue
        if not grp.embedded:
            tile = _sublane_tile(t.arr.dtype)
            sub = MIX_ROWS // tile
            ops.append(t.arr)
            specs.append(pl.BlockSpec(
                (tile, t.width), lambda b, g, c, t=t, sub=sub: (jnp.maximum((b * nc + c) * sub - 1, 0), t.col(g))))
        ops.append(t.hist)
        hrows = MIX_ROWS if grp.embedded else SUBLANES
        specs.append(pl.BlockSpec((hrows, t.width), lambda b, g, c, t=t: (b, t.hcol(g))))
    return ops, specs


def _read_windows(grp, refs, n_plain, n_hist):
    c = pl.program_id(2)
    cur = [r[...].astype(F32) for r in refs[:n_plain]]
    rest = refs[n_plain:]
    halos = []
    for i in range(n_hist):
        if grp.embedded:
            halos.append(rest[i][...])
        else:
            prev = rest[2 * i][...]
            prev = prev[prev.shape[0] - SUBLANES:].astype(F32)
            halos.append(jnp.where(c == 0, rest[2 * i + 1][...], prev))
    used = n_hist if grp.embedded else 2 * n_hist
    return cur, halos, rest[used:]


def _sublane_tile(dtype):
    return SUBLANES * 4 // jnp.dtype(dtype).itemsize


def _param_spec(width, col):
    return lambda rows: pl.BlockSpec((rows, width), lambda b, g, c: (0, col(g)))


def _matmul_body(*refs, norm, has_res, n_a):
    it = iter(refs)
    a_refs = [next(it) for _ in range(n_a)]
    g_ref = next(it) if norm else None
    w_ref = next(it)
    res_ref = next(it) if has_res else None
    o_ref = next(it)
    if norm:
        an_ref = next(it)

        @pl.when(pl.program_id(1) == 0)
        def _():
            x = a_refs[0][...]
            y = x * lax.rsqrt(jnp.mean(x * x, axis=-1, keepdims=True) + EPS)
            an_ref[...] = (y * g_ref[...]).astype(BF16)

        acc = jnp.dot(an_ref[...], w_ref[...], preferred_element_type=F32)
    else:
        acc, k0 = None, 0
        for a_ref in a_refs:
            k1 = k0 + a_ref.shape[1]
            part = jnp.dot(a_ref[...], w_ref[k0:k1, :], preferred_element_type=F32)
            acc = part if acc is None else acc + part
            k0 = k1
    if has_res:
        acc = acc + res_ref[...]
    o_ref[...] = acc.astype(o_ref.dtype)


def _matmul(a, w, *, layer=None, gain=None, res=None, out_dtype=F32, tm=MM_TM, tn=MM_TN):
    a_list = list(a) if isinstance(a, (list, tuple)) else [a]
    m = a_list[0].shape[0]
    k, n = w.shape[-2:]
    assert sum(x.shape[1] for x in a_list) == k
    norm = gain is not None
    assert not norm or len(a_list) == 1
    ops = list(a_list)
    specs = [pl.BlockSpec((tm, x.shape[1]), lambda i, j: (i, 0)) for x in a_list]
    if norm:
        ops.append(gain.reshape(1, k))
        specs.append(pl.BlockSpec((1, k), lambda i, j: (0, 0)))
    ops.append(w)
    specs.append(_weight_spec(w, layer, k, tn, lambda j: j))
    if res is not None:
        ops.append(res)
        specs.append(pl.BlockSpec((tm, tn), lambda i, j: (i, j)))
    return pl.pallas_call(
        functools.partial(_matmul_body, norm=norm, has_res=res is not None, n_a=len(a_list)),
        grid=(m // tm, n // tn),
        in_specs=specs,
        out_specs=pl.BlockSpec((tm, tn), lambda i, j: (i, j)),
        out_shape=jax.ShapeDtypeStruct((m, n), out_dtype),
        scratch_shapes=[pltpu.VMEM((tm, k), BF16)] if norm else [],
        compiler_params=_cp("parallel", "arbitrary"),
        name="matmul",
    )(*ops)


def _weight_spec(w, layer, k, tn, col):
    if w.ndim == 2:
        return pl.BlockSpec((k, tn), lambda i, j: (0, col(j)))
    return pl.BlockSpec((None, k, tn), lambda i, j: (layer, 0, col(j)))


def _rmsnorm_body(x_ref, g_ref, o_ref):
    x = x_ref[...]
    y = x * lax.rsqrt(jnp.mean(x * x, axis=-1, keepdims=True) + EPS)
    o_ref[...] = (y * g_ref[...]).astype(o_ref.dtype)


def _rmsnorm(x, gain, out_dtype=F32, tm=NORM_TM):
    m, k = x.shape
    return pl.pallas_call(
        _rmsnorm_body,
        grid=(m // tm,),
        in_specs=[pl.BlockSpec((tm, k), lambda i: (i, 0)), pl.BlockSpec((1, k), lambda i: (0, 0))],
        out_specs=pl.BlockSpec((tm, k), lambda i: (i, 0)),
        out_shape=jax.ShapeDtypeStruct((m, k), out_dtype),
        compiler_params=_cp("parallel"),
        name="rmsnorm",
    )(x, gain.reshape(1, k))


def _rwkv_body(*refs, grp):
    r_rows = MIX_ROWS
    qs, bb, nc = grp.qs, grp.bb, grp.nc
    nh = 2 * RW_HP
    sw = nh * RW_HD
    width = RW_SUB * sw
    n = nh * r_rows
    cur, halos, rest = _read_windows(grp, refs, 4, 4)
    (mu_r, mu_k, mu_v, mu_lo, w0_ref, a0_ref, lora_ref, kk_ref, ka_ref, rk_ref, gnw_ref, gnb_ref,
     s0_ref, y_ref, s_ref, sbig_ref) = rest
    c = pl.program_id(2)

    @pl.when(c == 0)
    def _():
        for j in range(bb):
            for sub in range(RW_SUB):
                st = jnp.concatenate([s0_ref[j, sub * nh + h] for h in range(nh)], axis=0)
                sbig_ref[j, sub] = _expand_rows(st, nh, RW_HD, RW_HD)

    m = _masks(grp, nh)

    def shifted(i, mu_ref):
        x = cur[i]
        prev = _taps(grp, x, halos[i], 2)[1]
        return x + (prev - x) * mu_ref[...]

    r = shifted(0, mu_r)
    k = shifted(1, mu_k)
    v = shifted(2, mu_v)
    lo = shifted(3, mu_lo)
    lora_in = jnp.concatenate([jnp.tanh(lo[:, :64]), lo[:, 64:128], _sigmoid(lo[:, 128:256])], axis=1)
    lora = _dot_x3(lora_in, lora_ref[0])
    w = -_softplus(-(w0_ref[...] + lora[:, :width])) - 0.5
    lw = jnp.where(m.valid, -jnp.exp(w), 0.0)
    a = _sigmoid(a0_ref[...] + lora[:, width:2 * width])
    g = lora[:, 2 * width:]

    bd = ((_iota2((sw, sw), 0) // RW_HD) == (_iota2((sw, sw), 1) // RW_HD)).astype(F32)

    def head_sums(x):
        return _unstack_heads(_dot_x3(_stack_heads(x, RW_SUB, sw), bd), RW_SUB)

    kx = k * kk_ref[...]
    kp = jnp.where(m.valid, k * (1.0 + (a - 1.0) * ka_ref[...]), 0.0)
    sums = head_sums(jnp.concatenate([kx * kx, r * kp * rk_ref[...]], axis=0))
    kkn = kx * lax.rsqrt(sums[:r_rows] + 1e-6)
    bonus = sums[r_rows:] * v
    at = jnp.where(m.valid, -kkn, 0.0)
    bt = jnp.where(m.valid, kkn * a, 0.0)

    cums = _dot_hi(m.cumsum, lw)
    cum, cum_end = cums[:r_rows], cums[r_rows:]
    e_neg = jnp.exp(-cum)
    e_end = jnp.exp(cum_end - cum)
    p_end = jnp.exp(cum_end)
    a_til, r_til = at * jnp.exp(cum - lw), r * jnp.exp(cum)
    b_til, k_til = bt * e_neg, kp * e_neg
    b_hat, k_hat = bt * e_end, kp * e_end

    def stack(sub):
        lanes = slice(sub * sw, (sub + 1) * sw)

        def tile(x):
            return _expand_rows(jnp.concatenate([x[:, lanes]] * nh, axis=0), nh, r_rows, RW_HD)

        ar = jnp.concatenate([tile(a_til), tile(r_til)], axis=0)
        bk = jnp.concatenate([tile(b_til), tile(k_til)], axis=0)
        bk_end = jnp.concatenate([tile(b_hat), tile(k_hat)], axis=0)
        v_exp = tile(v)
        gm = _dot_nt(ar, bk)
        yield
        m_ab = jnp.where(m.strict, gm[:n, :n], 0.0)
        m_ak = jnp.where(m.strict, gm[:n, n:], 0.0)
        m_r = jnp.concatenate([jnp.where(m.incl, gm[n:, :n], 0.0), jnp.where(m.incl, gm[n:, n:], 0.0)], axis=1)

        def seq2(x, j):
            if bb == 1:
                return x
            return jnp.concatenate([_seq_rows(grp, x[:n], j, nh), _seq_rows(grp, x[n:], j, nh)], axis=0)

        parts = []
        for j in range(bb):
            parts.append(_dot_nt(seq2(ar, j), sbig_ref[j, sub]))
            yield
        half = nh * qs
        as_a = _from_seq_rows(grp, [p[:half] for p in parts], nh)
        as_r = _from_seq_rows(grp, [p[half:] for p in parts], nh)
        rhs = as_a + _dot(m_ak, v_exp)
        yield
        u = yield from _solve_unit_lower(m_ab, rhs, m.eye, qs)
        uv = jnp.concatenate([u, v_exp], axis=0)
        y_exp = as_r + _dot(m_r, uv)
        yield
        y = y_exp[:r_rows]
        for h in range(1, nh):
            y = y + y_exp[h * r_rows:(h + 1) * r_rows]
        for j in range(bb):
            sbig_ref[j, sub] = (sbig_ref[j, sub] * p_end[j * qs:j * qs + 1, lanes]
                                + _dot_tn(seq2(uv, j), seq2(bk_end, j)))
            yield
        return y

    y = jnp.concatenate(_interleave([stack(sub) for sub in range(RW_SUB)]), axis=1)

    @pl.when(c == nc - 1)
    def _():
        for j in range(bb):
            for sub in range(RW_SUB):
                sb = sbig_ref[j, sub]
                for h in range(nh):
                    s_ref[j, sub * nh + h] = sb[h * RW_HD:(h + 1) * RW_HD, h * RW_HD:(h + 1) * RW_HD]

    mean = head_sums(y) * (1.0 / RW_HD)
    d = y - mean
    var = head_sums(d * d) * (1.0 / RW_HD)
    out = d * lax.rsqrt(var + RW_GN_EPS) * gnw_ref[...] + gnb_ref[...] + bonus
    y_ref[...] = (out * g).astype(y_ref.dtype)


def _rwkv(grp, proj, hist, s0, prm):
    width = RW_SUB * RW_HP * LANES
    ng = RW_DIM // width
    lora_blk = 3 * RW_DIM // RW_LORA
    p0 = AB_RW0 // width
    toks = [
        _Tok(proj, width, lambda g: p0 + g, hist, lambda g: g),
        _Tok(proj, width, lambda g: p0 + ng + g, hist, lambda g: ng + g),
        _Tok(proj, width, lambda g: p0 + 2 * ng + g, hist, lambda g: 2 * ng + g),
        _Tok(proj, RW_LORA, lambda g: AB_RW0 // RW_LORA + lora_blk, hist, lambda g: lora_blk),
    ]
    ops, specs = _mixer_specs(grp, toks)
    head = _param_spec(width, lambda g: g)
    lora_w = jnp.zeros((ng, RW_LORA, 3 * width), F32)
    for i, (name, r0, r1) in enumerate((("w_up", 0, 64), ("a_up", 64, 128), ("g_up", 128, 256))):
        blk = prm[name].reshape(r1 - r0, ng, width).transpose(1, 0, 2)
        lora_w = lora_w.at[:, r0:r1, i * width:(i + 1) * width].set(blk)
    mu = prm["mu"].reshape(1, RW_PROJ)
    flat = lambda x: x.reshape(1, RW_DIM)
    params = [
        (mu, pl.BlockSpec((1, width), lambda b, g, c: (0, g))),
        (mu, pl.BlockSpec((1, width), lambda b, g, c: (0, ng + g))),
        (mu, pl.BlockSpec((1, width), lambda b, g, c: (0, 2 * ng + g))),
        (mu, pl.BlockSpec((1, RW_LORA), lambda b, g, c: (0, lora_blk))),
        (flat(prm["w0"]), head(1)), (flat(prm["a0"]), head(1)),
        (lora_w, pl.BlockSpec((1, RW_LORA, 3 * width), lambda b, g, c: (g, 0, 0))),
        (flat(prm["k_k"]), head(1)), (flat(prm["k_a"]), head(1)), (flat(prm["r_k"]), head(1)),
        (flat(prm["gn_w"]), head(1)), (flat(prm["gn_b"]), head(1)),
    ]
    ops += [p for p, _ in params]
    specs += [s for _, s in params]
    st_spec = pl.BlockSpec((grp.bb, RW_SUB * 2 * RW_HP, RW_HD, RW_HD), lambda b, g, c: (b, g, 0, 0))
    ops.append(s0)
    specs.append(st_spec)
    rows = grp.B * grp.L
    nc = grp.nc
    return pl.pallas_call(
        functools.partial(_rwkv_body, grp=grp),
        grid=(grp.B // grp.bb, ng, nc),
        in_specs=specs,
        out_specs=[pl.BlockSpec((MIX_ROWS, width), lambda b, g, c: (b * nc + c, g)), st_spec],
        out_shape=[jax.ShapeDtypeStruct((rows, RW_DIM), BF16), jax.ShapeDtypeStruct(s0.shape, F32)],
        scratch_shapes=[pltpu.VMEM((grp.bb, RW_SUB, RW_HP * LANES, RW_HP * LANES), F32)],
        compiler_params=_cp("parallel", "parallel", "arbitrary", flags=MIXER_FLAGS),
        name="rwkv7",
    )(*ops)


def _ssd_body(*refs, grp):
    r_rows = MIX_ROWS
    qs, bb = grp.qs, grp.bb
    nh, hd, gdim = SSD_HPG, SSD_HD, SSD_GDIM
    cur, halos, rest = _read_windows(grp, refs, 5, 3)
    cw_x, cw_b, cw_c, cb_x, cb_b, cb_c, dtb_ref, alog_ref, dskip_ref, nw_ref, h0_ref, y_ref, h_ref = rest
    x_raw, b_raw, c_raw, z, dt_blk = cur
    c = pl.program_id(2)

    @pl.when(c == 0)
    def _():
        h_ref[...] = h0_ref[...]

    m = _masks(grp)
    xs = _silu(_conv(grp, x_raw, halos[0], cw_x, cb_x[...], CONV_W))
    bm = _silu(_conv(grp, b_raw, halos[1], cw_b, cb_b[...], CONV_W))
    cm = _silu(_conv(grp, c_raw, halos[2], cw_c, cb_c[...], CONV_W))

    dt = jnp.where(m.valid, _softplus(dt_blk[:, :SSD_HEADS] + dtb_ref[...]), 0.0)
    cums = _dot_hi(m.cumsum, dt * (-jnp.exp(alog_ref[...])))
    acs, acs_end = cums[:r_rows], cums[r_rows:]
    eye_h = (_iota2((SSD_HEADS, SSD_HEADS), 0) == _iota2((SSD_HEADS, SSD_HEADS), 1)).astype(F32)
    acs_row = _dot_nt_hi(eye_h, acs)
    ei, ej = _iota2((3 * SSD_HEADS, 3 * SSD_DIM), 0), _iota2((3 * SSD_HEADS, 3 * SSD_DIM), 1)
    spread = ((ei // SSD_HEADS == ej // SSD_DIM) & (ei % SSD_HEADS == (ej % SSD_DIM) // hd)).astype(F32)
    lanes = _dot_hi(jnp.concatenate([dt, acs_end - acs, acs], axis=1), spread)
    xd = xs * lanes[:, :SSD_DIM]
    xd_dec = xd * jnp.exp(lanes[:, SSD_DIM:2 * SSD_DIM])
    acs_lanes = lanes[:, 2 * SSD_DIM:]

    def group(g):
        gl = slice(g * gdim, (g + 1) * gdim)
        bmg, cmg = bm[:, g * SSD_STATE:(g + 1) * SSD_STATE], cm[:, g * SSD_STATE:(g + 1) * SSD_STATE]
        cb = _dot_nt(cmg, bmg)
        yield
        yd = []
        for r in range(nh):
            h = g * nh + r
            diff = acs[:, h:h + 1] - acs_row[h:h + 1, :]
            lmat = jnp.where(m.incl, jnp.exp(jnp.where(m.incl, diff, 0.0)), 0.0)
            yd.append(_dot(cb * lmat, xd[:, h * hd:(h + 1) * hd]))
            yield
        y = jnp.concatenate(yd, axis=1)
        head_rows = (_iota2((gdim, SSD_HEADS), 0) // hd + g * nh == _iota2((gdim, SSD_HEADS), 1)).astype(F32)
        yoff = []
        for j in range(bb):
            rows = slice(j * qs, (j + 1) * qs)
            hj = h_ref[j, g]
            yoff.append(_dot_nt(cmg[rows], hj))
            yield
            end_col = jnp.exp(_dot_nt_hi(head_rows, acs_end[j * qs:j * qs + SUBLANES]))[:, :1]
            yield
            h_ref[j, g] = hj * end_col + _dot_tn(xd_dec[rows, gl], bmg[rows])
            yield
        yoff = yoff[0] if bb == 1 else jnp.concatenate(yoff, axis=0)
        y = y + yoff * jnp.exp(acs_lanes[:, gl]) + xs[:, gl] * dskip_ref[:, gl]
        yg = y * _silu(z[:, gl])
        yg = yg * lax.rsqrt(jnp.mean(yg * yg, axis=-1, keepdims=True) + EPS)
        return yg * nw_ref[:, gl]

    y = jnp.concatenate(_interleave([group(g) for g in range(SSD_GROUPS)]), axis=1)
    y_ref[...] = y.astype(y_ref.dtype)


def _ssd(grp, proj, hist, h0, prm):
    bc = SSD_GROUPS * SSD_STATE
    toks = [
        _Tok(proj, SSD_DIM, lambda g: AB_X0 // SSD_DIM, hist, lambda g: 0),
        _Tok(proj, bc, lambda g: AB_B0 // bc, hist, lambda g: SSD_DIM // bc),
        _Tok(proj, bc, lambda g: AB_C0 // bc, hist, lambda g: SSD_DIM // bc + 1),
        _Tok(proj, SSD_DIM, lambda g: AB_Z0 // SSD_DIM),
        _Tok(proj, LANES, lambda g: AB_DT0 // LANES),
    ]
    ops, specs = _mixer_specs(grp, toks)
    cw, cbias = prm["conv_w"], prm["conv_b"].reshape(1, SSD_CONV_DIM)
    conv_cols = [(SSD_DIM, lambda g: 0), (bc, lambda g: SSD_DIM // bc), (bc, lambda g: SSD_DIM // bc + 1)]
    params = [(cw, _param_spec(wd, col)(CONV_W)) for wd, col in conv_cols]
    params += [(cbias, _param_spec(wd, col)(1)) for wd, col in conv_cols]
    full = lambda n: pl.BlockSpec((1, n), lambda b, g, c: (0, 0))
    params += [
        (prm["dt_bias"].reshape(1, SSD_HEADS), full(SSD_HEADS)),
        (prm["A_log"].reshape(1, SSD_HEADS), full(SSD_HEADS)),
        (jnp.repeat(prm["D"], SSD_HD).reshape(1, SSD_DIM), full(SSD_DIM)),
        (prm["norm_w"].reshape(1, SSD_DIM), full(SSD_DIM)),
    ]
    ops += [p for p, _ in params]
    specs += [s for _, s in params]
    st_spec = pl.BlockSpec((grp.bb, SSD_GROUPS, SSD_GDIM, SSD_STATE), lambda b, g, c: (b, 0, 0, 0))
    ops.append(h0)
    specs.append(st_spec)
    rows = grp.B * grp.L
    nc = grp.nc
    return pl.pallas_call(
        functools.partial(_ssd_body, grp=grp),
        grid=(grp.B // grp.bb, 1, nc),
        in_specs=specs,
        out_specs=[pl.BlockSpec((MIX_ROWS, SSD_DIM), lambda b, g, c: (b * nc + c, 0)), st_spec],
        out_shape=[jax.ShapeDtypeStruct((rows, SSD_DIM), BF16), jax.ShapeDtypeStruct(h0.shape, F32)],
        compiler_params=_cp("parallel", "arbitrary", "arbitrary"),
        name="ssd",
    )(*ops)


def _gdn_body(*refs, grp):
    r_rows = MIX_ROWS
    qs, bb = grp.qs, grp.bb
    nh = GDN_G
    n = nh * r_rows
    cur, halos, rest = _read_windows(grp, refs, 5, 3)
    cw_q, cw_k, cw_v, alog_ref, dtb_ref, nw_ref, s0_ref, o_ref, s_ref = rest
    q_raw, k_raw, v_raw, z, ba = cur
    c = pl.program_id(2)

    @pl.when(c == 0)
    def _():
        s_ref[...] = s0_ref[...]

    m = _masks(grp, nh)
    q_all = _silu(_conv(grp, q_raw, halos[0], cw_q, None, CONV_W))
    k_all = _silu(_conv(grp, k_raw, halos[1], cw_k, None, CONV_W))
    v_all = _silu(_conv(grp, v_raw, halos[2], cw_v, None, CONV_W))
    per_step = nh * GDN_SUB
    eye_g = (_iota2((SUBLANES, nh), 0) == _iota2((SUBLANES, nh), 1)).astype(F32)
    col = lambda x: jnp.concatenate([x[:, h:h + 1] for h in range(nh)], axis=0)

    def stack(sub):
        lanes = slice(sub * nh * GDN_D, (sub + 1) * nh * GDN_D)
        heads = slice(sub * nh, (sub + 1) * nh)
        q = _stack_heads(q_all[:, lanes], nh, GDN_D)
        k = _stack_heads(k_all[:, lanes], nh, GDN_D)
        v = _stack_heads(v_all[:, lanes], nh, GDN_D)
        q = q * lax.rsqrt(jnp.sum(q * q, axis=-1, keepdims=True) + 1e-6) * (GDN_D ** -0.5)
        k = k * lax.rsqrt(jnp.sum(k * k, axis=-1, keepdims=True) + 1e-6)

        beta = jnp.where(m.valid, _sigmoid(ba[:, heads]), 0.0)
        a_raw = ba[:, per_step + sub * nh:per_step + (sub + 1) * nh]
        gg = jnp.where(m.valid, -jnp.exp(alog_ref[0][:, heads]) * _softplus(a_raw + dtb_ref[0][:, heads]), 0.0)
        gcs = _dot_hi(m.cumsum, gg)
        yield
        gc, gc_end = gcs[:r_rows], gcs[r_rows:]
        gc_row = _dot_nt_hi(eye_g, gc)
        yield

        bcol, gcol, gend = col(beta), col(gc), col(gc_end)
        grow = jnp.concatenate([gc_row[h:h + 1, :] for h in range(nh)], axis=1)
        dec = jnp.where(m.incl, jnp.exp(jnp.where(m.incl, gcol - grow, 0.0)), 0.0)
        kb = k * bcol
        kq = _dot_nt(jnp.concatenate([kb, q], axis=0), k)
        yield
        lm = -jnp.where(m.strict, kq[:n] * dec, 0.0)
        attn = jnp.where(m.incl, kq[n:] * dec, 0.0)
        eg = jnp.exp(gcol)
        sol = yield from _solve_unit_lower(lm, jnp.concatenate([v * bcol, kb * eg], axis=1), m.eye, qs)
        vw, kcd = sol[:, :GDN_D], sol[:, GDN_D:]
        qg = q * eg
        kg = k * jnp.exp(gend - gcol)

        def state(j):
            return jnp.concatenate([s_ref[j, sub * nh + h] for h in range(nh)], axis=0)

        parts = []
        for j in range(bb):
            lhs = jnp.concatenate([_seq_rows(grp, kcd, j, nh), _seq_rows(grp, qg, j, nh)], axis=0)
            parts.append(_dot(_expand_rows(lhs, nh, qs, GDN_D), state(j)))
            yield
        half = nh * qs
        ks = _from_seq_rows(grp, [p[:half] for p in parts], nh)
        qsv = _from_seq_rows(grp, [p[half:] for p in parts], nh)
        v_new = vw - ks
        o = qsv + _dot(attn, v_new)
        yield
        for j in range(bb):
            kgj = _expand_rows(_seq_rows(grp, kg, j, nh), nh, qs, GDN_D)
            g_last = jnp.exp(gc_end[j * qs:j * qs + 1, :])
            g_last = jnp.concatenate([jnp.broadcast_to(g_last[:, h:h + 1], (GDN_D, 1)) for h in range(nh)], axis=0)
            s_new = state(j) * g_last + _dot_tn(kgj, _seq_rows(grp, v_new, j, nh))
            yield
            for h in range(nh):
                s_ref[j, sub * nh + h] = s_new[h * GDN_D:(h + 1) * GDN_D]
        o = o * lax.rsqrt(jnp.mean(o * o, axis=-1, keepdims=True) + EPS) * nw_ref[...]
        return _unstack_heads(o, nh)

    o = jnp.concatenate(_interleave([stack(sub) for sub in range(GDN_SUB)]), axis=1)
    o_ref[...] = (o * _silu(z)).astype(o_ref.dtype)


def _ab_layout(x):
    rw, z = x[..., :RW_PROJ], x[..., RW_PROJ:RW_PROJ + SSD_DIM]
    xbc = x[..., RW_PROJ + SSD_DIM:RW_PROJ + SSD_DIM + SSD_CONV_DIM]
    dt = x[..., RW_PROJ + SSD_DIM + SSD_CONV_DIM:]
    zeros = lambda n: jnp.zeros(x.shape[:-1] + (n,), x.dtype)
    return jnp.concatenate([z, xbc, dt, zeros(AB_RW0 - AB_DT0 - SSD_HEADS), rw, zeros(AB_COLS - AB_RW0 - RW_PROJ)],
                           axis=-1)


def _gdn_layout(x):
    per_step = GDN_G * GDN_SUB
    main = GDN_CONV_DIM + GDN_V
    b, a = x[..., main:main + GDN_HEADS], x[..., main + GDN_HEADS:main + 2 * GDN_HEADS]
    parts = [x[..., :main]]
    for g in range(GDN_HEADS // per_step):
        hs = slice(g * per_step, (g + 1) * per_step)
        parts += [b[..., hs], a[..., hs], jnp.zeros(x.shape[:-1] + (LANES - 2 * per_step,), x.dtype)]
    y = jnp.concatenate(parts, axis=-1)
    return jnp.pad(y, [(0, 0)] * (x.ndim - 1) + [(0, -y.shape[-1] % MM_TN)])


def _gdn(grp, proj, hist, s0, prm):
    per_step = GDN_G * GDN_SUB
    width = per_step * GDN_D
    ng = GDN_HEADS // per_step
    ba_blk = (GDN_CONV_DIM + GDN_V) // LANES
    toks = [
        _Tok(proj, width, lambda g: g, hist, lambda g: g),
        _Tok(proj, width, lambda g: ng + g, hist, lambda g: ng + g),
        _Tok(proj, width, lambda g: 2 * ng + g, hist, lambda g: 2 * ng + g),
        _Tok(proj, width, lambda g: 3 * ng + g),
        _Tok(proj, LANES, lambda g: ba_blk + g),
    ]
    ops, specs = _mixer_specs(grp, toks)
    cw = prm["conv_w"]
    grouped = lambda x: x.reshape(ng, 1, per_step)
    params = [
        (cw, _param_spec(width, lambda g: g)(CONV_W)),
        (cw, _param_spec(width, lambda g: ng + g)(CONV_W)),
        (cw, _param_spec(width, lambda g: 2 * ng + g)(CONV_W)),
        (grouped(prm["A_log"]), pl.BlockSpec((1, 1, per_step), lambda b, g, c: (g, 0, 0))),
        (grouped(prm["dt_bias"]), pl.BlockSpec((1, 1, per_step), lambda b, g, c: (g, 0, 0))),
        (prm["norm_w"].reshape(1, GDN_D), pl.BlockSpec((1, GDN_D), lambda b, g, c: (0, 0))),
    ]
    ops += [p for p, _ in params]
    specs += [s for _, s in params]
    st_spec = pl.BlockSpec((grp.bb, per_step, GDN_D, GDN_D), lambda b, g, c: (b, g, 0, 0))
    ops.append(s0)
    specs.append(st_spec)
    rows = grp.B * grp.L
    nc = grp.nc
    return pl.pallas_call(
        functools.partial(_gdn_body, grp=grp),
        grid=(grp.B // grp.bb, ng, nc),
        in_specs=specs,
        out_specs=[pl.BlockSpec((MIX_ROWS, width), lambda b, g, c: (b * nc + c, g)), st_spec],
        out_shape=[jax.ShapeDtypeStruct((rows, GDN_V), BF16), jax.ShapeDtypeStruct(s0.shape, F32)],
        compiler_params=_cp("parallel", "parallel", "arbitrary", flags=MIXER_FLAGS),
        name="gdn",
    )(*ops)


def _attend(q, k_ref, v_ref, nseq, lq):
    scale = XA_HD ** -0.5
    outs = []
    for j in range(nseq):
        qrows = slice(j * lq, (j + 1) * lq)
        mrows = slice(j * N_MEM, (j + 1) * N_MEM)
        heads = []
        for h in range(XA_HEADS):
            sl = slice(h * XA_HD, (h + 1) * XA_HD)
            s = _dot_nt(q[qrows, sl], k_ref[mrows, sl]) * scale
            p = jnp.exp(s - jnp.max(s, axis=-1, keepdims=True))
            heads.append(_dot(p, v_ref[mrows, sl]) / jnp.sum(p, axis=-1, keepdims=True))
        outs.append(jnp.concatenate(heads, axis=1))
    return outs[0] if nseq == 1 else jnp.concatenate(outs, axis=0)


def _attend_packed(q, k_ref, v_ref, nseq, lq):
    scale = XA_HD ** -0.5
    rows, cols = XA_HEADS * lq, N_MEM * XA_HEADS
    own_head = (_iota2((rows, cols), 0) // lq) == (_iota2((rows, cols), 1) % XA_HEADS)
    outs = []
    for j in range(nseq):
        qh = _stack_heads(q[j * lq:(j + 1) * lq], XA_HEADS, XA_HD)
        mrows = slice(j * cols, (j + 1) * cols)
        s = jnp.where(own_head, _dot_nt(qh, k_ref[mrows, :]) * scale, -1e30)
        p = jnp.where(own_head, jnp.exp(s - jnp.max(s, axis=-1, keepdims=True)), 0.0)
        o = _dot(p, v_ref[mrows, :]) / jnp.sum(p, axis=-1, keepdims=True)
        outs.append(_unstack_heads(o, XA_HEADS))
    return jnp.concatenate(outs, axis=0)


def _xattn_body(h_ref, g_ref, wq_ref, k_ref, v_ref, wo_ref, o_ref, *, attend, nseq, lq):
    x = h_ref[...]
    xn = x * lax.rsqrt(jnp.mean(x * x, axis=-1, keepdims=True) + EPS) * g_ref[...]
    q = jnp.dot(xn.astype(BF16), wq_ref[...], preferred_element_type=F32)
    o = attend(q, k_ref, v_ref, nseq, lq)
    o_ref[...] = x + jnp.dot(o.astype(BF16), wo_ref[...], preferred_element_type=F32)


def _xattn(grp, h, gain, w_q, w_o, layer, mem_k, mem_v, mem_row0):
    rows, d = h.shape
    if grp.L >= XA_TQ:
        nseq, lq, attend = 1, XA_TQ, _attend
        per = grp.L // XA_TQ
        mem_spec = pl.BlockSpec((N_MEM, XA_DIM), lambda i: (mem_row0 // N_MEM + i // per, 0))
    else:
        nseq, lq, attend = MIX_ROWS // grp.L, grp.L, _attend_packed
        blk = nseq * N_MEM * XA_HEADS
        mem_spec = pl.BlockSpec((blk, XA_HD), lambda i: (mem_row0 * XA_HEADS // blk + i, 0))
    h_spec = pl.BlockSpec((nseq * lq, d), lambda i: (i, 0))
    return pl.pallas_call(
        functools.partial(_xattn_body, attend=attend, nseq=nseq, lq=lq),
        grid=(rows // (nseq * lq),),
        in_specs=[h_spec, pl.BlockSpec((1, d), lambda i: (0, 0)),
                  pl.BlockSpec((None, d, XA_DIM), lambda i: (layer, 0, 0)), mem_spec, mem_spec,
                  pl.BlockSpec((None, XA_DIM, d), lambda i: (layer, 0, 0))],
        out_specs=h_spec,
        out_shape=jax.ShapeDtypeStruct((rows, d), F32),
        compiler_params=_cp("parallel"),
        name="xattn",
    )(h, gain.reshape(1, d), w_q, mem_k, mem_v, w_o)


def _ffn_in_body(*refs, grp, tiles_per_seq):
    it = iter(refs)
    a_ref, g_ref, wg_ref, wu_ref, hist_ref, cw_ref, cb_ref, act_ref, tail_ref, an_ref = (next(it) for _ in range(10))
    carry_ref = None if grp.embedded else next(it)
    i, j = pl.program_id(0), pl.program_id(1)

    @pl.when(j == 0)
    def _():
        x = a_ref[...]
        y = x * lax.rsqrt(jnp.mean(x * x, axis=-1, keepdims=True) + EPS)
        an_ref[...] = (y * g_ref[...]).astype(BF16)

    a = an_ref[...]
    gate = jnp.dot(a, wg_ref[...], preferred_element_type=F32)
    up = jnp.dot(a, wu_ref[...], preferred_element_type=F32)
    if grp.embedded:
        halo = hist_ref[...]
        tail_ref[...] = gate
    else:
        halo = jnp.where(i % tiles_per_seq == 0, hist_ref[...], carry_ref[j])
        last = gate[gate.shape[0] - SUBLANES:]
        carry_ref[j] = last
        tail_ref[...] = last
    conv = _conv(grp, gate, halo, cw_ref, cb_ref[...], FFN_CONV_W)
    act_ref[...] = (_silu(conv) * up).astype(act_ref.dtype)


def _ffn_in(grp, h, gain, w_in, layer, hist, conv_w, conv_b, tm=MM_TM, tn=MM_TN):
    assert grp.embedded or grp.L % tm == 0
    rows, k = h.shape
    nct = D_FF // tn
    tiles_per_seq = max(grp.L // tm, 1)
    hrows = tm if grp.embedded else SUBLANES
    tile_idx = lambda i, j: (i, j)
    hist_idx = tile_idx if grp.embedded else (lambda i, j: (i // tiles_per_seq, j))
    specs = [
        pl.BlockSpec((tm, k), lambda i, j: (i, 0)),
        pl.BlockSpec((1, k), lambda i, j: (0, 0)),
        _weight_spec(w_in, layer, k, tn, lambda j: j),
        _weight_spec(w_in, layer, k, tn, lambda j: nct + j),
        pl.BlockSpec((hrows, tn), hist_idx),
        pl.BlockSpec((FFN_CONV_W, tn), lambda i, j: (0, j)),
        pl.BlockSpec((1, tn), lambda i, j: (0, j)),
    ]
    scratch = [pltpu.VMEM((tm, k), BF16)]
    if not grp.embedded:
        scratch.append(pltpu.VMEM((nct, SUBLANES, tn), F32))
    act, tails = pl.pallas_call(
        functools.partial(_ffn_in_body, grp=grp, tiles_per_seq=tiles_per_seq),
        grid=(rows // tm, nct),
        in_specs=specs,
        out_specs=[pl.BlockSpec((tm, tn), tile_idx), pl.BlockSpec((hrows, tn), tile_idx)],
        out_shape=[jax.ShapeDtypeStruct((rows, D_FF), BF16), jax.ShapeDtypeStruct((rows // tm * hrows, D_FF), F32)],
        scratch_shapes=scratch,
        compiler_params=_cp("arbitrary", "arbitrary"),
        name="ffn_in",
    )(h, gain.reshape(1, k), w_in, w_in, hist, conv_w, conv_b.reshape(1, D_FF))
    if not grp.embedded:
        tails = tails.reshape(grp.B, tiles_per_seq, SUBLANES, D_FF)[:, -1].reshape(grp.B * SUBLANES, D_FF)
    return act, tails


def _history(grp, buf, width):
    b, k, c = buf.shape
    if grp.embedded:
        h = jnp.pad(buf, ((0, 0), (grp.first - k, grp.L - grp.first), (0, 0)))
        return h.reshape(b * grp.L, c)
    return jnp.pad(buf, ((0, 0), (SUBLANES - k, 0), (0, 0))).reshape(b * SUBLANES, c)


def _tail(grp, x2d, k, cols):
    x = x2d.reshape(grp.B, grp.L, x2d.shape[1])
    return x[:, grp.L - k:, cols].astype(F32)


def _run_group(grp, x2d, mem_k, mem_v, mem_row0, st, wts, prm):
    h = x2d
    out = {}
    proj = _matmul(h, wts["in_ab"], gain=prm["norm_mix"][0], out_dtype=BF16)
    y_rw, s_rw = _rwkv(grp, proj, _history(grp, st["rw_shift"][:, None, :], RW_PROJ), st["rwkv"], prm["rw"])
    y_ssd, s_ssd = _ssd(grp, proj, _history(grp, st["ssd_conv"], SSD_CONV_DIM),
                        st["ssd"].reshape(grp.B, SSD_GROUPS, SSD_GDIM, SSD_STATE), prm["ssd"])
    out["rwkv"] = s_rw
    out["rw_shift"] = _tail(grp, proj, 1, slice(AB_RW0, AB_RW0 + RW_PROJ))[:, 0]
    out["ssd"] = s_ssd.reshape(st["ssd"].shape)
    out["ssd_conv"] = _tail(grp, proj, CONV_W - 1, slice(AB_X0, AB_X0 + SSD_CONV_DIM))
    h = _matmul([y_rw, y_ssd], wts["out_ab"], res=h)
    ffn_bufs = []
    for l in range(2):
        if l == 1:
            proj = _matmul(h, wts["in_c"], gain=prm["norm_mix"][1], out_dtype=BF16)
            y_c, s_gdn = _gdn(grp, proj, _history(grp, st["gdn_conv"], GDN_CONV_DIM), st["gdn"], prm["gdn"])
            out["gdn"] = s_gdn
            out["gdn_conv"] = _tail(grp, proj, CONV_W - 1, slice(0, GDN_CONV_DIM))
            h = _matmul(y_c, wts["out_c"], res=h)
        h = _xattn(grp, h, prm["norm_xa"][l], wts["xq"], wts["xo"], l, mem_k, mem_v, mem_row0[l])
        act, gate_tail = _ffn_in(grp, h, prm["norm_ffn"][l], wts["ffn_in"], l, _history(grp, st["ffn_conv"][l], D_FF),
                                 prm["ffn_conv_w"][l], prm["ffn_conv_b"][l])
        ffn_bufs.append(gate_tail.reshape(grp.B, SUBLANES, D_FF)[:, SUBLANES - (FFN_CONV_W - 1):])
        h = _matmul(act, wts["ffn_out"], layer=l, res=h)
    out["ffn_conv"] = jnp.stack(ffn_bufs)
    out["y"] = _rmsnorm(h, prm["norm_final"])
    return out


def kernel(x_prompt, x_sample, mem_prompt, state_rwkv, state_rwkv_shift, state_ssd, state_ssd_conv, state_gdn,
           state_gdn_conv, state_ffn_conv, cache_mem_k, cache_mem_v, norm_mix, norm_xa, norm_mem, norm_ffn,
           norm_final, w_in_ab, rw_mu, rw_w0, rw_w_up, rw_a0, rw_a_up, rw_g_up, rw_k_k, rw_k_a, rw_r_k, rw_gn_w,
           rw_gn_b, ssd_conv_w, ssd_conv_b, ssd_dt_bias, ssd_A_log, ssd_D, ssd_norm_w, w_out_ab, w_in_c,
           gdn_conv_w, gdn_A_log, gdn_dt_bias, gdn_norm_w, w_out_c, w_xq, w_xk, w_xv, w_xo, ffn_w_in, ffn_conv_w,
           ffn_conv_b, ffn_w_out):
    bp, lp, _ = x_prompt.shape
    bs, ls, _ = x_sample.shape
    depth = norm_mix.shape[0]
    assert depth == 2 and w_in_ab.shape[0] == 1 and w_in_c.shape[0] == 1
    assert lp % MIX_ROWS == 0 and lp % XA_TQ == 0 and lp % MM_TM == 0
    pad_rows = SUBLANES - ls
    assert CONV_W - 1 <= pad_rows and CONV_W - 1 <= ls and MIX_ROWS % SUBLANES == 0
    gp = Group(B=bp, L=lp, first=0, bb=1)
    gs = Group(B=bs, L=SUBLANES, first=pad_rows, bb=MIX_ROWS // SUBLANES)
    assert bs % gs.bb == 0 and (bs * SUBLANES) % MM_TM == 0 and (bp * lp) % MM_TM == 0

    wts = dict(
        in_ab=_ab_layout(w_in_ab[0]).astype(BF16), out_ab=w_out_ab[0].astype(BF16),
        in_c=_gdn_layout(w_in_c[0]).astype(BF16), out_c=w_out_c[0].astype(BF16),
        xq=w_xq.astype(BF16), xo=w_xo.astype(BF16), ffn_in=ffn_w_in.astype(BF16), ffn_out=ffn_w_out.astype(BF16),
    )
    prm = dict(
        norm_mix=norm_mix, norm_xa=norm_xa, norm_ffn=norm_ffn, norm_final=norm_final,
        ffn_conv_w=ffn_conv_w, ffn_conv_b=ffn_conv_b,
        rw=dict(mu=rw_mu[0], w0=rw_w0[0], w_up=rw_w_up[0], a0=rw_a0[0], a_up=rw_a_up[0], g_up=rw_g_up[0],
                k_k=rw_k_k[0], k_a=rw_k_a[0], r_k=rw_r_k[0], gn_w=rw_gn_w[0], gn_b=rw_gn_b[0]),
        ssd=dict(conv_w=ssd_conv_w[0], conv_b=ssd_conv_b[0], dt_bias=ssd_dt_bias[0], A_log=ssd_A_log[0],
                 D=ssd_D[0], norm_w=ssd_norm_w[0]),
        gdn=dict(conv_w=gdn_conv_w[0], A_log=gdn_A_log[0], dt_bias=gdn_dt_bias[0], norm_w=gdn_norm_w[0]),
    )

    mem2d = mem_prompt.reshape(bp * N_MEM, D_MODEL)
    mem_tm = min(MM_TM, bp * N_MEM)
    mk, mv = [], []
    for l in range(depth):
        mk.append(_matmul(mem2d, w_xk[l].astype(BF16), gain=norm_mem[l], tm=mem_tm))
        mv.append(_matmul(mem2d, w_xv[l].astype(BF16), gain=norm_mem[l], tm=mem_tm))
    mem_k_p = jnp.stack(mk)
    mem_v_p = jnp.stack(mv)

    zeros = lambda *s: jnp.zeros(s, F32)
    st_p = dict(
        rwkv=zeros(bp, RW_HEADS, RW_HD, RW_HD), rw_shift=zeros(bp, RW_PROJ),
        ssd=zeros(bp, SSD_GROUPS, SSD_HPG, SSD_HD, SSD_STATE), ssd_conv=zeros(bp, CONV_W - 1, SSD_CONV_DIM),
        gdn=zeros(bp, GDN_HEADS, GDN_D, GDN_D), gdn_conv=zeros(bp, CONV_W - 1, GDN_CONV_DIM),
        ffn_conv=zeros(depth, bp, FFN_CONV_W - 1, D_FF),
    )
    rp = _run_group(gp, x_prompt.reshape(bp * lp, D_MODEL), mem_k_p.reshape(depth * bp * N_MEM, XA_DIM),
                    mem_v_p.reshape(depth * bp * N_MEM, XA_DIM), [l * bp * N_MEM for l in range(depth)],
                    st_p, wts, prm)

    st_s = dict(rwkv=state_rwkv[0], rw_shift=state_rwkv_shift[0], ssd=state_ssd[0], ssd_conv=state_ssd_conv[0],
                gdn=state_gdn[0], gdn_conv=state_gdn_conv[0], ffn_conv=state_ffn_conv)
    xs = jnp.pad(x_sample, ((0, 0), (pad_rows, 0), (0, 0))).reshape(bs * SUBLANES, D_MODEL)
    rs = _run_group(gs, xs, cache_mem_k.reshape(depth * bs * N_MEM * XA_HEADS, XA_HD),
                    cache_mem_v.reshape(depth * bs * N_MEM * XA_HEADS, XA_HD), [l * bs * N_MEM for l in range(depth)],
                    st_s, wts, prm)

    y_p = rp["y"].reshape(bp, lp, D_MODEL)
    y_s = rs["y"].reshape(bs, SUBLANES, D_MODEL)[:, pad_rows:]
    lead = lambda x: x[None]
    mem_shape = (depth, bp, N_MEM, XA_HEADS, XA_HD)
    return (y_p, y_s, lead(rp["rwkv"]), lead(rs["rwkv"]), lead(rp["rw_shift"]), lead(rs["rw_shift"]),
            lead(rp["ssd"]), lead(rs["ssd"]), lead(rp["ssd_conv"]), lead(rs["ssd_conv"]),
            lead(rp["gdn"]), lead(rs["gdn"]), lead(rp["gdn_conv"]), lead(rs["gdn_conv"]),
            rp["ffn_conv"], rs["ffn_conv"], mem_k_p.reshape(mem_shape), mem_v_p.reshape(mem_shape))
```

```python
import functools
from typing import NamedTuple

import jax
import jax.numpy as jnp
from jax import lax
from jax.experimental import pallas as pl
from jax.experimental.pallas import tpu as pltpu

F32 = jnp.float32
BF16 = jnp.bfloat16
HI = lax.Precision.HIGHEST

D_MODEL = 2048
EPS = 1e-6
RW_HEADS, RW_HD = 16, 64
RW_DIM = RW_HEADS * RW_HD
RW_LORA = 256
RW_PROJ = 3 * RW_DIM + RW_LORA
RW_GN_EPS = 6.4e-4
SSD_HEADS, SSD_HD, SSD_GROUPS, SSD_STATE = 16, 64, 2, 128
SSD_HPG = SSD_HEADS // SSD_GROUPS
SSD_DIM = SSD_HEADS * SSD_HD
SSD_GDIM = SSD_HPG * SSD_HD
SSD_CONV_DIM = SSD_DIM + 2 * SSD_GROUPS * SSD_STATE
SSD_PROJ = SSD_DIM + SSD_CONV_DIM + SSD_HEADS
AB_PROJ = RW_PROJ + SSD_PROJ
GDN_HEADS, GDN_D = 16, 128
GDN_V = GDN_HEADS * GDN_D
GDN_CONV_DIM = 3 * GDN_V
GDN_PROJ = GDN_CONV_DIM + GDN_V + 2 * GDN_HEADS
D_FF = 5632
XA_HEADS, XA_HD, N_MEM = 4, 128, 256
XA_DIM = XA_HEADS * XA_HD
CONV_W = 4
FFN_CONV_W = 3

AB_Z0 = 0
AB_X0 = SSD_DIM
AB_B0 = 2 * SSD_DIM
AB_C0 = AB_B0 + SSD_GROUPS * SSD_STATE
AB_DT0 = AB_C0 + SSD_GROUPS * SSD_STATE
AB_RW0 = 3 * RW_DIM
AB_COLS = 6656
assert AB_DT0 + 128 <= AB_RW0 and AB_RW0 + RW_PROJ <= AB_COLS and AB_RW0 % RW_DIM == 0

SUBLANES = 8
LANES = 128
VMEM_LIMIT = 56 * 1024 * 1024

MIX_ROWS = 64
INV_BLOCK = 16
INV_SPLIT = 128
MM_TM, MM_TN = 1024, 512
MM_TM_PROJ = 2048
RW_HP = 2
RW_SUB = 4
GDN_G = 4
GDN_SUB = 4
XA_TQ = 512
NORM_TM = 512


class Group(NamedTuple):
    B: int
    L: int
    first: int
    bb: int

    @property
    def qs(self):
        return MIX_ROWS // self.bb

    @property
    def nc(self):
        return self.L // self.qs

    @property
    def embedded(self):
        return self.first > 0


MIXER_FLAGS = None


def _cp(*sem, flags=None):
    return pltpu.CompilerParams(dimension_semantics=sem, vmem_limit_bytes=VMEM_LIMIT, flags=flags)


def _dot(a, b):
    return jnp.dot(a.astype(BF16), b.astype(BF16), preferred_element_type=F32)


def _dot_nt(a, b):
    return lax.dot_general(a.astype(BF16), b.astype(BF16), (((1,), (1,)), ((), ())), preferred_element_type=F32)


def _dot_tn(a, b):
    return lax.dot_general(a.astype(BF16), b.astype(BF16), (((0,), (0,)), ((), ())), preferred_element_type=F32)


def _dot_hi(a, b):
    return jnp.dot(a, b, preferred_element_type=F32, precision=HI)


def _split_bf16(x):
    hi = x.astype(BF16)
    return hi, (x - hi.astype(F32)).astype(BF16)


def _dot_x3(a, b):
    ah, al = _split_bf16(a)
    bh, bl = _split_bf16(b)
    return jnp.dot(jnp.concatenate([ah, ah, al], axis=1), jnp.concatenate([bh, bl, bh], axis=0),
                   preferred_element_type=F32)


def _dot_nt_hi(a, b):
    return lax.dot_general(a, b, (((1,), (1,)), ((), ())), preferred_element_type=F32, precision=HI)


def _sigmoid(x):
    return 1.0 / (1.0 + jnp.exp(-x))


def _silu(x):
    return x * _sigmoid(x)


def _softplus(x):
    return jnp.maximum(x, 0.0) + jnp.log(1.0 + jnp.exp(-jnp.abs(x)))


def _iota2(shape, axis):
    return lax.broadcasted_iota(jnp.int32, shape, axis)


class _Masks(NamedTuple):
    incl: jax.Array
    strict: jax.Array
    eye: jax.Array
    cumsum: jax.Array
    valid: jax.Array


def _masks(grp, heads=1):
    r = MIX_ROWS
    n = heads * r
    ri, ci = _iota2((n, n), 0), _iota2((n, n), 1)
    same = (ri // grp.qs) == (ci // grp.qs)
    incl = same & (ci <= ri)
    strict = same & (ci < ri)
    si, sj = _iota2((2 * r, r), 0), _iota2((2 * r, r), 1)
    same_seq = ((si % r) // grp.qs) == (sj // grp.qs)
    cumsum = (same_seq & ((sj <= si) | (si >= r))).astype(F32)
    t = _iota2((r, 1), 0) % grp.qs
    return _Masks(incl, strict, (ri == ci).astype(F32), cumsum, t >= grp.first)


def _neumann(lm, rhs, nil, eye=None):
    n = lm.shape[0]
    p = None
    if rhs is None:
        t = eye + lm
        if nil > 2:
            p = _dot(lm, lm)
            yield
    elif nil > 2:
        both = _dot(lm, jnp.concatenate([lm, rhs], axis=1))
        yield
        p, t = both[:, :n], rhs + both[:, n:]
    else:
        t = rhs + _dot(lm, rhs)
        yield
    k = 2
    while k < nil:
        if 2 * k < nil:
            both = _dot(p, jnp.concatenate([p, t], axis=1))
            p, t = both[:, :n], t + both[:, n:]
        else:
            t = t + _dot(p, t)
        yield
        k *= 2
    return t


def _inv_unit_lower(lm, eye, qs):
    if qs <= INV_BLOCK:
        return (yield from _neumann(lm, None, qs, eye))
    n = lm.shape[0]
    diag = (_iota2((n, n), 0) // INV_BLOCK) == (_iota2((n, n), 1) // INV_BLOCK)
    d = jnp.where(diag, lm, 0.0)
    dinv = yield from _neumann(d, None, INV_BLOCK, eye)
    nm = _dot(dinv, lm - d)
    yield
    return (yield from _neumann(nm, dinv, qs // INV_BLOCK))


def _lockstep(gens):
    results = [None] * len(gens)
    active = list(range(len(gens)))
    while active:
        for i in list(active):
            try:
                next(gens[i])
            except StopIteration as stop:
                results[i] = stop.value
                active.remove(i)
        if active:
            yield
    return results


def _solve_unit_lower(lm, rhs, eye, qs):
    n = lm.shape[0]
    if n > INV_SPLIT:
        blocks = [slice(i, i + INV_SPLIT) for i in range(0, n, INV_SPLIT)]
        inv = yield from _lockstep([_inv_unit_lower(lm[b, b], eye[b, b], qs) for b in blocks])
        zero = jnp.zeros((INV_SPLIT, INV_SPLIT), F32)
        t = jnp.concatenate([jnp.concatenate([inv[i] if i == j else zero for j in range(len(blocks))], axis=1)
                             for i in range(len(blocks))], axis=0)
    else:
        t = yield from _inv_unit_lower(lm, eye, qs)
    x = _dot(t, rhs)
    yield
    return x


def _interleave(gens):
    results = [None] * len(gens)
    active = list(range(len(gens)))
    while active:
        for i in list(active):
            try:
                next(gens[i])
            except StopIteration as stop:
                results[i] = stop.value
                active.remove(i)
    return results


def _stack_heads(x, heads, width):
    return jnp.concatenate([x[:, h * width:(h + 1) * width] for h in range(heads)], axis=0)


def _unstack_heads(x, heads):
    r = x.shape[0] // heads
    return jnp.concatenate([x[h * r:(h + 1) * r] for h in range(heads)], axis=1)


def _seq_rows(grp, x, j, heads):
    if grp.bb == 1:
        return x
    qs = grp.qs
    return jnp.concatenate([x[h * MIX_ROWS + j * qs:h * MIX_ROWS + (j + 1) * qs] for h in range(heads)], axis=0)


def _from_seq_rows(grp, parts, heads):
    if grp.bb == 1:
        return parts[0]
    qs = grp.qs
    return jnp.concatenate([p[h * qs:(h + 1) * qs] for h in range(heads) for p in parts], axis=0)


def _expand_rows(x, heads, rows_per_head, width):
    m = x.shape[0]
    if x.shape[1] == width:
        x = jnp.concatenate([x] * heads, axis=1)
    keep = ((_iota2((m, heads * width), 0) // rows_per_head) % heads) == (_iota2((m, heads * width), 1) // width)
    return jnp.where(keep, x, 0.0)


def _taps(grp, cur, halo, width):
    if grp.embedded:
        t = _iota2((cur.shape[0], 1), 0) % grp.qs
        full = jnp.where(t < grp.first, halo, cur)
        return [full] + [pltpu.roll(full, s, axis=0) for s in range(1, width)]
    full = jnp.concatenate([halo, cur], axis=0)
    return [cur] + [pltpu.roll(full, s, axis=0)[SUBLANES:] for s in range(1, width)]


def _conv(grp, cur, halo, w_ref, bias, width):
    taps = _taps(grp, cur, halo, width)
    y = taps[0] * w_ref[width - 1:width, :]
    for s in range(1, width):
        y = y + taps[s] * w_ref[width - 1 - s:width - s, :]
    return y if bias is None else y + bias


class _Tok(NamedTuple):
    arr: jax.Array
    width: int
    col: object
    hist: jax.Array = None
    hcol: object = None


def _mixer_specs(grp, toks):
    nc = grp.nc
    ops, specs = [], []
    for t in toks:
        ops.append(t.arr)
        specs.append(pl.BlockSpec((MIX_ROWS, t.width), lambda b, g, c, t=t: (b * nc + c, t.col(g))))
    for t in toks:
        if t.hist is None:
            continue
        if not grp.embedded:
            tile = _sublane_tile(t.arr.dtype)
            sub = MIX_ROWS // tile
            ops.append(t.arr)
            specs.append(pl.BlockSpec(
                (tile, t.width), lambda b, g, c, t=t, sub=sub: (jnp.maximum((b * nc + c) * sub - 1, 0), t.col(g))))
        ops.append(t.hist)
        hrows = MIX_ROWS if grp.embedded else SUBLANES
        specs.append(pl.BlockSpec((hrows, t.width), lambda b, g, c, t=t: (b, t.hcol(g))))
    return ops, specs


def _read_windows(grp, refs, n_plain, n_hist):
    c = pl.program_id(2)
    cur = [r[...].astype(F32) for r in refs[:n_plain]]
    rest = refs[n_plain:]
    halos = []
    for i in range(n_hist):
        if grp.embedded:
            halos.append(rest[i][...])
        else:
            prev = rest[2 * i][...]
            prev = prev[prev.shape[0] - SUBLANES:].astype(F32)
            halos.append(jnp.where(c == 0, rest[2 * i + 1][...], prev))
    used = n_hist if grp.embedded else 2 * n_hist
    return cur, halos, rest[used:]


def _sublane_tile(dtype):
    return SUBLANES * 4 // jnp.dtype(dtype).itemsize


def _param_spec(width, col):
    return lambda rows: pl.BlockSpec((rows, width), lambda b, g, c: (0, col(g)))


def _matmul_body(*refs, norm, has_res, n_a, w_rows_are_outputs):
    it = iter(refs)
    a_refs = [next(it) for _ in range(n_a)]
    g_ref = next(it) if norm else None
    w_ref = next(it)
    res_ref = next(it) if has_res else None
    o_ref = next(it)
    if norm:
        an_ref = next(it)

        @pl.when(pl.program_id(1) == 0)
        def _():
            x = a_refs[0][...]
            y = x * lax.rsqrt(jnp.mean(x * x, axis=-1, keepdims=True) + EPS)
            an_ref[...] = (y * g_ref[...]).astype(BF16)

        if w_rows_are_outputs:
            acc = lax.dot_general(an_ref[...], w_ref[...], (((1,), (1,)), ((), ())), preferred_element_type=F32)
        else:
            acc = jnp.dot(an_ref[...], w_ref[...], preferred_element_type=F32)
    else:
        acc, k0 = None, 0
        for a_ref in a_refs:
            k1 = k0 + a_ref.shape[1]
            part = jnp.dot(a_ref[...], w_ref[k0:k1, :], preferred_element_type=F32)
            acc = part if acc is None else acc + part
            k0 = k1
    if has_res:
        acc = acc + res_ref[...]
    o_ref[...] = acc.astype(o_ref.dtype)


def _matmul(a, w, *, layer=None, gain=None, res=None, out_dtype=F32, tm=MM_TM, tn=MM_TN, w_rows_are_outputs=False):
    a_list = list(a) if isinstance(a, (list, tuple)) else [a]
    m = a_list[0].shape[0]
    k, n = w.shape[-2:][::-1] if w_rows_are_outputs else w.shape[-2:]
    assert not w_rows_are_outputs or (gain is not None and w.ndim == 2)
    assert sum(x.shape[1] for x in a_list) == k
    norm = gain is not None
    assert not norm or len(a_list) == 1
    ops = list(a_list)
    specs = [pl.BlockSpec((tm, x.shape[1]), lambda i, j: (i, 0)) for x in a_list]
    if norm:
        ops.append(gain.reshape(1, k))
        specs.append(pl.BlockSpec((1, k), lambda i, j: (0, 0)))
    ops.append(w)
    if w_rows_are_outputs:
        specs.append(pl.BlockSpec((tn, k), lambda i, j: (j, 0)))
    else:
        specs.append(_weight_spec(w, layer, k, tn, lambda j: j))
    if res is not None:
        ops.append(res)
        specs.append(pl.BlockSpec((tm, tn), lambda i, j: (i, j)))
    return pl.pallas_call(
        functools.partial(_matmul_body, norm=norm, has_res=res is not None, n_a=len(a_list),
                          w_rows_are_outputs=w_rows_are_outputs),
        grid=(m // tm, n // tn),
        in_specs=specs,
        out_specs=pl.BlockSpec((tm, tn), lambda i, j: (i, j)),
        out_shape=jax.ShapeDtypeStruct((m, n), out_dtype),
        scratch_shapes=[pltpu.VMEM((tm, k), BF16)] if norm else [],
        compiler_params=_cp("parallel", "arbitrary"),
        name="matmul",
    )(*ops)


def _weight_spec(w, layer, k, tn, col):
    if w.ndim == 2:
        return pl.BlockSpec((k, tn), lambda i, j: (0, col(j)))
    return pl.BlockSpec((None, k, tn), lambda i, j: (layer, 0, col(j)))


def _rmsnorm_body(x_ref, g_ref, o_ref):
    x = x_ref[...]
    y = x * lax.rsqrt(jnp.mean(x * x, axis=-1, keepdims=True) + EPS)
    o_ref[...] = (y * g_ref[...]).astype(o_ref.dtype)


def _rmsnorm(x, gain, out_dtype=F32, tm=NORM_TM):
    m, k = x.shape
    return pl.pallas_call(
        _rmsnorm_body,
        grid=(m // tm,),
        in_specs=[pl.BlockSpec((tm, k), lambda i: (i, 0)), pl.BlockSpec((1, k), lambda i: (0, 0))],
        out_specs=pl.BlockSpec((tm, k), lambda i: (i, 0)),
        out_shape=jax.ShapeDtypeStruct((m, k), out_dtype),
        compiler_params=_cp("parallel"),
        name="rmsnorm",
    )(x, gain.reshape(1, k))


def _rwkv_body(*refs, grp):
    r_rows = MIX_ROWS
    qs, bb, nc = grp.qs, grp.bb, grp.nc
    nh = 2 * RW_HP
    sw = nh * RW_HD
    width = RW_SUB * sw
    n = nh * r_rows
    cur, halos, rest = _read_windows(grp, refs, 4, 4)
    (mu_r, mu_k, mu_v, mu_lo, w0_ref, a0_ref, lora_ref, kk_ref, ka_ref, rk_ref, gnw_ref, gnb_ref,
     s0_ref, y_ref, s_ref, sbig_ref) = rest
    c = pl.program_id(2)

    @pl.when(c == 0)
    def _():
        for j in range(bb):
            for sub in range(RW_SUB):
                st = jnp.concatenate([s0_ref[j, sub * nh + h] for h in range(nh)], axis=0)
                sbig_ref[j, sub] = _expand_rows(st, nh, RW_HD, RW_HD)

    m = _masks(grp, nh)

    def shifted(i, mu_ref):
        x = cur[i]
        prev = _taps(grp, x, halos[i], 2)[1]
        return x + (prev - x) * mu_ref[...]

    r = shifted(0, mu_r)
    k = shifted(1, mu_k)
    v = shifted(2, mu_v)
    lo = shifted(3, mu_lo)
    lora_in = jnp.concatenate([jnp.tanh(lo[:, :64]), lo[:, 64:128], _sigmoid(lo[:, 128:256])], axis=1)
    lora = _dot_x3(lora_in, lora_ref[0])
    w = -_softplus(-(w0_ref[...] + lora[:, :width])) - 0.5
    lw = jnp.where(m.valid, -jnp.exp(w), 0.0)
    a = _sigmoid(a0_ref[...] + lora[:, width:2 * width])
    g = lora[:, 2 * width:]

    bd = ((_iota2((sw, sw), 0) // RW_HD) == (_iota2((sw, sw), 1) // RW_HD)).astype(F32)

    def head_sums(x):
        return _unstack_heads(_dot_x3(_stack_heads(x, RW_SUB, sw), bd), RW_SUB)

    kx = k * kk_ref[...]
    kp = jnp.where(m.valid, k * (1.0 + (a - 1.0) * ka_ref[...]), 0.0)
    sums = head_sums(jnp.concatenate([kx * kx, r * kp * rk_ref[...]], axis=0))
    kkn = kx * lax.rsqrt(sums[:r_rows] + 1e-6)
    bonus = sums[r_rows:] * v
    at = jnp.where(m.valid, -kkn, 0.0)
    bt = jnp.where(m.valid, kkn * a, 0.0)

    cums = _dot_hi(m.cumsum, lw)
    cum, cum_end = cums[:r_rows], cums[r_rows:]
    e_neg = jnp.exp(-cum)
    e_end = jnp.exp(cum_end - cum)
    p_end = jnp.exp(cum_end)
    a_til, r_til = at * jnp.exp(cum - lw), r * jnp.exp(cum)
    b_til, k_til = bt * e_neg, kp * e_neg
    b_hat, k_hat = bt * e_end, kp * e_end

    def stack(sub):
        lanes = slice(sub * sw, (sub + 1) * sw)

        def tile(x):
            return _expand_rows(jnp.concatenate([x[:, lanes]] * nh, axis=0), nh, r_rows, RW_HD)

        ar = jnp.concatenate([tile(a_til), tile(r_til)], axis=0)
        bk = jnp.concatenate([tile(b_til), tile(k_til)], axis=0)
        bk_end = jnp.concatenate([tile(b_hat), tile(k_hat)], axis=0)
        v_exp = tile(v)
        gm = _dot_nt(ar, bk)
        yield
        m_ab = jnp.where(m.strict, gm[:n, :n], 0.0)
        m_ak = jnp.where(m.strict, gm[:n, n:], 0.0)
        m_r = jnp.concatenate([jnp.where(m.incl, gm[n:, :n], 0.0), jnp.where(m.incl, gm[n:, n:], 0.0)], axis=1)

        def seq2(x, j):
            if bb == 1:
                return x
            return jnp.concatenate([_seq_rows(grp, x[:n], j, nh), _seq_rows(grp, x[n:], j, nh)], axis=0)

        parts = []
        for j in range(bb):
            parts.append(_dot_nt(seq2(ar, j), sbig_ref[j, sub]))
            yield
        half = nh * qs
        as_a = _from_seq_rows(grp, [p[:half] for p in parts], nh)
        as_r = _from_seq_rows(grp, [p[half:] for p in parts], nh)
        rhs = as_a + _dot(m_ak, v_exp)
        yield
        u = yield from _solve_unit_lower(m_ab, rhs, m.eye, qs)
        uv = jnp.concatenate([u, v_exp], axis=0)
        y_exp = as_r + _dot(m_r, uv)
        yield
        y = y_exp[:r_rows]
        for h in range(1, nh):
            y = y + y_exp[h * r_rows:(h + 1) * r_rows]
        for j in range(bb):
            sbig_ref[j, sub] = (sbig_ref[j, sub] * p_end[j * qs:j * qs + 1, lanes]
                                + _dot_tn(seq2(uv, j), seq2(bk_end, j)))
            yield
        return y

    y = jnp.concatenate(_interleave([stack(sub) for sub in range(RW_SUB)]), axis=1)

    @pl.when(c == nc - 1)
    def _():
        for j in range(bb):
            for sub in range(RW_SUB):
                sb = sbig_ref[j, sub]
                for h in range(nh):
                    s_ref[j, sub * nh + h] = sb[h * RW_HD:(h + 1) * RW_HD, h * RW_HD:(h + 1) * RW_HD]

    mean = head_sums(y) * (1.0 / RW_HD)
    d = y - mean
    var = head_sums(d * d) * (1.0 / RW_HD)
    out = d * lax.rsqrt(var + RW_GN_EPS) * gnw_ref[...] + gnb_ref[...] + bonus
    y_ref[...] = (out * g).astype(y_ref.dtype)


def _rwkv(grp, proj, hist, s0, prm):
    width = RW_SUB * RW_HP * LANES
    ng = RW_DIM // width
    lora_blk = 3 * RW_DIM // RW_LORA
    p0 = AB_RW0 // width
    toks = [
        _Tok(proj, width, lambda g: p0 + g, hist, lambda g: g),
        _Tok(proj, width, lambda g: p0 + ng + g, hist, lambda g: ng + g),
        _Tok(proj, width, lambda g: p0 + 2 * ng + g, hist, lambda g: 2 * ng + g),
        _Tok(proj, RW_LORA, lambda g: AB_RW0 // RW_LORA + lora_blk, hist, lambda g: lora_blk),
    ]
    ops, specs = _mixer_specs(grp, toks)
    head = _param_spec(width, lambda g: g)
    lora_w = jnp.zeros((ng, RW_LORA, 3 * width), F32)
    for i, (name, r0, r1) in enumerate((("w_up", 0, 64), ("a_up", 64, 128), ("g_up", 128, 256))):
        blk = prm[name].reshape(r1 - r0, ng, width).transpose(1, 0, 2)
        lora_w = lora_w.at[:, r0:r1, i * width:(i + 1) * width].set(blk)
    mu = prm["mu"].reshape(1, RW_PROJ)
    flat = lambda x: x.reshape(1, RW_DIM)
    params = [
        (mu, pl.BlockSpec((1, width), lambda b, g, c: (0, g))),
        (mu, pl.BlockSpec((1, width), lambda b, g, c: (0, ng + g))),
        (mu, pl.BlockSpec((1, width), lambda b, g, c: (0, 2 * ng + g))),
        (mu, pl.BlockSpec((1, RW_LORA), lambda b, g, c: (0, lora_blk))),
        (flat(prm["w0"]), head(1)), (flat(prm["a0"]), head(1)),
        (lora_w, pl.BlockSpec((1, RW_LORA, 3 * width), lambda b, g, c: (g, 0, 0))),
        (flat(prm["k_k"]), head(1)), (flat(prm["k_a"]), head(1)), (flat(prm["r_k"]), head(1)),
        (flat(prm["gn_w"]), head(1)), (flat(prm["gn_b"]), head(1)),
    ]
    ops += [p for p, _ in params]
    specs += [s for _, s in params]
    st_spec = pl.BlockSpec((grp.bb, RW_SUB * 2 * RW_HP, RW_HD, RW_HD), lambda b, g, c: (b, g, 0, 0))
    ops.append(s0)
    specs.append(st_spec)
    rows = grp.B * grp.L
    nc = grp.nc
    return pl.pallas_call(
        functools.partial(_rwkv_body, grp=grp),
        grid=(grp.B // grp.bb, ng, nc),
        in_specs=specs,
        out_specs=[pl.BlockSpec((MIX_ROWS, width), lambda b, g, c: (b * nc + c, g)), st_spec],
        out_shape=[jax.ShapeDtypeStruct((rows, RW_DIM), BF16), jax.ShapeDtypeStruct(s0.shape, F32)],
        scratch_shapes=[pltpu.VMEM((grp.bb, RW_SUB, RW_HP * LANES, RW_HP * LANES), F32)],
        compiler_params=_cp("parallel", "parallel", "arbitrary", flags=MIXER_FLAGS),
        name="rwkv7",
    )(*ops)


def _ssd_body(*refs, grp):
    r_rows = MIX_ROWS
    qs, bb = grp.qs, grp.bb
    nh, hd, gdim = SSD_HPG, SSD_HD, SSD_GDIM
    cur, halos, rest = _read_windows(grp, refs, 5, 3)
    cw_x, cw_b, cw_c, cb_x, cb_b, cb_c, dtb_ref, alog_ref, dskip_ref, nw_ref, h0_ref, y_ref, h_ref = rest
    x_raw, b_raw, c_raw, z, dt_blk = cur
    c = pl.program_id(2)

    @pl.when(c == 0)
    def _():
        h_ref[...] = h0_ref[...]

    m = _masks(grp)
    xs = _silu(_conv(grp, x_raw, halos[0], cw_x, cb_x[...], CONV_W))
    bm = _silu(_conv(grp, b_raw, halos[1], cw_b, cb_b[...], CONV_W))
    cm = _silu(_conv(grp, c_raw, halos[2], cw_c, cb_c[...], CONV_W))

    dt = jnp.where(m.valid, _softplus(dt_blk[:, :SSD_HEADS] + dtb_ref[...]), 0.0)
    cums = _dot_hi(m.cumsum, dt * (-jnp.exp(alog_ref[...])))
    acs, acs_end = cums[:r_rows], cums[r_rows:]
    eye_h = (_iota2((SSD_HEADS, SSD_HEADS), 0) == _iota2((SSD_HEADS, SSD_HEADS), 1)).astype(F32)
    acs_row = _dot_nt_hi(eye_h, acs)
    ei, ej = _iota2((3 * SSD_HEADS, 3 * SSD_DIM), 0), _iota2((3 * SSD_HEADS, 3 * SSD_DIM), 1)
    spread = ((ei // SSD_HEADS == ej // SSD_DIM) & (ei % SSD_HEADS == (ej % SSD_DIM) // hd)).astype(F32)
    lanes = _dot_hi(jnp.concatenate([dt, acs_end - acs, acs], axis=1), spread)
    xd = xs * lanes[:, :SSD_DIM]
    xd_dec = xd * jnp.exp(lanes[:, SSD_DIM:2 * SSD_DIM])
    acs_lanes = lanes[:, 2 * SSD_DIM:]

    def group(g):
        gl = slice(g * gdim, (g + 1) * gdim)
        bmg, cmg = bm[:, g * SSD_STATE:(g + 1) * SSD_STATE], cm[:, g * SSD_STATE:(g + 1) * SSD_STATE]
        cb = _dot_nt(cmg, bmg)
        yield
        yd = []
        for r in range(nh):
            h = g * nh + r
            diff = acs[:, h:h + 1] - acs_row[h:h + 1, :]
            lmat = jnp.where(m.incl, jnp.exp(jnp.where(m.incl, diff, 0.0)), 0.0)
            yd.append(_dot(cb * lmat, xd[:, h * hd:(h + 1) * hd]))
            yield
        y = jnp.concatenate(yd, axis=1)
        head_rows = (_iota2((gdim, SSD_HEADS), 0) // hd + g * nh == _iota2((gdim, SSD_HEADS), 1)).astype(F32)
        yoff = []
        for j in range(bb):
            rows = slice(j * qs, (j + 1) * qs)
            hj = h_ref[j, g]
            yoff.append(_dot_nt(cmg[rows], hj))
            yield
            end_col = jnp.exp(_dot_nt_hi(head_rows, acs_end[j * qs:j * qs + SUBLANES]))[:, :1]
            yield
            h_ref[j, g] = hj * end_col + _dot_tn(xd_dec[rows, gl], bmg[rows])
            yield
        yoff = yoff[0] if bb == 1 else jnp.concatenate(yoff, axis=0)
        y = y + yoff * jnp.exp(acs_lanes[:, gl]) + xs[:, gl] * dskip_ref[:, gl]
        yg = y * _silu(z[:, gl])
        yg = yg * lax.rsqrt(jnp.mean(yg * yg, axis=-1, keepdims=True) + EPS)
        return yg * nw_ref[:, gl]

    y = jnp.concatenate(_interleave([group(g) for g in range(SSD_GROUPS)]), axis=1)
    y_ref[...] = y.astype(y_ref.dtype)


def _ssd(grp, proj, hist, h0, prm):
    bc = SSD_GROUPS * SSD_STATE
    toks = [
        _Tok(proj, SSD_DIM, lambda g: AB_X0 // SSD_DIM, hist, lambda g: 0),
        _Tok(proj, bc, lambda g: AB_B0 // bc, hist, lambda g: SSD_DIM // bc),
        _Tok(proj, bc, lambda g: AB_C0 // bc, hist, lambda g: SSD_DIM // bc + 1),
        _Tok(proj, SSD_DIM, lambda g: AB_Z0 // SSD_DIM),
        _Tok(proj, LANES, lambda g: AB_DT0 // LANES),
    ]
    ops, specs = _mixer_specs(grp, toks)
    cw, cbias = prm["conv_w"], prm["conv_b"].reshape(1, SSD_CONV_DIM)
    conv_cols = [(SSD_DIM, lambda g: 0), (bc, lambda g: SSD_DIM // bc), (bc, lambda g: SSD_DIM // bc + 1)]
    params = [(cw, _param_spec(wd, col)(CONV_W)) for wd, col in conv_cols]
    params += [(cbias, _param_spec(wd, col)(1)) for wd, col in conv_cols]
    full = lambda n: pl.BlockSpec((1, n), lambda b, g, c: (0, 0))
    params += [
        (prm["dt_bias"].reshape(1, SSD_HEADS), full(SSD_HEADS)),
        (prm["A_log"].reshape(1, SSD_HEADS), full(SSD_HEADS)),
        (jnp.repeat(prm["D"], SSD_HD).reshape(1, SSD_DIM), full(SSD_DIM)),
        (prm["norm_w"].reshape(1, SSD_DIM), full(SSD_DIM)),
    ]
    ops += [p for p, _ in params]
    specs += [s for _, s in params]
    st_spec = pl.BlockSpec((grp.bb, SSD_GROUPS, SSD_GDIM, SSD_STATE), lambda b, g, c: (b, 0, 0, 0))
    ops.append(h0)
    specs.append(st_spec)
    rows = grp.B * grp.L
    nc = grp.nc
    return pl.pallas_call(
        functools.partial(_ssd_body, grp=grp),
        grid=(grp.B // grp.bb, 1, nc),
        in_specs=specs,
        out_specs=[pl.BlockSpec((MIX_ROWS, SSD_DIM), lambda b, g, c: (b * nc + c, 0)), st_spec],
        out_shape=[jax.ShapeDtypeStruct((rows, SSD_DIM), BF16), jax.ShapeDtypeStruct(h0.shape, F32)],
        compiler_params=_cp("parallel", "arbitrary", "arbitrary"),
        name="ssd",
    )(*ops)


def _gdn_body(*refs, grp):
    r_rows = MIX_ROWS
    qs, bb = grp.qs, grp.bb
    nh = GDN_G
    n = nh * r_rows
    cur, halos, rest = _read_windows(grp, refs, 5, 3)
    cw_q, cw_k, cw_v, alog_ref, dtb_ref, nw_ref, s0_ref, o_ref, s_ref = rest
    q_raw, k_raw, v_raw, z, ba = cur
    c = pl.program_id(2)

    @pl.when(c == 0)
    def _():
        s_ref[...] = s0_ref[...]

    m = _masks(grp, nh)
    q_all = _silu(_conv(grp, q_raw, halos[0], cw_q, None, CONV_W))
    k_all = _silu(_conv(grp, k_raw, halos[1], cw_k, None, CONV_W))
    v_all = _silu(_conv(grp, v_raw, halos[2], cw_v, None, CONV_W))
    per_step = nh * GDN_SUB
    eye_g = (_iota2((SUBLANES, nh), 0) == _iota2((SUBLANES, nh), 1)).astype(F32)
    col = lambda x: jnp.concatenate([x[:, h:h + 1] for h in range(nh)], axis=0)

    def stack(sub):
        lanes = slice(sub * nh * GDN_D, (sub + 1) * nh * GDN_D)
        heads = slice(sub * nh, (sub + 1) * nh)
        q = _stack_heads(q_all[:, lanes], nh, GDN_D)
        k = _stack_heads(k_all[:, lanes], nh, GDN_D)
        v = _stack_heads(v_all[:, lanes], nh, GDN_D)
        q = q * lax.rsqrt(jnp.sum(q * q, axis=-1, keepdims=True) + 1e-6) * (GDN_D ** -0.5)
        k = k * lax.rsqrt(jnp.sum(k * k, axis=-1, keepdims=True) + 1e-6)

        beta = jnp.where(m.valid, _sigmoid(ba[:, heads]), 0.0)
        a_raw = ba[:, per_step + sub * nh:per_step + (sub + 1) * nh]
        gg = jnp.where(m.valid, -jnp.exp(alog_ref[0][:, heads]) * _softplus(a_raw + dtb_ref[0][:, heads]), 0.0)
        gcs = _dot_hi(m.cumsum, gg)
        yield
        gc, gc_end = gcs[:r_rows], gcs[r_rows:]
        gc_row = _dot_nt_hi(eye_g, gc)
        yield

        bcol, gcol, gend = col(beta), col(gc), col(gc_end)
        grow = jnp.concatenate([gc_row[h:h + 1, :] for h in range(nh)], axis=1)
        dec = jnp.where(m.incl, jnp.exp(jnp.where(m.incl, gcol - grow, 0.0)), 0.0)
        kb = k * bcol
        kq = _dot_nt(jnp.concatenate([kb, q], axis=0), k)
        yield
        lm = -jnp.where(m.strict, kq[:n] * dec, 0.0)
        attn = jnp.where(m.incl, kq[n:] * dec, 0.0)
        eg = jnp.exp(gcol)
        sol = yield from _solve_unit_lower(lm, jnp.concatenate([v * bcol, kb * eg], axis=1), m.eye, qs)
        vw, kcd = sol[:, :GDN_D], sol[:, GDN_D:]
        qg = q * eg
        kg = k * jnp.exp(gend - gcol)

        def state(j):
            return jnp.concatenate([s_ref[j, sub * nh + h] for h in range(nh)], axis=0)

        parts = []
        for j in range(bb):
            lhs = jnp.concatenate([_seq_rows(grp, kcd, j, nh), _seq_rows(grp, qg, j, nh)], axis=0)
            parts.append(_dot(_expand_rows(lhs, nh, qs, GDN_D), state(j)))
            yield
        half = nh * qs
        ks = _from_seq_rows(grp, [p[:half] for p in parts], nh)
        qsv = _from_seq_rows(grp, [p[half:] for p in parts], nh)
        v_new = vw - ks
        o = qsv + _dot(attn, v_new)
        yield
        for j in range(bb):
            kgj = _expand_rows(_seq_rows(grp, kg, j, nh), nh, qs, GDN_D)
            g_last = jnp.exp(gc_end[j * qs:j * qs + 1, :])
            g_last = jnp.concatenate([jnp.broadcast_to(g_last[:, h:h + 1], (GDN_D, 1)) for h in range(nh)], axis=0)
            s_new = state(j) * g_last + _dot_tn(kgj, _seq_rows(grp, v_new, j, nh))
            yield
            for h in range(nh):
                s_ref[j, sub * nh + h] = s_new[h * GDN_D:(h + 1) * GDN_D]
        o = o * lax.rsqrt(jnp.mean(o * o, axis=-1, keepdims=True) + EPS) * nw_ref[...]
        return _unstack_heads(o, nh)

    o = jnp.concatenate(_interleave([stack(sub) for sub in range(GDN_SUB)]), axis=1)
    o_ref[...] = (o * _silu(z)).astype(o_ref.dtype)


def _zeros_like_axis(x, n, axis):
    shape = list(x.shape)
    shape[axis] = n
    return jnp.zeros(shape, x.dtype)


def _ab_layout(x, axis=-1):
    cut = lambda a, b: lax.slice_in_dim(x, a, b, axis=axis)
    zeros = lambda n: _zeros_like_axis(x, n, axis)
    rw, z = cut(0, RW_PROJ), cut(RW_PROJ, RW_PROJ + SSD_DIM)
    xbc = cut(RW_PROJ + SSD_DIM, RW_PROJ + SSD_DIM + SSD_CONV_DIM)
    dt = cut(RW_PROJ + SSD_DIM + SSD_CONV_DIM, AB_PROJ)
    return jnp.concatenate([z, xbc, dt, zeros(AB_RW0 - AB_DT0 - SSD_HEADS), rw, zeros(AB_COLS - AB_RW0 - RW_PROJ)],
                           axis=axis)


def _gdn_layout(x, axis=-1):
    per_step = GDN_G * GDN_SUB
    main = GDN_CONV_DIM + GDN_V
    cut = lambda a, b: lax.slice_in_dim(x, a, b, axis=axis)
    parts = [cut(0, main)]
    for g in range(GDN_HEADS // per_step):
        h0, h1 = g * per_step, (g + 1) * per_step
        parts += [cut(main + h0, main + h1), cut(main + GDN_HEADS + h0, main + GDN_HEADS + h1),
                  _zeros_like_axis(x, LANES - 2 * per_step, axis)]
    total = main + (GDN_HEADS // per_step) * LANES
    parts.append(_zeros_like_axis(x, -total % MM_TN, axis))
    return jnp.concatenate(parts, axis=axis)


def _gdn(grp, proj, hist, s0, prm):
    per_step = GDN_G * GDN_SUB
    width = per_step * GDN_D
    ng = GDN_HEADS // per_step
    ba_blk = (GDN_CONV_DIM + GDN_V) // LANES
    toks = [
        _Tok(proj, width, lambda g: g, hist, lambda g: g),
        _Tok(proj, width, lambda g: ng + g, hist, lambda g: ng + g),
        _Tok(proj, width, lambda g: 2 * ng + g, hist, lambda g: 2 * ng + g),
        _Tok(proj, width, lambda g: 3 * ng + g),
        _Tok(proj, LANES, lambda g: ba_blk + g),
    ]
    ops, specs = _mixer_specs(grp, toks)
    cw = prm["conv_w"]
    grouped = lambda x: x.reshape(ng, 1, per_step)
    params = [
        (cw, _param_spec(width, lambda g: g)(CONV_W)),
        (cw, _param_spec(width, lambda g: ng + g)(CONV_W)),
        (cw, _param_spec(width, lambda g: 2 * ng + g)(CONV_W)),
        (grouped(prm["A_log"]), pl.BlockSpec((1, 1, per_step), lambda b, g, c: (g, 0, 0))),
        (grouped(prm["dt_bias"]), pl.BlockSpec((1, 1, per_step), lambda b, g, c: (g, 0, 0))),
        (prm["norm_w"].reshape(1, GDN_D), pl.BlockSpec((1, GDN_D), lambda b, g, c: (0, 0))),
    ]
    ops += [p for p, _ in params]
    specs += [s for _, s in params]
    st_spec = pl.BlockSpec((grp.bb, per_step, GDN_D, GDN_D), lambda b, g, c: (b, g, 0, 0))
    ops.append(s0)
    specs.append(st_spec)
    rows = grp.B * grp.L
    nc = grp.nc
    return pl.pallas_call(
        functools.partial(_gdn_body, grp=grp),
        grid=(grp.B // grp.bb, ng, nc),
        in_specs=specs,
        out_specs=[pl.BlockSpec((MIX_ROWS, width), lambda b, g, c: (b * nc + c, g)), st_spec],
        out_shape=[jax.ShapeDtypeStruct((rows, GDN_V), BF16), jax.ShapeDtypeStruct(s0.shape, F32)],
        compiler_params=_cp("parallel", "parallel", "arbitrary", flags=MIXER_FLAGS),
        name="gdn",
    )(*ops)


def _attend(q, k_ref, v_ref, nseq, lq):
    scale = XA_HD ** -0.5
    outs = []
    for j in range(nseq):
        qrows = slice(j * lq, (j + 1) * lq)
        mrows = slice(j * N_MEM, (j + 1) * N_MEM)
        heads = []
        for h in range(XA_HEADS):
            sl = slice(h * XA_HD, (h + 1) * XA_HD)
            s = _dot_nt(q[qrows, sl], k_ref[mrows, sl]) * scale
            p = jnp.exp(s - jnp.max(s, axis=-1, keepdims=True))
            heads.append(_dot(p, v_ref[mrows, sl]) / jnp.sum(p, axis=-1, keepdims=True))
        outs.append(jnp.concatenate(heads, axis=1))
    return outs[0] if nseq == 1 else jnp.concatenate(outs, axis=0)


def _attend_packed(q, k_ref, v_ref, nseq, lq):
    scale = XA_HD ** -0.5
    rows, cols = XA_HEADS * lq, N_MEM * XA_HEADS
    own_head = (_iota2((rows, cols), 0) // lq) == (_iota2((rows, cols), 1) % XA_HEADS)
    outs = []
    for j in range(nseq):
        qh = _stack_heads(q[j * lq:(j + 1) * lq], XA_HEADS, XA_HD)
        mrows = slice(j * cols, (j + 1) * cols)
        s = jnp.where(own_head, _dot_nt(qh, k_ref[mrows, :]) * scale, -1e30)
        p = jnp.where(own_head, jnp.exp(s - jnp.max(s, axis=-1, keepdims=True)), 0.0)
        o = _dot(p, v_ref[mrows, :]) / jnp.sum(p, axis=-1, keepdims=True)
        outs.append(_unstack_heads(o, XA_HEADS))
    return jnp.concatenate(outs, axis=0)


def _xattn_body(h_ref, g_ref, wq_ref, k_ref, v_ref, wo_ref, o_ref, *, attend, nseq, lq):
    x = h_ref[...]
    xn = x * lax.rsqrt(jnp.mean(x * x, axis=-1, keepdims=True) + EPS) * g_ref[...]
    q = jnp.dot(xn.astype(BF16), wq_ref[...], preferred_element_type=F32)
    o = attend(q, k_ref, v_ref, nseq, lq)
    o_ref[...] = x + jnp.dot(o.astype(BF16), wo_ref[...], preferred_element_type=F32)


def _xattn(grp, h, gain, w_q, w_o, layer, mem_k, mem_v, mem_row0):
    rows, d = h.shape
    if grp.L >= XA_TQ:
        nseq, lq, attend = 1, XA_TQ, _attend
        per = grp.L // XA_TQ
        mem_spec = pl.BlockSpec((N_MEM, XA_DIM), lambda i: (mem_row0 // N_MEM + i // per, 0))
    else:
        nseq, lq, attend = MIX_ROWS // grp.L, grp.L, _attend_packed
        blk = nseq * N_MEM * XA_HEADS
        mem_spec = pl.BlockSpec((blk, XA_HD), lambda i: (mem_row0 * XA_HEADS // blk + i, 0))
    h_spec = pl.BlockSpec((nseq * lq, d), lambda i: (i, 0))
    return pl.pallas_call(
        functools.partial(_xattn_body, attend=attend, nseq=nseq, lq=lq),
        grid=(rows // (nseq * lq),),
        in_specs=[h_spec, pl.BlockSpec((1, d), lambda i: (0, 0)),
                  pl.BlockSpec((None, d, XA_DIM), lambda i: (layer, 0, 0)), mem_spec, mem_spec,
                  pl.BlockSpec((None, XA_DIM, d), lambda i: (layer, 0, 0))],
        out_specs=h_spec,
        out_shape=jax.ShapeDtypeStruct((rows, d), F32),
        compiler_params=_cp("parallel"),
        name="xattn",
    )(h, gain.reshape(1, d), w_q, mem_k, mem_v, w_o)


def _ffn_in_body(*refs, grp, tiles_per_seq):
    it = iter(refs)
    a_ref, g_ref, wg_ref, wu_ref, hist_ref, cw_ref, cb_ref, act_ref, tail_ref, an_ref = (next(it) for _ in range(10))
    carry_ref = None if grp.embedded else next(it)
    i, j = pl.program_id(0), pl.program_id(1)

    @pl.when(j == 0)
    def _():
        x = a_ref[...]
        y = x * lax.rsqrt(jnp.mean(x * x, axis=-1, keepdims=True) + EPS)
        an_ref[...] = (y * g_ref[...]).astype(BF16)

    a = an_ref[...]
    gate = jnp.dot(a, wg_ref[...], preferred_element_type=F32)
    up = jnp.dot(a, wu_ref[...], preferred_element_type=F32)
    if grp.embedded:
        halo = hist_ref[...]
        tail_ref[...] = gate
    else:
        halo = jnp.where(i % tiles_per_seq == 0, hist_ref[...], carry_ref[j])
        last = gate[gate.shape[0] - SUBLANES:]
        carry_ref[j] = last
        tail_ref[...] = last
    conv = _conv(grp, gate, halo, cw_ref, cb_ref[...], FFN_CONV_W)
    act_ref[...] = (_silu(conv) * up).astype(act_ref.dtype)


def _ffn_in(grp, h, gain, w_in, layer, hist, conv_w, conv_b, tm=MM_TM, tn=MM_TN):
    assert grp.embedded or grp.L % tm == 0
    rows, k = h.shape
    nct = D_FF // tn
    tiles_per_seq = max(grp.L // tm, 1)
    hrows = tm if grp.embedded else SUBLANES
    tile_idx = lambda i, j: (i, j)
    hist_idx = tile_idx if grp.embedded else (lambda i, j: (i // tiles_per_seq, j))
    specs = [
        pl.BlockSpec((tm, k), lambda i, j: (i, 0)),
        pl.BlockSpec((1, k), lambda i, j: (0, 0)),
        _weight_spec(w_in, layer, k, tn, lambda j: j),
        _weight_spec(w_in, layer, k, tn, lambda j: nct + j),
        pl.BlockSpec((hrows, tn), hist_idx),
        pl.BlockSpec((FFN_CONV_W, tn), lambda i, j: (0, j)),
        pl.BlockSpec((1, tn), lambda i, j: (0, j)),
    ]
    scratch = [pltpu.VMEM((tm, k), BF16)]
    if not grp.embedded:
        scratch.append(pltpu.VMEM((nct, SUBLANES, tn), F32))
    act, tails = pl.pallas_call(
        functools.partial(_ffn_in_body, grp=grp, tiles_per_seq=tiles_per_seq),
        grid=(rows // tm, nct),
        in_specs=specs,
        out_specs=[pl.BlockSpec((tm, tn), tile_idx), pl.BlockSpec((hrows, tn), tile_idx)],
        out_shape=[jax.ShapeDtypeStruct((rows, D_FF), BF16), jax.ShapeDtypeStruct((rows // tm * hrows, D_FF), F32)],
        scratch_shapes=scratch,
        compiler_params=_cp("arbitrary", "arbitrary"),
        name="ffn_in",
    )(h, gain.reshape(1, k), w_in, w_in, hist, conv_w, conv_b.reshape(1, D_FF))
    if not grp.embedded:
        tails = tails.reshape(grp.B, tiles_per_seq, SUBLANES, D_FF)[:, -1].reshape(grp.B * SUBLANES, D_FF)
    return act, tails


def _history(grp, buf, width):
    b, k, c = buf.shape
    if grp.embedded:
        h = jnp.pad(buf, ((0, 0), (grp.first - k, grp.L - grp.first), (0, 0)))
        return h.reshape(b * grp.L, c)
    return jnp.pad(buf, ((0, 0), (SUBLANES - k, 0), (0, 0))).reshape(b * SUBLANES, c)


def _tail(grp, x2d, k, cols):
    x = x2d.reshape(grp.B, grp.L, x2d.shape[1])
    return x[:, grp.L - k:, cols].astype(F32)


def _run_group(grp, x2d, mem_k, mem_v, mem_row0, st, wts, prm):
    h = x2d
    out = {}
    tm_proj = min(MM_TM_PROJ, h.shape[0])
    proj = _matmul(h, wts["in_ab"], gain=prm["norm_mix"][0], out_dtype=BF16, tm=tm_proj, w_rows_are_outputs=True)
    y_rw, s_rw = _rwkv(grp, proj, _history(grp, st["rw_shift"][:, None, :], RW_PROJ), st["rwkv"], prm["rw"])
    y_ssd, s_ssd = _ssd(grp, proj, _history(grp, st["ssd_conv"], SSD_CONV_DIM),
                        st["ssd"].reshape(grp.B, SSD_GROUPS, SSD_GDIM, SSD_STATE), prm["ssd"])
    out["rwkv"] = s_rw
    out["rw_shift"] = _tail(grp, proj, 1, slice(AB_RW0, AB_RW0 + RW_PROJ))[:, 0]
    out["ssd"] = s_ssd.reshape(st["ssd"].shape)
    out["ssd_conv"] = _tail(grp, proj, CONV_W - 1, slice(AB_X0, AB_X0 + SSD_CONV_DIM))
    h = _matmul([y_rw, y_ssd], wts["out_ab"], res=h)
    ffn_bufs = []
    for l in range(2):
        if l == 1:
            proj = _matmul(h, wts["in_c"], gain=prm["norm_mix"][1], out_dtype=BF16, tm=tm_proj, w_rows_are_outputs=True)
            y_c, s_gdn = _gdn(grp, proj, _history(grp, st["gdn_conv"], GDN_CONV_DIM), st["gdn"], prm["gdn"])
            out["gdn"] = s_gdn
            out["gdn_conv"] = _tail(grp, proj, CONV_W - 1, slice(0, GDN_CONV_DIM))
            h = _matmul(y_c, wts["out_c"], res=h)
        h = _xattn(grp, h, prm["norm_xa"][l], wts["xq"], wts["xo"], l, mem_k, mem_v, mem_row0[l])
        act, gate_tail = _ffn_in(grp, h, prm["norm_ffn"][l], wts["ffn_in"], l, _history(grp, st["ffn_conv"][l], D_FF),
                                 prm["ffn_conv_w"][l], prm["ffn_conv_b"][l])
        ffn_bufs.append(gate_tail.reshape(grp.B, SUBLANES, D_FF)[:, SUBLANES - (FFN_CONV_W - 1):])
        h = _matmul(act, wts["ffn_out"], layer=l, res=h)
    out["ffn_conv"] = jnp.stack(ffn_bufs)
    out["y"] = _rmsnorm(h, prm["norm_final"])
    return out


def kernel(x_prompt, x_sample, mem_prompt, state_rwkv, state_rwkv_shift, state_ssd, state_ssd_conv, state_gdn,
           state_gdn_conv, state_ffn_conv, cache_mem_k, cache_mem_v, norm_mix, norm_xa, norm_mem, norm_ffn,
           norm_final, w_in_ab, rw_mu, rw_w0, rw_w_up, rw_a0, rw_a_up, rw_g_up, rw_k_k, rw_k_a, rw_r_k, rw_gn_w,
           rw_gn_b, ssd_conv_w, ssd_conv_b, ssd_dt_bias, ssd_A_log, ssd_D, ssd_norm_w, w_out_ab, w_in_c,
           gdn_conv_w, gdn_A_log, gdn_dt_bias, gdn_norm_w, w_out_c, w_xq, w_xk, w_xv, w_xo, ffn_w_in, ffn_conv_w,
           ffn_conv_b, ffn_w_out):
    bp, lp, _ = x_prompt.shape
    bs, ls, _ = x_sample.shape
    depth = norm_mix.shape[0]
    assert depth == 2 and w_in_ab.shape[0] == 1 and w_in_c.shape[0] == 1
    assert lp % MIX_ROWS == 0 and lp % XA_TQ == 0 and lp % MM_TM == 0
    pad_rows = SUBLANES - ls
    assert CONV_W - 1 <= pad_rows and CONV_W - 1 <= ls and MIX_ROWS % SUBLANES == 0
    gp = Group(B=bp, L=lp, first=0, bb=1)
    gs = Group(B=bs, L=SUBLANES, first=pad_rows, bb=MIX_ROWS // SUBLANES)
    assert bs % gs.bb == 0 and (bs * SUBLANES) % MM_TM == 0 and (bp * lp) % MM_TM == 0

    wts = dict(
        in_ab=_ab_layout(jnp.swapaxes(w_in_ab[0], 0, 1), axis=0).astype(BF16), out_ab=w_out_ab[0].astype(BF16),
        in_c=_gdn_layout(jnp.swapaxes(w_in_c[0], 0, 1), axis=0).astype(BF16), out_c=w_out_c[0].astype(BF16),
        xq=w_xq.astype(BF16), xo=w_xo.astype(BF16), ffn_in=ffn_w_in.astype(BF16), ffn_out=ffn_w_out.astype(BF16),
    )
    prm = dict(
        norm_mix=norm_mix, norm_xa=norm_xa, norm_ffn=norm_ffn, norm_final=norm_final,
        ffn_conv_w=ffn_conv_w, ffn_conv_b=ffn_conv_b,
        rw=dict(mu=rw_mu[0], w0=rw_w0[0], w_up=rw_w_up[0], a0=rw_a0[0], a_up=rw_a_up[0], g_up=rw_g_up[0],
                k_k=rw_k_k[0], k_a=rw_k_a[0], r_k=rw_r_k[0], gn_w=rw_gn_w[0], gn_b=rw_gn_b[0]),
        ssd=dict(conv_w=ssd_conv_w[0], conv_b=ssd_conv_b[0], dt_bias=ssd_dt_bias[0], A_log=ssd_A_log[0],
                 D=ssd_D[0], norm_w=ssd_norm_w[0]),
        gdn=dict(conv_w=gdn_conv_w[0], A_log=gdn_A_log[0], dt_bias=gdn_dt_bias[0], norm_w=gdn_norm_w[0]),
    )

    mem2d = mem_prompt.reshape(bp * N_MEM, D_MODEL)
    mem_tm = min(MM_TM, bp * N_MEM)
    mk, mv = [], []
    for l in range(depth):
        mk.append(_matmul(mem2d, w_xk[l].astype(BF16), gain=norm_mem[l], tm=mem_tm))
        mv.append(_matmul(mem2d, w_xv[l].astype(BF16), gain=norm_mem[l], tm=mem_tm))
    mem_k_p = jnp.stack(mk)
    mem_v_p = jnp.stack(mv)

    zeros = lambda *s: jnp.zeros(s, F32)
    st_p = dict(
        rwkv=zeros(bp, RW_HEADS, RW_HD, RW_HD), rw_shift=zeros(bp, RW_PROJ),
        ssd=zeros(bp, SSD_GROUPS, SSD_HPG, SSD_HD, SSD_STATE), ssd_conv=zeros(bp, CONV_W - 1, SSD_CONV_DIM),
        gdn=zeros(bp, GDN_HEADS, GDN_D, GDN_D), gdn_conv=zeros(bp, CONV_W - 1, GDN_CONV_DIM),
        ffn_conv=zeros(depth, bp, FFN_CONV_W - 1, D_FF),
    )
    rp = _run_group(gp, x_prompt.reshape(bp * lp, D_MODEL), mem_k_p.reshape(depth * bp * N_MEM, XA_DIM),
                    mem_v_p.reshape(depth * bp * N_MEM, XA_DIM), [l * bp * N_MEM for l in range(depth)],
                    st_p, wts, prm)

    st_s = dict(rwkv=state_rwkv[0], rw_shift=state_rwkv_shift[0], ssd=state_ssd[0], ssd_conv=state_ssd_conv[0],
                gdn=state_gdn[0], gdn_conv=state_gdn_conv[0], ffn_conv=state_ffn_conv)
    xs = jnp.pad(x_sample, ((0, 0), (pad_rows, 0), (0, 0))).reshape(bs * SUBLANES, D_MODEL)
    rs = _run_group(gs, xs, cache_mem_k.reshape(depth * bs * N_MEM * XA_HEADS, XA_HD),
                    cache_mem_v.reshape(depth * bs * N_MEM * XA_HEADS, XA_HD), [l * bs * N_MEM for l in range(depth)],
                    st_s, wts, prm)

    y_p = rp["y"].reshape(bp, lp, D_MODEL)
    y_s = rs["y"].reshape(bs, SUBLANES, D_MODEL)[:, pad_rows:]
    lead = lambda x: x[None]
    mem_shape = (depth, bp, N_MEM, XA_HEADS, XA_HD)
    return (y_p, y_s, lead(rp["rwkv"]), lead(rs["rwkv"]), lead(rp["rw_shift"]), lead(rs["rw_shift"]),
            lead(rp["ssd"]), lead(rs["ssd"]), lead(rp["ssd_conv"]), lead(rs["ssd_conv"]),
            lead(rp["gdn"]), lead(rs["gdn"]), lead(rp["gdn_conv"]), lead(rs["gdn_conv"]),
            rp["ffn_conv"], rs["ffn_conv"], mem_k_p.reshape(mem_shape), mem_v_p.reshape(mem_shape))
```

```python
import functools
from typing import NamedTuple

import jax
import jax.numpy as jnp
from jax import lax
from jax.experimental import pallas as pl
from jax.experimental.pallas import tpu as pltpu

F32 = jnp.float32
BF16 = jnp.bfloat16
HI = lax.Precision.HIGHEST

D_MODEL = 2048
EPS = 1e-6
RW_HEADS, RW_HD = 16, 64
RW_DIM = RW_HEADS * RW_HD
RW_LORA = 256
RW_PROJ = 3 * RW_DIM + RW_LORA
RW_GN_EPS = 6.4e-4
SSD_HEADS, SSD_HD, SSD_GROUPS, SSD_STATE = 16, 64, 2, 128
SSD_HPG = SSD_HEADS // SSD_GROUPS
SSD_DIM = SSD_HEADS * SSD_HD
SSD_GDIM = SSD_HPG * SSD_HD
SSD_CONV_DIM = SSD_DIM + 2 * SSD_GROUPS * SSD_STATE
SSD_PROJ = SSD_DIM + SSD_CONV_DIM + SSD_HEADS
AB_PROJ = RW_PROJ + SSD_PROJ
GDN_HEADS, GDN_D = 16, 128
GDN_V = GDN_HEADS * GDN_D
GDN_CONV_DIM = 3 * GDN_V
GDN_PROJ = GDN_CONV_DIM + GDN_V + 2 * GDN_HEADS
D_FF = 5632
XA_HEADS, XA_HD, N_MEM = 4, 128, 256
XA_DIM = XA_HEADS * XA_HD
CONV_W = 4
FFN_CONV_W = 3

AB_Z0 = 0
AB_X0 = SSD_DIM
AB_B0 = 2 * SSD_DIM
AB_C0 = AB_B0 + SSD_GROUPS * SSD_STATE
AB_DT0 = AB_C0 + SSD_GROUPS * SSD_STATE
AB_RW0 = 3 * RW_DIM
AB_COLS = 6656
assert AB_DT0 + 128 <= AB_RW0 and AB_RW0 + RW_PROJ <= AB_COLS and AB_RW0 % RW_DIM == 0

SUBLANES = 8
LANES = 128
VMEM_LIMIT = 56 * 1024 * 1024

MIX_ROWS = 64
INV_BLOCK = 16
INV_SPLIT = 128
MIX_NB = 2
MM_TM, MM_TN = 1024, 512
MM_TM_PROJ = 2048
RW_HP = 2
RW_SUB = 4
GDN_G = 4
GDN_SUB = 4
XA_TQ = 512
NORM_TM = 512


class Group(NamedTuple):
    B: int
    L: int
    first: int
    bb: int
    nb: int = 1

    @property
    def outer(self):
        return self.B // self.bb

    @property
    def per_outer(self):
        return self.bb * self.L

    @property
    def qs(self):
        return MIX_ROWS // self.bb

    @property
    def nc(self):
        return self.L // self.qs

    @property
    def embedded(self):
        return self.first > 0


MIXER_FLAGS = None


def _cp(*sem, flags=None):
    return pltpu.CompilerParams(dimension_semantics=sem, vmem_limit_bytes=VMEM_LIMIT, flags=flags)


def _dot(a, b):
    return jnp.dot(a.astype(BF16), b.astype(BF16), preferred_element_type=F32)


def _dot_nt(a, b):
    return lax.dot_general(a.astype(BF16), b.astype(BF16), (((1,), (1,)), ((), ())), preferred_element_type=F32)


def _dot_tn(a, b):
    return lax.dot_general(a.astype(BF16), b.astype(BF16), (((0,), (0,)), ((), ())), preferred_element_type=F32)


def _dot_hi(a, b):
    return jnp.dot(a, b, preferred_element_type=F32, precision=HI)


def _split_bf16(x):
    hi = x.astype(BF16)
    return hi, (x - hi.astype(F32)).astype(BF16)


def _dot_x3(a, b):
    ah, al = _split_bf16(a)
    bh, bl = _split_bf16(b)
    return jnp.dot(jnp.concatenate([ah, ah, al], axis=1), jnp.concatenate([bh, bl, bh], axis=0),
                   preferred_element_type=F32)


def _dot_nt_hi(a, b):
    return lax.dot_general(a, b, (((1,), (1,)), ((), ())), preferred_element_type=F32, precision=HI)


def _sigmoid(x):
    return 1.0 / (1.0 + jnp.exp(-x))


def _silu(x):
    return x * _sigmoid(x)


def _softplus(x):
    return jnp.maximum(x, 0.0) + jnp.log(1.0 + jnp.exp(-jnp.abs(x)))


def _iota2(shape, axis):
    return lax.broadcasted_iota(jnp.int32, shape, axis)


class _Masks(NamedTuple):
    incl: jax.Array
    strict: jax.Array
    eye: jax.Array
    cumsum: jax.Array
    valid: jax.Array


def _masks(grp, heads=1):
    r = MIX_ROWS
    n = heads * r
    ri, ci = _iota2((n, n), 0), _iota2((n, n), 1)
    same = (ri // grp.qs) == (ci // grp.qs)
    incl = same & (ci <= ri)
    strict = same & (ci < ri)
    si, sj = _iota2((2 * r, r), 0), _iota2((2 * r, r), 1)
    same_seq = ((si % r) // grp.qs) == (sj // grp.qs)
    cumsum = (same_seq & ((sj <= si) | (si >= r))).astype(F32)
    t = _iota2((r, 1), 0) % grp.qs
    return _Masks(incl, strict, (ri == ci).astype(F32), cumsum, t >= grp.first)


def _neumann(lm, rhs, nil, eye=None):
    n = lm.shape[0]
    p = None
    if rhs is None:
        t = eye + lm
        if nil > 2:
            p = _dot(lm, lm)
            yield
    elif nil > 2:
        both = _dot(lm, jnp.concatenate([lm, rhs], axis=1))
        yield
        p, t = both[:, :n], rhs + both[:, n:]
    else:
        t = rhs + _dot(lm, rhs)
        yield
    k = 2
    while k < nil:
        if 2 * k < nil:
            both = _dot(p, jnp.concatenate([p, t], axis=1))
            p, t = both[:, :n], t + both[:, n:]
        else:
            t = t + _dot(p, t)
        yield
        k *= 2
    return t


def _inv_unit_lower(lm, eye, qs):
    if qs <= INV_BLOCK:
        return (yield from _neumann(lm, None, qs, eye))
    n = lm.shape[0]
    diag = (_iota2((n, n), 0) // INV_BLOCK) == (_iota2((n, n), 1) // INV_BLOCK)
    d = jnp.where(diag, lm, 0.0)
    dinv = yield from _neumann(d, None, INV_BLOCK, eye)
    nm = _dot(dinv, lm - d)
    yield
    return (yield from _neumann(nm, dinv, qs // INV_BLOCK))


def _lockstep(gens):
    results = [None] * len(gens)
    active = list(range(len(gens)))
    while active:
        for i in list(active):
            try:
                next(gens[i])
            except StopIteration as stop:
                results[i] = stop.value
                active.remove(i)
        if active:
            yield
    return results


def _solve_unit_lower(lm, rhs, eye, qs):
    n = lm.shape[0]
    if n > INV_SPLIT:
        blocks = [slice(i, i + INV_SPLIT) for i in range(0, n, INV_SPLIT)]
        inv = yield from _lockstep([_inv_unit_lower(lm[b, b], eye[b, b], qs) for b in blocks])
        zero = jnp.zeros((INV_SPLIT, INV_SPLIT), F32)
        t = jnp.concatenate([jnp.concatenate([inv[i] if i == j else zero for j in range(len(blocks))], axis=1)
                             for i in range(len(blocks))], axis=0)
    else:
        t = yield from _inv_unit_lower(lm, eye, qs)
    x = _dot(t, rhs)
    yield
    return x


def _interleave(gens):
    results = [None] * len(gens)
    active = list(range(len(gens)))
    while active:
        for i in list(active):
            try:
                next(gens[i])
            except StopIteration as stop:
                results[i] = stop.value
                active.remove(i)
    return results


def _stack_heads(x, heads, width):
    return jnp.concatenate([x[:, h * width:(h + 1) * width] for h in range(heads)], axis=0)


def _unstack_heads(x, heads):
    r = x.shape[0] // heads
    return jnp.concatenate([x[h * r:(h + 1) * r] for h in range(heads)], axis=1)


def _seq_rows(grp, x, j, heads):
    if grp.bb == 1:
        return x
    qs = grp.qs
    return jnp.concatenate([x[h * MIX_ROWS + j * qs:h * MIX_ROWS + (j + 1) * qs] for h in range(heads)], axis=0)


def _from_seq_rows(grp, parts, heads):
    if grp.bb == 1:
        return parts[0]
    qs = grp.qs
    return jnp.concatenate([p[h * qs:(h + 1) * qs] for h in range(heads) for p in parts], axis=0)


def _expand_rows(x, heads, rows_per_head, width):
    m = x.shape[0]
    if x.shape[1] == width:
        x = jnp.concatenate([x] * heads, axis=1)
    keep = ((_iota2((m, heads * width), 0) // rows_per_head) % heads) == (_iota2((m, heads * width), 1) // width)
    return jnp.where(keep, x, 0.0)


def _taps(grp, cur, halo, width):
    if grp.embedded:
        t = _iota2((cur.shape[0], 1), 0) % grp.qs
        full = jnp.where(t < grp.first, halo, cur)
        return [full] + [pltpu.roll(full, s, axis=0) for s in range(1, width)]
    full = jnp.concatenate([halo, cur], axis=0)
    return [cur] + [pltpu.roll(full, s, axis=0)[SUBLANES:] for s in range(1, width)]


def _conv(grp, cur, halo, w_ref, bias, width):
    taps = _taps(grp, cur, halo, width)
    y = taps[0] * w_ref[width - 1:width, :]
    for s in range(1, width):
        y = y + taps[s] * w_ref[width - 1 - s:width - s, :]
    return y if bias is None else y + bias


class _Tok(NamedTuple):
    arr: jax.Array
    width: int
    col: object
    hist: jax.Array = None
    hcol: object = None


def _view3(grp, x2d):
    return x2d.reshape(grp.outer, grp.per_outer, x2d.shape[1])


def _mixer_specs(grp, toks):
    nb = grp.nb
    ops, specs = [], []
    for t in toks:
        ops.append(_view3(grp, t.arr))
        specs.append(pl.BlockSpec((nb, MIX_ROWS, t.width), lambda b, g, c, t=t: (b, c, t.col(g))))
    for t in toks:
        if t.hist is None:
            continue
        if grp.embedded:
            ops.append(_view3(grp, t.hist))
            specs.append(pl.BlockSpec((nb, MIX_ROWS, t.width), lambda b, g, c, t=t: (b, 0, t.hcol(g))))
            continue
        tile = _sublane_tile(t.arr.dtype)
        sub = MIX_ROWS // tile
        ops.append(_view3(grp, t.arr))
        specs.append(pl.BlockSpec(
            (nb, tile, t.width), lambda b, g, c, t=t, sub=sub: (b, jnp.maximum(c * sub - 1, 0), t.col(g))))
        ops.append(t.hist.reshape(grp.B, SUBLANES, t.hist.shape[1]))
        specs.append(pl.BlockSpec((nb, SUBLANES, t.width), lambda b, g, c, t=t: (b, 0, t.hcol(g))))
    return ops, specs


def _read_windows(grp, refs, n_plain, n_hist):
    c = pl.program_id(2)
    nb = grp.nb
    cur = [[r[s].astype(F32) for s in range(nb)] for r in refs[:n_plain]]
    rest = refs[n_plain:]
    halos = []
    for i in range(n_hist):
        if grp.embedded:
            halos.append([rest[i][s] for s in range(nb)])
        else:
            per = []
            for s in range(nb):
                prev = rest[2 * i][s]
                prev = prev[prev.shape[0] - SUBLANES:].astype(F32)
                per.append(jnp.where(c == 0, rest[2 * i + 1][s], prev))
            halos.append(per)
    used = n_hist if grp.embedded else 2 * n_hist
    return cur, halos, rest[used:]


def _sublane_tile(dtype):
    return SUBLANES * 4 // jnp.dtype(dtype).itemsize


def _param_spec(width, col):
    return lambda rows: pl.BlockSpec((rows, width), lambda b, g, c: (0, col(g)))


def _matmul_body(*refs, norm, has_res, n_a, w_rows_are_outputs):
    it = iter(refs)
    a_refs = [next(it) for _ in range(n_a)]
    g_ref = next(it) if norm else None
    w_ref = next(it)
    res_ref = next(it) if has_res else None
    o_ref = next(it)
    if norm:
        an_ref = next(it)

        @pl.when(pl.program_id(1) == 0)
        def _():
            x = a_refs[0][...]
            y = x * lax.rsqrt(jnp.mean(x * x, axis=-1, keepdims=True) + EPS)
            an_ref[...] = (y * g_ref[...]).astype(BF16)

        if w_rows_are_outputs:
            acc = lax.dot_general(an_ref[...], w_ref[...], (((1,), (1,)), ((), ())), preferred_element_type=F32)
        else:
            acc = jnp.dot(an_ref[...], w_ref[...], preferred_element_type=F32)
    else:
        acc, k0 = None, 0
        for a_ref in a_refs:
            k1 = k0 + a_ref.shape[1]
            part = jnp.dot(a_ref[...], w_ref[k0:k1, :], preferred_element_type=F32)
            acc = part if acc is None else acc + part
            k0 = k1
    if has_res:
        acc = acc + res_ref[...]
    o_ref[...] = acc.astype(o_ref.dtype)


def _matmul(a, w, *, layer=None, gain=None, res=None, out_dtype=F32, tm=MM_TM, tn=MM_TN, w_rows_are_outputs=False):
    a_list = list(a) if isinstance(a, (list, tuple)) else [a]
    m = a_list[0].shape[0]
    k, n = w.shape[-2:][::-1] if w_rows_are_outputs else w.shape[-2:]
    assert not w_rows_are_outputs or (gain is not None and w.ndim == 2)
    assert sum(x.shape[1] for x in a_list) == k
    norm = gain is not None
    assert not norm or len(a_list) == 1
    ops = list(a_list)
    specs = [pl.BlockSpec((tm, x.shape[1]), lambda i, j: (i, 0)) for x in a_list]
    if norm:
        ops.append(gain.reshape(1, k))
        specs.append(pl.BlockSpec((1, k), lambda i, j: (0, 0)))
    ops.append(w)
    if w_rows_are_outputs:
        specs.append(pl.BlockSpec((tn, k), lambda i, j: (j, 0)))
    else:
        specs.append(_weight_spec(w, layer, k, tn, lambda j: j))
    if res is not None:
        ops.append(res)
        specs.append(pl.BlockSpec((tm, tn), lambda i, j: (i, j)))
    return pl.pallas_call(
        functools.partial(_matmul_body, norm=norm, has_res=res is not None, n_a=len(a_list),
                          w_rows_are_outputs=w_rows_are_outputs),
        grid=(m // tm, n // tn),
        in_specs=specs,
        out_specs=pl.BlockSpec((tm, tn), lambda i, j: (i, j)),
        out_shape=jax.ShapeDtypeStruct((m, n), out_dtype),
        scratch_shapes=[pltpu.VMEM((tm, k), BF16)] if norm else [],
        compiler_params=_cp("parallel", "arbitrary"),
        name="matmul",
    )(*ops)


def _weight_spec(w, layer, k, tn, col):
    if w.ndim == 2:
        return pl.BlockSpec((k, tn), lambda i, j: (0, col(j)))
    return pl.BlockSpec((None, k, tn), lambda i, j: (layer, 0, col(j)))


def _rmsnorm_body(x_ref, g_ref, o_ref):
    x = x_ref[...]
    y = x * lax.rsqrt(jnp.mean(x * x, axis=-1, keepdims=True) + EPS)
    o_ref[...] = (y * g_ref[...]).astype(o_ref.dtype)


def _rmsnorm(x, gain, out_dtype=F32, tm=NORM_TM):
    m, k = x.shape
    return pl.pallas_call(
        _rmsnorm_body,
        grid=(m // tm,),
        in_specs=[pl.BlockSpec((tm, k), lambda i: (i, 0)), pl.BlockSpec((1, k), lambda i: (0, 0))],
        out_specs=pl.BlockSpec((tm, k), lambda i: (i, 0)),
        out_shape=jax.ShapeDtypeStruct((m, k), out_dtype),
        compiler_params=_cp("parallel"),
        name="rmsnorm",
    )(x, gain.reshape(1, k))


def _rwkv_body(*refs, grp):
    r_rows = MIX_ROWS
    qs, bb, nc, nb = grp.qs, grp.bb, grp.nc, grp.nb
    nh = 2 * RW_HP
    sw = nh * RW_HD
    width = RW_SUB * sw
    n = nh * r_rows
    cur, halos, rest = _read_windows(grp, refs, 4, 4)
    (mu_r, mu_k, mu_v, mu_lo, w0_ref, a0_ref, lora_ref, kk_ref, ka_ref, rk_ref, gnw_ref, gnb_ref,
     s0_ref, y_ref, s_ref, sbig_ref) = rest
    c = pl.program_id(2)

    @pl.when(c == 0)
    def _():
        for j in range(nb * bb):
            for sub in range(RW_SUB):
                st = jnp.concatenate([s0_ref[j, sub * nh + h] for h in range(nh)], axis=0)
                sbig_ref[j, sub] = _expand_rows(st, nh, RW_HD, RW_HD)

    m = _masks(grp, nh)
    bd = ((_iota2((sw, sw), 0) // RW_HD) == (_iota2((sw, sw), 1) // RW_HD)).astype(F32)

    def head_sums(x):
        return _unstack_heads(_dot_x3(_stack_heads(x, RW_SUB, sw), bd), RW_SUB)

    def block(s):
        def shifted(i, mu_ref):
            x = cur[i][s]
            prev = _taps(grp, x, halos[i][s], 2)[1]
            return x + (prev - x) * mu_ref[...]

        r = shifted(0, mu_r)
        k = shifted(1, mu_k)
        v = shifted(2, mu_v)
        lo = shifted(3, mu_lo)
        lora_in = jnp.concatenate([jnp.tanh(lo[:, :64]), lo[:, 64:128], _sigmoid(lo[:, 128:256])], axis=1)
        lora = _dot_x3(lora_in, lora_ref[0])
        w = -_softplus(-(w0_ref[...] + lora[:, :width])) - 0.5
        lw = jnp.where(m.valid, -jnp.exp(w), 0.0)
        a = _sigmoid(a0_ref[...] + lora[:, width:2 * width])
        g = lora[:, 2 * width:]

        kx = k * kk_ref[...]
        kp = jnp.where(m.valid, k * (1.0 + (a - 1.0) * ka_ref[...]), 0.0)
        sums = head_sums(jnp.concatenate([kx * kx, r * kp * rk_ref[...]], axis=0))
        kkn = kx * lax.rsqrt(sums[:r_rows] + 1e-6)
        bonus = sums[r_rows:] * v
        at = jnp.where(m.valid, -kkn, 0.0)
        bt = jnp.where(m.valid, kkn * a, 0.0)

        cums = _dot_hi(m.cumsum, lw)
        cum, cum_end = cums[:r_rows], cums[r_rows:]
        e_neg = jnp.exp(-cum)
        e_end = jnp.exp(cum_end - cum)
        p_end = jnp.exp(cum_end)
        a_til, r_til = at * jnp.exp(cum - lw), r * jnp.exp(cum)
        b_til, k_til = bt * e_neg, kp * e_neg
        b_hat, k_hat = bt * e_end, kp * e_end

        def stack(sub):
            lanes = slice(sub * sw, (sub + 1) * sw)

            def tile(x):
                return _expand_rows(jnp.concatenate([x[:, lanes]] * nh, axis=0), nh, r_rows, RW_HD)

            ar = jnp.concatenate([tile(a_til), tile(r_til)], axis=0)
            bk = jnp.concatenate([tile(b_til), tile(k_til)], axis=0)
            bk_end = jnp.concatenate([tile(b_hat), tile(k_hat)], axis=0)
            v_exp = tile(v)
            gm = _dot_nt(ar, bk)
            yield
            m_ab = jnp.where(m.strict, gm[:n, :n], 0.0)
            m_ak = jnp.where(m.strict, gm[:n, n:], 0.0)
            m_r = jnp.concatenate([jnp.where(m.incl, gm[n:, :n], 0.0), jnp.where(m.incl, gm[n:, n:], 0.0)], axis=1)

            def seq2(x, j):
                if bb == 1:
                    return x
                return jnp.concatenate([_seq_rows(grp, x[:n], j, nh), _seq_rows(grp, x[n:], j, nh)], axis=0)

            parts = []
            for j in range(bb):
                parts.append(_dot_nt(seq2(ar, j), sbig_ref[s * bb + j, sub]))
                yield
            half = nh * qs
            as_a = _from_seq_rows(grp, [p[:half] for p in parts], nh)
            as_r = _from_seq_rows(grp, [p[half:] for p in parts], nh)
            rhs = as_a + _dot(m_ak, v_exp)
            yield
            u = yield from _solve_unit_lower(m_ab, rhs, m.eye, qs)
            uv = jnp.concatenate([u, v_exp], axis=0)
            y_exp = as_r + _dot(m_r, uv)
            yield
            y = y_exp[:r_rows]
            for h in range(1, nh):
                y = y + y_exp[h * r_rows:(h + 1) * r_rows]
            for j in range(bb):
                sbig_ref[s * bb + j, sub] = (sbig_ref[s * bb + j, sub] * p_end[j * qs:j * qs + 1, lanes]
                                             + _dot_tn(seq2(uv, j), seq2(bk_end, j)))
                yield
            return y

        def finish(ys):
            y = jnp.concatenate(ys, axis=1)
            mean = head_sums(y) * (1.0 / RW_HD)
            d = y - mean
            var = head_sums(d * d) * (1.0 / RW_HD)
            out = d * lax.rsqrt(var + RW_GN_EPS) * gnw_ref[...] + gnb_ref[...] + bonus
            y_ref[s] = (out * g).astype(y_ref.dtype)

        return [stack(sub) for sub in range(RW_SUB)], finish

    blocks = [block(s) for s in range(nb)]
    results = _interleave([chain for chains, _ in blocks for chain in chains])
    for s, (_, finish) in enumerate(blocks):
        finish(results[s * RW_SUB:(s + 1) * RW_SUB])

    @pl.when(c == nc - 1)
    def _():
        for j in range(nb * bb):
            for sub in range(RW_SUB):
                sb = sbig_ref[j, sub]
                for h in range(nh):
                    s_ref[j, sub * nh + h] = sb[h * RW_HD:(h + 1) * RW_HD, h * RW_HD:(h + 1) * RW_HD]


def _rwkv(grp, proj, hist, s0, prm):
    width = RW_SUB * RW_HP * LANES
    ng = RW_DIM // width
    lora_blk = 3 * RW_DIM // RW_LORA
    p0 = AB_RW0 // width
    toks = [
        _Tok(proj, width, lambda g: p0 + g, hist, lambda g: g),
        _Tok(proj, width, lambda g: p0 + ng + g, hist, lambda g: ng + g),
        _Tok(proj, width, lambda g: p0 + 2 * ng + g, hist, lambda g: 2 * ng + g),
        _Tok(proj, RW_LORA, lambda g: AB_RW0 // RW_LORA + lora_blk, hist, lambda g: lora_blk),
    ]
    ops, specs = _mixer_specs(grp, toks)
    head = _param_spec(width, lambda g: g)
    lora_w = jnp.zeros((ng, RW_LORA, 3 * width), F32)
    for i, (name, r0, r1) in enumerate((("w_up", 0, 64), ("a_up", 64, 128), ("g_up", 128, 256))):
        blk = prm[name].reshape(r1 - r0, ng, width).transpose(1, 0, 2)
        lora_w = lora_w.at[:, r0:r1, i * width:(i + 1) * width].set(blk)
    mu = prm["mu"].reshape(1, RW_PROJ)
    flat = lambda x: x.reshape(1, RW_DIM)
    params = [
        (mu, pl.BlockSpec((1, width), lambda b, g, c: (0, g))),
        (mu, pl.BlockSpec((1, width), lambda b, g, c: (0, ng + g))),
        (mu, pl.BlockSpec((1, width), lambda b, g, c: (0, 2 * ng + g))),
        (mu, pl.BlockSpec((1, RW_LORA), lambda b, g, c: (0, lora_blk))),
        (flat(prm["w0"]), head(1)), (flat(prm["a0"]), head(1)),
        (lora_w, pl.BlockSpec((1, RW_LORA, 3 * width), lambda b, g, c: (g, 0, 0))),
        (flat(prm["k_k"]), head(1)), (flat(prm["k_a"]), head(1)), (flat(prm["r_k"]), head(1)),
        (flat(prm["gn_w"]), head(1)), (flat(prm["gn_b"]), head(1)),
    ]
    ops += [p for p, _ in params]
    specs += [s for _, s in params]
    seqs = grp.nb * grp.bb
    st_spec = pl.BlockSpec((seqs, RW_SUB * 2 * RW_HP, RW_HD, RW_HD), lambda b, g, c: (b, g, 0, 0))
    ops.append(s0)
    specs.append(st_spec)
    y, s_out = pl.pallas_call(
        functools.partial(_rwkv_body, grp=grp),
        grid=(grp.outer // grp.nb, ng, grp.nc),
        in_specs=specs,
        out_specs=[pl.BlockSpec((grp.nb, MIX_ROWS, width), lambda b, g, c: (b, c, g)), st_spec],
        out_shape=[jax.ShapeDtypeStruct((grp.outer, grp.per_outer, RW_DIM), BF16),
                   jax.ShapeDtypeStruct(s0.shape, F32)],
        scratch_shapes=[pltpu.VMEM((seqs, RW_SUB, RW_HP * LANES, RW_HP * LANES), F32)],
        compiler_params=_cp("parallel", "parallel", "arbitrary", flags=MIXER_FLAGS),
        name="rwkv7",
    )(*ops)
    return y.reshape(grp.B * grp.L, RW_DIM), s_out


def _ssd_body(*refs, grp):
    r_rows = MIX_ROWS
    qs, bb, nb = grp.qs, grp.bb, grp.nb
    nh, hd, gdim = SSD_HPG, SSD_HD, SSD_GDIM
    cur, halos, rest = _read_windows(grp, refs, 5, 3)
    cw_x, cw_b, cw_c, cb_x, cb_b, cb_c, dtb_ref, alog_ref, dskip_ref, nw_ref, h0_ref, y_ref, h_ref = rest
    c = pl.program_id(2)

    @pl.when(c == 0)
    def _():
        h_ref[...] = h0_ref[...]

    m = _masks(grp)
    eye_h = (_iota2((SSD_HEADS, SSD_HEADS), 0) == _iota2((SSD_HEADS, SSD_HEADS), 1)).astype(F32)
    ei, ej = _iota2((3 * SSD_HEADS, 3 * SSD_DIM), 0), _iota2((3 * SSD_HEADS, 3 * SSD_DIM), 1)
    spread = ((ei // SSD_HEADS == ej // SSD_DIM) & (ei % SSD_HEADS == (ej % SSD_DIM) // hd)).astype(F32)

    def block(s):
        x_raw, b_raw, c_raw, z, dt_blk = (cur[i][s] for i in range(5))
        xs = _silu(_conv(grp, x_raw, halos[0][s], cw_x, cb_x[...], CONV_W))
        bm = _silu(_conv(grp, b_raw, halos[1][s], cw_b, cb_b[...], CONV_W))
        cm = _silu(_conv(grp, c_raw, halos[2][s], cw_c, cb_c[...], CONV_W))

        dt = jnp.where(m.valid, _softplus(dt_blk[:, :SSD_HEADS] + dtb_ref[...]), 0.0)
        cums = _dot_hi(m.cumsum, dt * (-jnp.exp(alog_ref[...])))
        acs, acs_end = cums[:r_rows], cums[r_rows:]
        acs_row = _dot_nt_hi(eye_h, acs)
        lanes = _dot_hi(jnp.concatenate([dt, acs_end - acs, acs], axis=1), spread)
        xd = xs * lanes[:, :SSD_DIM]
        xd_dec = xd * jnp.exp(lanes[:, SSD_DIM:2 * SSD_DIM])
        acs_lanes = lanes[:, 2 * SSD_DIM:]

        def group(g):
            gl = slice(g * gdim, (g + 1) * gdim)
            bmg, cmg = bm[:, g * SSD_STATE:(g + 1) * SSD_STATE], cm[:, g * SSD_STATE:(g + 1) * SSD_STATE]
            cb = _dot_nt(cmg, bmg)
            yield
            yd = []
            for r in range(nh):
                h = g * nh + r
                diff = acs[:, h:h + 1] - acs_row[h:h + 1, :]
                lmat = jnp.where(m.incl, jnp.exp(jnp.where(m.incl, diff, 0.0)), 0.0)
                yd.append(_dot(cb * lmat, xd[:, h * hd:(h + 1) * hd]))
                yield
            y = jnp.concatenate(yd, axis=1)
            head_rows = (_iota2((gdim, SSD_HEADS), 0) // hd + g * nh == _iota2((gdim, SSD_HEADS), 1)).astype(F32)
            yoff = []
            for j in range(bb):
                rows = slice(j * qs, (j + 1) * qs)
                hj = h_ref[s * bb + j, g]
                yoff.append(_dot_nt(cmg[rows], hj))
                yield
                end_col = jnp.exp(_dot_nt_hi(head_rows, acs_end[j * qs:j * qs + SUBLANES]))[:, :1]
                yield
                h_ref[s * bb + j, g] = hj * end_col + _dot_tn(xd_dec[rows, gl], bmg[rows])
                yield
            yoff = yoff[0] if bb == 1 else jnp.concatenate(yoff, axis=0)
            y = y + yoff * jnp.exp(acs_lanes[:, gl]) + xs[:, gl] * dskip_ref[:, gl]
            yg = y * _silu(z[:, gl])
            yg = yg * lax.rsqrt(jnp.mean(yg * yg, axis=-1, keepdims=True) + EPS)
            return yg * nw_ref[:, gl]

        return [group(g) for g in range(SSD_GROUPS)]

    results = _interleave([chain for s in range(nb) for chain in block(s)])
    for s in range(nb):
        y_ref[s] = jnp.concatenate(results[s * SSD_GROUPS:(s + 1) * SSD_GROUPS], axis=1).astype(y_ref.dtype)


def _ssd(grp, proj, hist, h0, prm):
    bc = SSD_GROUPS * SSD_STATE
    toks = [
        _Tok(proj, SSD_DIM, lambda g: AB_X0 // SSD_DIM, hist, lambda g: 0),
        _Tok(proj, bc, lambda g: AB_B0 // bc, hist, lambda g: SSD_DIM // bc),
        _Tok(proj, bc, lambda g: AB_C0 // bc, hist, lambda g: SSD_DIM // bc + 1),
        _Tok(proj, SSD_DIM, lambda g: AB_Z0 // SSD_DIM),
        _Tok(proj, LANES, lambda g: AB_DT0 // LANES),
    ]
    ops, specs = _mixer_specs(grp, toks)
    cw, cbias = prm["conv_w"], prm["conv_b"].reshape(1, SSD_CONV_DIM)
    conv_cols = [(SSD_DIM, lambda g: 0), (bc, lambda g: SSD_DIM // bc), (bc, lambda g: SSD_DIM // bc + 1)]
    params = [(cw, _param_spec(wd, col)(CONV_W)) for wd, col in conv_cols]
    params += [(cbias, _param_spec(wd, col)(1)) for wd, col in conv_cols]
    full = lambda n: pl.BlockSpec((1, n), lambda b, g, c: (0, 0))
    params += [
        (prm["dt_bias"].reshape(1, SSD_HEADS), full(SSD_HEADS)),
        (prm["A_log"].reshape(1, SSD_HEADS), full(SSD_HEADS)),
        (jnp.repeat(prm["D"], SSD_HD).reshape(1, SSD_DIM), full(SSD_DIM)),
        (prm["norm_w"].reshape(1, SSD_DIM), full(SSD_DIM)),
    ]
    ops += [p for p, _ in params]
    specs += [s for _, s in params]
    st_spec = pl.BlockSpec((grp.nb * grp.bb, SSD_GROUPS, SSD_GDIM, SSD_STATE), lambda b, g, c: (b, 0, 0, 0))
    ops.append(h0)
    specs.append(st_spec)
    y, h_out = pl.pallas_call(
        functools.partial(_ssd_body, grp=grp),
        grid=(grp.outer // grp.nb, 1, grp.nc),
        in_specs=specs,
        out_specs=[pl.BlockSpec((grp.nb, MIX_ROWS, SSD_DIM), lambda b, g, c: (b, c, 0)), st_spec],
        out_shape=[jax.ShapeDtypeStruct((grp.outer, grp.per_outer, SSD_DIM), BF16),
                   jax.ShapeDtypeStruct(h0.shape, F32)],
        compiler_params=_cp("parallel", "arbitrary", "arbitrary"),
        name="ssd",
    )(*ops)
    return y.reshape(grp.B * grp.L, SSD_DIM), h_out


def _gdn_body(*refs, grp):
    r_rows = MIX_ROWS
    qs, bb, nb = grp.qs, grp.bb, grp.nb
    nh = GDN_G
    n = nh * r_rows
    cur, halos, rest = _read_windows(grp, refs, 5, 3)
    cw_q, cw_k, cw_v, alog_ref, dtb_ref, nw_ref, s0_ref, o_ref, s_ref = rest
    c = pl.program_id(2)

    @pl.when(c == 0)
    def _():
        s_ref[...] = s0_ref[...]

    m = _masks(grp, nh)
    per_step = nh * GDN_SUB
    eye_g = (_iota2((SUBLANES, nh), 0) == _iota2((SUBLANES, nh), 1)).astype(F32)
    col = lambda x: jnp.concatenate([x[:, h:h + 1] for h in range(nh)], axis=0)

    def block(s):
        q_raw, k_raw, v_raw, z, ba = (cur[i][s] for i in range(5))
        q_all = _silu(_conv(grp, q_raw, halos[0][s], cw_q, None, CONV_W))
        k_all = _silu(_conv(grp, k_raw, halos[1][s], cw_k, None, CONV_W))
        v_all = _silu(_conv(grp, v_raw, halos[2][s], cw_v, None, CONV_W))

        def stack(sub):
            lanes = slice(sub * nh * GDN_D, (sub + 1) * nh * GDN_D)
            heads = slice(sub * nh, (sub + 1) * nh)
            q = _stack_heads(q_all[:, lanes], nh, GDN_D)
            k = _stack_heads(k_all[:, lanes], nh, GDN_D)
            v = _stack_heads(v_all[:, lanes], nh, GDN_D)
            q = q * lax.rsqrt(jnp.sum(q * q, axis=-1, keepdims=True) + 1e-6) * (GDN_D ** -0.5)
            k = k * lax.rsqrt(jnp.sum(k * k, axis=-1, keepdims=True) + 1e-6)

            beta = jnp.where(m.valid, _sigmoid(ba[:, heads]), 0.0)
            a_raw = ba[:, per_step + sub * nh:per_step + (sub + 1) * nh]
            gg = jnp.where(m.valid, -jnp.exp(alog_ref[0][:, heads]) * _softplus(a_raw + dtb_ref[0][:, heads]), 0.0)
            gcs = _dot_hi(m.cumsum, gg)
            yield
            gc, gc_end = gcs[:r_rows], gcs[r_rows:]
            gc_row = _dot_nt_hi(eye_g, gc)
            yield

            bcol, gcol, gend = col(beta), col(gc), col(gc_end)
            grow = jnp.concatenate([gc_row[h:h + 1, :] for h in range(nh)], axis=1)
            dec = jnp.where(m.incl, jnp.exp(jnp.where(m.incl, gcol - grow, 0.0)), 0.0)
            kb = k * bcol
            kq = _dot_nt(jnp.concatenate([kb, q], axis=0), k)
            yield
            lm = -jnp.where(m.strict, kq[:n] * dec, 0.0)
            attn = jnp.where(m.incl, kq[n:] * dec, 0.0)
            eg = jnp.exp(gcol)
            sol = yield from _solve_unit_lower(lm, jnp.concatenate([v * bcol, kb * eg], axis=1), m.eye, qs)
            vw, kcd = sol[:, :GDN_D], sol[:, GDN_D:]
            qg = q * eg
            kg = k * jnp.exp(gend - gcol)

            def state(j):
                return jnp.concatenate([s_ref[s * bb + j, sub * nh + h] for h in range(nh)], axis=0)

            parts = []
            for j in range(bb):
                lhs = jnp.concatenate([_seq_rows(grp, kcd, j, nh), _seq_rows(grp, qg, j, nh)], axis=0)
                parts.append(_dot(_expand_rows(lhs, nh, qs, GDN_D), state(j)))
                yield
            half = nh * qs
            ks = _from_seq_rows(grp, [p[:half] for p in parts], nh)
            qsv = _from_seq_rows(grp, [p[half:] for p in parts], nh)
            v_new = vw - ks
            o = qsv + _dot(attn, v_new)
            yield
            for j in range(bb):
                kgj = _expand_rows(_seq_rows(grp, kg, j, nh), nh, qs, GDN_D)
                g_last = jnp.exp(gc_end[j * qs:j * qs + 1, :])
                g_last = jnp.concatenate([jnp.broadcast_to(g_last[:, h:h + 1], (GDN_D, 1)) for h in range(nh)], axis=0)
                s_new = state(j) * g_last + _dot_tn(kgj, _seq_rows(grp, v_new, j, nh))
                yield
                for h in range(nh):
                    s_ref[s * bb + j, sub * nh + h] = s_new[h * GDN_D:(h + 1) * GDN_D]
            o = o * lax.rsqrt(jnp.mean(o * o, axis=-1, keepdims=True) + EPS) * nw_ref[...]
            return _unstack_heads(o, nh)

        def finish(os_):
            o_ref[s] = (jnp.concatenate(os_, axis=1) * _silu(z)).astype(o_ref.dtype)

        return [stack(sub) for sub in range(GDN_SUB)], finish

    blocks = [block(s) for s in range(nb)]
    results = _interleave([chain for chains, _ in blocks for chain in chains])
    for s, (_, finish) in enumerate(blocks):
        finish(results[s * GDN_SUB:(s + 1) * GDN_SUB])


def _zeros_like_axis(x, n, axis):
    shape = list(x.shape)
    shape[axis] = n
    return jnp.zeros(shape, x.dtype)


def _ab_layout(x, axis=-1):
    cut = lambda a, b: lax.slice_in_dim(x, a, b, axis=axis)
    zeros = lambda n: _zeros_like_axis(x, n, axis)
    rw, z = cut(0, RW_PROJ), cut(RW_PROJ, RW_PROJ + SSD_DIM)
    xbc = cut(RW_PROJ + SSD_DIM, RW_PROJ + SSD_DIM + SSD_CONV_DIM)
    dt = cut(RW_PROJ + SSD_DIM + SSD_CONV_DIM, AB_PROJ)
    return jnp.concatenate([z, xbc, dt, zeros(AB_RW0 - AB_DT0 - SSD_HEADS), rw, zeros(AB_COLS - AB_RW0 - RW_PROJ)],
                           axis=axis)


def _gdn_layout(x, axis=-1):
    per_step = GDN_G * GDN_SUB
    main = GDN_CONV_DIM + GDN_V
    cut = lambda a, b: lax.slice_in_dim(x, a, b, axis=axis)
    parts = [cut(0, main)]
    for g in range(GDN_HEADS // per_step):
        h0, h1 = g * per_step, (g + 1) * per_step
        parts += [cut(main + h0, main + h1), cut(main + GDN_HEADS + h0, main + GDN_HEADS + h1),
                  _zeros_like_axis(x, LANES - 2 * per_step, axis)]
    total = main + (GDN_HEADS // per_step) * LANES
    parts.append(_zeros_like_axis(x, -total % MM_TN, axis))
    return jnp.concatenate(parts, axis=axis)


def _gdn(grp, proj, hist, s0, prm):
    per_step = GDN_G * GDN_SUB
    width = per_step * GDN_D
    ng = GDN_HEADS // per_step
    ba_blk = (GDN_CONV_DIM + GDN_V) // LANES
    toks = [
        _Tok(proj, width, lambda g: g, hist, lambda g: g),
        _Tok(proj, width, lambda g: ng + g, hist, lambda g: ng + g),
        _Tok(proj, width, lambda g: 2 * ng + g, hist, lambda g: 2 * ng + g),
        _Tok(proj, width, lambda g: 3 * ng + g),
        _Tok(proj, LANES, lambda g: ba_blk + g),
    ]
    ops, specs = _mixer_specs(grp, toks)
    cw = prm["conv_w"]
    grouped = lambda x: x.reshape(ng, 1, per_step)
    params = [
        (cw, _param_spec(width, lambda g: g)(CONV_W)),
        (cw, _param_spec(width, lambda g: ng + g)(CONV_W)),
        (cw, _param_spec(width, lambda g: 2 * ng + g)(CONV_W)),
        (grouped(prm["A_log"]), pl.BlockSpec((1, 1, per_step), lambda b, g, c: (g, 0, 0))),
        (grouped(prm["dt_bias"]), pl.BlockSpec((1, 1, per_step), lambda b, g, c: (g, 0, 0))),
        (prm["norm_w"].reshape(1, GDN_D), pl.BlockSpec((1, GDN_D), lambda b, g, c: (0, 0))),
    ]
    ops += [p for p, _ in params]
    specs += [s for _, s in params]
    st_spec = pl.BlockSpec((grp.nb * grp.bb, per_step, GDN_D, GDN_D), lambda b, g, c: (b, g, 0, 0))
    ops.append(s0)
    specs.append(st_spec)
    o, s_out = pl.pallas_call(
        functools.partial(_gdn_body, grp=grp),
        grid=(grp.outer // grp.nb, ng, grp.nc),
        in_specs=specs,
        out_specs=[pl.BlockSpec((grp.nb, MIX_ROWS, width), lambda b, g, c: (b, c, g)), st_spec],
        out_shape=[jax.ShapeDtypeStruct((grp.outer, grp.per_outer, GDN_V), BF16),
                   jax.ShapeDtypeStruct(s0.shape, F32)],
        compiler_params=_cp("parallel", "parallel", "arbitrary", flags=MIXER_FLAGS),
        name="gdn",
    )(*ops)
    return o.reshape(grp.B * grp.L, GDN_V), s_out


def _attend(q, k_ref, v_ref, nseq, lq):
    scale = XA_HD ** -0.5
    outs = []
    for j in range(nseq):
        qrows = slice(j * lq, (j + 1) * lq)
        mrows = slice(j * N_MEM, (j + 1) * N_MEM)
        heads = []
        for h in range(XA_HEADS):
            sl = slice(h * XA_HD, (h + 1) * XA_HD)
            s = _dot_nt(q[qrows, sl], k_ref[mrows, sl]) * scale
            p = jnp.exp(s - jnp.max(s, axis=-1, keepdims=True))
            heads.append(_dot(p, v_ref[mrows, sl]) / jnp.sum(p, axis=-1, keepdims=True))
        outs.append(jnp.concatenate(heads, axis=1))
    return outs[0] if nseq == 1 else jnp.concatenate(outs, axis=0)


def _attend_packed(q, k_ref, v_ref, nseq, lq):
    scale = XA_HD ** -0.5
    rows, cols = XA_HEADS * lq, N_MEM * XA_HEADS
    own_head = (_iota2((rows, cols), 0) // lq) == (_iota2((rows, cols), 1) % XA_HEADS)
    outs = []
    for j in range(nseq):
        qh = _stack_heads(q[j * lq:(j + 1) * lq], XA_HEADS, XA_HD)
        mrows = slice(j * cols, (j + 1) * cols)
        s = jnp.where(own_head, _dot_nt(qh, k_ref[mrows, :]) * scale, -1e30)
        p = jnp.where(own_head, jnp.exp(s - jnp.max(s, axis=-1, keepdims=True)), 0.0)
        o = _dot(p, v_ref[mrows, :]) / jnp.sum(p, axis=-1, keepdims=True)
        outs.append(_unstack_heads(o, XA_HEADS))
    return jnp.concatenate(outs, axis=0)


def _xattn_body(h_ref, g_ref, wq_ref, k_ref, v_ref, wo_ref, o_ref, *, attend, nseq, lq):
    x = h_ref[...]
    xn = x * lax.rsqrt(jnp.mean(x * x, axis=-1, keepdims=True) + EPS) * g_ref[...]
    q = jnp.dot(xn.astype(BF16), wq_ref[...], preferred_element_type=F32)
    o = attend(q, k_ref, v_ref, nseq, lq)
    o_ref[...] = x + jnp.dot(o.astype(BF16), wo_ref[...], preferred_element_type=F32)


def _xattn(grp, h, gain, w_q, w_o, layer, mem_k, mem_v, mem_row0):
    rows, d = h.shape
    if grp.L >= XA_TQ:
        nseq, lq, attend = 1, XA_TQ, _attend
        per = grp.L // XA_TQ
        mem_spec = pl.BlockSpec((N_MEM, XA_DIM), lambda i: (mem_row0 // N_MEM + i // per, 0))
    else:
        nseq, lq, attend = MIX_ROWS // grp.L, grp.L, _attend_packed
        blk = nseq * N_MEM * XA_HEADS
        mem_spec = pl.BlockSpec((blk, XA_HD), lambda i: (mem_row0 * XA_HEADS // blk + i, 0))
    h_spec = pl.BlockSpec((nseq * lq, d), lambda i: (i, 0))
    return pl.pallas_call(
        functools.partial(_xattn_body, attend=attend, nseq=nseq, lq=lq),
        grid=(rows // (nseq * lq),),
        in_specs=[h_spec, pl.BlockSpec((1, d), lambda i: (0, 0)),
                  pl.BlockSpec((None, d, XA_DIM), lambda i: (layer, 0, 0)), mem_spec, mem_spec,
                  pl.BlockSpec((None, XA_DIM, d), lambda i: (layer, 0, 0))],
        out_specs=h_spec,
        out_shape=jax.ShapeDtypeStruct((rows, d), F32),
        compiler_params=_cp("parallel"),
        name="xattn",
    )(h, gain.reshape(1, d), w_q, mem_k, mem_v, w_o)


def _ffn_in_body(*refs, grp, tiles_per_seq):
    it = iter(refs)
    a_ref, g_ref, wg_ref, wu_ref, hist_ref, cw_ref, cb_ref, act_ref, tail_ref, an_ref = (next(it) for _ in range(10))
    carry_ref = None if grp.embedded else next(it)
    i, j = pl.program_id(0), pl.program_id(1)

    @pl.when(j == 0)
    def _():
        x = a_ref[...]
        y = x * lax.rsqrt(jnp.mean(x * x, axis=-1, keepdims=True) + EPS)
        an_ref[...] = (y * g_ref[...]).astype(BF16)

    a = an_ref[...]
    gate = jnp.dot(a, wg_ref[...], preferred_element_type=F32)
    up = jnp.dot(a, wu_ref[...], preferred_element_type=F32)
    if grp.embedded:
        halo = hist_ref[...]
        tail_ref[...] = gate
    else:
        halo = jnp.where(i % tiles_per_seq == 0, hist_ref[...], carry_ref[j])
        last = gate[gate.shape[0] - SUBLANES:]
        carry_ref[j] = last
        tail_ref[...] = last
    conv = _conv(grp, gate, halo, cw_ref, cb_ref[...], FFN_CONV_W)
    act_ref[...] = (_silu(conv) * up).astype(act_ref.dtype)


def _ffn_in(grp, h, gain, w_in, layer, hist, conv_w, conv_b, tm=MM_TM, tn=MM_TN):
    assert grp.embedded or grp.L % tm == 0
    rows, k = h.shape
    nct = D_FF // tn
    tiles_per_seq = max(grp.L // tm, 1)
    hrows = tm if grp.embedded else SUBLANES
    tile_idx = lambda i, j: (i, j)
    hist_idx = tile_idx if grp.embedded else (lambda i, j: (i // tiles_per_seq, j))
    specs = [
        pl.BlockSpec((tm, k), lambda i, j: (i, 0)),
        pl.BlockSpec((1, k), lambda i, j: (0, 0)),
        _weight_spec(w_in, layer, k, tn, lambda j: j),
        _weight_spec(w_in, layer, k, tn, lambda j: nct + j),
        pl.BlockSpec((hrows, tn), hist_idx),
        pl.BlockSpec((FFN_CONV_W, tn), lambda i, j: (0, j)),
        pl.BlockSpec((1, tn), lambda i, j: (0, j)),
    ]
    scratch = [pltpu.VMEM((tm, k), BF16)]
    if not grp.embedded:
        scratch.append(pltpu.VMEM((nct, SUBLANES, tn), F32))
    act, tails = pl.pallas_call(
        functools.partial(_ffn_in_body, grp=grp, tiles_per_seq=tiles_per_seq),
        grid=(rows // tm, nct),
        in_specs=specs,
        out_specs=[pl.BlockSpec((tm, tn), tile_idx), pl.BlockSpec((hrows, tn), tile_idx)],
        out_shape=[jax.ShapeDtypeStruct((rows, D_FF), BF16), jax.ShapeDtypeStruct((rows // tm * hrows, D_FF), F32)],
        scratch_shapes=scratch,
        compiler_params=_cp("arbitrary", "arbitrary"),
        name="ffn_in",
    )(h, gain.reshape(1, k), w_in, w_in, hist, conv_w, conv_b.reshape(1, D_FF))
    if not grp.embedded:
        tails = tails.reshape(grp.B, tiles_per_seq, SUBLANES, D_FF)[:, -1].reshape(grp.B * SUBLANES, D_FF)
    return act, tails


def _history(grp, buf, width):
    b, k, c = buf.shape
    if grp.embedded:
        h = jnp.pad(buf, ((0, 0), (grp.first - k, grp.L - grp.first), (0, 0)))
        return h.reshape(b * grp.L, c)
    return jnp.pad(buf, ((0, 0), (SUBLANES - k, 0), (0, 0))).reshape(b * SUBLANES, c)


def _tail(grp, x2d, k, cols):
    x = x2d.reshape(grp.B, grp.L, x2d.shape[1])
    return x[:, grp.L - k:, cols].astype(F32)


def _run_group(grp, x2d, mem_k, mem_v, mem_row0, st, wts, prm):
    h = x2d
    out = {}
    tm_proj = min(MM_TM_PROJ, h.shape[0])
    proj = _matmul(h, wts["in_ab"], gain=prm["norm_mix"][0], out_dtype=BF16, tm=tm_proj, w_rows_are_outputs=True)
    y_rw, s_rw = _rwkv(grp, proj, _history(grp, st["rw_shift"][:, None, :], RW_PROJ), st["rwkv"], prm["rw"])
    y_ssd, s_ssd = _ssd(grp, proj, _history(grp, st["ssd_conv"], SSD_CONV_DIM),
                        st["ssd"].reshape(grp.B, SSD_GROUPS, SSD_GDIM, SSD_STATE), prm["ssd"])
    out["rwkv"] = s_rw
    out["rw_shift"] = _tail(grp, proj, 1, slice(AB_RW0, AB_RW0 + RW_PROJ))[:, 0]
    out["ssd"] = s_ssd.reshape(st["ssd"].shape)
    out["ssd_conv"] = _tail(grp, proj, CONV_W - 1, slice(AB_X0, AB_X0 + SSD_CONV_DIM))
    h = _matmul([y_rw, y_ssd], wts["out_ab"], res=h)
    ffn_bufs = []
    for l in range(2):
        if l == 1:
            proj = _matmul(h, wts["in_c"], gain=prm["norm_mix"][1], out_dtype=BF16, tm=tm_proj, w_rows_are_outputs=True)
            y_c, s_gdn = _gdn(grp, proj, _history(grp, st["gdn_conv"], GDN_CONV_DIM), st["gdn"], prm["gdn"])
            out["gdn"] = s_gdn
            out["gdn_conv"] = _tail(grp, proj, CONV_W - 1, slice(0, GDN_CONV_DIM))
            h = _matmul(y_c, wts["out_c"], res=h)
        h = _xattn(grp, h, prm["norm_xa"][l], wts["xq"], wts["xo"], l, mem_k, mem_v, mem_row0[l])
        act, gate_tail = _ffn_in(grp, h, prm["norm_ffn"][l], wts["ffn_in"], l, _history(grp, st["ffn_conv"][l], D_FF),
                                 prm["ffn_conv_w"][l], prm["ffn_conv_b"][l])
        ffn_bufs.append(gate_tail.reshape(grp.B, SUBLANES, D_FF)[:, SUBLANES - (FFN_CONV_W - 1):])
        h = _matmul(act, wts["ffn_out"], layer=l, res=h)
    out["ffn_conv"] = jnp.stack(ffn_bufs)
    out["y"] = _rmsnorm(h, prm["norm_final"])
    return out


def kernel(x_prompt, x_sample, mem_prompt, state_rwkv, state_rwkv_shift, state_ssd, state_ssd_conv, state_gdn,
           state_gdn_conv, state_ffn_conv, cache_mem_k, cache_mem_v, norm_mix, norm_xa, norm_mem, norm_ffn,
           norm_final, w_in_ab, rw_mu, rw_w0, rw_w_up, rw_a0, rw_a_up, rw_g_up, rw_k_k, rw_k_a, rw_r_k, rw_gn_w,
           rw_gn_b, ssd_conv_w, ssd_conv_b, ssd_dt_bias, ssd_A_log, ssd_D, ssd_norm_w, w_out_ab, w_in_c,
           gdn_conv_w, gdn_A_log, gdn_dt_bias, gdn_norm_w, w_out_c, w_xq, w_xk, w_xv, w_xo, ffn_w_in, ffn_conv_w,
           ffn_conv_b, ffn_w_out):
    bp, lp, _ = x_prompt.shape
    bs, ls, _ = x_sample.shape
    depth = norm_mix.shape[0]
    assert depth == 2 and w_in_ab.shape[0] == 1 and w_in_c.shape[0] == 1
    assert lp % MIX_ROWS == 0 and lp % XA_TQ == 0 and lp % MM_TM == 0
    pad_rows = SUBLANES - ls
    assert CONV_W - 1 <= pad_rows and CONV_W - 1 <= ls and MIX_ROWS % SUBLANES == 0
    gp = Group(B=bp, L=lp, first=0, bb=1, nb=MIX_NB if bp % MIX_NB == 0 else 1)
    gs = Group(B=bs, L=SUBLANES, first=pad_rows, bb=MIX_ROWS // SUBLANES)
    assert bs % gs.bb == 0 and (bs * SUBLANES) % MM_TM == 0 and (bp * lp) % MM_TM == 0

    wts = dict(
        in_ab=_ab_layout(jnp.swapaxes(w_in_ab[0], 0, 1), axis=0).astype(BF16), out_ab=w_out_ab[0].astype(BF16),
        in_c=_gdn_layout(jnp.swapaxes(w_in_c[0], 0, 1), axis=0).astype(BF16), out_c=w_out_c[0].astype(BF16),
        xq=w_xq.astype(BF16), xo=w_xo.astype(BF16), ffn_in=ffn_w_in.astype(BF16), ffn_out=ffn_w_out.astype(BF16),
    )
    prm = dict(
        norm_mix=norm_mix, norm_xa=norm_xa, norm_ffn=norm_ffn, norm_final=norm_final,
        ffn_conv_w=ffn_conv_w, ffn_conv_b=ffn_conv_b,
        rw=dict(mu=rw_mu[0], w0=rw_w0[0], w_up=rw_w_up[0], a0=rw_a0[0], a_up=rw_a_up[0], g_up=rw_g_up[0],
                k_k=rw_k_k[0], k_a=rw_k_a[0], r_k=rw_r_k[0], gn_w=rw_gn_w[0], gn_b=rw_gn_b[0]),
        ssd=dict(conv_w=ssd_conv_w[0], conv_b=ssd_conv_b[0], dt_bias=ssd_dt_bias[0], A_log=ssd_A_log[0],
                 D=ssd_D[0], norm_w=ssd_norm_w[0]),
        gdn=dict(conv_w=gdn_conv_w[0], A_log=gdn_A_log[0], dt_bias=gdn_dt_bias[0], norm_w=gdn_norm_w[0]),
    )

    mem2d = mem_prompt.reshape(bp * N_MEM, D_MODEL)
    mem_tm = min(MM_TM, bp * N_MEM)
    mk, mv = [], []
    for l in range(depth):
        mk.append(_matmul(mem2d, w_xk[l].astype(BF16), gain=norm_mem[l], tm=mem_tm))
        mv.append(_matmul(mem2d, w_xv[l].astype(BF16), gain=norm_mem[l], tm=mem_tm))
    mem_k_p = jnp.stack(mk)
    mem_v_p = jnp.stack(mv)

    zeros = lambda *s: jnp.zeros(s, F32)
    st_p = dict(
        rwkv=zeros(bp, RW_HEADS, RW_HD, RW_HD), rw_shift=zeros(bp, RW_PROJ),
        ssd=zeros(bp, SSD_GROUPS, SSD_HPG, SSD_HD, SSD_STATE), ssd_conv=zeros(bp, CONV_W - 1, SSD_CONV_DIM),
        gdn=zeros(bp, GDN_HEADS, GDN_D, GDN_D), gdn_conv=zeros(bp, CONV_W - 1, GDN_CONV_DIM),
        ffn_conv=zeros(depth, bp, FFN_CONV_W - 1, D_FF),
    )
    rp = _run_group(gp, x_prompt.reshape(bp * lp, D_MODEL), mem_k_p.reshape(depth * bp * N_MEM, XA_DIM),
                    mem_v_p.reshape(depth * bp * N_MEM, XA_DIM), [l * bp * N_MEM for l in range(depth)],
                    st_p, wts, prm)

    st_s = dict(rwkv=state_rwkv[0], rw_shift=state_rwkv_shift[0], ssd=state_ssd[0], ssd_conv=state_ssd_conv[0],
                gdn=state_gdn[0], gdn_conv=state_gdn_conv[0], ffn_conv=state_ffn_conv)
    xs = jnp.pad(x_sample, ((0, 0), (pad_rows, 0), (0, 0))).reshape(bs * SUBLANES, D_MODEL)
    rs = _run_group(gs, xs, cache_mem_k.reshape(depth * bs * N_MEM * XA_HEADS, XA_HD),
                    cache_mem_v.reshape(depth * bs * N_MEM * XA_HEADS, XA_HD), [l * bs * N_MEM for l in range(depth)],
                    st_s, wts, prm)

    y_p = rp["y"].reshape(bp, lp, D_MODEL)
    y_s = rs["y"].reshape(bs, SUBLANES, D_MODEL)[:, pad_rows:]
    lead = lambda x: x[None]
    mem_shape = (depth, bp, N_MEM, XA_HEADS, XA_HD)
    return (y_p, y_s, lead(rp["rwkv"]), lead(rs["rwkv"]), lead(rp["rw_shift"]), lead(rs["rw_shift"]),
            lead(rp["ssd"]), lead(rs["ssd"]), lead(rp["ssd_conv"]), lead(rs["ssd_conv"]),
            lead(rp["gdn"]), lead(rs["gdn"]), lead(rp["gdn_conv"]), lead(rs["gdn_conv"]),
            rp["ffn_conv"], rs["ffn_conv"], mem_k_p.reshape(mem_shape), mem_v_p.reshape(mem_shape))
```

```python
import functools
from typing import NamedTuple

import jax
import jax.numpy as jnp
from jax import lax
from jax.experimental import pallas as pl
from jax.experimental.pallas import tpu as pltpu

F32 = jnp.float32
BF16 = jnp.bfloat16
HI = lax.Precision.HIGHEST

D_MODEL = 2048
EPS = 1e-6
RW_HEADS, RW_HD = 16, 64
RW_DIM = RW_HEADS * RW_HD
RW_LORA = 256
RW_PROJ = 3 * RW_DIM + RW_LORA
RW_GN_EPS = 6.4e-4
SSD_HEADS, SSD_HD, SSD_GROUPS, SSD_STATE = 16, 64, 2, 128
SSD_HPG = SSD_HEADS // SSD_GROUPS
SSD_DIM = SSD_HEADS * SSD_HD
SSD_GDIM = SSD_HPG * SSD_HD
SSD_CONV_DIM = SSD_DIM + 2 * SSD_GROUPS * SSD_STATE
SSD_PROJ = SSD_DIM + SSD_CONV_DIM + SSD_HEADS
AB_PROJ = RW_PROJ + SSD_PROJ
GDN_HEADS, GDN_D = 16, 128
GDN_V = GDN_HEADS * GDN_D
GDN_CONV_DIM = 3 * GDN_V
GDN_PROJ = GDN_CONV_DIM + GDN_V + 2 * GDN_HEADS
D_FF = 5632
XA_HEADS, XA_HD, N_MEM = 4, 128, 256
XA_DIM = XA_HEADS * XA_HD
CONV_W = 4
FFN_CONV_W = 3

AB_Z0 = 0
AB_X0 = SSD_DIM
AB_B0 = 2 * SSD_DIM
AB_C0 = AB_B0 + SSD_GROUPS * SSD_STATE
AB_DT0 = AB_C0 + SSD_GROUPS * SSD_STATE
AB_RW0 = 3 * RW_DIM
AB_COLS = 6656
assert AB_DT0 + 128 <= AB_RW0 and AB_RW0 + RW_PROJ <= AB_COLS and AB_RW0 % RW_DIM == 0

SUBLANES = 8
LANES = 128
VMEM_LIMIT = 56 * 1024 * 1024

MIX_ROWS = 64
INV_BLOCK = 16
INV_SPLIT = 128
MIX_NB = 2
MM_TM, MM_TN = 1024, 512
MM_TM_PROJ = 2048
RW_HP = 2
RW_SUB = 4
GDN_G = 4
GDN_SUB = 4
XA_TQ = 512
NORM_TM = 512


class Group(NamedTuple):
    B: int
    L: int
    first: int
    bb: int
    nb: int = 1

    @property
    def outer(self):
        return self.B // self.bb

    @property
    def per_outer(self):
        return self.bb * self.L

    @property
    def qs(self):
        return MIX_ROWS // self.bb

    @property
    def nc(self):
        return self.L // self.qs

    @property
    def embedded(self):
        return self.first > 0


MIXER_FLAGS = None


def _cp(*sem, flags=None):
    return pltpu.CompilerParams(dimension_semantics=sem, vmem_limit_bytes=VMEM_LIMIT, flags=flags)


def _dot(a, b):
    return jnp.dot(a.astype(BF16), b.astype(BF16), preferred_element_type=F32)


def _dot_nt(a, b):
    return lax.dot_general(a.astype(BF16), b.astype(BF16), (((1,), (1,)), ((), ())), preferred_element_type=F32)


def _dot_tn(a, b):
    return lax.dot_general(a.astype(BF16), b.astype(BF16), (((0,), (0,)), ((), ())), preferred_element_type=F32)


def _dot_hi(a, b):
    return jnp.dot(a, b, preferred_element_type=F32, precision=HI)


def _split_bf16(x):
    hi = x.astype(BF16)
    return hi, (x - hi.astype(F32)).astype(BF16)


def _dot_x3(a, b):
    ah, al = _split_bf16(a)
    bh, bl = _split_bf16(b)
    return jnp.dot(jnp.concatenate([ah, ah, al], axis=1), jnp.concatenate([bh, bl, bh], axis=0),
                   preferred_element_type=F32)


def _dot_nt_hi(a, b):
    return lax.dot_general(a, b, (((1,), (1,)), ((), ())), preferred_element_type=F32, precision=HI)


def _sigmoid(x):
    return 1.0 / (1.0 + jnp.exp(-x))


def _silu(x):
    return x * _sigmoid(x)


def _softplus(x):
    return jnp.maximum(x, 0.0) + jnp.log(1.0 + jnp.exp(-jnp.abs(x)))


def _iota2(shape, axis):
    return lax.broadcasted_iota(jnp.int32, shape, axis)


class _Masks(NamedTuple):
    incl: jax.Array
    strict: jax.Array
    eye: jax.Array
    cumsum: jax.Array
    valid: jax.Array


def _masks(grp, heads=1):
    r = MIX_ROWS
    n = heads * r
    ri, ci = _iota2((n, n), 0), _iota2((n, n), 1)
    same = (ri // grp.qs) == (ci // grp.qs)
    incl = same & (ci <= ri)
    strict = same & (ci < ri)
    si, sj = _iota2((2 * r, r), 0), _iota2((2 * r, r), 1)
    same_seq = ((si % r) // grp.qs) == (sj // grp.qs)
    cumsum = (same_seq & ((sj <= si) | (si >= r))).astype(F32)
    t = _iota2((r, 1), 0) % grp.qs
    return _Masks(incl, strict, (ri == ci).astype(F32), cumsum, t >= grp.first)


def _neumann(lm, rhs, nil, eye=None):
    n = lm.shape[0]
    p = None
    if rhs is None:
        t = eye + lm
        if nil > 2:
            p = _dot(lm, lm)
            yield
    elif nil > 2:
        both = _dot(lm, jnp.concatenate([lm, rhs], axis=1))
        yield
        p, t = both[:, :n], rhs + both[:, n:]
    else:
        t = rhs + _dot(lm, rhs)
        yield
    k = 2
    while k < nil:
        if 2 * k < nil:
            both = _dot(p, jnp.concatenate([p, t], axis=1))
            p, t = both[:, :n], t + both[:, n:]
        else:
            t = t + _dot(p, t)
        yield
        k *= 2
    return t


def _inv_unit_lower(lm, eye, qs):
    if qs <= INV_BLOCK:
        return (yield from _neumann(lm, None, qs, eye))
    n = lm.shape[0]
    diag = (_iota2((n, n), 0) // INV_BLOCK) == (_iota2((n, n), 1) // INV_BLOCK)
    d = jnp.where(diag, lm, 0.0)
    dinv = yield from _neumann(d, None, INV_BLOCK, eye)
    nm = _dot(dinv, lm - d)
    yield
    return (yield from _neumann(nm, dinv, qs // INV_BLOCK))


def _lockstep(gens):
    results = [None] * len(gens)
    active = list(range(len(gens)))
    while active:
        for i in list(active):
            try:
                next(gens[i])
            except StopIteration as stop:
                results[i] = stop.value
                active.remove(i)
        if active:
            yield
    return results


def _solve_unit_lower(lm, rhs, eye, qs):
    n = lm.shape[0]
    if n > INV_SPLIT:
        blocks = [slice(i, i + INV_SPLIT) for i in range(0, n, INV_SPLIT)]
        inv = yield from _lockstep([_inv_unit_lower(lm[b, b], eye[b, b], qs) for b in blocks])
        zero = jnp.zeros((INV_SPLIT, INV_SPLIT), F32)
        t = jnp.concatenate([jnp.concatenate([inv[i] if i == j else zero for j in range(len(blocks))], axis=1)
                             for i in range(len(blocks))], axis=0)
    else:
        t = yield from _inv_unit_lower(lm, eye, qs)
    x = _dot(t, rhs)
    yield
    return x


def _interleave(gens):
    results = [None] * len(gens)
    active = list(range(len(gens)))
    while active:
        for i in list(active):
            try:
                next(gens[i])
            except StopIteration as stop:
                results[i] = stop.value
                active.remove(i)
    return results


def _stack_heads(x, heads, width):
    return jnp.concatenate([x[:, h * width:(h + 1) * width] for h in range(heads)], axis=0)


def _unstack_heads(x, heads):
    r = x.shape[0] // heads
    return jnp.concatenate([x[h * r:(h + 1) * r] for h in range(heads)], axis=1)


def _seq_rows(grp, x, j, heads):
    if grp.bb == 1:
        return x
    qs = grp.qs
    return jnp.concatenate([x[h * MIX_ROWS + j * qs:h * MIX_ROWS + (j + 1) * qs] for h in range(heads)], axis=0)


def _from_seq_rows(grp, parts, heads):
    if grp.bb == 1:
        return parts[0]
    qs = grp.qs
    return jnp.concatenate([p[h * qs:(h + 1) * qs] for h in range(heads) for p in parts], axis=0)


def _expand_rows(x, heads, rows_per_head, width):
    m = x.shape[0]
    if x.shape[1] == width:
        x = jnp.concatenate([x] * heads, axis=1)
    keep = ((_iota2((m, heads * width), 0) // rows_per_head) % heads) == (_iota2((m, heads * width), 1) // width)
    return jnp.where(keep, x, 0.0)


def _taps(grp, cur, halo, width):
    if grp.embedded:
        t = _iota2((cur.shape[0], 1), 0) % grp.qs
        full = jnp.where(t < grp.first, halo, cur)
        return [full] + [pltpu.roll(full, s, axis=0) for s in range(1, width)]
    full = jnp.concatenate([halo, cur], axis=0)
    return [cur] + [pltpu.roll(full, s, axis=0)[SUBLANES:] for s in range(1, width)]


def _conv(grp, cur, halo, w_ref, bias, width):
    taps = _taps(grp, cur, halo, width)
    y = taps[0] * w_ref[width - 1:width, :]
    for s in range(1, width):
        y = y + taps[s] * w_ref[width - 1 - s:width - s, :]
    return y if bias is None else y + bias


class _Tok(NamedTuple):
    arr: jax.Array
    width: int
    col: object
    hist: jax.Array = None
    hcol: object = None


def _view3(grp, x2d):
    return x2d.reshape(grp.outer, grp.per_outer, x2d.shape[1])


def _mixer_specs(grp, toks):
    nb = grp.nb
    ops, specs = [], []
    for t in toks:
        ops.append(_view3(grp, t.arr))
        specs.append(pl.BlockSpec((nb, MIX_ROWS, t.width), lambda b, g, c, t=t: (b, c, t.col(g))))
    for t in toks:
        if t.hist is None:
            continue
        if grp.embedded:
            ops.append(_view3(grp, t.hist))
            specs.append(pl.BlockSpec((nb, MIX_ROWS, t.width), lambda b, g, c, t=t: (b, 0, t.hcol(g))))
            continue
        tile = _sublane_tile(t.arr.dtype)
        sub = MIX_ROWS // tile
        ops.append(_view3(grp, t.arr))
        specs.append(pl.BlockSpec(
            (nb, tile, t.width), lambda b, g, c, t=t, sub=sub: (b, jnp.maximum(c * sub - 1, 0), t.col(g))))
        ops.append(t.hist.reshape(grp.B, SUBLANES, t.hist.shape[1]))
        specs.append(pl.BlockSpec((nb, SUBLANES, t.width), lambda b, g, c, t=t: (b, 0, t.hcol(g))))
    return ops, specs


def _read_windows(grp, refs, n_plain, n_hist):
    c = pl.program_id(2)
    nb = grp.nb
    cur = [[r[s].astype(F32) for s in range(nb)] for r in refs[:n_plain]]
    rest = refs[n_plain:]
    halos = []
    for i in range(n_hist):
        if grp.embedded:
            halos.append([rest[i][s] for s in range(nb)])
        else:
            per = []
            for s in range(nb):
                prev = rest[2 * i][s]
                prev = prev[prev.shape[0] - SUBLANES:].astype(F32)
                per.append(jnp.where(c == 0, rest[2 * i + 1][s], prev))
            halos.append(per)
    used = n_hist if grp.embedded else 2 * n_hist
    return cur, halos, rest[used:]


def _sublane_tile(dtype):
    return SUBLANES * 4 // jnp.dtype(dtype).itemsize


def _param_spec(width, col):
    return lambda rows: pl.BlockSpec((rows, width), lambda b, g, c: (0, col(g)))


def _matmul_body(*refs, norm, has_res, n_a, w_rows_are_outputs):
    it = iter(refs)
    a_refs = [next(it) for _ in range(n_a)]
    g_ref = next(it) if norm else None
    w_ref = next(it)
    res_ref = next(it) if has_res else None
    o_ref = next(it)
    if norm:
        an_ref = next(it)

        @pl.when(pl.program_id(1) == 0)
        def _():
            x = a_refs[0][...]
            y = x * lax.rsqrt(jnp.mean(x * x, axis=-1, keepdims=True) + EPS)
            an_ref[...] = (y * g_ref[...]).astype(BF16)

        if w_rows_are_outputs:
            acc = lax.dot_general(an_ref[...], w_ref[...], (((1,), (1,)), ((), ())), preferred_element_type=F32)
        else:
            acc = jnp.dot(an_ref[...], w_ref[...], preferred_element_type=F32)
    else:
        acc, k0 = None, 0
        for a_ref in a_refs:
            k1 = k0 + a_ref.shape[1]
            part = jnp.dot(a_ref[...], w_ref[k0:k1, :], preferred_element_type=F32)
            acc = part if acc is None else acc + part
            k0 = k1
    if has_res:
        acc = acc + res_ref[...]
    o_ref[...] = acc.astype(o_ref.dtype)


def _matmul(a, w, *, layer=None, gain=None, res=None, out_dtype=F32, tm=MM_TM, tn=MM_TN, w_rows_are_outputs=False):
    a_list = list(a) if isinstance(a, (list, tuple)) else [a]
    m = a_list[0].shape[0]
    k, n = w.shape[-2:][::-1] if w_rows_are_outputs else w.shape[-2:]
    assert not w_rows_are_outputs or (gain is not None and w.ndim == 2)
    assert sum(x.shape[1] for x in a_list) == k
    norm = gain is not None
    assert not norm or len(a_list) == 1
    ops = list(a_list)
    specs = [pl.BlockSpec((tm, x.shape[1]), lambda i, j: (i, 0)) for x in a_list]
    if norm:
        ops.append(gain.reshape(1, k))
        specs.append(pl.BlockSpec((1, k), lambda i, j: (0, 0)))
    ops.append(w)
    if w_rows_are_outputs:
        specs.append(pl.BlockSpec((tn, k), lambda i, j: (j, 0)))
    else:
        specs.append(_weight_spec(w, layer, k, tn, lambda j: j))
    if res is not None:
        ops.append(res)
        specs.append(pl.BlockSpec((tm, tn), lambda i, j: (i, j)))
    return pl.pallas_call(
        functools.partial(_matmul_body, norm=norm, has_res=res is not None, n_a=len(a_list),
                          w_rows_are_outputs=w_rows_are_outputs),
        grid=(m // tm, n // tn),
        in_specs=specs,
        out_specs=pl.BlockSpec((tm, tn), lambda i, j: (i, j)),
        out_shape=jax.ShapeDtypeStruct((m, n), out_dtype),
        scratch_shapes=[pltpu.VMEM((tm, k), BF16)] if norm else [],
        compiler_params=_cp("parallel", "arbitrary"),
        name="matmul",
    )(*ops)


def _weight_spec(w, layer, k, tn, col):
    if w.ndim == 2:
        return pl.BlockSpec((k, tn), lambda i, j: (0, col(j)))
    return pl.BlockSpec((None, k, tn), lambda i, j: (layer, 0, col(j)))


def _rmsnorm_body(x_ref, g_ref, o_ref):
    x = x_ref[...]
    y = x * lax.rsqrt(jnp.mean(x * x, axis=-1, keepdims=True) + EPS)
    o_ref[...] = (y * g_ref[...]).astype(o_ref.dtype)


def _rmsnorm(x, gain, out_dtype=F32, tm=NORM_TM):
    m, k = x.shape
    return pl.pallas_call(
        _rmsnorm_body,
        grid=(m // tm,),
        in_specs=[pl.BlockSpec((tm, k), lambda i: (i, 0)), pl.BlockSpec((1, k), lambda i: (0, 0))],
        out_specs=pl.BlockSpec((tm, k), lambda i: (i, 0)),
        out_shape=jax.ShapeDtypeStruct((m, k), out_dtype),
        compiler_params=_cp("parallel"),
        name="rmsnorm",
    )(x, gain.reshape(1, k))


def _rwkv_body(*refs, grp):
    r_rows = MIX_ROWS
    qs, bb, nc, nb = grp.qs, grp.bb, grp.nc, grp.nb
    nh = 2 * RW_HP
    sw = nh * RW_HD
    width = RW_SUB * sw
    n = nh * r_rows
    cur, halos, rest = _read_windows(grp, refs, 4, 4)
    (mu_r, mu_k, mu_v, mu_lo, w0_ref, a0_ref, lora_ref, kk_ref, ka_ref, rk_ref, gnw_ref, gnb_ref,
     s0_ref, y_ref, s_ref, sbig_ref) = rest
    c = pl.program_id(2)

    @pl.when(c == 0)
    def _():
        for j in range(nb * bb):
            for sub in range(RW_SUB):
                st = jnp.concatenate([s0_ref[j, sub * nh + h] for h in range(nh)], axis=0)
                sbig_ref[j, sub] = _expand_rows(st, nh, RW_HD, RW_HD)

    m = _masks(grp, nh)
    bd = ((_iota2((sw, sw), 0) // RW_HD) == (_iota2((sw, sw), 1) // RW_HD)).astype(F32)

    def head_sums(x):
        return _unstack_heads(_dot_x3(_stack_heads(x, RW_SUB, sw), bd), RW_SUB)

    def block(s):
        def shifted(i, mu_ref):
            x = cur[i][s]
            prev = _taps(grp, x, halos[i][s], 2)[1]
            return x + (prev - x) * mu_ref[...]

        r = shifted(0, mu_r)
        k = shifted(1, mu_k)
        v = shifted(2, mu_v)
        lo = shifted(3, mu_lo)
        lora_in = jnp.concatenate([jnp.tanh(lo[:, :64]), lo[:, 64:128], _sigmoid(lo[:, 128:256])], axis=1)
        lora = _dot_x3(lora_in, lora_ref[0])
        w = -_softplus(-(w0_ref[...] + lora[:, :width])) - 0.5
        lw = jnp.where(m.valid, -jnp.exp(w), 0.0)
        a = _sigmoid(a0_ref[...] + lora[:, width:2 * width])
        g = lora[:, 2 * width:]

        kx = k * kk_ref[...]
        kp = jnp.where(m.valid, k * (1.0 + (a - 1.0) * ka_ref[...]), 0.0)
        sums = head_sums(jnp.concatenate([kx * kx, r * kp * rk_ref[...]], axis=0))
        kkn = kx * lax.rsqrt(sums[:r_rows] + 1e-6)
        bonus = sums[r_rows:] * v
        at = jnp.where(m.valid, -kkn, 0.0)
        bt = jnp.where(m.valid, kkn * a, 0.0)

        cums = _dot_hi(m.cumsum, lw)
        cum, cum_end = cums[:r_rows], cums[r_rows:]
        e_neg = jnp.exp(-cum)
        e_end = jnp.exp(cum_end - cum)
        p_end = jnp.exp(cum_end)
        a_til, r_til = at * jnp.exp(cum - lw), r * jnp.exp(cum)
        b_til, k_til = bt * e_neg, kp * e_neg
        b_hat, k_hat = bt * e_end, kp * e_end

        def stack(sub):
            lanes = slice(sub * sw, (sub + 1) * sw)

            def tile(x):
                return _expand_rows(jnp.concatenate([x[:, lanes]] * nh, axis=0), nh, r_rows, RW_HD)

            ar = jnp.concatenate([tile(a_til), tile(r_til)], axis=0)
            bk = jnp.concatenate([tile(b_til), tile(k_til)], axis=0)
            bk_end = jnp.concatenate([tile(b_hat), tile(k_hat)], axis=0)
            v_exp = tile(v)
            gm = _dot_nt(ar, bk)
            yield
            m_ab = jnp.where(m.strict, gm[:n, :n], 0.0)
            m_ak = jnp.where(m.strict, gm[:n, n:], 0.0)
            m_r = jnp.concatenate([jnp.where(m.incl, gm[n:, :n], 0.0), jnp.where(m.incl, gm[n:, n:], 0.0)], axis=1)

            def seq2(x, j):
                if bb == 1:
                    return x
                return jnp.concatenate([_seq_rows(grp, x[:n], j, nh), _seq_rows(grp, x[n:], j, nh)], axis=0)

            parts = []
            for j in range(bb):
                parts.append(_dot_nt(seq2(ar, j), sbig_ref[s * bb + j, sub]))
                yield
            half = nh * qs
            as_a = _from_seq_rows(grp, [p[:half] for p in parts], nh)
            as_r = _from_seq_rows(grp, [p[half:] for p in parts], nh)
            rhs = as_a + _dot(m_ak, v_exp)
            yield
            u = yield from _solve_unit_lower(m_ab, rhs, m.eye, qs)
            uv = jnp.concatenate([u, v_exp], axis=0)
            y_exp = as_r + _dot(m_r, uv)
            yield
            y = y_exp[:r_rows]
            for h in range(1, nh):
                y = y + y_exp[h * r_rows:(h + 1) * r_rows]
            for j in range(bb):
                sbig_ref[s * bb + j, sub] = (sbig_ref[s * bb + j, sub] * p_end[j * qs:j * qs + 1, lanes]
                                             + _dot_tn(seq2(uv, j), seq2(bk_end, j)))
                yield
            return y

        def finish(ys):
            y = jnp.concatenate(ys, axis=1)
            mean = head_sums(y) * (1.0 / RW_HD)
            d = y - mean
            var = head_sums(d * d) * (1.0 / RW_HD)
            out = d * lax.rsqrt(var + RW_GN_EPS) * gnw_ref[...] + gnb_ref[...] + bonus
            y_ref[s] = (out * g).astype(y_ref.dtype)

        return [stack(sub) for sub in range(RW_SUB)], finish

    blocks = [block(s) for s in range(nb)]
    results = _interleave([chain for chains, _ in blocks for chain in chains])
    for s, (_, finish) in enumerate(blocks):
        finish(results[s * RW_SUB:(s + 1) * RW_SUB])

    @pl.when(c == nc - 1)
    def _():
        for j in range(nb * bb):
            for sub in range(RW_SUB):
                sb = sbig_ref[j, sub]
                for h in range(nh):
                    s_ref[j, sub * nh + h] = sb[h * RW_HD:(h + 1) * RW_HD, h * RW_HD:(h + 1) * RW_HD]


def _rwkv(grp, proj, hist, s0, prm):
    width = RW_SUB * RW_HP * LANES
    ng = RW_DIM // width
    lora_blk = 3 * RW_DIM // RW_LORA
    p0 = AB_RW0 // width
    toks = [
        _Tok(proj, width, lambda g: p0 + g, hist, lambda g: g),
        _Tok(proj, width, lambda g: p0 + ng + g, hist, lambda g: ng + g),
        _Tok(proj, width, lambda g: p0 + 2 * ng + g, hist, lambda g: 2 * ng + g),
        _Tok(proj, RW_LORA, lambda g: AB_RW0 // RW_LORA + lora_blk, hist, lambda g: lora_blk),
    ]
    ops, specs = _mixer_specs(grp, toks)
    head = _param_spec(width, lambda g: g)
    lora_w = jnp.zeros((ng, RW_LORA, 3 * width), F32)
    for i, (name, r0, r1) in enumerate((("w_up", 0, 64), ("a_up", 64, 128), ("g_up", 128, 256))):
        blk = prm[name].reshape(r1 - r0, ng, width).transpose(1, 0, 2)
        lora_w = lora_w.at[:, r0:r1, i * width:(i + 1) * width].set(blk)
    mu = prm["mu"].reshape(1, RW_PROJ)
    flat = lambda x: x.reshape(1, RW_DIM)
    params = [
        (mu, pl.BlockSpec((1, width), lambda b, g, c: (0, g))),
        (mu, pl.BlockSpec((1, width), lambda b, g, c: (0, ng + g))),
        (mu, pl.BlockSpec((1, width), lambda b, g, c: (0, 2 * ng + g))),
        (mu, pl.BlockSpec((1, RW_LORA), lambda b, g, c: (0, lora_blk))),
        (flat(prm["w0"]), head(1)), (flat(prm["a0"]), head(1)),
        (lora_w, pl.BlockSpec((1, RW_LORA, 3 * width), lambda b, g, c: (g, 0, 0))),
        (flat(prm["k_k"]), head(1)), (flat(prm["k_a"]), head(1)), (flat(prm["r_k"]), head(1)),
        (flat(prm["gn_w"]), head(1)), (flat(prm["gn_b"]), head(1)),
    ]
    ops += [p for p, _ in params]
    specs += [s for _, s in params]
    seqs = grp.nb * grp.bb
    st_spec = pl.BlockSpec((seqs, RW_SUB * 2 * RW_HP, RW_HD, RW_HD), lambda b, g, c: (b, g, 0, 0))
    ops.append(s0)
    specs.append(st_spec)
    y, s_out = pl.pallas_call(
        functools.partial(_rwkv_body, grp=grp),
        grid=(grp.outer // grp.nb, ng, grp.nc),
        in_specs=specs,
        out_specs=[pl.BlockSpec((grp.nb, MIX_ROWS, width), lambda b, g, c: (b, c, g)), st_spec],
        out_shape=[jax.ShapeDtypeStruct((grp.outer, grp.per_outer, RW_DIM), BF16),
                   jax.ShapeDtypeStruct(s0.shape, F32)],
        scratch_shapes=[pltpu.VMEM((seqs, RW_SUB, RW_HP * LANES, RW_HP * LANES), F32)],
        compiler_params=_cp("parallel", "parallel", "arbitrary", flags=MIXER_FLAGS),
        name="rwkv7",
    )(*ops)
    return y.reshape(grp.B * grp.L, RW_DIM), s_out


def _ssd_body(*refs, grp):
    r_rows = MIX_ROWS
    qs, bb, nb = grp.qs, grp.bb, grp.nb
    nh, hd, gdim = SSD_HPG, SSD_HD, SSD_GDIM
    cur, halos, rest = _read_windows(grp, refs, 5, 3)
    cw_x, cw_b, cw_c, cb_x, cb_b, cb_c, dtb_ref, alog_ref, dskip_ref, nw_ref, h0_ref, y_ref, h_ref = rest
    c = pl.program_id(2)

    @pl.when(c == 0)
    def _():
        h_ref[...] = h0_ref[...]

    m = _masks(grp)
    eye_h = (_iota2((SSD_HEADS, SSD_HEADS), 0) == _iota2((SSD_HEADS, SSD_HEADS), 1)).astype(F32)
    ei, ej = _iota2((3 * SSD_HEADS, 3 * SSD_DIM), 0), _iota2((3 * SSD_HEADS, 3 * SSD_DIM), 1)
    spread = ((ei // SSD_HEADS == ej // SSD_DIM) & (ei % SSD_HEADS == (ej % SSD_DIM) // hd)).astype(F32)

    def block(s):
        x_raw, b_raw, c_raw, z, dt_blk = (cur[i][s] for i in range(5))
        xs = _silu(_conv(grp, x_raw, halos[0][s], cw_x, cb_x[...], CONV_W))
        bm = _silu(_conv(grp, b_raw, halos[1][s], cw_b, cb_b[...], CONV_W))
        cm = _silu(_conv(grp, c_raw, halos[2][s], cw_c, cb_c[...], CONV_W))

        dt = jnp.where(m.valid, _softplus(dt_blk[:, :SSD_HEADS] + dtb_ref[...]), 0.0)
        cums = _dot_hi(m.cumsum, dt * (-jnp.exp(alog_ref[...])))
        acs, acs_end = cums[:r_rows], cums[r_rows:]
        acs_row = _dot_nt_hi(eye_h, acs)
        lanes = _dot_hi(jnp.concatenate([dt, acs_end - acs, acs], axis=1), spread)
        xd = xs * lanes[:, :SSD_DIM]
        xd_dec = xd * jnp.exp(lanes[:, SSD_DIM:2 * SSD_DIM])
        acs_lanes = lanes[:, 2 * SSD_DIM:]

        def group(g):
            gl = slice(g * gdim, (g + 1) * gdim)
            bmg, cmg = bm[:, g * SSD_STATE:(g + 1) * SSD_STATE], cm[:, g * SSD_STATE:(g + 1) * SSD_STATE]
            cb = _dot_nt(cmg, bmg)
            yield
            yd = []
            for r in range(nh):
                h = g * nh + r
                diff = acs[:, h:h + 1] - acs_row[h:h + 1, :]
                lmat = jnp.where(m.incl, jnp.exp(jnp.where(m.incl, diff, 0.0)), 0.0)
                yd.append(_dot(cb * lmat, xd[:, h * hd:(h + 1) * hd]))
                yield
            y = jnp.concatenate(yd, axis=1)
            head_rows = (_iota2((gdim, SSD_HEADS), 0) // hd + g * nh == _iota2((gdim, SSD_HEADS), 1)).astype(F32)
            yoff = []
            for j in range(bb):
                rows = slice(j * qs, (j + 1) * qs)
                hj = h_ref[s * bb + j, g]
                yoff.append(_dot_nt(cmg[rows], hj))
                yield
                end_col = jnp.exp(_dot_nt_hi(head_rows, acs_end[j * qs:j * qs + SUBLANES]))[:, :1]
                yield
                h_ref[s * bb + j, g] = hj * end_col + _dot_tn(xd_dec[rows, gl], bmg[rows])
                yield
            yoff = yoff[0] if bb == 1 else jnp.concatenate(yoff, axis=0)
            y = y + yoff * jnp.exp(acs_lanes[:, gl]) + xs[:, gl] * dskip_ref[:, gl]
            yg = y * _silu(z[:, gl])
            yg = yg * lax.rsqrt(jnp.mean(yg * yg, axis=-1, keepdims=True) + EPS)
            return yg * nw_ref[:, gl]

        return [group(g) for g in range(SSD_GROUPS)]

    results = _interleave([chain for s in range(nb) for chain in block(s)])
    for s in range(nb):
        y_ref[s] = jnp.concatenate(results[s * SSD_GROUPS:(s + 1) * SSD_GROUPS], axis=1).astype(y_ref.dtype)


def _ssd(grp, proj, hist, h0, prm):
    bc = SSD_GROUPS * SSD_STATE
    toks = [
        _Tok(proj, SSD_DIM, lambda g: AB_X0 // SSD_DIM, hist, lambda g: 0),
        _Tok(proj, bc, lambda g: AB_B0 // bc, hist, lambda g: SSD_DIM // bc),
        _Tok(proj, bc, lambda g: AB_C0 // bc, hist, lambda g: SSD_DIM // bc + 1),
        _Tok(proj, SSD_DIM, lambda g: AB_Z0 // SSD_DIM),
        _Tok(proj, LANES, lambda g: AB_DT0 // LANES),
    ]
    ops, specs = _mixer_specs(grp, toks)
    cw, cbias = prm["conv_w"], prm["conv_b"].reshape(1, SSD_CONV_DIM)
    conv_cols = [(SSD_DIM, lambda g: 0), (bc, lambda g: SSD_DIM // bc), (bc, lambda g: SSD_DIM // bc + 1)]
    params = [(cw, _param_spec(wd, col)(CONV_W)) for wd, col in conv_cols]
    params += [(cbias, _param_spec(wd, col)(1)) for wd, col in conv_cols]
    full = lambda n: pl.BlockSpec((1, n), lambda b, g, c: (0, 0))
    params += [
        (prm["dt_bias"].reshape(1, SSD_HEADS), full(SSD_HEADS)),
        (prm["A_log"].reshape(1, SSD_HEADS), full(SSD_HEADS)),
        (jnp.repeat(prm["D"], SSD_HD).reshape(1, SSD_DIM), full(SSD_DIM)),
        (prm["norm_w"].reshape(1, SSD_DIM), full(SSD_DIM)),
    ]
    ops += [p for p, _ in params]
    specs += [s for _, s in params]
    st_spec = pl.BlockSpec((grp.nb * grp.bb, SSD_GROUPS, SSD_GDIM, SSD_STATE), lambda b, g, c: (b, 0, 0, 0))
    ops.append(h0)
    specs.append(st_spec)
    y, h_out = pl.pallas_call(
        functools.partial(_ssd_body, grp=grp),
        grid=(grp.outer // grp.nb, 1, grp.nc),
        in_specs=specs,
        out_specs=[pl.BlockSpec((grp.nb, MIX_ROWS, SSD_DIM), lambda b, g, c: (b, c, 0)), st_spec],
        out_shape=[jax.ShapeDtypeStruct((grp.outer, grp.per_outer, SSD_DIM), BF16),
                   jax.ShapeDtypeStruct(h0.shape, F32)],
        compiler_params=_cp("parallel", "arbitrary", "arbitrary"),
        name="ssd",
    )(*ops)
    return y.reshape(grp.B * grp.L, SSD_DIM), h_out


def _gdn_body(*refs, grp):
    r_rows = MIX_ROWS
    qs, bb, nb = grp.qs, grp.bb, grp.nb
    nh = GDN_G
    n = nh * r_rows
    cur, halos, rest = _read_windows(grp, refs, 5, 3)
    cw_q, cw_k, cw_v, alog_ref, dtb_ref, nw_ref, s0_ref, o_ref, s_ref = rest
    c = pl.program_id(2)

    @pl.when(c == 0)
    def _():
        s_ref[...] = s0_ref[...]

    m = _masks(grp, nh)
    per_step = nh * GDN_SUB
    eye_g = (_iota2((SUBLANES, nh), 0) == _iota2((SUBLANES, nh), 1)).astype(F32)
    col = lambda x: jnp.concatenate([x[:, h:h + 1] for h in range(nh)], axis=0)

    def block(s):
        q_raw, k_raw, v_raw, z, ba = (cur[i][s] for i in range(5))
        q_all = _silu(_conv(grp, q_raw, halos[0][s], cw_q, None, CONV_W))
        k_all = _silu(_conv(grp, k_raw, halos[1][s], cw_k, None, CONV_W))
        v_all = _silu(_conv(grp, v_raw, halos[2][s], cw_v, None, CONV_W))

        def stack(sub):
            lanes = slice(sub * nh * GDN_D, (sub + 1) * nh * GDN_D)
            heads = slice(sub * nh, (sub + 1) * nh)
            q = _stack_heads(q_all[:, lanes], nh, GDN_D)
            k = _stack_heads(k_all[:, lanes], nh, GDN_D)
            v = _stack_heads(v_all[:, lanes], nh, GDN_D)
            q = q * lax.rsqrt(jnp.sum(q * q, axis=-1, keepdims=True) + 1e-6) * (GDN_D ** -0.5)
            k = k * lax.rsqrt(jnp.sum(k * k, axis=-1, keepdims=True) + 1e-6)

            beta = jnp.where(m.valid, _sigmoid(ba[:, heads]), 0.0)
            a_raw = ba[:, per_step + sub * nh:per_step + (sub + 1) * nh]
            gg = jnp.where(m.valid, -jnp.exp(alog_ref[0][:, heads]) * _softplus(a_raw + dtb_ref[0][:, heads]), 0.0)
            gcs = _dot_hi(m.cumsum, gg)
            yield
            gc, gc_end = gcs[:r_rows], gcs[r_rows:]
            gc_row = _dot_nt_hi(eye_g, gc)
            yield

            bcol, gcol, gend = col(beta), col(gc), col(gc_end)
            grow = jnp.concatenate([gc_row[h:h + 1, :] for h in range(nh)], axis=1)
            dec = jnp.where(m.incl, jnp.exp(jnp.where(m.incl, gcol - grow, 0.0)), 0.0)
            kb = k * bcol
            kq = _dot_nt(jnp.concatenate([kb, q], axis=0), k)
            yield
            lm = -jnp.where(m.strict, kq[:n] * dec, 0.0)
            attn = jnp.where(m.incl, kq[n:] * dec, 0.0)
            eg = jnp.exp(gcol)
            sol = yield from _solve_unit_lower(lm, jnp.concatenate([v * bcol, kb * eg], axis=1), m.eye, qs)
            vw, kcd = sol[:, :GDN_D], sol[:, GDN_D:]
            qg = q * eg
            kg = k * jnp.exp(gend - gcol)

            def state(j):
                return jnp.concatenate([s_ref[s * bb + j, sub * nh + h] for h in range(nh)], axis=0)

            parts = []
            for j in range(bb):
                lhs = jnp.concatenate([_seq_rows(grp, kcd, j, nh), _seq_rows(grp, qg, j, nh)], axis=0)
                parts.append(_dot(_expand_rows(lhs, nh, qs, GDN_D), state(j)))
                yield
            half = nh * qs
            ks = _from_seq_rows(grp, [p[:half] for p in parts], nh)
            qsv = _from_seq_rows(grp, [p[half:] for p in parts], nh)
            v_new = vw - ks
            o = qsv + _dot(attn, v_new)
            yield
            for j in range(bb):
                kgj = _expand_rows(_seq_rows(grp, kg, j, nh), nh, qs, GDN_D)
                g_last = jnp.exp(gc_end[j * qs:j * qs + 1, :])
                g_last = jnp.concatenate([jnp.broadcast_to(g_last[:, h:h + 1], (GDN_D, 1)) for h in range(nh)], axis=0)
                s_new = state(j) * g_last + _dot_tn(kgj, _seq_rows(grp, v_new, j, nh))
                yield
                for h in range(nh):
                    s_ref[s * bb + j, sub * nh + h] = s_new[h * GDN_D:(h + 1) * GDN_D]
            o = o * lax.rsqrt(jnp.mean(o * o, axis=-1, keepdims=True) + EPS) * nw_ref[...]
            return _unstack_heads(o, nh)

        def finish(os_):
            o_ref[s] = (jnp.concatenate(os_, axis=1) * _silu(z)).astype(o_ref.dtype)

        return [stack(sub) for sub in range(GDN_SUB)], finish

    blocks = [block(s) for s in range(nb)]
    results = _interleave([chain for chains, _ in blocks for chain in chains])
    for s, (_, finish) in enumerate(blocks):
        finish(results[s * GDN_SUB:(s + 1) * GDN_SUB])


def _zeros_like_axis(x, n, axis):
    shape = list(x.shape)
    shape[axis] = n
    return jnp.zeros(shape, x.dtype)


def _ab_layout(x, axis=-1):
    cut = lambda a, b: lax.slice_in_dim(x, a, b, axis=axis)
    zeros = lambda n: _zeros_like_axis(x, n, axis)
    rw, z = cut(0, RW_PROJ), cut(RW_PROJ, RW_PROJ + SSD_DIM)
    xbc = cut(RW_PROJ + SSD_DIM, RW_PROJ + SSD_DIM + SSD_CONV_DIM)
    dt = cut(RW_PROJ + SSD_DIM + SSD_CONV_DIM, AB_PROJ)
    return jnp.concatenate([z, xbc, dt, zeros(AB_RW0 - AB_DT0 - SSD_HEADS), rw, zeros(AB_COLS - AB_RW0 - RW_PROJ)],
                           axis=axis)


def _gdn_layout(x, axis=-1):
    per_step = GDN_G * GDN_SUB
    main = GDN_CONV_DIM + GDN_V
    cut = lambda a, b: lax.slice_in_dim(x, a, b, axis=axis)
    parts = [cut(0, main)]
    for g in range(GDN_HEADS // per_step):
        h0, h1 = g * per_step, (g + 1) * per_step
        parts += [cut(main + h0, main + h1), cut(main + GDN_HEADS + h0, main + GDN_HEADS + h1),
                  _zeros_like_axis(x, LANES - 2 * per_step, axis)]
    total = main + (GDN_HEADS // per_step) * LANES
    parts.append(_zeros_like_axis(x, -total % MM_TN, axis))
    return jnp.concatenate(parts, axis=axis)


def _gdn(grp, proj, hist, s0, prm):
    per_step = GDN_G * GDN_SUB
    width = per_step * GDN_D
    ng = GDN_HEADS // per_step
    ba_blk = (GDN_CONV_DIM + GDN_V) // LANES
    toks = [
        _Tok(proj, width, lambda g: g, hist, lambda g: g),
        _Tok(proj, width, lambda g: ng + g, hist, lambda g: ng + g),
        _Tok(proj, width, lambda g: 2 * ng + g, hist, lambda g: 2 * ng + g),
        _Tok(proj, width, lambda g: 3 * ng + g),
        _Tok(proj, LANES, lambda g: ba_blk + g),
    ]
    ops, specs = _mixer_specs(grp, toks)
    cw = prm["conv_w"]
    grouped = lambda x: x.reshape(ng, 1, per_step)
    params = [
        (cw, _param_spec(width, lambda g: g)(CONV_W)),
        (cw, _param_spec(width, lambda g: ng + g)(CONV_W)),
        (cw, _param_spec(width, lambda g: 2 * ng + g)(CONV_W)),
        (grouped(prm["A_log"]), pl.BlockSpec((1, 1, per_step), lambda b, g, c: (g, 0, 0))),
        (grouped(prm["dt_bias"]), pl.BlockSpec((1, 1, per_step), lambda b, g, c: (g, 0, 0))),
        (prm["norm_w"].reshape(1, GDN_D), pl.BlockSpec((1, GDN_D), lambda b, g, c: (0, 0))),
    ]
    ops += [p for p, _ in params]
    specs += [s for _, s in params]
    st_spec = pl.BlockSpec((grp.nb * grp.bb, per_step, GDN_D, GDN_D), lambda b, g, c: (b, g, 0, 0))
    ops.append(s0)
    specs.append(st_spec)
    o, s_out = pl.pallas_call(
        functools.partial(_gdn_body, grp=grp),
        grid=(grp.outer // grp.nb, ng, grp.nc),
        in_specs=specs,
        out_specs=[pl.BlockSpec((grp.nb, MIX_ROWS, width), lambda b, g, c: (b, c, g)), st_spec],
        out_shape=[jax.ShapeDtypeStruct((grp.outer, grp.per_outer, GDN_V), BF16),
                   jax.ShapeDtypeStruct(s0.shape, F32)],
        compiler_params=_cp("parallel", "parallel", "arbitrary", flags=MIXER_FLAGS),
        name="gdn",
    )(*ops)
    return o.reshape(grp.B * grp.L, GDN_V), s_out


def _attend(q, k_ref, v_ref, nseq, lq):
    scale = XA_HD ** -0.5
    outs = []
    for j in range(nseq):
        qrows = slice(j * lq, (j + 1) * lq)
        mrows = slice(j * N_MEM, (j + 1) * N_MEM)
        heads = []
        for h in range(XA_HEADS):
            sl = slice(h * XA_HD, (h + 1) * XA_HD)
            s = _dot_nt(q[qrows, sl], k_ref[mrows, sl]) * scale
            p = jnp.exp(s - jnp.max(s, axis=-1, keepdims=True))
            heads.append(_dot(p, v_ref[mrows, sl]) / jnp.sum(p, axis=-1, keepdims=True))
        outs.append(jnp.concatenate(heads, axis=1))
    return outs[0] if nseq == 1 else jnp.concatenate(outs, axis=0)


def _attend_packed(q, k_ref, v_ref, nseq, lq):
    scale = XA_HD ** -0.5
    rows, cols = XA_HEADS * lq, N_MEM * XA_HEADS
    own_head = (_iota2((rows, cols), 0) // lq) == (_iota2((rows, cols), 1) % XA_HEADS)
    outs = []
    for j in range(nseq):
        qh = _stack_heads(q[j * lq:(j + 1) * lq], XA_HEADS, XA_HD)
        mrows = slice(j * cols, (j + 1) * cols)
        s = jnp.where(own_head, _dot_nt(qh, k_ref[mrows, :]) * scale, -1e30)
        p = jnp.where(own_head, jnp.exp(s - jnp.max(s, axis=-1, keepdims=True)), 0.0)
        o = _dot(p, v_ref[mrows, :]) / jnp.sum(p, axis=-1, keepdims=True)
        outs.append(_unstack_heads(o, XA_HEADS))
    return jnp.concatenate(outs, axis=0)


def _xattn_body(*refs, attend, nseq, lq, n_mix):
    mix_refs, wmix_ref = refs[:n_mix], refs[n_mix]
    h_ref, g_ref, wq_ref, k_ref, v_ref, wo_ref, o_ref = refs[n_mix + 1:]
    x, k0 = h_ref[...], 0
    for y_ref in mix_refs:
        k1 = k0 + y_ref.shape[1]
        x = x + jnp.dot(y_ref[...], wmix_ref[k0:k1, :], preferred_element_type=F32)
        k0 = k1
    xn = x * lax.rsqrt(jnp.mean(x * x, axis=-1, keepdims=True) + EPS) * g_ref[...]
    q = jnp.dot(xn.astype(BF16), wq_ref[...], preferred_element_type=F32)
    o = attend(q, k_ref, v_ref, nseq, lq)
    o_ref[...] = x + jnp.dot(o.astype(BF16), wo_ref[...], preferred_element_type=F32)


def _xattn(grp, h, mix, w_mix, gain, w_q, w_o, layer, mem_k, mem_v, mem_row0):
    rows, d = h.shape
    if grp.L >= XA_TQ:
        nseq, lq, attend = 1, XA_TQ, _attend
        per = grp.L // XA_TQ
        mem_spec = pl.BlockSpec((N_MEM, XA_DIM), lambda i: (mem_row0 // N_MEM + i // per, 0))
    else:
        nseq, lq, attend = MIX_ROWS // grp.L, grp.L, _attend_packed
        blk = nseq * N_MEM * XA_HEADS
        mem_spec = pl.BlockSpec((blk, XA_HD), lambda i: (mem_row0 * XA_HEADS // blk + i, 0))
    tile = nseq * lq
    h_spec = pl.BlockSpec((tile, d), lambda i: (i, 0))
    mix_specs = [pl.BlockSpec((tile, y.shape[1]), lambda i: (i, 0)) for y in mix]
    return pl.pallas_call(
        functools.partial(_xattn_body, attend=attend, nseq=nseq, lq=lq, n_mix=len(mix)),
        grid=(rows // tile,),
        in_specs=mix_specs + [pl.BlockSpec(w_mix.shape, lambda i: (0, 0)), h_spec,
                              pl.BlockSpec((1, d), lambda i: (0, 0)),
                              pl.BlockSpec((None, d, XA_DIM), lambda i: (layer, 0, 0)), mem_spec, mem_spec,
                              pl.BlockSpec((None, XA_DIM, d), lambda i: (layer, 0, 0))],
        out_specs=h_spec,
        out_shape=jax.ShapeDtypeStruct((rows, d), F32),
        compiler_params=_cp("parallel"),
        name="xattn",
    )(*mix, w_mix, h, gain.reshape(1, d), w_q, mem_k, mem_v, w_o)


def _ffn_in_body(*refs, grp, tiles_per_seq):
    it = iter(refs)
    a_ref, g_ref, wg_ref, wu_ref, hist_ref, cw_ref, cb_ref, act_ref, tail_ref, an_ref = (next(it) for _ in range(10))
    carry_ref = None if grp.embedded else next(it)
    i, j = pl.program_id(0), pl.program_id(1)

    @pl.when(j == 0)
    def _():
        x = a_ref[...]
        y = x * lax.rsqrt(jnp.mean(x * x, axis=-1, keepdims=True) + EPS)
        an_ref[...] = (y * g_ref[...]).astype(BF16)

    a = an_ref[...]
    gate = jnp.dot(a, wg_ref[...].astype(BF16), preferred_element_type=F32)
    up = jnp.dot(a, wu_ref[...].astype(BF16), preferred_element_type=F32)
    if grp.embedded:
        halo = hist_ref[...]
        tail_ref[...] = gate
    else:
        halo = jnp.where(i % tiles_per_seq == 0, hist_ref[...], carry_ref[j])
        last = gate[gate.shape[0] - SUBLANES:]
        carry_ref[j] = last
        tail_ref[...] = last
    conv = _conv(grp, gate, halo, cw_ref, cb_ref[...], FFN_CONV_W)
    act_ref[...] = (_silu(conv) * up).astype(act_ref.dtype)


def _ffn_in(grp, h, gain, w_in, layer, hist, conv_w, conv_b, tm=MM_TM, tn=MM_TN):
    assert grp.embedded or grp.L % tm == 0
    rows, k = h.shape
    nct = D_FF // tn
    tiles_per_seq = max(grp.L // tm, 1)
    hrows = tm if grp.embedded else SUBLANES
    tile_idx = lambda i, j: (i, j)
    hist_idx = tile_idx if grp.embedded else (lambda i, j: (i // tiles_per_seq, j))
    specs = [
        pl.BlockSpec((tm, k), lambda i, j: (i, 0)),
        pl.BlockSpec((1, k), lambda i, j: (0, 0)),
        _weight_spec(w_in, layer, k, tn, lambda j: j),
        _weight_spec(w_in, layer, k, tn, lambda j: nct + j),
        pl.BlockSpec((hrows, tn), hist_idx),
        pl.BlockSpec((FFN_CONV_W, tn), lambda i, j: (0, j)),
        pl.BlockSpec((1, tn), lambda i, j: (0, j)),
    ]
    scratch = [pltpu.VMEM((tm, k), BF16)]
    if not grp.embedded:
        scratch.append(pltpu.VMEM((nct, SUBLANES, tn), F32))
    act, tails = pl.pallas_call(
        functools.partial(_ffn_in_body, grp=grp, tiles_per_seq=tiles_per_seq),
        grid=(rows // tm, nct),
        in_specs=specs,
        out_specs=[pl.BlockSpec((tm, tn), tile_idx), pl.BlockSpec((hrows, tn), tile_idx)],
        out_shape=[jax.ShapeDtypeStruct((rows, D_FF), BF16), jax.ShapeDtypeStruct((rows // tm * hrows, D_FF), F32)],
        scratch_shapes=scratch,
        compiler_params=_cp("arbitrary", "arbitrary"),
        name="ffn_in",
    )(h, gain.reshape(1, k), w_in, w_in, hist, conv_w, conv_b.reshape(1, D_FF))
    if not grp.embedded:
        tails = tails.reshape(grp.B, tiles_per_seq, SUBLANES, D_FF)[:, -1].reshape(grp.B * SUBLANES, D_FF)
    return act, tails


def _history(grp, buf, width):
    b, k, c = buf.shape
    if grp.embedded:
        h = jnp.pad(buf, ((0, 0), (grp.first - k, grp.L - grp.first), (0, 0)))
        return h.reshape(b * grp.L, c)
    return jnp.pad(buf, ((0, 0), (SUBLANES - k, 0), (0, 0))).reshape(b * SUBLANES, c)


def _tail(grp, x2d, k, cols):
    x = x2d.reshape(grp.B, grp.L, x2d.shape[1])
    return x[:, grp.L - k:, cols].astype(F32)


def _run_group(grp, x2d, mem_k, mem_v, mem_row0, st, wts, prm):
    h = x2d
    out = {}
    tm_proj = min(MM_TM_PROJ, h.shape[0])
    proj = _matmul(h, wts["in_ab"], gain=prm["norm_mix"][0], out_dtype=BF16, tm=tm_proj, w_rows_are_outputs=True)
    y_rw, s_rw = _rwkv(grp, proj, _history(grp, st["rw_shift"][:, None, :], RW_PROJ), st["rwkv"], prm["rw"])
    y_ssd, s_ssd = _ssd(grp, proj, _history(grp, st["ssd_conv"], SSD_CONV_DIM),
                        st["ssd"].reshape(grp.B, SSD_GROUPS, SSD_GDIM, SSD_STATE), prm["ssd"])
    out["rwkv"] = s_rw
    out["rw_shift"] = _tail(grp, proj, 1, slice(AB_RW0, AB_RW0 + RW_PROJ))[:, 0]
    out["ssd"] = s_ssd.reshape(st["ssd"].shape)
    out["ssd_conv"] = _tail(grp, proj, CONV_W - 1, slice(AB_X0, AB_X0 + SSD_CONV_DIM))
    mix, w_mix = [y_rw, y_ssd], wts["out_ab"]
    ffn_bufs = []
    for l in range(2):
        if l == 1:
            proj = _matmul(h, wts["in_c"], gain=prm["norm_mix"][1], out_dtype=BF16, tm=tm_proj, w_rows_are_outputs=True)
            y_c, s_gdn = _gdn(grp, proj, _history(grp, st["gdn_conv"], GDN_CONV_DIM), st["gdn"], prm["gdn"])
            out["gdn"] = s_gdn
            out["gdn_conv"] = _tail(grp, proj, CONV_W - 1, slice(0, GDN_CONV_DIM))
            mix, w_mix = [y_c], wts["out_c"]
        h = _xattn(grp, h, mix, w_mix, prm["norm_xa"][l], wts["xq"], wts["xo"], l, mem_k, mem_v, mem_row0[l])
        act, gate_tail = _ffn_in(grp, h, prm["norm_ffn"][l], wts["ffn_in"], l, _history(grp, st["ffn_conv"][l], D_FF),
                                 prm["ffn_conv_w"][l], prm["ffn_conv_b"][l])
        ffn_bufs.append(gate_tail.reshape(grp.B, SUBLANES, D_FF)[:, SUBLANES - (FFN_CONV_W - 1):])
        h = _matmul(act, wts["ffn_out"], layer=l, res=h)
    out["ffn_conv"] = jnp.stack(ffn_bufs)
    out["y"] = _rmsnorm(h, prm["norm_final"])
    return out


def kernel(x_prompt, x_sample, mem_prompt, state_rwkv, state_rwkv_shift, state_ssd, state_ssd_conv, state_gdn,
           state_gdn_conv, state_ffn_conv, cache_mem_k, cache_mem_v, norm_mix, norm_xa, norm_mem, norm_ffn,
           norm_final, w_in_ab, rw_mu, rw_w0, rw_w_up, rw_a0, rw_a_up, rw_g_up, rw_k_k, rw_k_a, rw_r_k, rw_gn_w,
           rw_gn_b, ssd_conv_w, ssd_conv_b, ssd_dt_bias, ssd_A_log, ssd_D, ssd_norm_w, w_out_ab, w_in_c,
           gdn_conv_w, gdn_A_log, gdn_dt_bias, gdn_norm_w, w_out_c, w_xq, w_xk, w_xv, w_xo, ffn_w_in, ffn_conv_w,
           ffn_conv_b, ffn_w_out):
    bp, lp, _ = x_prompt.shape
    bs, ls, _ = x_sample.shape
    depth = norm_mix.shape[0]
    assert depth == 2 and w_in_ab.shape[0] == 1 and w_in_c.shape[0] == 1
    assert lp % MIX_ROWS == 0 and lp % XA_TQ == 0 and lp % MM_TM == 0
    pad_rows = SUBLANES - ls
    assert CONV_W - 1 <= pad_rows and CONV_W - 1 <= ls and MIX_ROWS % SUBLANES == 0
    gp = Group(B=bp, L=lp, first=0, bb=1, nb=MIX_NB if bp % MIX_NB == 0 else 1)
    gs = Group(B=bs, L=SUBLANES, first=pad_rows, bb=MIX_ROWS // SUBLANES)
    assert bs % gs.bb == 0 and (bs * SUBLANES) % MM_TM == 0 and (bp * lp) % MM_TM == 0

    wts = dict(
        in_ab=_ab_layout(jnp.swapaxes(w_in_ab[0], 0, 1), axis=0).astype(BF16), out_ab=w_out_ab[0].astype(BF16),
        in_c=_gdn_layout(jnp.swapaxes(w_in_c[0], 0, 1), axis=0).astype(BF16), out_c=w_out_c[0].astype(BF16),
        xq=w_xq.astype(BF16), xo=w_xo.astype(BF16), ffn_in=ffn_w_in, ffn_out=ffn_w_out.astype(BF16),
    )
    prm = dict(
        norm_mix=norm_mix, norm_xa=norm_xa, norm_ffn=norm_ffn, norm_final=norm_final,
        ffn_conv_w=ffn_conv_w, ffn_conv_b=ffn_conv_b,
        rw=dict(mu=rw_mu[0], w0=rw_w0[0], w_up=rw_w_up[0], a0=rw_a0[0], a_up=rw_a_up[0], g_up=rw_g_up[0],
                k_k=rw_k_k[0], k_a=rw_k_a[0], r_k=rw_r_k[0], gn_w=rw_gn_w[0], gn_b=rw_gn_b[0]),
        ssd=dict(conv_w=ssd_conv_w[0], conv_b=ssd_conv_b[0], dt_bias=ssd_dt_bias[0], A_log=ssd_A_log[0],
                 D=ssd_D[0], norm_w=ssd_norm_w[0]),
        gdn=dict(conv_w=gdn_conv_w[0], A_log=gdn_A_log[0], dt_bias=gdn_dt_bias[0], norm_w=gdn_norm_w[0]),
    )

    mem2d = mem_prompt.reshape(bp * N_MEM, D_MODEL)
    mem_tm = min(MM_TM, bp * N_MEM)
    mk, mv = [], []
    for l in range(depth):
        mk.append(_matmul(mem2d, w_xk[l].astype(BF16), gain=norm_mem[l], tm=mem_tm))
        mv.append(_matmul(mem2d, w_xv[l].astype(BF16), gain=norm_mem[l], tm=mem_tm))
    mem_k_p = jnp.stack(mk)
    mem_v_p = jnp.stack(mv)

    zeros = lambda *s: jnp.zeros(s, F32)
    st_p = dict(
        rwkv=zeros(bp, RW_HEADS, RW_HD, RW_HD), rw_shift=zeros(bp, RW_PROJ),
        ssd=zeros(bp, SSD_GROUPS, SSD_HPG, SSD_HD, SSD_STATE), ssd_conv=zeros(bp, CONV_W - 1, SSD_CONV_DIM),
        gdn=zeros(bp, GDN_HEADS, GDN_D, GDN_D), gdn_conv=zeros(bp, CONV_W - 1, GDN_CONV_DIM),
        ffn_conv=zeros(depth, bp, FFN_CONV_W - 1, D_FF),
    )
    rp = _run_group(gp, x_prompt.reshape(bp * lp, D_MODEL), mem_k_p.reshape(depth * bp * N_MEM, XA_DIM),
                    mem_v_p.reshape(depth * bp * N_MEM, XA_DIM), [l * bp * N_MEM for l in range(depth)],
                    st_p, wts, prm)

    st_s = dict(rwkv=state_rwkv[0], rw_shift=state_rwkv_shift[0], ssd=state_ssd[0], ssd_conv=state_ssd_conv[0],
                gdn=state_gdn[0], gdn_conv=state_gdn_conv[0], ffn_conv=state_ffn_conv)
    xs = jnp.pad(x_sample, ((0, 0), (pad_rows, 0), (0, 0))).reshape(bs * SUBLANES, D_MODEL)
    rs = _run_group(gs, xs, cache_mem_k.reshape(depth * bs * N_MEM * XA_HEADS, XA_HD),
                    cache_mem_v.reshape(depth * bs * N_MEM * XA_HEADS, XA_HD), [l * bs * N_MEM for l in range(depth)],
                    st_s, wts, prm)

    y_p = rp["y"].reshape(bp, lp, D_MODEL)
    y_s = rs["y"].reshape(bs, SUBLANES, D_MODEL)[:, pad_rows:]
    lead = lambda x: x[None]
    mem_shape = (depth, bp, N_MEM, XA_HEADS, XA_HD)
    return (y_p, y_s, lead(rp["rwkv"]), lead(rs["rwkv"]), lead(rp["rw_shift"]), lead(rs["rw_shift"]),
            lead(rp["ssd"]), lead(rs["ssd"]), lead(rp["ssd_conv"]), lead(rs["ssd_conv"]),
            lead(rp["gdn"]), lead(rs["gdn"]), lead(rp["gdn_conv"]), lead(rs["gdn_conv"]),
            rp["ffn_conv"], rs["ffn_conv"], mem_k_p.reshape(mem_shape), mem_v_p.reshape(mem_shape))
```

```python
import functools
from typing import NamedTuple

import jax
import jax.numpy as jnp
from jax import lax
from jax.experimental import pallas as pl
from jax.experimental.pallas import tpu as pltpu

F32 = jnp.float32
BF16 = jnp.bfloat16
HI = lax.Precision.HIGHEST

D_MODEL = 2048
EPS = 1e-6
RW_HEADS, RW_HD = 16, 64
RW_DIM = RW_HEADS * RW_HD
RW_LORA = 256
RW_PROJ = 3 * RW_DIM + RW_LORA
RW_GN_EPS = 6.4e-4
SSD_HEADS, SSD_HD, SSD_GROUPS, SSD_STATE = 16, 64, 2, 128
SSD_HPG = SSD_HEADS // SSD_GROUPS
SSD_DIM = SSD_HEADS * SSD_HD
SSD_GDIM = SSD_HPG * SSD_HD
SSD_CONV_DIM = SSD_DIM + 2 * SSD_GROUPS * SSD_STATE
SSD_PROJ = SSD_DIM + SSD_CONV_DIM + SSD_HEADS
AB_PROJ = RW_PROJ + SSD_PROJ
GDN_HEADS, GDN_D = 16, 128
GDN_V = GDN_HEADS * GDN_D
GDN_CONV_DIM = 3 * GDN_V
GDN_PROJ = GDN_CONV_DIM + GDN_V + 2 * GDN_HEADS
D_FF = 5632
XA_HEADS, XA_HD, N_MEM = 4, 128, 256
XA_DIM = XA_HEADS * XA_HD
CONV_W = 4
FFN_CONV_W = 3

AB_Z0 = 0
AB_X0 = SSD_DIM
AB_B0 = 2 * SSD_DIM
AB_C0 = AB_B0 + SSD_GROUPS * SSD_STATE
AB_DT0 = AB_C0 + SSD_GROUPS * SSD_STATE
AB_RW0 = 3 * RW_DIM
AB_COLS = 6656
assert AB_DT0 + 128 <= AB_RW0 and AB_RW0 + RW_PROJ <= AB_COLS and AB_RW0 % RW_DIM == 0

SUBLANES = 8
LANES = 128
VMEM_LIMIT = 56 * 1024 * 1024

MIX_ROWS = 64
INV_BLOCK = 16
INV_SPLIT = 128
MIX_NB = 2
MM_TM, MM_TN = 1024, 512
MM_TM_PROJ = 2048
RW_HP = 2
RW_SUB = 4
GDN_G = 4
GDN_SUB = 4
XA_TQ = 512
NORM_TM = 512


class Group(NamedTuple):
    B: int
    L: int
    first: int
    bb: int
    nb: int = 1

    @property
    def outer(self):
        return self.B // self.bb

    @property
    def per_outer(self):
        return self.bb * self.L

    @property
    def qs(self):
        return MIX_ROWS // self.bb

    @property
    def nc(self):
        return self.L // self.qs

    @property
    def embedded(self):
        return self.first > 0


MIXER_FLAGS = None


def _cp(*sem, flags=None):
    return pltpu.CompilerParams(dimension_semantics=sem, vmem_limit_bytes=VMEM_LIMIT, flags=flags)


def _dot(a, b):
    return jnp.dot(a.astype(BF16), b.astype(BF16), preferred_element_type=F32)


def _dot_nt(a, b):
    return lax.dot_general(a.astype(BF16), b.astype(BF16), (((1,), (1,)), ((), ())), preferred_element_type=F32)


def _dot_tn(a, b):
    return lax.dot_general(a.astype(BF16), b.astype(BF16), (((0,), (0,)), ((), ())), preferred_element_type=F32)


def _dot_hi(a, b):
    return jnp.dot(a, b, preferred_element_type=F32, precision=HI)


def _split_bf16(x):
    hi = x.astype(BF16)
    return hi, (x - hi.astype(F32)).astype(BF16)


def _dot_sel(a, sel):
    ah, al = _split_bf16(a)
    sb = sel.astype(BF16)
    return jnp.dot(jnp.concatenate([ah, al], axis=1), jnp.concatenate([sb, sb], axis=0), preferred_element_type=F32)


def _dot_nt_hi(a, b):
    return lax.dot_general(a, b, (((1,), (1,)), ((), ())), preferred_element_type=F32, precision=HI)


def _sigmoid(x):
    return 1.0 / (1.0 + jnp.exp(-x))


def _silu(x):
    return x * _sigmoid(x)


def _softplus(x):
    return jnp.maximum(x, 0.0) + jnp.log(1.0 + jnp.exp(-jnp.abs(x)))


def _iota2(shape, axis):
    return lax.broadcasted_iota(jnp.int32, shape, axis)


class _Masks(NamedTuple):
    incl: jax.Array
    strict: jax.Array
    eye: jax.Array
    cumsum: jax.Array
    valid: jax.Array


def _masks(grp, heads=1):
    r = MIX_ROWS
    n = heads * r
    ri, ci = _iota2((n, n), 0), _iota2((n, n), 1)
    same = (ri // grp.qs) == (ci // grp.qs)
    incl = same & (ci <= ri)
    strict = same & (ci < ri)
    si, sj = _iota2((2 * r, r), 0), _iota2((2 * r, r), 1)
    same_seq = ((si % r) // grp.qs) == (sj // grp.qs)
    cumsum = (same_seq & ((sj <= si) | (si >= r))).astype(F32)
    t = _iota2((r, 1), 0) % grp.qs
    return _Masks(incl, strict, (ri == ci).astype(F32), cumsum, t >= grp.first)


def _neumann(lm, rhs, nil, eye=None):
    n = lm.shape[0]
    p = None
    if rhs is None:
        t = eye + lm
        if nil > 2:
            p = _dot(lm, lm)
            yield
    elif nil > 2:
        both = _dot(lm, jnp.concatenate([lm, rhs], axis=1))
        yield
        p, t = both[:, :n], rhs + both[:, n:]
    else:
        t = rhs + _dot(lm, rhs)
        yield
    k = 2
    while k < nil:
        if 2 * k < nil:
            both = _dot(p, jnp.concatenate([p, t], axis=1))
            p, t = both[:, :n], t + both[:, n:]
        else:
            t = t + _dot(p, t)
        yield
        k *= 2
    return t


def _inv_unit_lower(lm, eye, qs):
    if qs <= INV_BLOCK:
        return (yield from _neumann(lm, None, qs, eye))
    n = lm.shape[0]
    diag = (_iota2((n, n), 0) // INV_BLOCK) == (_iota2((n, n), 1) // INV_BLOCK)
    d = jnp.where(diag, lm, 0.0)
    dinv = yield from _neumann(d, None, INV_BLOCK, eye)
    nm = _dot(dinv, lm - d)
    yield
    return (yield from _neumann(nm, dinv, qs // INV_BLOCK))


def _lockstep(gens):
    results = [None] * len(gens)
    active = list(range(len(gens)))
    while active:
        for i in list(active):
            try:
                next(gens[i])
            except StopIteration as stop:
                results[i] = stop.value
                active.remove(i)
        if active:
            yield
    return results


def _solve_unit_lower(lm, rhs, eye, qs):
    n = lm.shape[0]
    if n > INV_SPLIT:
        blocks = [slice(i, i + INV_SPLIT) for i in range(0, n, INV_SPLIT)]
        inv = yield from _lockstep([_inv_unit_lower(lm[b, b], eye[b, b], qs) for b in blocks])
        zero = jnp.zeros((INV_SPLIT, INV_SPLIT), F32)
        t = jnp.concatenate([jnp.concatenate([inv[i] if i == j else zero for j in range(len(blocks))], axis=1)
                             for i in range(len(blocks))], axis=0)
    else:
        t = yield from _inv_unit_lower(lm, eye, qs)
    x = _dot(t, rhs)
    yield
    return x


def _interleave(gens):
    results = [None] * len(gens)
    active = list(range(len(gens)))
    while active:
        for i in list(active):
            try:
                next(gens[i])
            except StopIteration as stop:
                results[i] = stop.value
                active.remove(i)
    return results


def _stack_heads(x, heads, width):
    return jnp.concatenate([x[:, h * width:(h + 1) * width] for h in range(heads)], axis=0)


def _unstack_heads(x, heads):
    r = x.shape[0] // heads
    return jnp.concatenate([x[h * r:(h + 1) * r] for h in range(heads)], axis=1)


def _seq_rows(grp, x, j, heads):
    if grp.bb == 1:
        return x
    qs = grp.qs
    return jnp.concatenate([x[h * MIX_ROWS + j * qs:h * MIX_ROWS + (j + 1) * qs] for h in range(heads)], axis=0)


def _from_seq_rows(grp, parts, heads):
    if grp.bb == 1:
        return parts[0]
    qs = grp.qs
    return jnp.concatenate([p[h * qs:(h + 1) * qs] for h in range(heads) for p in parts], axis=0)


def _expand_rows(x, heads, rows_per_head, width):
    m = x.shape[0]
    if x.shape[1] == width:
        x = jnp.concatenate([x] * heads, axis=1)
    keep = ((_iota2((m, heads * width), 0) // rows_per_head) % heads) == (_iota2((m, heads * width), 1) // width)
    return jnp.where(keep, x, 0.0)


def _taps(grp, cur, halo, width):
    if grp.embedded:
        t = _iota2((cur.shape[0], 1), 0) % grp.qs
        full = jnp.where(t < grp.first, halo, cur)
        return [full] + [pltpu.roll(full, s, axis=0) for s in range(1, width)]
    full = jnp.concatenate([halo, cur], axis=0)
    return [cur] + [pltpu.roll(full, s, axis=0)[SUBLANES:] for s in range(1, width)]


def _conv(grp, cur, halo, w_ref, bias, width):
    taps = _taps(grp, cur, halo, width)
    y = taps[0] * w_ref[width - 1:width, :]
    for s in range(1, width):
        y = y + taps[s] * w_ref[width - 1 - s:width - s, :]
    return y if bias is None else y + bias


class _Tok(NamedTuple):
    arr: jax.Array
    width: int
    col: object
    hist: jax.Array = None
    hcol: object = None


def _view3(grp, x2d):
    return x2d.reshape(grp.outer, grp.per_outer, x2d.shape[1])


def _mixer_specs(grp, toks):
    nb = grp.nb
    ops, specs = [], []
    for t in toks:
        ops.append(_view3(grp, t.arr))
        specs.append(pl.BlockSpec((nb, MIX_ROWS, t.width), lambda b, g, c, t=t: (b, c, t.col(g))))
    for t in toks:
        if t.hist is None:
            continue
        if grp.embedded:
            ops.append(_view3(grp, t.hist))
            specs.append(pl.BlockSpec((nb, MIX_ROWS, t.width), lambda b, g, c, t=t: (b, 0, t.hcol(g))))
            continue
        tile = _sublane_tile(t.arr.dtype)
        sub = MIX_ROWS // tile
        ops.append(_view3(grp, t.arr))
        specs.append(pl.BlockSpec(
            (nb, tile, t.width), lambda b, g, c, t=t, sub=sub: (b, jnp.maximum(c * sub - 1, 0), t.col(g))))
        ops.append(t.hist.reshape(grp.B, SUBLANES, t.hist.shape[1]))
        specs.append(pl.BlockSpec((nb, SUBLANES, t.width), lambda b, g, c, t=t: (b, 0, t.hcol(g))))
    return ops, specs


def _read_windows(grp, refs, n_plain, n_hist):
    c = pl.program_id(2)
    nb = grp.nb
    cur = [[r[s].astype(F32) for s in range(nb)] for r in refs[:n_plain]]
    rest = refs[n_plain:]
    halos = []
    for i in range(n_hist):
        if grp.embedded:
            halos.append([rest[i][s] for s in range(nb)])
        else:
            per = []
            for s in range(nb):
                prev = rest[2 * i][s]
                prev = prev[prev.shape[0] - SUBLANES:].astype(F32)
                per.append(jnp.where(c == 0, rest[2 * i + 1][s], prev))
            halos.append(per)
    used = n_hist if grp.embedded else 2 * n_hist
    return cur, halos, rest[used:]


def _sublane_tile(dtype):
    return SUBLANES * 4 // jnp.dtype(dtype).itemsize


def _param_spec(width, col):
    return lambda rows: pl.BlockSpec((rows, width), lambda b, g, c: (0, col(g)))


def _matmul_body(*refs, norm, has_res, n_a, w_rows_are_outputs):
    it = iter(refs)
    a_refs = [next(it) for _ in range(n_a)]
    g_ref = next(it) if norm else None
    w_ref = next(it)
    res_ref = next(it) if has_res else None
    o_ref = next(it)
    if norm:
        an_ref = next(it)

        @pl.when(pl.program_id(1) == 0)
        def _():
            x = a_refs[0][...]
            y = x * lax.rsqrt(jnp.mean(x * x, axis=-1, keepdims=True) + EPS)
            an_ref[...] = (y * g_ref[...]).astype(BF16)

        if w_rows_are_outputs:
            acc = lax.dot_general(an_ref[...], w_ref[...], (((1,), (1,)), ((), ())), preferred_element_type=F32)
        else:
            acc = jnp.dot(an_ref[...], w_ref[...], preferred_element_type=F32)
    else:
        acc, k0 = None, 0
        for a_ref in a_refs:
            k1 = k0 + a_ref.shape[1]
            part = jnp.dot(a_ref[...], w_ref[k0:k1, :], preferred_element_type=F32)
            acc = part if acc is None else acc + part
            k0 = k1
    if has_res:
        acc = acc + res_ref[...]
    o_ref[...] = acc.astype(o_ref.dtype)


def _matmul(a, w, *, layer=None, gain=None, res=None, out_dtype=F32, tm=MM_TM, tn=MM_TN, w_rows_are_outputs=False):
    a_list = list(a) if isinstance(a, (list, tuple)) else [a]
    m = a_list[0].shape[0]
    k, n = w.shape[-2:][::-1] if w_rows_are_outputs else w.shape[-2:]
    assert not w_rows_are_outputs or (gain is not None and w.ndim == 2)
    assert sum(x.shape[1] for x in a_list) == k
    norm = gain is not None
    assert not norm or len(a_list) == 1
    ops = list(a_list)
    specs = [pl.BlockSpec((tm, x.shape[1]), lambda i, j: (i, 0)) for x in a_list]
    if norm:
        ops.append(gain.reshape(1, k))
        specs.append(pl.BlockSpec((1, k), lambda i, j: (0, 0)))
    ops.append(w)
    if w_rows_are_outputs:
        specs.append(pl.BlockSpec((tn, k), lambda i, j: (j, 0)))
    else:
        specs.append(_weight_spec(w, layer, k, tn, lambda j: j))
    if res is not None:
        ops.append(res)
        specs.append(pl.BlockSpec((tm, tn), lambda i, j: (i, j)))
    return pl.pallas_call(
        functools.partial(_matmul_body, norm=norm, has_res=res is not None, n_a=len(a_list),
                          w_rows_are_outputs=w_rows_are_outputs),
        grid=(m // tm, n // tn),
        in_specs=specs,
        out_specs=pl.BlockSpec((tm, tn), lambda i, j: (i, j)),
        out_shape=jax.ShapeDtypeStruct((m, n), out_dtype),
        scratch_shapes=[pltpu.VMEM((tm, k), BF16)] if norm else [],
        compiler_params=_cp("parallel", "arbitrary"),
        name="matmul",
    )(*ops)


def _weight_spec(w, layer, k, tn, col):
    if w.ndim == 2:
        return pl.BlockSpec((k, tn), lambda i, j: (0, col(j)))
    return pl.BlockSpec((None, k, tn), lambda i, j: (layer, 0, col(j)))


def _rmsnorm_body(x_ref, g_ref, o_ref):
    x = x_ref[...]
    y = x * lax.rsqrt(jnp.mean(x * x, axis=-1, keepdims=True) + EPS)
    o_ref[...] = (y * g_ref[...]).astype(o_ref.dtype)


def _rmsnorm(x, gain, out_dtype=F32, tm=NORM_TM):
    m, k = x.shape
    return pl.pallas_call(
        _rmsnorm_body,
        grid=(m // tm,),
        in_specs=[pl.BlockSpec((tm, k), lambda i: (i, 0)), pl.BlockSpec((1, k), lambda i: (0, 0))],
        out_specs=pl.BlockSpec((tm, k), lambda i: (i, 0)),
        out_shape=jax.ShapeDtypeStruct((m, k), out_dtype),
        compiler_params=_cp("parallel"),
        name="rmsnorm",
    )(x, gain.reshape(1, k))


def _rwkv_body(*refs, grp):
    r_rows = MIX_ROWS
    qs, bb, nc, nb = grp.qs, grp.bb, grp.nc, grp.nb
    nh = 2 * RW_HP
    sw = nh * RW_HD
    width = RW_SUB * sw
    n = nh * r_rows
    cur, halos, rest = _read_windows(grp, refs, 4, 4)
    (mu_r, mu_k, mu_v, mu_lo, w0_ref, a0_ref, lora_ref, kk_ref, ka_ref, rk_ref, gnw_ref, gnb_ref,
     s0_ref, y_ref, s_ref, sbig_ref) = rest
    c = pl.program_id(2)

    @pl.when(c == 0)
    def _():
        for j in range(nb * bb):
            for sub in range(RW_SUB):
                st = jnp.concatenate([s0_ref[j, sub * nh + h] for h in range(nh)], axis=0)
                sbig_ref[j, sub] = _expand_rows(st, nh, RW_HD, RW_HD)

    m = _masks(grp, nh)
    bd = ((_iota2((sw, sw), 0) // RW_HD) == (_iota2((sw, sw), 1) // RW_HD)).astype(F32)

    def head_sums(x):
        return _unstack_heads(_dot_sel(_stack_heads(x, RW_SUB, sw), bd), RW_SUB)

    def block(s):
        def shifted(i, mu_ref):
            x = cur[i][s]
            prev = _taps(grp, x, halos[i][s], 2)[1]
            return x + (prev - x) * mu_ref[...]

        r = shifted(0, mu_r)
        k = shifted(1, mu_k)
        v = shifted(2, mu_v)
        lo = shifted(3, mu_lo)
        lora_in = jnp.concatenate([jnp.tanh(lo[:, :64]), lo[:, 64:128], _sigmoid(lo[:, 128:256])], axis=1)
        lora = _dot(lora_in, lora_ref[0])
        w = -_softplus(-(w0_ref[...] + lora[:, :width])) - 0.5
        lw = jnp.where(m.valid, -jnp.exp(w), 0.0)
        a = _sigmoid(a0_ref[...] + lora[:, width:2 * width])
        g = lora[:, 2 * width:]

        kx = k * kk_ref[...]
        kp = jnp.where(m.valid, k * (1.0 + (a - 1.0) * ka_ref[...]), 0.0)
        sums = head_sums(jnp.concatenate([kx * kx, r * kp * rk_ref[...]], axis=0))
        kkn = kx * lax.rsqrt(sums[:r_rows] + 1e-6)
        bonus = sums[r_rows:] * v
        at = jnp.where(m.valid, -kkn, 0.0)
        bt = jnp.where(m.valid, kkn * a, 0.0)

        cums = _dot_hi(m.cumsum, lw)
        cum, cum_end = cums[:r_rows], cums[r_rows:]
        e_neg = jnp.exp(-cum)
        e_end = jnp.exp(cum_end - cum)
        p_end = jnp.exp(cum_end)
        a_til, r_til = at * jnp.exp(cum - lw), r * jnp.exp(cum)
        b_til, k_til = bt * e_neg, kp * e_neg
        b_hat, k_hat = bt * e_end, kp * e_end

        def stack(sub):
            lanes = slice(sub * sw, (sub + 1) * sw)

            def tile(x):
                return _expand_rows(jnp.concatenate([x[:, lanes]] * nh, axis=0), nh, r_rows, RW_HD)

            ar = jnp.concatenate([tile(a_til), tile(r_til)], axis=0)
            bk = jnp.concatenate([tile(b_til), tile(k_til)], axis=0)
            bk_end = jnp.concatenate([tile(b_hat), tile(k_hat)], axis=0)
            v_exp = tile(v)
            gm = _dot_nt(ar, bk)
            yield
            m_ab = jnp.where(m.strict, gm[:n, :n], 0.0)
            m_ak = jnp.where(m.strict, gm[:n, n:], 0.0)
            m_r = jnp.concatenate([jnp.where(m.incl, gm[n:, :n], 0.0), jnp.where(m.incl, gm[n:, n:], 0.0)], axis=1)

            def seq2(x, j):
                if bb == 1:
                    return x
                return jnp.concatenate([_seq_rows(grp, x[:n], j, nh), _seq_rows(grp, x[n:], j, nh)], axis=0)

            parts = []
            for j in range(bb):
                parts.append(_dot_nt(seq2(ar, j), sbig_ref[s * bb + j, sub]))
                yield
            half = nh * qs
            as_a = _from_seq_rows(grp, [p[:half] for p in parts], nh)
            as_r = _from_seq_rows(grp, [p[half:] for p in parts], nh)
            rhs = as_a + _dot(m_ak, v_exp)
            yield
            u = yield from _solve_unit_lower(m_ab, rhs, m.eye, qs)
            uv = jnp.concatenate([u, v_exp], axis=0)
            y_exp = as_r + _dot(m_r, uv)
            yield
            y = y_exp[:r_rows]
            for h in range(1, nh):
                y = y + y_exp[h * r_rows:(h + 1) * r_rows]
            for j in range(bb):
                sbig_ref[s * bb + j, sub] = (sbig_ref[s * bb + j, sub] * p_end[j * qs:j * qs + 1, lanes]
                                             + _dot_tn(seq2(uv, j), seq2(bk_end, j)))
                yield
            return y

        def finish(ys):
            y = jnp.concatenate(ys, axis=1)
            mean = head_sums(y) * (1.0 / RW_HD)
            d = y - mean
            var = head_sums(d * d) * (1.0 / RW_HD)
            out = d * lax.rsqrt(var + RW_GN_EPS) * gnw_ref[...] + gnb_ref[...] + bonus
            y_ref[s] = (out * g).astype(y_ref.dtype)

        return [stack(sub) for sub in range(RW_SUB)], finish

    blocks = [block(s) for s in range(nb)]
    results = _interleave([chain for chains, _ in blocks for chain in chains])
    for s, (_, finish) in enumerate(blocks):
        finish(results[s * RW_SUB:(s + 1) * RW_SUB])

    @pl.when(c == nc - 1)
    def _():
        for j in range(nb * bb):
            for sub in range(RW_SUB):
                sb = sbig_ref[j, sub]
                for h in range(nh):
                    s_ref[j, sub * nh + h] = sb[h * RW_HD:(h + 1) * RW_HD, h * RW_HD:(h + 1) * RW_HD]


def _rwkv(grp, proj, hist, s0, prm):
    width = RW_SUB * RW_HP * LANES
    ng = RW_DIM // width
    lora_blk = 3 * RW_DIM // RW_LORA
    p0 = AB_RW0 // width
    toks = [
        _Tok(proj, width, lambda g: p0 + g, hist, lambda g: g),
        _Tok(proj, width, lambda g: p0 + ng + g, hist, lambda g: ng + g),
        _Tok(proj, width, lambda g: p0 + 2 * ng + g, hist, lambda g: 2 * ng + g),
        _Tok(proj, RW_LORA, lambda g: AB_RW0 // RW_LORA + lora_blk, hist, lambda g: lora_blk),
    ]
    ops, specs = _mixer_specs(grp, toks)
    head = _param_spec(width, lambda g: g)
    lora_w = jnp.zeros((ng, RW_LORA, 3 * width), F32)
    for i, (name, r0, r1) in enumerate((("w_up", 0, 64), ("a_up", 64, 128), ("g_up", 128, 256))):
        blk = prm[name].reshape(r1 - r0, ng, width).transpose(1, 0, 2)
        lora_w = lora_w.at[:, r0:r1, i * width:(i + 1) * width].set(blk)
    mu = prm["mu"].reshape(1, RW_PROJ)
    flat = lambda x: x.reshape(1, RW_DIM)
    params = [
        (mu, pl.BlockSpec((1, width), lambda b, g, c: (0, g))),
        (mu, pl.BlockSpec((1, width), lambda b, g, c: (0, ng + g))),
        (mu, pl.BlockSpec((1, width), lambda b, g, c: (0, 2 * ng + g))),
        (mu, pl.BlockSpec((1, RW_LORA), lambda b, g, c: (0, lora_blk))),
        (flat(prm["w0"]), head(1)), (flat(prm["a0"]), head(1)),
        (lora_w, pl.BlockSpec((1, RW_LORA, 3 * width), lambda b, g, c: (g, 0, 0))),
        (flat(prm["k_k"]), head(1)), (flat(prm["k_a"]), head(1)), (flat(prm["r_k"]), head(1)),
        (flat(prm["gn_w"]), head(1)), (flat(prm["gn_b"]), head(1)),
    ]
    ops += [p for p, _ in params]
    specs += [s for _, s in params]
    seqs = grp.nb * grp.bb
    st_spec = pl.BlockSpec((seqs, RW_SUB * 2 * RW_HP, RW_HD, RW_HD), lambda b, g, c: (b, g, 0, 0))
    ops.append(s0)
    specs.append(st_spec)
    y, s_out = pl.pallas_call(
        functools.partial(_rwkv_body, grp=grp),
        grid=(grp.outer // grp.nb, ng, grp.nc),
        in_specs=specs,
        out_specs=[pl.BlockSpec((grp.nb, MIX_ROWS, width), lambda b, g, c: (b, c, g)), st_spec],
        out_shape=[jax.ShapeDtypeStruct((grp.outer, grp.per_outer, RW_DIM), BF16),
                   jax.ShapeDtypeStruct(s0.shape, F32)],
        scratch_shapes=[pltpu.VMEM((seqs, RW_SUB, RW_HP * LANES, RW_HP * LANES), F32)],
        compiler_params=_cp("parallel", "parallel", "arbitrary", flags=MIXER_FLAGS),
        name="rwkv7",
    )(*ops)
    return y.reshape(grp.B * grp.L, RW_DIM), s_out


def _ssd_body(*refs, grp):
    r_rows = MIX_ROWS
    qs, bb, nb = grp.qs, grp.bb, grp.nb
    nh, hd, gdim = SSD_HPG, SSD_HD, SSD_GDIM
    cur, halos, rest = _read_windows(grp, refs, 5, 3)
    cw_x, cw_b, cw_c, cb_x, cb_b, cb_c, dtb_ref, alog_ref, dskip_ref, nw_ref, h0_ref, y_ref, h_ref = rest
    c = pl.program_id(2)

    @pl.when(c == 0)
    def _():
        h_ref[...] = h0_ref[...]

    m = _masks(grp)
    eye_h = (_iota2((SSD_HEADS, SSD_HEADS), 0) == _iota2((SSD_HEADS, SSD_HEADS), 1)).astype(F32)
    ei, ej = _iota2((3 * SSD_HEADS, 3 * SSD_DIM), 0), _iota2((3 * SSD_HEADS, 3 * SSD_DIM), 1)
    spread = ((ei // SSD_HEADS == ej // SSD_DIM) & (ei % SSD_HEADS == (ej % SSD_DIM) // hd)).astype(F32)

    def block(s):
        x_raw, b_raw, c_raw, z, dt_blk = (cur[i][s] for i in range(5))
        xs = _silu(_conv(grp, x_raw, halos[0][s], cw_x, cb_x[...], CONV_W))
        bm = _silu(_conv(grp, b_raw, halos[1][s], cw_b, cb_b[...], CONV_W))
        cm = _silu(_conv(grp, c_raw, halos[2][s], cw_c, cb_c[...], CONV_W))

        dt = jnp.where(m.valid, _softplus(dt_blk[:, :SSD_HEADS] + dtb_ref[...]), 0.0)
        cums = _dot_hi(m.cumsum, dt * (-jnp.exp(alog_ref[...])))
        acs, acs_end = cums[:r_rows], cums[r_rows:]
        acs_row = _dot_nt_hi(eye_h, acs)
        lanes = _dot_hi(jnp.concatenate([dt, acs_end - acs, acs], axis=1), spread)
        xd = xs * lanes[:, :SSD_DIM]
        xd_dec = xd * jnp.exp(lanes[:, SSD_DIM:2 * SSD_DIM])
        acs_lanes = lanes[:, 2 * SSD_DIM:]

        def group(g):
            gl = slice(g * gdim, (g + 1) * gdim)
            bmg, cmg = bm[:, g * SSD_STATE:(g + 1) * SSD_STATE], cm[:, g * SSD_STATE:(g + 1) * SSD_STATE]
            cb = _dot_nt(cmg, bmg)
            yield
            yd = []
            for r in range(nh):
                h = g * nh + r
                diff = acs[:, h:h + 1] - acs_row[h:h + 1, :]
                lmat = jnp.where(m.incl, jnp.exp(jnp.where(m.incl, diff, 0.0)), 0.0)
                yd.append(_dot(cb * lmat, xd[:, h * hd:(h + 1) * hd]))
                yield
            y = jnp.concatenate(yd, axis=1)
            head_rows = (_iota2((gdim, SSD_HEADS), 0) // hd + g * nh == _iota2((gdim, SSD_HEADS), 1)).astype(F32)
            yoff = []
            for j in range(bb):
                rows = slice(j * qs, (j + 1) * qs)
                hj = h_ref[s * bb + j, g]
                yoff.append(_dot_nt(cmg[rows], hj))
                yield
                end_col = jnp.exp(_dot_nt_hi(head_rows, acs_end[j * qs:j * qs + SUBLANES]))[:, :1]
                yield
                h_ref[s * bb + j, g] = hj * end_col + _dot_tn(xd_dec[rows, gl], bmg[rows])
                yield
            yoff = yoff[0] if bb == 1 else jnp.concatenate(yoff, axis=0)
            y = y + yoff * jnp.exp(acs_lanes[:, gl]) + xs[:, gl] * dskip_ref[:, gl]
            yg = y * _silu(z[:, gl])
            yg = yg * lax.rsqrt(jnp.mean(yg * yg, axis=-1, keepdims=True) + EPS)
            return yg * nw_ref[:, gl]

        return [group(g) for g in range(SSD_GROUPS)]

    results = _interleave([chain for s in range(nb) for chain in block(s)])
    for s in range(nb):
        y_ref[s] = jnp.concatenate(results[s * SSD_GROUPS:(s + 1) * SSD_GROUPS], axis=1).astype(y_ref.dtype)


def _ssd(grp, proj, hist, h0, prm):
    bc = SSD_GROUPS * SSD_STATE
    toks = [
        _Tok(proj, SSD_DIM, lambda g: AB_X0 // SSD_DIM, hist, lambda g: 0),
        _Tok(proj, bc, lambda g: AB_B0 // bc, hist, lambda g: SSD_DIM // bc),
        _Tok(proj, bc, lambda g: AB_C0 // bc, hist, lambda g: SSD_DIM // bc + 1),
        _Tok(proj, SSD_DIM, lambda g: AB_Z0 // SSD_DIM),
        _Tok(proj, LANES, lambda g: AB_DT0 // LANES),
    ]
    ops, specs = _mixer_specs(grp, toks)
    cw, cbias = prm["conv_w"], prm["conv_b"].reshape(1, SSD_CONV_DIM)
    conv_cols = [(SSD_DIM, lambda g: 0), (bc, lambda g: SSD_DIM // bc), (bc, lambda g: SSD_DIM // bc + 1)]
    params = [(cw, _param_spec(wd, col)(CONV_W)) for wd, col in conv_cols]
    params += [(cbias, _param_spec(wd, col)(1)) for wd, col in conv_cols]
    full = lambda n: pl.BlockSpec((1, n), lambda b, g, c: (0, 0))
    params += [
        (prm["dt_bias"].reshape(1, SSD_HEADS), full(SSD_HEADS)),
        (prm["A_log"].reshape(1, SSD_HEADS), full(SSD_HEADS)),
        (jnp.repeat(prm["D"], SSD_HD).reshape(1, SSD_DIM), full(SSD_DIM)),
        (prm["norm_w"].reshape(1, SSD_DIM), full(SSD_DIM)),
    ]
    ops += [p for p, _ in params]
    specs += [s for _, s in params]
    st_spec = pl.BlockSpec((grp.nb * grp.bb, SSD_GROUPS, SSD_GDIM, SSD_STATE), lambda b, g, c: (b, 0, 0, 0))
    ops.append(h0)
    specs.append(st_spec)
    y, h_out = pl.pallas_call(
        functools.partial(_ssd_body, grp=grp),
        grid=(grp.outer // grp.nb, 1, grp.nc),
        in_specs=specs,
        out_specs=[pl.BlockSpec((grp.nb, MIX_ROWS, SSD_DIM), lambda b, g, c: (b, c, 0)), st_spec],
        out_shape=[jax.ShapeDtypeStruct((grp.outer, grp.per_outer, SSD_DIM), BF16),
                   jax.ShapeDtypeStruct(h0.shape, F32)],
        compiler_params=_cp("parallel", "arbitrary", "arbitrary"),
        name="ssd",
    )(*ops)
    return y.reshape(grp.B * grp.L, SSD_DIM), h_out


def _gdn_body(*refs, grp):
    r_rows = MIX_ROWS
    qs, bb, nb = grp.qs, grp.bb, grp.nb
    nh = GDN_G
    n = nh * r_rows
    cur, halos, rest = _read_windows(grp, refs, 5, 3)
    cw_q, cw_k, cw_v, alog_ref, dtb_ref, nw_ref, s0_ref, o_ref, s_ref = rest
    c = pl.program_id(2)

    @pl.when(c == 0)
    def _():
        s_ref[...] = s0_ref[...]

    m = _masks(grp, nh)
    per_step = nh * GDN_SUB
    eye_g = (_iota2((SUBLANES, nh), 0) == _iota2((SUBLANES, nh), 1)).astype(F32)
    col = lambda x: jnp.concatenate([x[:, h:h + 1] for h in range(nh)], axis=0)

    def block(s):
        q_raw, k_raw, v_raw, z, ba = (cur[i][s] for i in range(5))
        q_all = _silu(_conv(grp, q_raw, halos[0][s], cw_q, None, CONV_W))
        k_all = _silu(_conv(grp, k_raw, halos[1][s], cw_k, None, CONV_W))
        v_all = _silu(_conv(grp, v_raw, halos[2][s], cw_v, None, CONV_W))

        def stack(sub):
            lanes = slice(sub * nh * GDN_D, (sub + 1) * nh * GDN_D)
            heads = slice(sub * nh, (sub + 1) * nh)
            q = _stack_heads(q_all[:, lanes], nh, GDN_D)
            k = _stack_heads(k_all[:, lanes], nh, GDN_D)
            v = _stack_heads(v_all[:, lanes], nh, GDN_D)
            q = q * lax.rsqrt(jnp.sum(q * q, axis=-1, keepdims=True) + 1e-6) * (GDN_D ** -0.5)
            k = k * lax.rsqrt(jnp.sum(k * k, axis=-1, keepdims=True) + 1e-6)

            beta = jnp.where(m.valid, _sigmoid(ba[:, heads]), 0.0)
            a_raw = ba[:, per_step + sub * nh:per_step + (sub + 1) * nh]
            gg = jnp.where(m.valid, -jnp.exp(alog_ref[0][:, heads]) * _softplus(a_raw + dtb_ref[0][:, heads]), 0.0)
            gcs = _dot_hi(m.cumsum, gg)
            yield
            gc, gc_end = gcs[:r_rows], gcs[r_rows:]
            gc_row = _dot_nt_hi(eye_g, gc)
            yield

            bcol, gcol, gend = col(beta), col(gc), col(gc_end)
            grow = jnp.concatenate([gc_row[h:h + 1, :] for h in range(nh)], axis=1)
            dec = jnp.where(m.incl, jnp.exp(jnp.where(m.incl, gcol - grow, 0.0)), 0.0)
            kb = k * bcol
            kq = _dot_nt(jnp.concatenate([kb, q], axis=0), k)
            yield
            lm = -jnp.where(m.strict, kq[:n] * dec, 0.0)
            attn = jnp.where(m.incl, kq[n:] * dec, 0.0)
            eg = jnp.exp(gcol)
            sol = yield from _solve_unit_lower(lm, jnp.concatenate([v * bcol, kb * eg], axis=1), m.eye, qs)
            vw, kcd = sol[:, :GDN_D], sol[:, GDN_D:]
            qg = q * eg
            kg = k * jnp.exp(gend - gcol)

            def state(j):
                return jnp.concatenate([s_ref[s * bb + j, sub * nh + h] for h in range(nh)], axis=0)

            parts = []
            for j in range(bb):
                lhs = jnp.concatenate([_seq_rows(grp, kcd, j, nh), _seq_rows(grp, qg, j, nh)], axis=0)
                parts.append(_dot(_expand_rows(lhs, nh, qs, GDN_D), state(j)))
                yield
            half = nh * qs
            ks = _from_seq_rows(grp, [p[:half] for p in parts], nh)
            qsv = _from_seq_rows(grp, [p[half:] for p in parts], nh)
            v_new = vw - ks
            o = qsv + _dot(attn, v_new)
            yield
            for j in range(bb):
                kgj = _expand_rows(_seq_rows(grp, kg, j, nh), nh, qs, GDN_D)
                g_last = jnp.exp(gc_end[j * qs:j * qs + 1, :])
                g_last = jnp.concatenate([jnp.broadcast_to(g_last[:, h:h + 1], (GDN_D, 1)) for h in range(nh)], axis=0)
                s_new = state(j) * g_last + _dot_tn(kgj, _seq_rows(grp, v_new, j, nh))
                yield
                for h in range(nh):
                    s_ref[s * bb + j, sub * nh + h] = s_new[h * GDN_D:(h + 1) * GDN_D]
            o = o * lax.rsqrt(jnp.mean(o * o, axis=-1, keepdims=True) + EPS) * nw_ref[...]
            return _unstack_heads(o, nh)

        def finish(os_):
            o_ref[s] = (jnp.concatenate(os_, axis=1) * _silu(z)).astype(o_ref.dtype)

        return [stack(sub) for sub in range(GDN_SUB)], finish

    blocks = [block(s) for s in range(nb)]
    results = _interleave([chain for chains, _ in blocks for chain in chains])
    for s, (_, finish) in enumerate(blocks):
        finish(results[s * GDN_SUB:(s + 1) * GDN_SUB])


def _zeros_like_axis(x, n, axis):
    shape = list(x.shape)
    shape[axis] = n
    return jnp.zeros(shape, x.dtype)


def _ab_layout(x, axis=-1):
    cut = lambda a, b: lax.slice_in_dim(x, a, b, axis=axis)
    zeros = lambda n: _zeros_like_axis(x, n, axis)
    rw, z = cut(0, RW_PROJ), cut(RW_PROJ, RW_PROJ + SSD_DIM)
    xbc = cut(RW_PROJ + SSD_DIM, RW_PROJ + SSD_DIM + SSD_CONV_DIM)
    dt = cut(RW_PROJ + SSD_DIM + SSD_CONV_DIM, AB_PROJ)
    return jnp.concatenate([z, xbc, dt, zeros(AB_RW0 - AB_DT0 - SSD_HEADS), rw, zeros(AB_COLS - AB_RW0 - RW_PROJ)],
                           axis=axis)


def _gdn_layout(x, axis=-1):
    per_step = GDN_G * GDN_SUB
    main = GDN_CONV_DIM + GDN_V
    cut = lambda a, b: lax.slice_in_dim(x, a, b, axis=axis)
    parts = [cut(0, main)]
    for g in range(GDN_HEADS // per_step):
        h0, h1 = g * per_step, (g + 1) * per_step
        parts += [cut(main + h0, main + h1), cut(main + GDN_HEADS + h0, main + GDN_HEADS + h1),
                  _zeros_like_axis(x, LANES - 2 * per_step, axis)]
    total = main + (GDN_HEADS // per_step) * LANES
    parts.append(_zeros_like_axis(x, -total % MM_TN, axis))
    return jnp.concatenate(parts, axis=axis)


def _gdn(grp, proj, hist, s0, prm):
    per_step = GDN_G * GDN_SUB
    width = per_step * GDN_D
    ng = GDN_HEADS // per_step
    ba_blk = (GDN_CONV_DIM + GDN_V) // LANES
    toks = [
        _Tok(proj, width, lambda g: g, hist, lambda g: g),
        _Tok(proj, width, lambda g: ng + g, hist, lambda g: ng + g),
        _Tok(proj, width, lambda g: 2 * ng + g, hist, lambda g: 2 * ng + g),
        _Tok(proj, width, lambda g: 3 * ng + g),
        _Tok(proj, LANES, lambda g: ba_blk + g),
    ]
    ops, specs = _mixer_specs(grp, toks)
    cw = prm["conv_w"]
    grouped = lambda x: x.reshape(ng, 1, per_step)
    params = [
        (cw, _param_spec(width, lambda g: g)(CONV_W)),
        (cw, _param_spec(width, lambda g: ng + g)(CONV_W)),
        (cw, _param_spec(width, lambda g: 2 * ng + g)(CONV_W)),
        (grouped(prm["A_log"]), pl.BlockSpec((1, 1, per_step), lambda b, g, c: (g, 0, 0))),
        (grouped(prm["dt_bias"]), pl.BlockSpec((1, 1, per_step), lambda b, g, c: (g, 0, 0))),
        (prm["norm_w"].reshape(1, GDN_D), pl.BlockSpec((1, GDN_D), lambda b, g, c: (0, 0))),
    ]
    ops += [p for p, _ in params]
    specs += [s for _, s in params]
    st_spec = pl.BlockSpec((grp.nb * grp.bb, per_step, GDN_D, GDN_D), lambda b, g, c: (b, g, 0, 0))
    ops.append(s0)
    specs.append(st_spec)
    o, s_out = pl.pallas_call(
        functools.partial(_gdn_body, grp=grp),
        grid=(grp.outer // grp.nb, ng, grp.nc),
        in_specs=specs,
        out_specs=[pl.BlockSpec((grp.nb, MIX_ROWS, width), lambda b, g, c: (b, c, g)), st_spec],
        out_shape=[jax.ShapeDtypeStruct((grp.outer, grp.per_outer, GDN_V), BF16),
                   jax.ShapeDtypeStruct(s0.shape, F32)],
        compiler_params=_cp("parallel", "parallel", "arbitrary", flags=MIXER_FLAGS),
        name="gdn",
    )(*ops)
    return o.reshape(grp.B * grp.L, GDN_V), s_out


def _attend(q, k_ref, v_ref, nseq, lq):
    scale = XA_HD ** -0.5
    outs = []
    for j in range(nseq):
        qrows = slice(j * lq, (j + 1) * lq)
        mrows = slice(j * N_MEM, (j + 1) * N_MEM)
        heads = []
        for h in range(XA_HEADS):
            sl = slice(h * XA_HD, (h + 1) * XA_HD)
            s = _dot_nt(q[qrows, sl], k_ref[mrows, sl]) * scale
            p = jnp.exp(s - jnp.max(s, axis=-1, keepdims=True))
            heads.append(_dot(p, v_ref[mrows, sl]) / jnp.sum(p, axis=-1, keepdims=True))
        outs.append(jnp.concatenate(heads, axis=1))
    return outs[0] if nseq == 1 else jnp.concatenate(outs, axis=0)


def _attend_packed(q, k_ref, v_ref, nseq, lq):
    scale = XA_HD ** -0.5
    rows, cols = XA_HEADS * lq, N_MEM * XA_HEADS
    own_head = (_iota2((rows, cols), 0) // lq) == (_iota2((rows, cols), 1) % XA_HEADS)
    outs = []
    for j in range(nseq):
        qh = _stack_heads(q[j * lq:(j + 1) * lq], XA_HEADS, XA_HD)
        mrows = slice(j * cols, (j + 1) * cols)
        s = jnp.where(own_head, _dot_nt(qh, k_ref[mrows, :]) * scale, -1e30)
        p = jnp.where(own_head, jnp.exp(s - jnp.max(s, axis=-1, keepdims=True)), 0.0)
        o = _dot(p, v_ref[mrows, :]) / jnp.sum(p, axis=-1, keepdims=True)
        outs.append(_unstack_heads(o, XA_HEADS))
    return jnp.concatenate(outs, axis=0)


def _xattn_body(*refs, attend, nseq, lq, n_mix):
    n_lead = n_mix + 1 if n_mix else 0
    h_ref, g_ref, wq_ref, k_ref, v_ref, wo_ref, o_ref = refs[n_lead:]
    x, k0 = h_ref[...], 0
    for y_ref in refs[:n_mix]:
        k1 = k0 + y_ref.shape[1]
        x = x + jnp.dot(y_ref[...], refs[n_mix][k0:k1, :], preferred_element_type=F32)
        k0 = k1
    xn = x * lax.rsqrt(jnp.mean(x * x, axis=-1, keepdims=True) + EPS) * g_ref[...]
    q = jnp.dot(xn.astype(BF16), wq_ref[...], preferred_element_type=F32)
    o = attend(q, k_ref, v_ref, nseq, lq)
    o_ref[...] = x + jnp.dot(o.astype(BF16), wo_ref[...], preferred_element_type=F32)


def _xattn(grp, h, mix, w_mix, gain, w_q, w_o, layer, mem_k, mem_v, mem_row0):
    rows, d = h.shape
    if grp.L >= XA_TQ:
        nseq, lq, attend = 1, XA_TQ, _attend
        per = grp.L // XA_TQ
        mem_spec = pl.BlockSpec((N_MEM, XA_DIM), lambda i: (mem_row0 // N_MEM + i // per, 0))
    else:
        nseq, lq, attend = MIX_ROWS // grp.L, grp.L, _attend_packed
        blk = nseq * N_MEM * XA_HEADS
        mem_spec = pl.BlockSpec((blk, XA_HD), lambda i: (mem_row0 * XA_HEADS // blk + i, 0))
    tile = nseq * lq
    h_spec = pl.BlockSpec((tile, d), lambda i: (i, 0))
    lead_ops, lead_specs = [], []
    if mix:
        lead_ops = list(mix) + [w_mix]
        lead_specs = [pl.BlockSpec((tile, y.shape[1]), lambda i: (i, 0)) for y in mix]
        lead_specs.append(pl.BlockSpec(w_mix.shape, lambda i: (0, 0)))
    return pl.pallas_call(
        functools.partial(_xattn_body, attend=attend, nseq=nseq, lq=lq, n_mix=len(mix)),
        grid=(rows // tile,),
        in_specs=lead_specs + [h_spec, pl.BlockSpec((1, d), lambda i: (0, 0)),
                               pl.BlockSpec((None, d, XA_DIM), lambda i: (layer, 0, 0)), mem_spec, mem_spec,
                               pl.BlockSpec((None, XA_DIM, d), lambda i: (layer, 0, 0))],
        out_specs=h_spec,
        out_shape=jax.ShapeDtypeStruct((rows, d), F32),
        compiler_params=_cp("parallel"),
        name="xattn",
    )(*lead_ops, h, gain.reshape(1, d), w_q, mem_k, mem_v, w_o)


def _ffn_in_body(*refs, grp, tiles_per_seq):
    it = iter(refs)
    a_ref, g_ref, wg_ref, wu_ref, hist_ref, cw_ref, cb_ref, act_ref, tail_ref, an_ref = (next(it) for _ in range(10))
    carry_ref = None if grp.embedded else next(it)
    i, j = pl.program_id(0), pl.program_id(1)

    @pl.when(j == 0)
    def _():
        x = a_ref[...]
        y = x * lax.rsqrt(jnp.mean(x * x, axis=-1, keepdims=True) + EPS)
        an_ref[...] = (y * g_ref[...]).astype(BF16)

    a = an_ref[...]
    gate = jnp.dot(a, wg_ref[...].astype(BF16), preferred_element_type=F32)
    up = jnp.dot(a, wu_ref[...].astype(BF16), preferred_element_type=F32)
    if grp.embedded:
        halo = hist_ref[...]
        tail_ref[...] = gate
    else:
        halo = jnp.where(i % tiles_per_seq == 0, hist_ref[...], carry_ref[j])
        last = gate[gate.shape[0] - SUBLANES:]
        carry_ref[j] = last
        tail_ref[...] = last
    conv = _conv(grp, gate, halo, cw_ref, cb_ref[...], FFN_CONV_W)
    act_ref[...] = (_silu(conv) * up).astype(act_ref.dtype)


def _ffn_in(grp, h, gain, w_in, layer, hist, conv_w, conv_b, tm=MM_TM, tn=MM_TN):
    assert grp.embedded or grp.L % tm == 0
    rows, k = h.shape
    nct = D_FF // tn
    tiles_per_seq = max(grp.L // tm, 1)
    hrows = tm if grp.embedded else SUBLANES
    tile_idx = lambda i, j: (i, j)
    hist_idx = tile_idx if grp.embedded else (lambda i, j: (i // tiles_per_seq, j))
    specs = [
        pl.BlockSpec((tm, k), lambda i, j: (i, 0)),
        pl.BlockSpec((1, k), lambda i, j: (0, 0)),
        _weight_spec(w_in, layer, k, tn, lambda j: j),
        _weight_spec(w_in, layer, k, tn, lambda j: nct + j),
        pl.BlockSpec((hrows, tn), hist_idx),
        pl.BlockSpec((FFN_CONV_W, tn), lambda i, j: (0, j)),
        pl.BlockSpec((1, tn), lambda i, j: (0, j)),
    ]
    scratch = [pltpu.VMEM((tm, k), BF16)]
    if not grp.embedded:
        scratch.append(pltpu.VMEM((nct, SUBLANES, tn), F32))
    act, tails = pl.pallas_call(
        functools.partial(_ffn_in_body, grp=grp, tiles_per_seq=tiles_per_seq),
        grid=(rows // tm, nct),
        in_specs=specs,
        out_specs=[pl.BlockSpec((tm, tn), tile_idx), pl.BlockSpec((hrows, tn), tile_idx)],
        out_shape=[jax.ShapeDtypeStruct((rows, D_FF), BF16), jax.ShapeDtypeStruct((rows // tm * hrows, D_FF), F32)],
        scratch_shapes=scratch,
        compiler_params=_cp("arbitrary", "arbitrary"),
        name="ffn_in",
    )(h, gain.reshape(1, k), w_in, w_in, hist, conv_w, conv_b.reshape(1, D_FF))
    if not grp.embedded:
        tails = tails.reshape(grp.B, tiles_per_seq, SUBLANES, D_FF)[:, -1].reshape(grp.B * SUBLANES, D_FF)
    return act, tails


def _history(grp, buf, width):
    b, k, c = buf.shape
    if grp.embedded:
        h = jnp.pad(buf, ((0, 0), (grp.first - k, grp.L - grp.first), (0, 0)))
        return h.reshape(b * grp.L, c)
    return jnp.pad(buf, ((0, 0), (SUBLANES - k, 0), (0, 0))).reshape(b * SUBLANES, c)


def _tail(grp, x2d, k, cols):
    x = x2d.reshape(grp.B, grp.L, x2d.shape[1])
    return x[:, grp.L - k:, cols].astype(F32)


def _run_group(grp, x2d, mem_k, mem_v, mem_row0, st, wts, prm):
    h = x2d
    out = {}
    tm_proj = min(MM_TM_PROJ, h.shape[0])
    proj = _matmul(h, wts["in_ab"], gain=prm["norm_mix"][0], out_dtype=BF16, tm=tm_proj, w_rows_are_outputs=True)
    y_rw, s_rw = _rwkv(grp, proj, _history(grp, st["rw_shift"][:, None, :], RW_PROJ), st["rwkv"], prm["rw"])
    y_ssd, s_ssd = _ssd(grp, proj, _history(grp, st["ssd_conv"], SSD_CONV_DIM),
                        st["ssd"].reshape(grp.B, SSD_GROUPS, SSD_GDIM, SSD_STATE), prm["ssd"])
    out["rwkv"] = s_rw
    out["rw_shift"] = _tail(grp, proj, 1, slice(AB_RW0, AB_RW0 + RW_PROJ))[:, 0]
    out["ssd"] = s_ssd.reshape(st["ssd"].shape)
    out["ssd_conv"] = _tail(grp, proj, CONV_W - 1, slice(AB_X0, AB_X0 + SSD_CONV_DIM))
    mix, w_mix = [y_rw, y_ssd], wts["out_ab"]
    ffn_bufs = []
    for l in range(2):
        if l == 1:
            proj = _matmul(h, wts["in_c"], gain=prm["norm_mix"][1], out_dtype=BF16, tm=tm_proj, w_rows_are_outputs=True)
            y_c, s_gdn = _gdn(grp, proj, _history(grp, st["gdn_conv"], GDN_CONV_DIM), st["gdn"], prm["gdn"])
            out["gdn"] = s_gdn
            out["gdn_conv"] = _tail(grp, proj, CONV_W - 1, slice(0, GDN_CONV_DIM))
            mix, w_mix = [y_c], wts["out_c"]
        if grp.embedded:
            h, mix = _matmul(mix, w_mix, res=h), []
        h = _xattn(grp, h, mix, w_mix, prm["norm_xa"][l], wts["xq"], wts["xo"], l, mem_k, mem_v, mem_row0[l])
        act, gate_tail = _ffn_in(grp, h, prm["norm_ffn"][l], wts["ffn_in"], l, _history(grp, st["ffn_conv"][l], D_FF),
                                 prm["ffn_conv_w"][l], prm["ffn_conv_b"][l])
        ffn_bufs.append(gate_tail.reshape(grp.B, SUBLANES, D_FF)[:, SUBLANES - (FFN_CONV_W - 1):])
        h = _matmul(act, wts["ffn_out"], layer=l, res=h)
    out["ffn_conv"] = jnp.stack(ffn_bufs)
    out["y"] = _rmsnorm(h, prm["norm_final"])
    return out


def kernel(x_prompt, x_sample, mem_prompt, state_rwkv, state_rwkv_shift, state_ssd, state_ssd_conv, state_gdn,
           state_gdn_conv, state_ffn_conv, cache_mem_k, cache_mem_v, norm_mix, norm_xa, norm_mem, norm_ffn,
           norm_final, w_in_ab, rw_mu, rw_w0, rw_w_up, rw_a0, rw_a_up, rw_g_up, rw_k_k, rw_k_a, rw_r_k, rw_gn_w,
           rw_gn_b, ssd_conv_w, ssd_conv_b, ssd_dt_bias, ssd_A_log, ssd_D, ssd_norm_w, w_out_ab, w_in_c,
           gdn_conv_w, gdn_A_log, gdn_dt_bias, gdn_norm_w, w_out_c, w_xq, w_xk, w_xv, w_xo, ffn_w_in, ffn_conv_w,
           ffn_conv_b, ffn_w_out):
    bp, lp, _ = x_prompt.shape
    bs, ls, _ = x_sample.shape
    depth = norm_mix.shape[0]
    assert depth == 2 and w_in_ab.shape[0] == 1 and w_in_c.shape[0] == 1
    assert lp % MIX_ROWS == 0 and lp % XA_TQ == 0 and lp % MM_TM == 0
    pad_rows = SUBLANES - ls
    assert CONV_W - 1 <= pad_rows and CONV_W - 1 <= ls and MIX_ROWS % SUBLANES == 0
    gp = Group(B=bp, L=lp, first=0, bb=1, nb=MIX_NB if bp % MIX_NB == 0 else 1)
    gs = Group(B=bs, L=SUBLANES, first=pad_rows, bb=MIX_ROWS // SUBLANES)
    assert bs % gs.bb == 0 and (bs * SUBLANES) % MM_TM == 0 and (bp * lp) % MM_TM == 0

    wts = dict(
        in_ab=_ab_layout(jnp.swapaxes(w_in_ab[0], 0, 1), axis=0).astype(BF16), out_ab=w_out_ab[0].astype(BF16),
        in_c=_gdn_layout(jnp.swapaxes(w_in_c[0], 0, 1), axis=0).astype(BF16), out_c=w_out_c[0].astype(BF16),
        xq=w_xq.astype(BF16), xo=w_xo.astype(BF16), ffn_in=ffn_w_in, ffn_out=ffn_w_out.astype(BF16),
    )
    prm = dict(
        norm_mix=norm_mix, norm_xa=norm_xa, norm_ffn=norm_ffn, norm_final=norm_final,
        ffn_conv_w=ffn_conv_w, ffn_conv_b=ffn_conv_b,
        rw=dict(mu=rw_mu[0], w0=rw_w0[0], w_up=rw_w_up[0], a0=rw_a0[0], a_up=rw_a_up[0], g_up=rw_g_up[0],
                k_k=rw_k_k[0], k_a=rw_k_a[0], r_k=rw_r_k[0], gn_w=rw_gn_w[0], gn_b=rw_gn_b[0]),
        ssd=dict(conv_w=ssd_conv_w[0], conv_b=ssd_conv_b[0], dt_bias=ssd_dt_bias[0], A_log=ssd_A_log[0],
                 D=ssd_D[0], norm_w=ssd_norm_w[0]),
        gdn=dict(conv_w=gdn_conv_w[0], A_log=gdn_A_log[0], dt_bias=gdn_dt_bias[0], norm_w=gdn_norm_w[0]),
    )

    mem2d = mem_prompt.reshape(bp * N_MEM, D_MODEL)
    mem_tm = min(MM_TM, bp * N_MEM)
    mk, mv = [], []
    for l in range(depth):
        mk.append(_matmul(mem2d, w_xk[l].astype(BF16), gain=norm_mem[l], tm=mem_tm))
        mv.append(_matmul(mem2d, w_xv[l].astype(BF16), gain=norm_mem[l], tm=mem_tm))
    mem_k_p = jnp.stack(mk)
    mem_v_p = jnp.stack(mv)

    zeros = lambda *s: jnp.zeros(s, F32)
    st_p = dict(
        rwkv=zeros(bp, RW_HEADS, RW_HD, RW_HD), rw_shift=zeros(bp, RW_PROJ),
        ssd=zeros(bp, SSD_GROUPS, SSD_HPG, SSD_HD, SSD_STATE), ssd_conv=zeros(bp, CONV_W - 1, SSD_CONV_DIM),
        gdn=zeros(bp, GDN_HEADS, GDN_D, GDN_D), gdn_conv=zeros(bp, CONV_W - 1, GDN_CONV_DIM),
        ffn_conv=zeros(depth, bp, FFN_CONV_W - 1, D_FF),
    )
    rp = _run_group(gp, x_prompt.reshape(bp * lp, D_MODEL), mem_k_p.reshape(depth * bp * N_MEM, XA_DIM),
                    mem_v_p.reshape(depth * bp * N_MEM, XA_DIM), [l * bp * N_MEM for l in range(depth)],
                    st_p, wts, prm)

    st_s = dict(rwkv=state_rwkv[0], rw_shift=state_rwkv_shift[0], ssd=state_ssd[0], ssd_conv=state_ssd_conv[0],
                gdn=state_gdn[0], gdn_conv=state_gdn_conv[0], ffn_conv=state_ffn_conv)
    xs = jnp.pad(x_sample, ((0, 0), (pad_rows, 0), (0, 0))).reshape(bs * SUBLANES, D_MODEL)
    rs = _run_group(gs, xs, cache_mem_k.reshape(depth * bs * N_MEM * XA_HEADS, XA_HD),
                    cache_mem_v.reshape(depth * bs * N_MEM * XA_HEADS, XA_HD), [l * bs * N_MEM for l in range(depth)],
                    st_s, wts, prm)

    y_p = rp["y"].reshape(bp, lp, D_MODEL)
    y_s = rs["y"].reshape(bs, SUBLANES, D_MODEL)[:, pad_rows:]
    lead = lambda x: x[None]
    mem_shape = (depth, bp, N_MEM, XA_HEADS, XA_HD)
    return (y_p, y_s, lead(rp["rwkv"]), lead(rs["rwkv"]), lead(rp["rw_shift"]), lead(rs["rw_shift"]),
            lead(rp["ssd"]), lead(rs["ssd"]), lead(rp["ssd_conv"]), lead(rs["ssd_conv"]),
            lead(rp["gdn"]), lead(rs["gdn"]), lead(rp["gdn_conv"]), lead(rs["gdn_conv"]),
            rp["ffn_conv"], rs["ffn_conv"], mem_k_p.reshape(mem_shape), mem_v_p.reshape(mem_shape))
```

```python
import functools
from typing import NamedTuple

import jax
import jax.numpy as jnp
from jax import lax
from jax.experimental import pallas as pl
from jax.experimental.pallas import tpu as pltpu

F32 = jnp.float32
BF16 = jnp.bfloat16
HI = lax.Precision.HIGHEST

D_MODEL = 2048
EPS = 1e-6
RW_HEADS, RW_HD = 16, 64
RW_DIM = RW_HEADS * RW_HD
RW_LORA = 256
RW_PROJ = 3 * RW_DIM + RW_LORA
RW_GN_EPS = 6.4e-4
SSD_HEADS, SSD_HD, SSD_GROUPS, SSD_STATE = 16, 64, 2, 128
SSD_HPG = SSD_HEADS // SSD_GROUPS
SSD_DIM = SSD_HEADS * SSD_HD
SSD_GDIM = SSD_HPG * SSD_HD
SSD_CONV_DIM = SSD_DIM + 2 * SSD_GROUPS * SSD_STATE
SSD_PROJ = SSD_DIM + SSD_CONV_DIM + SSD_HEADS
AB_PROJ = RW_PROJ + SSD_PROJ
GDN_HEADS, GDN_D = 16, 128
GDN_V = GDN_HEADS * GDN_D
GDN_CONV_DIM = 3 * GDN_V
GDN_PROJ = GDN_CONV_DIM + GDN_V + 2 * GDN_HEADS
D_FF = 5632
XA_HEADS, XA_HD, N_MEM = 4, 128, 256
XA_DIM = XA_HEADS * XA_HD
CONV_W = 4
FFN_CONV_W = 3

AB_Z0 = 0
AB_X0 = SSD_DIM
AB_B0 = 2 * SSD_DIM
AB_C0 = AB_B0 + SSD_GROUPS * SSD_STATE
AB_DT0 = AB_C0 + SSD_GROUPS * SSD_STATE
AB_RW0 = 3 * RW_DIM
AB_COLS = 6656
assert AB_DT0 + 128 <= AB_RW0 and AB_RW0 + RW_PROJ <= AB_COLS and AB_RW0 % RW_DIM == 0

SUBLANES = 8
LANES = 128
VMEM_LIMIT = 56 * 1024 * 1024

MIX_ROWS = 64
INV_BLOCK = 16
INV_SPLIT = 128
MIX_NB = 2
MM_TM, MM_TN = 1024, 512
MM_TM_PROJ = 2048
RW_HP = 2
RW_SUB = 4
GDN_G = 4
GDN_SUB = 4
XA_TQ = 512
NORM_TM = 512


class Group(NamedTuple):
    B: int
    L: int
    first: int
    bb: int
    nb: int = 1

    @property
    def outer(self):
        return self.B // self.bb

    @property
    def per_outer(self):
        return self.bb * self.L

    @property
    def qs(self):
        return MIX_ROWS // self.bb

    @property
    def nc(self):
        return self.L // self.qs

    @property
    def embedded(self):
        return self.first > 0


MIXER_FLAGS = None


def _cp(*sem, flags=None):
    return pltpu.CompilerParams(dimension_semantics=sem, vmem_limit_bytes=VMEM_LIMIT, flags=flags)


def _dot(a, b):
    return jnp.dot(a.astype(BF16), b.astype(BF16), preferred_element_type=F32)


def _dot_nt(a, b):
    return lax.dot_general(a.astype(BF16), b.astype(BF16), (((1,), (1,)), ((), ())), preferred_element_type=F32)


def _dot_tn(a, b):
    return lax.dot_general(a.astype(BF16), b.astype(BF16), (((0,), (0,)), ((), ())), preferred_element_type=F32)


def _dot_hi(a, b):
    return jnp.dot(a, b, preferred_element_type=F32, precision=HI)


def _split_bf16(x):
    hi = x.astype(BF16)
    return hi, (x - hi.astype(F32)).astype(BF16)


def _dot_sel(a, sel):
    ah, al = _split_bf16(a)
    sb = sel.astype(BF16)
    return jnp.dot(jnp.concatenate([ah, al], axis=1), jnp.concatenate([sb, sb], axis=0), preferred_element_type=F32)


def _dot_nt_hi(a, b):
    return lax.dot_general(a, b, (((1,), (1,)), ((), ())), preferred_element_type=F32, precision=HI)


def _sigmoid(x):
    return 1.0 / (1.0 + jnp.exp(-x))


def _silu(x):
    return x * _sigmoid(x)


def _softplus(x):
    return jnp.maximum(x, 0.0) + jnp.log(1.0 + jnp.exp(-jnp.abs(x)))


def _iota2(shape, axis):
    return lax.broadcasted_iota(jnp.int32, shape, axis)


class _Masks(NamedTuple):
    incl: jax.Array
    strict: jax.Array
    eye: jax.Array
    cumsum: jax.Array
    valid: jax.Array


def _masks(grp, heads=1):
    r = MIX_ROWS
    n = heads * r
    ri, ci = _iota2((n, n), 0), _iota2((n, n), 1)
    same = (ri // grp.qs) == (ci // grp.qs)
    incl = same & (ci <= ri)
    strict = same & (ci < ri)
    si, sj = _iota2((2 * r, r), 0), _iota2((2 * r, r), 1)
    same_seq = ((si % r) // grp.qs) == (sj // grp.qs)
    cumsum = (same_seq & ((sj <= si) | (si >= r))).astype(F32)
    t = _iota2((r, 1), 0) % grp.qs
    return _Masks(incl, strict, (ri == ci).astype(F32), cumsum, t >= grp.first)


def _neumann(lm, rhs, nil, eye=None):
    n = lm.shape[0]
    p = None
    if rhs is None:
        t = eye + lm
        if nil > 2:
            p = _dot(lm, lm)
            yield
    elif nil > 2:
        both = _dot(lm, jnp.concatenate([lm, rhs], axis=1))
        yield
        p, t = both[:, :n], rhs + both[:, n:]
    else:
        t = rhs + _dot(lm, rhs)
        yield
    k = 2
    while k < nil:
        if 2 * k < nil:
            both = _dot(p, jnp.concatenate([p, t], axis=1))
            p, t = both[:, :n], t + both[:, n:]
        else:
            t = t + _dot(p, t)
        yield
        k *= 2
    return t


def _inv_unit_lower(lm, eye, qs):
    if qs <= INV_BLOCK:
        return (yield from _neumann(lm, None, qs, eye))
    n = lm.shape[0]
    diag = (_iota2((n, n), 0) // INV_BLOCK) == (_iota2((n, n), 1) // INV_BLOCK)
    d = jnp.where(diag, lm, 0.0)
    dinv = yield from _neumann(d, None, INV_BLOCK, eye)
    nm = _dot(dinv, lm - d)
    yield
    return (yield from _neumann(nm, dinv, qs // INV_BLOCK))


def _lockstep(gens):
    results = [None] * len(gens)
    active = list(range(len(gens)))
    while active:
        for i in list(active):
            try:
                next(gens[i])
            except StopIteration as stop:
                results[i] = stop.value
                active.remove(i)
        if active:
            yield
    return results


def _solve_unit_lower(lm, rhs, eye, qs):
    n = lm.shape[0]
    if n > INV_SPLIT:
        blocks = [slice(i, i + INV_SPLIT) for i in range(0, n, INV_SPLIT)]
        inv = yield from _lockstep([_inv_unit_lower(lm[b, b], eye[b, b], qs) for b in blocks])
        zero = jnp.zeros((INV_SPLIT, INV_SPLIT), F32)
        t = jnp.concatenate([jnp.concatenate([inv[i] if i == j else zero for j in range(len(blocks))], axis=1)
                             for i in range(len(blocks))], axis=0)
    else:
        t = yield from _inv_unit_lower(lm, eye, qs)
    x = _dot(t, rhs)
    yield
    return x


def _interleave(gens):
    results = [None] * len(gens)
    active = list(range(len(gens)))
    while active:
        for i in list(active):
            try:
                next(gens[i])
            except StopIteration as stop:
                results[i] = stop.value
                active.remove(i)
    return results


def _stack_heads(x, heads, width):
    return jnp.concatenate([x[:, h * width:(h + 1) * width] for h in range(heads)], axis=0)


def _unstack_heads(x, heads):
    r = x.shape[0] // heads
    return jnp.concatenate([x[h * r:(h + 1) * r] for h in range(heads)], axis=1)


def _seq_rows(grp, x, j, heads):
    if grp.bb == 1:
        return x
    qs = grp.qs
    return jnp.concatenate([x[h * MIX_ROWS + j * qs:h * MIX_ROWS + (j + 1) * qs] for h in range(heads)], axis=0)


def _from_seq_rows(grp, parts, heads):
    if grp.bb == 1:
        return parts[0]
    qs = grp.qs
    return jnp.concatenate([p[h * qs:(h + 1) * qs] for h in range(heads) for p in parts], axis=0)


def _expand_rows(x, heads, rows_per_head, width):
    m = x.shape[0]
    if x.shape[1] == width:
        x = jnp.concatenate([x] * heads, axis=1)
    keep = ((_iota2((m, heads * width), 0) // rows_per_head) % heads) == (_iota2((m, heads * width), 1) // width)
    return jnp.where(keep, x, 0.0)


def _taps(grp, cur, halo, width):
    if grp.embedded:
        t = _iota2((cur.shape[0], 1), 0) % grp.qs
        full = jnp.where(t < grp.first, halo, cur)
        return [full] + [pltpu.roll(full, s, axis=0) for s in range(1, width)]
    full = jnp.concatenate([halo, cur], axis=0)
    return [cur] + [pltpu.roll(full, s, axis=0)[SUBLANES:] for s in range(1, width)]


def _conv(grp, cur, halo, w_ref, bias, width):
    taps = _taps(grp, cur, halo, width)
    y = taps[0] * w_ref[width - 1:width, :]
    for s in range(1, width):
        y = y + taps[s] * w_ref[width - 1 - s:width - s, :]
    return y if bias is None else y + bias


class _Tok(NamedTuple):
    arr: jax.Array
    width: int
    col: object
    hist: jax.Array = None
    hcol: object = None


def _view3(grp, x2d):
    return x2d.reshape(grp.outer, grp.per_outer, x2d.shape[1])


def _mixer_specs(grp, toks):
    nb = grp.nb
    ops, specs = [], []
    for t in toks:
        ops.append(_view3(grp, t.arr))
        specs.append(pl.BlockSpec((nb, MIX_ROWS, t.width), lambda b, g, c, t=t: (b, c, t.col(g))))
    for t in toks:
        if t.hist is None:
            continue
        if grp.embedded:
            ops.append(_view3(grp, t.hist))
            specs.append(pl.BlockSpec((nb, MIX_ROWS, t.width), lambda b, g, c, t=t: (b, 0, t.hcol(g))))
            continue
        tile = _sublane_tile(t.arr.dtype)
        sub = MIX_ROWS // tile
        ops.append(_view3(grp, t.arr))
        specs.append(pl.BlockSpec(
            (nb, tile, t.width), lambda b, g, c, t=t, sub=sub: (b, jnp.maximum(c * sub - 1, 0), t.col(g))))
        ops.append(t.hist.reshape(grp.B, SUBLANES, t.hist.shape[1]))
        specs.append(pl.BlockSpec((nb, SUBLANES, t.width), lambda b, g, c, t=t: (b, 0, t.hcol(g))))
    return ops, specs


def _read_windows(grp, refs, n_plain, n_hist):
    c = pl.program_id(2)
    nb = grp.nb
    cur = [[r[s].astype(F32) for s in range(nb)] for r in refs[:n_plain]]
    rest = refs[n_plain:]
    halos = []
    for i in range(n_hist):
        if grp.embedded:
            halos.append([rest[i][s] for s in range(nb)])
        else:
            per = []
            for s in range(nb):
                prev = rest[2 * i][s]
                prev = prev[prev.shape[0] - SUBLANES:].astype(F32)
                per.append(jnp.where(c == 0, rest[2 * i + 1][s], prev))
            halos.append(per)
    used = n_hist if grp.embedded else 2 * n_hist
    return cur, halos, rest[used:]


def _sublane_tile(dtype):
    return SUBLANES * 4 // jnp.dtype(dtype).itemsize


def _param_spec(width, col):
    return lambda rows: pl.BlockSpec((rows, width), lambda b, g, c: (0, col(g)))


def _matmul_body(*refs, norm, has_res, n_a, w_rows_are_outputs):
    it = iter(refs)
    a_refs = [next(it) for _ in range(n_a)]
    g_ref = next(it) if norm else None
    w_ref = next(it)
    res_ref = next(it) if has_res else None
    o_ref = next(it)
    if norm:
        an_ref = next(it)

        @pl.when(pl.program_id(1) == 0)
        def _():
            x = a_refs[0][...]
            y = x * lax.rsqrt(jnp.mean(x * x, axis=-1, keepdims=True) + EPS)
            an_ref[...] = (y * g_ref[...]).astype(BF16)

        if w_rows_are_outputs:
            acc = lax.dot_general(an_ref[...], w_ref[...], (((1,), (1,)), ((), ())), preferred_element_type=F32)
        else:
            acc = jnp.dot(an_ref[...], w_ref[...], preferred_element_type=F32)
    else:
        acc, k0 = None, 0
        for a_ref in a_refs:
            k1 = k0 + a_ref.shape[1]
            part = jnp.dot(a_ref[...], w_ref[k0:k1, :], preferred_element_type=F32)
            acc = part if acc is None else acc + part
            k0 = k1
    if has_res:
        acc = acc + res_ref[...]
    o_ref[...] = acc.astype(o_ref.dtype)


def _matmul(a, w, *, layer=None, gain=None, res=None, out_dtype=F32, tm=MM_TM, tn=MM_TN, w_rows_are_outputs=False):
    a_list = list(a) if isinstance(a, (list, tuple)) else [a]
    m = a_list[0].shape[0]
    k, n = w.shape[-2:][::-1] if w_rows_are_outputs else w.shape[-2:]
    assert not w_rows_are_outputs or (gain is not None and w.ndim == 2)
    assert sum(x.shape[1] for x in a_list) == k
    norm = gain is not None
    assert not norm or len(a_list) == 1
    ops = list(a_list)
    specs = [pl.BlockSpec((tm, x.shape[1]), lambda i, j: (i, 0)) for x in a_list]
    if norm:
        ops.append(gain.reshape(1, k))
        specs.append(pl.BlockSpec((1, k), lambda i, j: (0, 0)))
    ops.append(w)
    if w_rows_are_outputs:
        specs.append(pl.BlockSpec((tn, k), lambda i, j: (j, 0)))
    else:
        specs.append(_weight_spec(w, layer, k, tn, lambda j: j))
    if res is not None:
        ops.append(res)
        specs.append(pl.BlockSpec((tm, tn), lambda i, j: (i, j)))
    return pl.pallas_call(
        functools.partial(_matmul_body, norm=norm, has_res=res is not None, n_a=len(a_list),
                          w_rows_are_outputs=w_rows_are_outputs),
        grid=(m // tm, n // tn),
        in_specs=specs,
        out_specs=pl.BlockSpec((tm, tn), lambda i, j: (i, j)),
        out_shape=jax.ShapeDtypeStruct((m, n), out_dtype),
        scratch_shapes=[pltpu.VMEM((tm, k), BF16)] if norm else [],
        compiler_params=_cp("parallel", "arbitrary"),
        name="matmul",
    )(*ops)


def _weight_spec(w, layer, k, tn, col):
    if w.ndim == 2:
        return pl.BlockSpec((k, tn), lambda i, j: (0, col(j)))
    return pl.BlockSpec((None, k, tn), lambda i, j: (layer, 0, col(j)))


def _rmsnorm_body(x_ref, g_ref, o_ref):
    x = x_ref[...]
    y = x * lax.rsqrt(jnp.mean(x * x, axis=-1, keepdims=True) + EPS)
    o_ref[...] = (y * g_ref[...]).astype(o_ref.dtype)


def _rmsnorm(x, gain, out_dtype=F32, tm=NORM_TM):
    m, k = x.shape
    return pl.pallas_call(
        _rmsnorm_body,
        grid=(m // tm,),
        in_specs=[pl.BlockSpec((tm, k), lambda i: (i, 0)), pl.BlockSpec((1, k), lambda i: (0, 0))],
        out_specs=pl.BlockSpec((tm, k), lambda i: (i, 0)),
        out_shape=jax.ShapeDtypeStruct((m, k), out_dtype),
        compiler_params=_cp("parallel"),
        name="rmsnorm",
    )(x, gain.reshape(1, k))


def _rwkv_body(*refs, grp):
    r_rows = MIX_ROWS
    qs, bb, nc, nb = grp.qs, grp.bb, grp.nc, grp.nb
    nh = 2 * RW_HP
    sw = nh * RW_HD
    width = RW_SUB * sw
    n = nh * r_rows
    cur, halos, rest = _read_windows(grp, refs, 4, 4)
    (mu_r, mu_k, mu_v, mu_lo, w0_ref, a0_ref, lora_ref, kk_ref, ka_ref, rk_ref, gnw_ref, gnb_ref,
     s0_ref, y_ref, s_ref, sbig_ref) = rest
    c = pl.program_id(2)

    @pl.when(c == 0)
    def _():
        for j in range(nb * bb):
            for sub in range(RW_SUB):
                st = jnp.concatenate([s0_ref[j, sub * nh + h] for h in range(nh)], axis=0)
                sbig_ref[j, sub] = _expand_rows(st, nh, RW_HD, RW_HD)

    m = _masks(grp, nh)
    bd = ((_iota2((sw, sw), 0) // RW_HD) == (_iota2((sw, sw), 1) // RW_HD)).astype(F32)

    def head_sums(x):
        return _unstack_heads(_dot_sel(_stack_heads(x, RW_SUB, sw), bd), RW_SUB)

    def block(s):
        def shifted(i, mu_ref):
            x = cur[i][s]
            prev = _taps(grp, x, halos[i][s], 2)[1]
            return x + (prev - x) * mu_ref[...]

        r = shifted(0, mu_r)
        k = shifted(1, mu_k)
        v = shifted(2, mu_v)
        lo = shifted(3, mu_lo)
        lora_in = jnp.concatenate([jnp.tanh(lo[:, :64]), lo[:, 64:128], _sigmoid(lo[:, 128:256])], axis=1)
        lora = _dot(lora_in, lora_ref[0])
        w = -_softplus(-(w0_ref[...] + lora[:, :width])) - 0.5
        lw = jnp.where(m.valid, -jnp.exp(w), 0.0)
        a = _sigmoid(a0_ref[...] + lora[:, width:2 * width])
        g = lora[:, 2 * width:]

        kx = k * kk_ref[...]
        kp = jnp.where(m.valid, k * (1.0 + (a - 1.0) * ka_ref[...]), 0.0)
        sums = head_sums(jnp.concatenate([kx * kx, r * kp * rk_ref[...]], axis=0))
        kkn = kx * lax.rsqrt(sums[:r_rows] + 1e-6)
        bonus = sums[r_rows:] * v
        at = jnp.where(m.valid, -kkn, 0.0)
        bt = jnp.where(m.valid, kkn * a, 0.0)

        cums = _dot_hi(m.cumsum, lw)
        cum, cum_end = cums[:r_rows], cums[r_rows:]
        e_neg = jnp.exp(-cum)
        e_end = jnp.exp(cum_end - cum)
        p_end = jnp.exp(cum_end)
        a_til, r_til = at * jnp.exp(cum - lw), r * jnp.exp(cum)
        b_til, k_til = bt * e_neg, kp * e_neg
        b_hat, k_hat = bt * e_end, kp * e_end

        def stack(sub):
            lanes = slice(sub * sw, (sub + 1) * sw)

            def tile(x):
                return _expand_rows(jnp.concatenate([x[:, lanes]] * nh, axis=0), nh, r_rows, RW_HD)

            ar = jnp.concatenate([tile(a_til), tile(r_til)], axis=0)
            bk = jnp.concatenate([tile(b_til), tile(k_til)], axis=0)
            bk_end = jnp.concatenate([tile(b_hat), tile(k_hat)], axis=0)
            v_exp = tile(v)
            gm = _dot_nt(ar, bk)
            yield
            m_ab = jnp.where(m.strict, gm[:n, :n], 0.0)
            m_ak = jnp.where(m.strict, gm[:n, n:], 0.0)
            m_r = jnp.concatenate([jnp.where(m.incl, gm[n:, :n], 0.0), jnp.where(m.incl, gm[n:, n:], 0.0)], axis=1)

            def seq2(x, j):
                if bb == 1:
                    return x
                return jnp.concatenate([_seq_rows(grp, x[:n], j, nh), _seq_rows(grp, x[n:], j, nh)], axis=0)

            parts = []
            for j in range(bb):
                parts.append(_dot_nt(seq2(ar, j), sbig_ref[s * bb + j, sub]))
                yield
            half = nh * qs
            as_a = _from_seq_rows(grp, [p[:half] for p in parts], nh)
            as_r = _from_seq_rows(grp, [p[half:] for p in parts], nh)
            rhs = as_a + _dot(m_ak, v_exp)
            yield
            u = yield from _solve_unit_lower(m_ab, rhs, m.eye, qs)
            uv = jnp.concatenate([u, v_exp], axis=0)
            y_exp = as_r + _dot(m_r, uv)
            yield
            y = y_exp[:r_rows]
            for h in range(1, nh):
                y = y + y_exp[h * r_rows:(h + 1) * r_rows]
            for j in range(bb):
                sbig_ref[s * bb + j, sub] = (sbig_ref[s * bb + j, sub] * p_end[j * qs:j * qs + 1, lanes]
                                             + _dot_tn(seq2(uv, j), seq2(bk_end, j)))
                yield
            return y

        def finish(ys):
            y = jnp.concatenate(ys, axis=1)
            mean = head_sums(y) * (1.0 / RW_HD)
            d = y - mean
            var = head_sums(d * d) * (1.0 / RW_HD)
            out = d * lax.rsqrt(var + RW_GN_EPS) * gnw_ref[...] + gnb_ref[...] + bonus
            y_ref[s] = (out * g).astype(y_ref.dtype)

        return [stack(sub) for sub in range(RW_SUB)], finish

    blocks = [block(s) for s in range(nb)]
    results = _interleave([chain for chains, _ in blocks for chain in chains])
    for s, (_, finish) in enumerate(blocks):
        finish(results[s * RW_SUB:(s + 1) * RW_SUB])

    @pl.when(c == nc - 1)
    def _():
        for j in range(nb * bb):
            for sub in range(RW_SUB):
                sb = sbig_ref[j, sub]
                for h in range(nh):
                    s_ref[j, sub * nh + h] = sb[h * RW_HD:(h + 1) * RW_HD, h * RW_HD:(h + 1) * RW_HD]


def _rwkv(grp, proj, hist, s0, prm):
    width = RW_SUB * RW_HP * LANES
    ng = RW_DIM // width
    lora_blk = 3 * RW_DIM // RW_LORA
    p0 = AB_RW0 // width
    toks = [
        _Tok(proj, width, lambda g: p0 + g, hist, lambda g: g),
        _Tok(proj, width, lambda g: p0 + ng + g, hist, lambda g: ng + g),
        _Tok(proj, width, lambda g: p0 + 2 * ng + g, hist, lambda g: 2 * ng + g),
        _Tok(proj, RW_LORA, lambda g: AB_RW0 // RW_LORA + lora_blk, hist, lambda g: lora_blk),
    ]
    ops, specs = _mixer_specs(grp, toks)
    head = _param_spec(width, lambda g: g)
    lora_w = jnp.zeros((ng, RW_LORA, 3 * width), F32)
    for i, (name, r0, r1) in enumerate((("w_up", 0, 64), ("a_up", 64, 128), ("g_up", 128, 256))):
        blk = prm[name].reshape(r1 - r0, ng, width).transpose(1, 0, 2)
        lora_w = lora_w.at[:, r0:r1, i * width:(i + 1) * width].set(blk)
    mu = prm["mu"].reshape(1, RW_PROJ)
    flat = lambda x: x.reshape(1, RW_DIM)
    params = [
        (mu, pl.BlockSpec((1, width), lambda b, g, c: (0, g))),
        (mu, pl.BlockSpec((1, width), lambda b, g, c: (0, ng + g))),
        (mu, pl.BlockSpec((1, width), lambda b, g, c: (0, 2 * ng + g))),
        (mu, pl.BlockSpec((1, RW_LORA), lambda b, g, c: (0, lora_blk))),
        (flat(prm["w0"]), head(1)), (flat(prm["a0"]), head(1)),
        (lora_w, pl.BlockSpec((1, RW_LORA, 3 * width), lambda b, g, c: (g, 0, 0))),
        (flat(prm["k_k"]), head(1)), (flat(prm["k_a"]), head(1)), (flat(prm["r_k"]), head(1)),
        (flat(prm["gn_w"]), head(1)), (flat(prm["gn_b"]), head(1)),
    ]
    ops += [p for p, _ in params]
    specs += [s for _, s in params]
    seqs = grp.nb * grp.bb
    st_spec = pl.BlockSpec((seqs, RW_SUB * 2 * RW_HP, RW_HD, RW_HD), lambda b, g, c: (b, g, 0, 0))
    ops.append(s0)
    specs.append(st_spec)
    y, s_out = pl.pallas_call(
        functools.partial(_rwkv_body, grp=grp),
        grid=(grp.outer // grp.nb, ng, grp.nc),
        in_specs=specs,
        out_specs=[pl.BlockSpec((grp.nb, MIX_ROWS, width), lambda b, g, c: (b, c, g)), st_spec],
        out_shape=[jax.ShapeDtypeStruct((grp.outer, grp.per_outer, RW_DIM), BF16),
                   jax.ShapeDtypeStruct(s0.shape, F32)],
        scratch_shapes=[pltpu.VMEM((seqs, RW_SUB, RW_HP * LANES, RW_HP * LANES), F32)],
        compiler_params=_cp("parallel", "parallel", "arbitrary", flags=MIXER_FLAGS),
        name="rwkv7",
    )(*ops)
    return y.reshape(grp.B * grp.L, RW_DIM), s_out


def _ssd_body(*refs, grp):
    r_rows = MIX_ROWS
    qs, bb, nb = grp.qs, grp.bb, grp.nb
    nh, hd, gdim = SSD_HPG, SSD_HD, SSD_GDIM
    cur, halos, rest = _read_windows(grp, refs, 5, 3)
    cw_x, cw_b, cw_c, cb_x, cb_b, cb_c, dtb_ref, alog_ref, dskip_ref, nw_ref, h0_ref, y_ref, h_ref = rest
    c = pl.program_id(2)

    @pl.when(c == 0)
    def _():
        h_ref[...] = h0_ref[...]

    m = _masks(grp)
    eye_h = (_iota2((SSD_HEADS, SSD_HEADS), 0) == _iota2((SSD_HEADS, SSD_HEADS), 1)).astype(F32)
    ei, ej = _iota2((3 * SSD_HEADS, 3 * SSD_DIM), 0), _iota2((3 * SSD_HEADS, 3 * SSD_DIM), 1)
    spread = ((ei // SSD_HEADS == ej // SSD_DIM) & (ei % SSD_HEADS == (ej % SSD_DIM) // hd)).astype(F32)

    def block(s):
        x_raw, b_raw, c_raw, z, dt_blk = (cur[i][s] for i in range(5))
        xs = _silu(_conv(grp, x_raw, halos[0][s], cw_x, cb_x[...], CONV_W))
        bm = _silu(_conv(grp, b_raw, halos[1][s], cw_b, cb_b[...], CONV_W))
        cm = _silu(_conv(grp, c_raw, halos[2][s], cw_c, cb_c[...], CONV_W))

        dt = jnp.where(m.valid, _softplus(dt_blk[:, :SSD_HEADS] + dtb_ref[...]), 0.0)
        cums = _dot_hi(m.cumsum, dt * (-jnp.exp(alog_ref[...])))
        acs, acs_end = cums[:r_rows], cums[r_rows:]
        acs_row = _dot_nt_hi(eye_h, acs)
        lanes = _dot_hi(jnp.concatenate([dt, acs_end - acs, acs], axis=1), spread)
        xd = xs * lanes[:, :SSD_DIM]
        xd_dec = xd * jnp.exp(lanes[:, SSD_DIM:2 * SSD_DIM])
        acs_lanes = lanes[:, 2 * SSD_DIM:]

        def group(g):
            gl = slice(g * gdim, (g + 1) * gdim)
            bmg, cmg = bm[:, g * SSD_STATE:(g + 1) * SSD_STATE], cm[:, g * SSD_STATE:(g + 1) * SSD_STATE]
            cb = _dot_nt(cmg, bmg)
            yield
            yd = []
            for r in range(nh):
                h = g * nh + r
                diff = acs[:, h:h + 1] - acs_row[h:h + 1, :]
                lmat = jnp.where(m.incl, jnp.exp(jnp.where(m.incl, diff, 0.0)), 0.0)
                yd.append(_dot(cb * lmat, xd[:, h * hd:(h + 1) * hd]))
                yield
            y = jnp.concatenate(yd, axis=1)
            head_rows = (_iota2((gdim, SSD_HEADS), 0) // hd + g * nh == _iota2((gdim, SSD_HEADS), 1)).astype(F32)
            yoff = []
            for j in range(bb):
                rows = slice(j * qs, (j + 1) * qs)
                hj = h_ref[s * bb + j, g]
                yoff.append(_dot_nt(cmg[rows], hj))
                yield
                end_col = jnp.exp(_dot_nt_hi(head_rows, acs_end[j * qs:j * qs + SUBLANES]))[:, :1]
                yield
                h_ref[s * bb + j, g] = hj * end_col + _dot_tn(xd_dec[rows, gl], bmg[rows])
                yield
            yoff = yoff[0] if bb == 1 else jnp.concatenate(yoff, axis=0)
            y = y + yoff * jnp.exp(acs_lanes[:, gl]) + xs[:, gl] * dskip_ref[:, gl]
            yg = y * _silu(z[:, gl])
            yg = yg * lax.rsqrt(jnp.mean(yg * yg, axis=-1, keepdims=True) + EPS)
            return yg * nw_ref[:, gl]

        return [group(g) for g in range(SSD_GROUPS)]

    results = _interleave([chain for s in range(nb) for chain in block(s)])
    for s in range(nb):
        y_ref[s] = jnp.concatenate(results[s * SSD_GROUPS:(s + 1) * SSD_GROUPS], axis=1).astype(y_ref.dtype)


def _ssd(grp, proj, hist, h0, prm):
    bc = SSD_GROUPS * SSD_STATE
    toks = [
        _Tok(proj, SSD_DIM, lambda g: AB_X0 // SSD_DIM, hist, lambda g: 0),
        _Tok(proj, bc, lambda g: AB_B0 // bc, hist, lambda g: SSD_DIM // bc),
        _Tok(proj, bc, lambda g: AB_C0 // bc, hist, lambda g: SSD_DIM // bc + 1),
        _Tok(proj, SSD_DIM, lambda g: AB_Z0 // SSD_DIM),
        _Tok(proj, LANES, lambda g: AB_DT0 // LANES),
    ]
    ops, specs = _mixer_specs(grp, toks)
    cw, cbias = prm["conv_w"], prm["conv_b"].reshape(1, SSD_CONV_DIM)
    conv_cols = [(SSD_DIM, lambda g: 0), (bc, lambda g: SSD_DIM // bc), (bc, lambda g: SSD_DIM // bc + 1)]
    params = [(cw, _param_spec(wd, col)(CONV_W)) for wd, col in conv_cols]
    params += [(cbias, _param_spec(wd, col)(1)) for wd, col in conv_cols]
    full = lambda n: pl.BlockSpec((1, n), lambda b, g, c: (0, 0))
    params += [
        (prm["dt_bias"].reshape(1, SSD_HEADS), full(SSD_HEADS)),
        (prm["A_log"].reshape(1, SSD_HEADS), full(SSD_HEADS)),
        (jnp.repeat(prm["D"], SSD_HD).reshape(1, SSD_DIM), full(SSD_DIM)),
        (prm["norm_w"].reshape(1, SSD_DIM), full(SSD_DIM)),
    ]
    ops += [p for p, _ in params]
    specs += [s for _, s in params]
    st_spec = pl.BlockSpec((grp.nb * grp.bb, SSD_GROUPS, SSD_GDIM, SSD_STATE), lambda b, g, c: (b, 0, 0, 0))
    ops.append(h0)
    specs.append(st_spec)
    y, h_out = pl.pallas_call(
        functools.partial(_ssd_body, grp=grp),
        grid=(grp.outer // grp.nb, 1, grp.nc),
        in_specs=specs,
        out_specs=[pl.BlockSpec((grp.nb, MIX_ROWS, SSD_DIM), lambda b, g, c: (b, c, 0)), st_spec],
        out_shape=[jax.ShapeDtypeStruct((grp.outer, grp.per_outer, SSD_DIM), BF16),
                   jax.ShapeDtypeStruct(h0.shape, F32)],
        compiler_params=_cp("parallel", "arbitrary", "arbitrary"),
        name="ssd",
    )(*ops)
    return y.reshape(grp.B * grp.L, SSD_DIM), h_out


def _gdn_body(*refs, grp):
    r_rows = MIX_ROWS
    qs, bb, nb = grp.qs, grp.bb, grp.nb
    nh = GDN_G
    n = nh * r_rows
    cur, halos, rest = _read_windows(grp, refs, 5, 3)
    cw_q, cw_k, cw_v, alog_ref, dtb_ref, nw_ref, s0_ref, o_ref, s_ref = rest
    c = pl.program_id(2)

    @pl.when(c == 0)
    def _():
        s_ref[...] = s0_ref[...]

    m = _masks(grp, nh)
    per_step = nh * GDN_SUB
    eye_g = (_iota2((SUBLANES, nh), 0) == _iota2((SUBLANES, nh), 1)).astype(F32)
    col = lambda x: jnp.concatenate([x[:, h:h + 1] for h in range(nh)], axis=0)

    def block(s):
        q_raw, k_raw, v_raw, z, ba = (cur[i][s] for i in range(5))
        q_all = _silu(_conv(grp, q_raw, halos[0][s], cw_q, None, CONV_W))
        k_all = _silu(_conv(grp, k_raw, halos[1][s], cw_k, None, CONV_W))
        v_all = _silu(_conv(grp, v_raw, halos[2][s], cw_v, None, CONV_W))

        def stack(sub):
            lanes = slice(sub * nh * GDN_D, (sub + 1) * nh * GDN_D)
            heads = slice(sub * nh, (sub + 1) * nh)
            q = _stack_heads(q_all[:, lanes], nh, GDN_D)
            k = _stack_heads(k_all[:, lanes], nh, GDN_D)
            v = _stack_heads(v_all[:, lanes], nh, GDN_D)
            q = q * lax.rsqrt(jnp.sum(q * q, axis=-1, keepdims=True) + 1e-6) * (GDN_D ** -0.5)
            k = k * lax.rsqrt(jnp.sum(k * k, axis=-1, keepdims=True) + 1e-6)

            beta = jnp.where(m.valid, _sigmoid(ba[:, heads]), 0.0)
            a_raw = ba[:, per_step + sub * nh:per_step + (sub + 1) * nh]
            gg = jnp.where(m.valid, -jnp.exp(alog_ref[0][:, heads]) * _softplus(a_raw + dtb_ref[0][:, heads]), 0.0)
            gcs = _dot_hi(m.cumsum, gg)
            yield
            gc, gc_end = gcs[:r_rows], gcs[r_rows:]
            gc_row = _dot_nt_hi(eye_g, gc)
            yield

            bcol, gcol, gend = col(beta), col(gc), col(gc_end)
            grow = jnp.concatenate([gc_row[h:h + 1, :] for h in range(nh)], axis=1)
            dec = jnp.where(m.incl, jnp.exp(jnp.where(m.incl, gcol - grow, 0.0)), 0.0)
            kb = k * bcol
            kq = _dot_nt(jnp.concatenate([kb, q], axis=0), k)
            yield
            lm = -jnp.where(m.strict, kq[:n] * dec, 0.0)
            attn = jnp.where(m.incl, kq[n:] * dec, 0.0)
            eg = jnp.exp(gcol)
            sol = yield from _solve_unit_lower(lm, jnp.concatenate([v * bcol, kb * eg], axis=1), m.eye, qs)
            vw, kcd = sol[:, :GDN_D], sol[:, GDN_D:]
            qg = q * eg
            kg = k * jnp.exp(gend - gcol)

            def state(j):
                return jnp.concatenate([s_ref[s * bb + j, sub * nh + h] for h in range(nh)], axis=0)

            if bb == 1:
                parts = []
                for h in range(nh):
                    rows = slice(h * r_rows, (h + 1) * r_rows)
                    parts.append(_dot(jnp.concatenate([kcd[rows], qg[rows]], axis=0), s_ref[s, sub * nh + h]))
                    yield
                ks = jnp.concatenate([p[:r_rows] for p in parts], axis=0)
                qsv = jnp.concatenate([p[r_rows:] for p in parts], axis=0)
            else:
                parts = []
                for j in range(bb):
                    lhs = jnp.concatenate([_seq_rows(grp, kcd, j, nh), _seq_rows(grp, qg, j, nh)], axis=0)
                    parts.append(_dot(_expand_rows(lhs, nh, qs, GDN_D), state(j)))
                    yield
                half = nh * qs
                ks = _from_seq_rows(grp, [p[:half] for p in parts], nh)
                qsv = _from_seq_rows(grp, [p[half:] for p in parts], nh)
            v_new = vw - ks
            o = qsv + _dot(attn, v_new)
            yield
            for j in range(bb):
                kgj = _expand_rows(_seq_rows(grp, kg, j, nh), nh, qs, GDN_D)
                g_last = jnp.exp(gc_end[j * qs:j * qs + 1, :])
                g_last = jnp.concatenate([jnp.broadcast_to(g_last[:, h:h + 1], (GDN_D, 1)) for h in range(nh)], axis=0)
                s_new = state(j) * g_last + _dot_tn(kgj, _seq_rows(grp, v_new, j, nh))
                yield
                for h in range(nh):
                    s_ref[s * bb + j, sub * nh + h] = s_new[h * GDN_D:(h + 1) * GDN_D]
            o = o * lax.rsqrt(jnp.mean(o * o, axis=-1, keepdims=True) + EPS) * nw_ref[...]
            return _unstack_heads(o, nh)

        def finish(os_):
            o_ref[s] = (jnp.concatenate(os_, axis=1) * _silu(z)).astype(o_ref.dtype)

        return [stack(sub) for sub in range(GDN_SUB)], finish

    blocks = [block(s) for s in range(nb)]
    results = _interleave([chain for chains, _ in blocks for chain in chains])
    for s, (_, finish) in enumerate(blocks):
        finish(results[s * GDN_SUB:(s + 1) * GDN_SUB])


def _zeros_like_axis(x, n, axis):
    shape = list(x.shape)
    shape[axis] = n
    return jnp.zeros(shape, x.dtype)


def _ab_layout(x, axis=-1):
    cut = lambda a, b: lax.slice_in_dim(x, a, b, axis=axis)
    zeros = lambda n: _zeros_like_axis(x, n, axis)
    rw, z = cut(0, RW_PROJ), cut(RW_PROJ, RW_PROJ + SSD_DIM)
    xbc = cut(RW_PROJ + SSD_DIM, RW_PROJ + SSD_DIM + SSD_CONV_DIM)
    dt = cut(RW_PROJ + SSD_DIM + SSD_CONV_DIM, AB_PROJ)
    return jnp.concatenate([z, xbc, dt, zeros(AB_RW0 - AB_DT0 - SSD_HEADS), rw, zeros(AB_COLS - AB_RW0 - RW_PROJ)],
                           axis=axis)


def _gdn_layout(x, axis=-1):
    per_step = GDN_G * GDN_SUB
    main = GDN_CONV_DIM + GDN_V
    cut = lambda a, b: lax.slice_in_dim(x, a, b, axis=axis)
    parts = [cut(0, main)]
    for g in range(GDN_HEADS // per_step):
        h0, h1 = g * per_step, (g + 1) * per_step
        parts += [cut(main + h0, main + h1), cut(main + GDN_HEADS + h0, main + GDN_HEADS + h1),
                  _zeros_like_axis(x, LANES - 2 * per_step, axis)]
    total = main + (GDN_HEADS // per_step) * LANES
    parts.append(_zeros_like_axis(x, -total % MM_TN, axis))
    return jnp.concatenate(parts, axis=axis)


def _gdn(grp, proj, hist, s0, prm):
    per_step = GDN_G * GDN_SUB
    width = per_step * GDN_D
    ng = GDN_HEADS // per_step
    ba_blk = (GDN_CONV_DIM + GDN_V) // LANES
    toks = [
        _Tok(proj, width, lambda g: g, hist, lambda g: g),
        _Tok(proj, width, lambda g: ng + g, hist, lambda g: ng + g),
        _Tok(proj, width, lambda g: 2 * ng + g, hist, lambda g: 2 * ng + g),
        _Tok(proj, width, lambda g: 3 * ng + g),
        _Tok(proj, LANES, lambda g: ba_blk + g),
    ]
    ops, specs = _mixer_specs(grp, toks)
    cw = prm["conv_w"]
    grouped = lambda x: x.reshape(ng, 1, per_step)
    params = [
        (cw, _param_spec(width, lambda g: g)(CONV_W)),
        (cw, _param_spec(width, lambda g: ng + g)(CONV_W)),
        (cw, _param_spec(width, lambda g: 2 * ng + g)(CONV_W)),
        (grouped(prm["A_log"]), pl.BlockSpec((1, 1, per_step), lambda b, g, c: (g, 0, 0))),
        (grouped(prm["dt_bias"]), pl.BlockSpec((1, 1, per_step), lambda b, g, c: (g, 0, 0))),
        (prm["norm_w"].reshape(1, GDN_D), pl.BlockSpec((1, GDN_D), lambda b, g, c: (0, 0))),
    ]
    ops += [p for p, _ in params]
    specs += [s for _, s in params]
    st_spec = pl.BlockSpec((grp.nb * grp.bb, per_step, GDN_D, GDN_D), lambda b, g, c: (b, g, 0, 0))
    ops.append(s0)
    specs.append(st_spec)
    o, s_out = pl.pallas_call(
        functools.partial(_gdn_body, grp=grp),
        grid=(grp.outer // grp.nb, ng, grp.nc),
        in_specs=specs,
        out_specs=[pl.BlockSpec((grp.nb, MIX_ROWS, width), lambda b, g, c: (b, c, g)), st_spec],
        out_shape=[jax.ShapeDtypeStruct((grp.outer, grp.per_outer, GDN_V), BF16),
                   jax.ShapeDtypeStruct(s0.shape, F32)],
        compiler_params=_cp("parallel", "parallel", "arbitrary", flags=MIXER_FLAGS),
        name="gdn",
    )(*ops)
    return o.reshape(grp.B * grp.L, GDN_V), s_out


def _attend(q, k_ref, v_ref, nseq, lq):
    scale = XA_HD ** -0.5
    outs = []
    for j in range(nseq):
        qrows = slice(j * lq, (j + 1) * lq)
        mrows = slice(j * N_MEM, (j + 1) * N_MEM)
        heads = []
        for h in range(XA_HEADS):
            sl = slice(h * XA_HD, (h + 1) * XA_HD)
            s = _dot_nt(q[qrows, sl], k_ref[mrows, sl]) * scale
            p = jnp.exp(s - jnp.max(s, axis=-1, keepdims=True))
            heads.append(_dot(p, v_ref[mrows, sl]) / jnp.sum(p, axis=-1, keepdims=True))
        outs.append(jnp.concatenate(heads, axis=1))
    return outs[0] if nseq == 1 else jnp.concatenate(outs, axis=0)


def _attend_packed(q, k_ref, v_ref, nseq, lq):
    scale = XA_HD ** -0.5
    rows, cols = XA_HEADS * lq, N_MEM * XA_HEADS
    own_head = (_iota2((rows, cols), 0) // lq) == (_iota2((rows, cols), 1) % XA_HEADS)
    outs = []
    for j in range(nseq):
        qh = _stack_heads(q[j * lq:(j + 1) * lq], XA_HEADS, XA_HD)
        mrows = slice(j * cols, (j + 1) * cols)
        s = jnp.where(own_head, _dot_nt(qh, k_ref[mrows, :]) * scale, -1e30)
        p = jnp.where(own_head, jnp.exp(s - jnp.max(s, axis=-1, keepdims=True)), 0.0)
        o = _dot(p, v_ref[mrows, :]) / jnp.sum(p, axis=-1, keepdims=True)
        outs.append(_unstack_heads(o, XA_HEADS))
    return jnp.concatenate(outs, axis=0)


def _xattn_body(*refs, attend, nseq, lq, n_mix):
    n_lead = n_mix + 1 if n_mix else 0
    h_ref, g_ref, wq_ref, k_ref, v_ref, wo_ref, o_ref = refs[n_lead:]
    x, k0 = h_ref[...], 0
    for y_ref in refs[:n_mix]:
        k1 = k0 + y_ref.shape[1]
        x = x + jnp.dot(y_ref[...], refs[n_mix][k0:k1, :], preferred_element_type=F32)
        k0 = k1
    xn = x * lax.rsqrt(jnp.mean(x * x, axis=-1, keepdims=True) + EPS) * g_ref[...]
    q = jnp.dot(xn.astype(BF16), wq_ref[...], preferred_element_type=F32)
    o = attend(q, k_ref, v_ref, nseq, lq)
    o_ref[...] = x + jnp.dot(o.astype(BF16), wo_ref[...], preferred_element_type=F32)


def _xattn(grp, h, mix, w_mix, gain, w_q, w_o, layer, mem_k, mem_v, mem_row0):
    rows, d = h.shape
    if grp.L >= XA_TQ:
        nseq, lq, attend = 1, XA_TQ, _attend
        per = grp.L // XA_TQ
        mem_spec = pl.BlockSpec((N_MEM, XA_DIM), lambda i: (mem_row0 // N_MEM + i // per, 0))
    else:
        nseq, lq, attend = MIX_ROWS // grp.L, grp.L, _attend_packed
        blk = nseq * N_MEM * XA_HEADS
        mem_spec = pl.BlockSpec((blk, XA_HD), lambda i: (mem_row0 * XA_HEADS // blk + i, 0))
    tile = nseq * lq
    h_spec = pl.BlockSpec((tile, d), lambda i: (i, 0))
    lead_ops, lead_specs = [], []
    if mix:
        lead_ops = list(mix) + [w_mix]
        lead_specs = [pl.BlockSpec((tile, y.shape[1]), lambda i: (i, 0)) for y in mix]
        lead_specs.append(pl.BlockSpec(w_mix.shape, lambda i: (0, 0)))
    return pl.pallas_call(
        functools.partial(_xattn_body, attend=attend, nseq=nseq, lq=lq, n_mix=len(mix)),
        grid=(rows // tile,),
        in_specs=lead_specs + [h_spec, pl.BlockSpec((1, d), lambda i: (0, 0)),
                               pl.BlockSpec((None, d, XA_DIM), lambda i: (layer, 0, 0)), mem_spec, mem_spec,
                               pl.BlockSpec((None, XA_DIM, d), lambda i: (layer, 0, 0))],
        out_specs=h_spec,
        out_shape=jax.ShapeDtypeStruct((rows, d), F32),
        compiler_params=_cp("parallel"),
        name="xattn",
    )(*lead_ops, h, gain.reshape(1, d), w_q, mem_k, mem_v, w_o)


def _ffn_in_body(*refs, grp, tiles_per_seq):
    it = iter(refs)
    a_ref, g_ref, wg_ref, wu_ref, hist_ref, cw_ref, cb_ref, act_ref, tail_ref, an_ref = (next(it) for _ in range(10))
    carry_ref = None if grp.embedded else next(it)
    i, j = pl.program_id(0), pl.program_id(1)

    @pl.when(j == 0)
    def _():
        x = a_ref[...]
        y = x * lax.rsqrt(jnp.mean(x * x, axis=-1, keepdims=True) + EPS)
        an_ref[...] = (y * g_ref[...]).astype(BF16)

    a = an_ref[...]
    gate = jnp.dot(a, wg_ref[...].astype(BF16), preferred_element_type=F32)
    up = jnp.dot(a, wu_ref[...].astype(BF16), preferred_element_type=F32)
    if grp.embedded:
        halo = hist_ref[...]
        tail_ref[...] = gate
    else:
        halo = jnp.where(i % tiles_per_seq == 0, hist_ref[...], carry_ref[j])
        last = gate[gate.shape[0] - SUBLANES:]
        carry_ref[j] = last
        tail_ref[...] = last
    conv = _conv(grp, gate, halo, cw_ref, cb_ref[...], FFN_CONV_W)
    act_ref[...] = (_silu(conv) * up).astype(act_ref.dtype)


def _ffn_in(grp, h, gain, w_in, layer, hist, conv_w, conv_b, tm=MM_TM, tn=MM_TN):
    assert grp.embedded or grp.L % tm == 0
    rows, k = h.shape
    nct = D_FF // tn
    tiles_per_seq = max(grp.L // tm, 1)
    hrows = tm if grp.embedded else SUBLANES
    tile_idx = lambda i, j: (i, j)
    hist_idx = tile_idx if grp.embedded else (lambda i, j: (i // tiles_per_seq, j))
    specs = [
        pl.BlockSpec((tm, k), lambda i, j: (i, 0)),
        pl.BlockSpec((1, k), lambda i, j: (0, 0)),
        _weight_spec(w_in, layer, k, tn, lambda j: j),
        _weight_spec(w_in, layer, k, tn, lambda j: nct + j),
        pl.BlockSpec((hrows, tn), hist_idx),
        pl.BlockSpec((FFN_CONV_W, tn), lambda i, j: (0, j)),
        pl.BlockSpec((1, tn), lambda i, j: (0, j)),
    ]
    scratch = [pltpu.VMEM((tm, k), BF16)]
    if not grp.embedded:
        scratch.append(pltpu.VMEM((nct, SUBLANES, tn), F32))
    act, tails = pl.pallas_call(
        functools.partial(_ffn_in_body, grp=grp, tiles_per_seq=tiles_per_seq),
        grid=(rows // tm, nct),
        in_specs=specs,
        out_specs=[pl.BlockSpec((tm, tn), tile_idx), pl.BlockSpec((hrows, tn), tile_idx)],
        out_shape=[jax.ShapeDtypeStruct((rows, D_FF), BF16), jax.ShapeDtypeStruct((rows // tm * hrows, D_FF), F32)],
        scratch_shapes=scratch,
        compiler_params=_cp("arbitrary", "arbitrary"),
        name="ffn_in",
    )(h, gain.reshape(1, k), w_in, w_in, hist, conv_w, conv_b.reshape(1, D_FF))
    if not grp.embedded:
        tails = tails.reshape(grp.B, tiles_per_seq, SUBLANES, D_FF)[:, -1].reshape(grp.B * SUBLANES, D_FF)
    return act, tails


def _history(grp, buf, width):
    b, k, c = buf.shape
    if grp.embedded:
        h = jnp.pad(buf, ((0, 0), (grp.first - k, grp.L - grp.first), (0, 0)))
        return h.reshape(b * grp.L, c)
    return jnp.pad(buf, ((0, 0), (SUBLANES - k, 0), (0, 0))).reshape(b * SUBLANES, c)


def _tail(grp, x2d, k, cols):
    x = x2d.reshape(grp.B, grp.L, x2d.shape[1])
    return x[:, grp.L - k:, cols].astype(F32)


def _run_group(grp, x2d, mem_k, mem_v, mem_row0, st, wts, prm):
    h = x2d
    out = {}
    tm_proj = min(MM_TM_PROJ, h.shape[0])
    proj = _matmul(h, wts["in_ab"], gain=prm["norm_mix"][0], out_dtype=BF16, tm=tm_proj, w_rows_are_outputs=True)
    y_rw, s_rw = _rwkv(grp, proj, _history(grp, st["rw_shift"][:, None, :], RW_PROJ), st["rwkv"], prm["rw"])
    y_ssd, s_ssd = _ssd(grp, proj, _history(grp, st["ssd_conv"], SSD_CONV_DIM),
                        st["ssd"].reshape(grp.B, SSD_GROUPS, SSD_GDIM, SSD_STATE), prm["ssd"])
    out["rwkv"] = s_rw
    out["rw_shift"] = _tail(grp, proj, 1, slice(AB_RW0, AB_RW0 + RW_PROJ))[:, 0]
    out["ssd"] = s_ssd.reshape(st["ssd"].shape)
    out["ssd_conv"] = _tail(grp, proj, CONV_W - 1, slice(AB_X0, AB_X0 + SSD_CONV_DIM))
    mix, w_mix = [y_rw, y_ssd], wts["out_ab"]
    ffn_bufs = []
    for l in range(2):
        if l == 1:
            proj = _matmul(h, wts["in_c"], gain=prm["norm_mix"][1], out_dtype=BF16, tm=tm_proj, w_rows_are_outputs=True)
            y_c, s_gdn = _gdn(grp, proj, _history(grp, st["gdn_conv"], GDN_CONV_DIM), st["gdn"], prm["gdn"])
            out["gdn"] = s_gdn
            out["gdn_conv"] = _tail(grp, proj, CONV_W - 1, slice(0, GDN_CONV_DIM))
            mix, w_mix = [y_c], wts["out_c"]
        if grp.embedded:
            h, mix = _matmul(mix, w_mix, res=h), []
        h = _xattn(grp, h, mix, w_mix, prm["norm_xa"][l], wts["xq"], wts["xo"], l, mem_k, mem_v, mem_row0[l])
        act, gate_tail = _ffn_in(grp, h, prm["norm_ffn"][l], wts["ffn_in"], l, _history(grp, st["ffn_conv"][l], D_FF),
                                 prm["ffn_conv_w"][l], prm["ffn_conv_b"][l])
        ffn_bufs.append(gate_tail.reshape(grp.B, SUBLANES, D_FF)[:, SUBLANES - (FFN_CONV_W - 1):])
        h = _matmul(act, wts["ffn_out"], layer=l, res=h)
    out["ffn_conv"] = jnp.stack(ffn_bufs)
    out["y"] = _rmsnorm(h, prm["norm_final"])
    return out


def kernel(x_prompt, x_sample, mem_prompt, state_rwkv, state_rwkv_shift, state_ssd, state_ssd_conv, state_gdn,
           state_gdn_conv, state_ffn_conv, cache_mem_k, cache_mem_v, norm_mix, norm_xa, norm_mem, norm_ffn,
           norm_final, w_in_ab, rw_mu, rw_w0, rw_w_up, rw_a0, rw_a_up, rw_g_up, rw_k_k, rw_k_a, rw_r_k, rw_gn_w,
           rw_gn_b, ssd_conv_w, ssd_conv_b, ssd_dt_bias, ssd_A_log, ssd_D, ssd_norm_w, w_out_ab, w_in_c,
           gdn_conv_w, gdn_A_log, gdn_dt_bias, gdn_norm_w, w_out_c, w_xq, w_xk, w_xv, w_xo, ffn_w_in, ffn_conv_w,
           ffn_conv_b, ffn_w_out):
    bp, lp, _ = x_prompt.shape
    bs, ls, _ = x_sample.shape
    depth = norm_mix.shape[0]
    assert depth == 2 and w_in_ab.shape[0] == 1 and w_in_c.shape[0] == 1
    assert lp % MIX_ROWS == 0 and lp % XA_TQ == 0 and lp % MM_TM == 0
    pad_rows = SUBLANES - ls
    assert CONV_W - 1 <= pad_rows and CONV_W - 1 <= ls and MIX_ROWS % SUBLANES == 0
    gp = Group(B=bp, L=lp, first=0, bb=1, nb=MIX_NB if bp % MIX_NB == 0 else 1)
    gs = Group(B=bs, L=SUBLANES, first=pad_rows, bb=MIX_ROWS // SUBLANES)
    assert bs % gs.bb == 0 and (bs * SUBLANES) % MM_TM == 0 and (bp * lp) % MM_TM == 0

    wts = dict(
        in_ab=_ab_layout(jnp.swapaxes(w_in_ab[0], 0, 1), axis=0).astype(BF16), out_ab=w_out_ab[0].astype(BF16),
        in_c=_gdn_layout(jnp.swapaxes(w_in_c[0], 0, 1), axis=0).astype(BF16), out_c=w_out_c[0].astype(BF16),
        xq=w_xq.astype(BF16), xo=w_xo.astype(BF16), ffn_in=ffn_w_in, ffn_out=ffn_w_out.astype(BF16),
    )
    prm = dict(
        norm_mix=norm_mix, norm_xa=norm_xa, norm_ffn=norm_ffn, norm_final=norm_final,
        ffn_conv_w=ffn_conv_w, ffn_conv_b=ffn_conv_b,
        rw=dict(mu=rw_mu[0], w0=rw_w0[0], w_up=rw_w_up[0], a0=rw_a0[0], a_up=rw_a_up[0], g_up=rw_g_up[0],
                k_k=rw_k_k[0], k_a=rw_k_a[0], r_k=rw_r_k[0], gn_w=rw_gn_w[0], gn_b=rw_gn_b[0]),
        ssd=dict(conv_w=ssd_conv_w[0], conv_b=ssd_conv_b[0], dt_bias=ssd_dt_bias[0], A_log=ssd_A_log[0],
                 D=ssd_D[0], norm_w=ssd_norm_w[0]),
        gdn=dict(conv_w=gdn_conv_w[0], A_log=gdn_A_log[0], dt_bias=gdn_dt_bias[0], norm_w=gdn_norm_w[0]),
    )

    mem2d = mem_prompt.reshape(bp * N_MEM, D_MODEL)
    mem_tm = min(MM_TM, bp * N_MEM)
    mk, mv = [], []
    for l in range(depth):
        mk.append(_matmul(mem2d, w_xk[l].astype(BF16), gain=norm_mem[l], tm=mem_tm))
        mv.append(_matmul(mem2d, w_xv[l].astype(BF16), gain=norm_mem[l], tm=mem_tm))
    mem_k_p = jnp.stack(mk)
    mem_v_p = jnp.stack(mv)

    zeros = lambda *s: jnp.zeros(s, F32)
    st_p = dict(
        rwkv=zeros(bp, RW_HEADS, RW_HD, RW_HD), rw_shift=zeros(bp, RW_PROJ),
        ssd=zeros(bp, SSD_GROUPS, SSD_HPG, SSD_HD, SSD_STATE), ssd_conv=zeros(bp, CONV_W - 1, SSD_CONV_DIM),
        gdn=zeros(bp, GDN_HEADS, GDN_D, GDN_D), gdn_conv=zeros(bp, CONV_W - 1, GDN_CONV_DIM),
        ffn_conv=zeros(depth, bp, FFN_CONV_W - 1, D_FF),
    )
    rp = _run_group(gp, x_prompt.reshape(bp * lp, D_MODEL), mem_k_p.reshape(depth * bp * N_MEM, XA_DIM),
                    mem_v_p.reshape(depth * bp * N_MEM, XA_DIM), [l * bp * N_MEM for l in range(depth)],
                    st_p, wts, prm)

    st_s = dict(rwkv=state_rwkv[0], rw_shift=state_rwkv_shift[0], ssd=state_ssd[0], ssd_conv=state_ssd_conv[0],
                gdn=state_gdn[0], gdn_conv=state_gdn_conv[0], ffn_conv=state_ffn_conv)
    xs = jnp.pad(x_sample, ((0, 0), (pad_rows, 0), (0, 0))).reshape(bs * SUBLANES, D_MODEL)
    rs = _run_group(gs, xs, cache_mem_k.reshape(depth * bs * N_MEM * XA_HEADS, XA_HD),
                    cache_mem_v.reshape(depth * bs * N_MEM * XA_HEADS, XA_HD), [l * bs * N_MEM for l in range(depth)],
                    st_s, wts, prm)

    y_p = rp["y"].reshape(bp, lp, D_MODEL)
    y_s = rs["y"].reshape(bs, SUBLANES, D_MODEL)[:, pad_rows:]
    lead = lambda x: x[None]
    mem_shape = (depth, bp, N_MEM, XA_HEADS, XA_HD)
    return (y_p, y_s, lead(rp["rwkv"]), lead(rs["rwkv"]), lead(rp["rw_shift"]), lead(rs["rw_shift"]),
            lead(rp["ssd"]), lead(rs["ssd"]), lead(rp["ssd_conv"]), lead(rs["ssd_conv"]),
            lead(rp["gdn"]), lead(rs["gdn"]), lead(rp["gdn_conv"]), lead(rs["gdn_conv"]),
            rp["ffn_conv"], rs["ffn_conv"], mem_k_p.reshape(mem_shape), mem_v_p.reshape(mem_shape))
```

```python
import functools
from typing import NamedTuple

import jax
import jax.numpy as jnp
from jax import lax
from jax.experimental import pallas as pl
from jax.experimental.pallas import tpu as pltpu

F32 = jnp.float32
BF16 = jnp.bfloat16
HI = lax.Precision.HIGHEST

D_MODEL = 2048
EPS = 1e-6
RW_HEADS, RW_HD = 16, 64
RW_DIM = RW_HEADS * RW_HD
RW_LORA = 256
RW_PROJ = 3 * RW_DIM + RW_LORA
RW_GN_EPS = 6.4e-4
SSD_HEADS, SSD_HD, SSD_GROUPS, SSD_STATE = 16, 64, 2, 128
SSD_HPG = SSD_HEADS // SSD_GROUPS
SSD_DIM = SSD_HEADS * SSD_HD
SSD_GDIM = SSD_HPG * SSD_HD
SSD_CONV_DIM = SSD_DIM + 2 * SSD_GROUPS * SSD_STATE
SSD_PROJ = SSD_DIM + SSD_CONV_DIM + SSD_HEADS
AB_PROJ = RW_PROJ + SSD_PROJ
GDN_HEADS, GDN_D = 16, 128
GDN_V = GDN_HEADS * GDN_D
GDN_CONV_DIM = 3 * GDN_V
GDN_PROJ = GDN_CONV_DIM + GDN_V + 2 * GDN_HEADS
D_FF = 5632
XA_HEADS, XA_HD, N_MEM = 4, 128, 256
XA_DIM = XA_HEADS * XA_HD
CONV_W = 4
FFN_CONV_W = 3

AB_Z0 = 0
AB_X0 = SSD_DIM
AB_B0 = 2 * SSD_DIM
AB_C0 = AB_B0 + SSD_GROUPS * SSD_STATE
AB_DT0 = AB_C0 + SSD_GROUPS * SSD_STATE
AB_RW0 = 3 * RW_DIM
AB_COLS = 6656
assert AB_DT0 + 128 <= AB_RW0 and AB_RW0 + RW_PROJ <= AB_COLS and AB_RW0 % RW_DIM == 0

SUBLANES = 8
LANES = 128
VMEM_LIMIT = 56 * 1024 * 1024

MIX_ROWS = 64
INV_BLOCK = 16
INV_SPLIT = 128
MIX_NB = 2
MM_TM, MM_TN = 1024, 512
MM_TM_PROJ = 2048
RW_HP = 2
RW_SUB = 4
GDN_G = 4
GDN_SUB = 4
XA_TQ = 512
NORM_TM = 512


class Group(NamedTuple):
    B: int
    L: int
    first: int
    bb: int
    nb: int = 1

    @property
    def outer(self):
        return self.B // self.bb

    @property
    def per_outer(self):
        return self.bb * self.L

    @property
    def qs(self):
        return MIX_ROWS // self.bb

    @property
    def nc(self):
        return self.L // self.qs

    @property
    def embedded(self):
        return self.first > 0


MIXER_FLAGS = None


def _cp(*sem, flags=None):
    return pltpu.CompilerParams(dimension_semantics=sem, vmem_limit_bytes=VMEM_LIMIT, flags=flags)


def _dot(a, b):
    return jnp.dot(a.astype(BF16), b.astype(BF16), preferred_element_type=F32)


def _dot_nt(a, b):
    return lax.dot_general(a.astype(BF16), b.astype(BF16), (((1,), (1,)), ((), ())), preferred_element_type=F32)


def _dot_tn(a, b):
    return lax.dot_general(a.astype(BF16), b.astype(BF16), (((0,), (0,)), ((), ())), preferred_element_type=F32)


def _dot_hi(a, b):
    return jnp.dot(a, b, preferred_element_type=F32, precision=HI)


def _split_bf16(x):
    hi = x.astype(BF16)
    return hi, (x - hi.astype(F32)).astype(BF16)


def _dot_sel(a, sel):
    ah, al = _split_bf16(a)
    sb = sel.astype(BF16)
    return jnp.dot(jnp.concatenate([ah, al], axis=1), jnp.concatenate([sb, sb], axis=0), preferred_element_type=F32)


def _dot_nt_hi(a, b):
    return lax.dot_general(a, b, (((1,), (1,)), ((), ())), preferred_element_type=F32, precision=HI)


def _sigmoid(x):
    return 1.0 / (1.0 + jnp.exp(-x))


def _silu(x):
    return x * _sigmoid(x)


def _softplus(x):
    return jnp.maximum(x, 0.0) + jnp.log(1.0 + jnp.exp(-jnp.abs(x)))


def _iota2(shape, axis):
    return lax.broadcasted_iota(jnp.int32, shape, axis)


class _Masks(NamedTuple):
    incl: jax.Array
    strict: jax.Array
    eye: jax.Array
    cumsum: jax.Array
    valid: jax.Array


def _masks(grp, heads=1):
    r = MIX_ROWS
    n = heads * r
    ri, ci = _iota2((n, n), 0), _iota2((n, n), 1)
    same = (ri // grp.qs) == (ci // grp.qs)
    incl = same & (ci <= ri)
    strict = same & (ci < ri)
    si, sj = _iota2((2 * r, r), 0), _iota2((2 * r, r), 1)
    same_seq = ((si % r) // grp.qs) == (sj // grp.qs)
    cumsum = (same_seq & ((sj <= si) | (si >= r))).astype(F32)
    t = _iota2((r, 1), 0) % grp.qs
    return _Masks(incl, strict, (ri == ci).astype(F32), cumsum, t >= grp.first)


def _neumann(lm, rhs, nil, eye=None):
    n = lm.shape[0]
    p = None
    if rhs is None:
        t = eye + lm
        if nil > 2:
            p = _dot(lm, lm)
            yield
    elif nil > 2:
        both = _dot(lm, jnp.concatenate([lm, rhs], axis=1))
        yield
        p, t = both[:, :n], rhs + both[:, n:]
    else:
        t = rhs + _dot(lm, rhs)
        yield
    k = 2
    while k < nil:
        if 2 * k < nil:
            both = _dot(p, jnp.concatenate([p, t], axis=1))
            p, t = both[:, :n], t + both[:, n:]
        else:
            t = t + _dot(p, t)
        yield
        k *= 2
    return t


def _inv_unit_lower(lm, eye, qs):
    if qs <= INV_BLOCK:
        return (yield from _neumann(lm, None, qs, eye))
    n = lm.shape[0]
    diag = (_iota2((n, n), 0) // INV_BLOCK) == (_iota2((n, n), 1) // INV_BLOCK)
    d = jnp.where(diag, lm, 0.0)
    dinv = yield from _neumann(d, None, INV_BLOCK, eye)
    nm = _dot(dinv, lm - d)
    yield
    return (yield from _neumann(nm, dinv, qs // INV_BLOCK))


def _lockstep(gens):
    results = [None] * len(gens)
    active = list(range(len(gens)))
    while active:
        for i in list(active):
            try:
                next(gens[i])
            except StopIteration as stop:
                results[i] = stop.value
                active.remove(i)
        if active:
            yield
    return results


def _solve_unit_lower(lm, rhs, eye, qs):
    n = lm.shape[0]
    if n > INV_SPLIT:
        blocks = [slice(i, i + INV_SPLIT) for i in range(0, n, INV_SPLIT)]
        inv = yield from _lockstep([_inv_unit_lower(lm[b, b], eye[b, b], qs) for b in blocks])
        zero = jnp.zeros((INV_SPLIT, INV_SPLIT), F32)
        t = jnp.concatenate([jnp.concatenate([inv[i] if i == j else zero for j in range(len(blocks))], axis=1)
                             for i in range(len(blocks))], axis=0)
    else:
        t = yield from _inv_unit_lower(lm, eye, qs)
    x = _dot(t, rhs)
    yield
    return x


def _interleave(gens):
    results = [None] * len(gens)
    active = list(range(len(gens)))
    while active:
        for i in list(active):
            try:
                next(gens[i])
            except StopIteration as stop:
                results[i] = stop.value
                active.remove(i)
    return results


def _stack_heads(x, heads, width):
    return jnp.concatenate([x[:, h * width:(h + 1) * width] for h in range(heads)], axis=0)


def _unstack_heads(x, heads):
    r = x.shape[0] // heads
    return jnp.concatenate([x[h * r:(h + 1) * r] for h in range(heads)], axis=1)


def _seq_rows(grp, x, j, heads):
    if grp.bb == 1:
        return x
    qs = grp.qs
    return jnp.concatenate([x[h * MIX_ROWS + j * qs:h * MIX_ROWS + (j + 1) * qs] for h in range(heads)], axis=0)


def _from_seq_rows(grp, parts, heads):
    if grp.bb == 1:
        return parts[0]
    qs = grp.qs
    return jnp.concatenate([p[h * qs:(h + 1) * qs] for h in range(heads) for p in parts], axis=0)


def _expand_rows(x, heads, rows_per_head, width):
    m = x.shape[0]
    if x.shape[1] == width:
        x = jnp.concatenate([x] * heads, axis=1)
    keep = ((_iota2((m, heads * width), 0) // rows_per_head) % heads) == (_iota2((m, heads * width), 1) // width)
    return jnp.where(keep, x, 0.0)


def _taps(grp, cur, halo, width):
    if grp.embedded:
        t = _iota2((cur.shape[0], 1), 0) % grp.qs
        full = jnp.where(t < grp.first, halo, cur)
        return [full] + [pltpu.roll(full, s, axis=0) for s in range(1, width)]
    full = jnp.concatenate([halo, cur], axis=0)
    return [cur] + [pltpu.roll(full, s, axis=0)[SUBLANES:] for s in range(1, width)]


def _conv(grp, cur, halo, w_ref, bias, width):
    taps = _taps(grp, cur, halo, width)
    y = taps[0] * w_ref[width - 1:width, :]
    for s in range(1, width):
        y = y + taps[s] * w_ref[width - 1 - s:width - s, :]
    return y if bias is None else y + bias


class _Tok(NamedTuple):
    arr: jax.Array
    width: int
    col: object
    hist: jax.Array = None
    hcol: object = None


def _view3(grp, x2d):
    return x2d.reshape(grp.outer, grp.per_outer, x2d.shape[1])


def _mixer_specs(grp, toks):
    nb = grp.nb
    ops, specs = [], []
    for t in toks:
        ops.append(_view3(grp, t.arr))
        specs.append(pl.BlockSpec((nb, MIX_ROWS, t.width), lambda b, g, c, t=t: (b, c, t.col(g))))
    for t in toks:
        if t.hist is None:
            continue
        if grp.embedded:
            ops.append(_view3(grp, t.hist))
            specs.append(pl.BlockSpec((nb, MIX_ROWS, t.width), lambda b, g, c, t=t: (b, 0, t.hcol(g))))
            continue
        tile = _sublane_tile(t.arr.dtype)
        sub = MIX_ROWS // tile
        ops.append(_view3(grp, t.arr))
        specs.append(pl.BlockSpec(
            (nb, tile, t.width), lambda b, g, c, t=t, sub=sub: (b, jnp.maximum(c * sub - 1, 0), t.col(g))))
        ops.append(t.hist.reshape(grp.B, SUBLANES, t.hist.shape[1]))
        specs.append(pl.BlockSpec((nb, SUBLANES, t.width), lambda b, g, c, t=t: (b, 0, t.hcol(g))))
    return ops, specs


def _read_windows(grp, refs, n_plain, n_hist):
    c = pl.program_id(2)
    nb = grp.nb
    cur = [[r[s].astype(F32) for s in range(nb)] for r in refs[:n_plain]]
    rest = refs[n_plain:]
    halos = []
    for i in range(n_hist):
        if grp.embedded:
            halos.append([rest[i][s] for s in range(nb)])
        else:
            per = []
            for s in range(nb):
                prev = rest[2 * i][s]
                prev = prev[prev.shape[0] - SUBLANES:].astype(F32)
                per.append(jnp.where(c == 0, rest[2 * i + 1][s], prev))
            halos.append(per)
    used = n_hist if grp.embedded else 2 * n_hist
    return cur, halos, rest[used:]


def _sublane_tile(dtype):
    return SUBLANES * 4 // jnp.dtype(dtype).itemsize


def _param_spec(width, col):
    return lambda rows: pl.BlockSpec((rows, width), lambda b, g, c: (0, col(g)))


def _matmul_body(*refs, norm, has_res, n_a, w_rows_are_outputs):
    it = iter(refs)
    a_refs = [next(it) for _ in range(n_a)]
    g_ref = next(it) if norm else None
    w_ref = next(it)
    res_ref = next(it) if has_res else None
    o_ref = next(it)
    if norm:
        an_ref = next(it)

        @pl.when(pl.program_id(1) == 0)
        def _():
            x = a_refs[0][...]
            y = x * lax.rsqrt(jnp.mean(x * x, axis=-1, keepdims=True) + EPS)
            an_ref[...] = (y * g_ref[...]).astype(BF16)

        if w_rows_are_outputs:
            acc = lax.dot_general(an_ref[...], w_ref[...], (((1,), (1,)), ((), ())), preferred_element_type=F32)
        else:
            acc = jnp.dot(an_ref[...], w_ref[...], preferred_element_type=F32)
    else:
        acc, k0 = None, 0
        for a_ref in a_refs:
            k1 = k0 + a_ref.shape[1]
            part = jnp.dot(a_ref[...], w_ref[k0:k1, :], preferred_element_type=F32)
            acc = part if acc is None else acc + part
            k0 = k1
    if has_res:
        acc = acc + res_ref[...]
    o_ref[...] = acc.astype(o_ref.dtype)


def _matmul(a, w, *, layer=None, gain=None, res=None, out_dtype=F32, tm=MM_TM, tn=MM_TN, w_rows_are_outputs=False):
    a_list = list(a) if isinstance(a, (list, tuple)) else [a]
    m = a_list[0].shape[0]
    k, n = w.shape[-2:][::-1] if w_rows_are_outputs else w.shape[-2:]
    assert not w_rows_are_outputs or (gain is not None and w.ndim == 2)
    assert sum(x.shape[1] for x in a_list) == k
    norm = gain is not None
    assert not norm or len(a_list) == 1
    ops = list(a_list)
    specs = [pl.BlockSpec((tm, x.shape[1]), lambda i, j: (i, 0)) for x in a_list]
    if norm:
        ops.append(gain.reshape(1, k))
        specs.append(pl.BlockSpec((1, k), lambda i, j: (0, 0)))
    ops.append(w)
    if w_rows_are_outputs:
        specs.append(pl.BlockSpec((tn, k), lambda i, j: (j, 0)))
    else:
        specs.append(_weight_spec(w, layer, k, tn, lambda j: j))
    if res is not None:
        ops.append(res)
        specs.append(pl.BlockSpec((tm, tn), lambda i, j: (i, j)))
    return pl.pallas_call(
        functools.partial(_matmul_body, norm=norm, has_res=res is not None, n_a=len(a_list),
                          w_rows_are_outputs=w_rows_are_outputs),
        grid=(m // tm, n // tn),
        in_specs=specs,
        out_specs=pl.BlockSpec((tm, tn), lambda i, j: (i, j)),
        out_shape=jax.ShapeDtypeStruct((m, n), out_dtype),
        scratch_shapes=[pltpu.VMEM((tm, k), BF16)] if norm else [],
        compiler_params=_cp("parallel", "arbitrary"),
        name="matmul",
    )(*ops)


def _weight_spec(w, layer, k, tn, col):
    if w.ndim == 2:
        return pl.BlockSpec((k, tn), lambda i, j: (0, col(j)))
    return pl.BlockSpec((None, k, tn), lambda i, j: (layer, 0, col(j)))


def _rmsnorm_body(x_ref, g_ref, o_ref):
    x = x_ref[...]
    y = x * lax.rsqrt(jnp.mean(x * x, axis=-1, keepdims=True) + EPS)
    o_ref[...] = (y * g_ref[...]).astype(o_ref.dtype)


def _rmsnorm(x, gain, out_dtype=F32, tm=NORM_TM):
    m, k = x.shape
    return pl.pallas_call(
        _rmsnorm_body,
        grid=(m // tm,),
        in_specs=[pl.BlockSpec((tm, k), lambda i: (i, 0)), pl.BlockSpec((1, k), lambda i: (0, 0))],
        out_specs=pl.BlockSpec((tm, k), lambda i: (i, 0)),
        out_shape=jax.ShapeDtypeStruct((m, k), out_dtype),
        compiler_params=_cp("parallel"),
        name="rmsnorm",
    )(x, gain.reshape(1, k))


def _rwkv_body(*refs, grp):
    r_rows = MIX_ROWS
    qs, bb, nc, nb = grp.qs, grp.bb, grp.nc, grp.nb
    nh = 2 * RW_HP
    sw = nh * RW_HD
    width = RW_SUB * sw
    n = nh * r_rows
    cur, halos, rest = _read_windows(grp, refs, 4, 4)
    (mu_r, mu_k, mu_v, mu_lo, w0_ref, a0_ref, lora_ref, kk_ref, ka_ref, rk_ref, gnw_ref, gnb_ref,
     s0_ref, y_ref, s_ref, sbig_ref) = rest
    c = pl.program_id(2)

    @pl.when(c == 0)
    def _():
        for j in range(nb * bb):
            for sub in range(RW_SUB):
                st = jnp.concatenate([s0_ref[j, sub * nh + h] for h in range(nh)], axis=0)
                sbig_ref[j, sub] = _expand_rows(st, nh, RW_HD, RW_HD)

    m = _masks(grp, nh)
    bd = ((_iota2((sw, sw), 0) // RW_HD) == (_iota2((sw, sw), 1) // RW_HD)).astype(F32)

    def head_sums(x):
        return _unstack_heads(_dot_sel(_stack_heads(x, RW_SUB, sw), bd), RW_SUB)

    def block(s):
        def shifted(i, mu_ref):
            x = cur[i][s]
            prev = _taps(grp, x, halos[i][s], 2)[1]
            return x + (prev - x) * mu_ref[...]

        r = shifted(0, mu_r)
        k = shifted(1, mu_k)
        v = shifted(2, mu_v)
        lo = shifted(3, mu_lo)
        lora_in = jnp.concatenate([jnp.tanh(lo[:, :64]), lo[:, 64:128], _sigmoid(lo[:, 128:256])], axis=1)
        lora = _dot(lora_in, lora_ref[0])
        w = -_softplus(-(w0_ref[...] + lora[:, :width])) - 0.5
        lw = jnp.where(m.valid, -jnp.exp(w), 0.0)
        a = _sigmoid(a0_ref[...] + lora[:, width:2 * width])
        g = lora[:, 2 * width:]

        kx = k * kk_ref[...]
        kp = jnp.where(m.valid, k * (1.0 + (a - 1.0) * ka_ref[...]), 0.0)
        sums = head_sums(jnp.concatenate([kx * kx, r * kp * rk_ref[...]], axis=0))
        kkn = kx * lax.rsqrt(sums[:r_rows] + 1e-6)
        bonus = sums[r_rows:] * v
        at = jnp.where(m.valid, -kkn, 0.0)
        bt = jnp.where(m.valid, kkn * a, 0.0)

        cums = _dot_hi(m.cumsum, lw)
        cum, cum_end = cums[:r_rows], cums[r_rows:]
        e_neg = jnp.exp(-cum)
        e_end = jnp.exp(cum_end - cum)
        p_end = jnp.exp(cum_end)
        a_til, r_til = at * jnp.exp(cum - lw), r * jnp.exp(cum)
        b_til, k_til = bt * e_neg, kp * e_neg
        b_hat, k_hat = bt * e_end, kp * e_end

        def stack(sub):
            lanes = slice(sub * sw, (sub + 1) * sw)

            def tile(x):
                return _expand_rows(jnp.concatenate([x[:, lanes]] * nh, axis=0), nh, r_rows, RW_HD)

            ar = jnp.concatenate([tile(a_til), tile(r_til)], axis=0)
            bk = jnp.concatenate([tile(b_til), tile(k_til)], axis=0)
            bk_end = jnp.concatenate([tile(b_hat), tile(k_hat)], axis=0)
            v_exp = tile(v)
            gm = _dot_nt(ar, bk)
            yield
            m_ab = jnp.where(m.strict, gm[:n, :n], 0.0)
            m_ak = jnp.where(m.strict, gm[:n, n:], 0.0)
            m_r = jnp.concatenate([jnp.where(m.incl, gm[n:, :n], 0.0), jnp.where(m.incl, gm[n:, n:], 0.0)], axis=1)

            def seq2(x, j):
                if bb == 1:
                    return x
                return jnp.concatenate([_seq_rows(grp, x[:n], j, nh), _seq_rows(grp, x[n:], j, nh)], axis=0)

            parts = []
            for j in range(bb):
                parts.append(_dot_nt(seq2(ar, j), sbig_ref[s * bb + j, sub]))
                yield
            half = nh * qs
            as_a = _from_seq_rows(grp, [p[:half] for p in parts], nh)
            as_r = _from_seq_rows(grp, [p[half:] for p in parts], nh)
            rhs = as_a + _dot(m_ak, v_exp)
            yield
            u = yield from _solve_unit_lower(m_ab, rhs, m.eye, qs)
            uv = jnp.concatenate([u, v_exp], axis=0)
            y_exp = as_r + _dot(m_r, uv)
            yield
            y = y_exp[:r_rows]
            for h in range(1, nh):
                y = y + y_exp[h * r_rows:(h + 1) * r_rows]
            for j in range(bb):
                sbig_ref[s * bb + j, sub] = (sbig_ref[s * bb + j, sub] * p_end[j * qs:j * qs + 1, lanes]
                                             + _dot_tn(seq2(uv, j), seq2(bk_end, j)))
                yield
            return y

        def finish(ys):
            y = jnp.concatenate(ys, axis=1)
            mean = head_sums(y) * (1.0 / RW_HD)
            d = y - mean
            var = head_sums(d * d) * (1.0 / RW_HD)
            out = d * lax.rsqrt(var + RW_GN_EPS) * gnw_ref[...] + gnb_ref[...] + bonus
            y_ref[s] = (out * g).astype(y_ref.dtype)

        return [stack(sub) for sub in range(RW_SUB)], finish

    blocks = [block(s) for s in range(nb)]
    results = _interleave([chain for chains, _ in blocks for chain in chains])
    for s, (_, finish) in enumerate(blocks):
        finish(results[s * RW_SUB:(s + 1) * RW_SUB])

    @pl.when(c == nc - 1)
    def _():
        for j in range(nb * bb):
            for sub in range(RW_SUB):
                sb = sbig_ref[j, sub]
                for h in range(nh):
                    s_ref[j, sub * nh + h] = sb[h * RW_HD:(h + 1) * RW_HD, h * RW_HD:(h + 1) * RW_HD]


def _rwkv(grp, proj, hist, s0, prm):
    width = RW_SUB * RW_HP * LANES
    ng = RW_DIM // width
    lora_blk = 3 * RW_DIM // RW_LORA
    p0 = AB_RW0 // width
    toks = [
        _Tok(proj, width, lambda g: p0 + g, hist, lambda g: g),
        _Tok(proj, width, lambda g: p0 + ng + g, hist, lambda g: ng + g),
        _Tok(proj, width, lambda g: p0 + 2 * ng + g, hist, lambda g: 2 * ng + g),
        _Tok(proj, RW_LORA, lambda g: AB_RW0 // RW_LORA + lora_blk, hist, lambda g: lora_blk),
    ]
    ops, specs = _mixer_specs(grp, toks)
    head = _param_spec(width, lambda g: g)
    lora_w = jnp.zeros((ng, RW_LORA, 3 * width), F32)
    for i, (name, r0, r1) in enumerate((("w_up", 0, 64), ("a_up", 64, 128), ("g_up", 128, 256))):
        blk = prm[name].reshape(r1 - r0, ng, width).transpose(1, 0, 2)
        lora_w = lora_w.at[:, r0:r1, i * width:(i + 1) * width].set(blk)
    mu = prm["mu"].reshape(1, RW_PROJ)
    flat = lambda x: x.reshape(1, RW_DIM)
    params = [
        (mu, pl.BlockSpec((1, width), lambda b, g, c: (0, g))),
        (mu, pl.BlockSpec((1, width), lambda b, g, c: (0, ng + g))),
        (mu, pl.BlockSpec((1, width), lambda b, g, c: (0, 2 * ng + g))),
        (mu, pl.BlockSpec((1, RW_LORA), lambda b, g, c: (0, lora_blk))),
        (flat(prm["w0"]), head(1)), (flat(prm["a0"]), head(1)),
        (lora_w, pl.BlockSpec((1, RW_LORA, 3 * width), lambda b, g, c: (g, 0, 0))),
        (flat(prm["k_k"]), head(1)), (flat(prm["k_a"]), head(1)), (flat(prm["r_k"]), head(1)),
        (flat(prm["gn_w"]), head(1)), (flat(prm["gn_b"]), head(1)),
    ]
    ops += [p for p, _ in params]
    specs += [s for _, s in params]
    seqs = grp.nb * grp.bb
    st_spec = pl.BlockSpec((seqs, RW_SUB * 2 * RW_HP, RW_HD, RW_HD), lambda b, g, c: (b, g, 0, 0))
    ops.append(s0)
    specs.append(st_spec)
    y, s_out = pl.pallas_call(
        functools.partial(_rwkv_body, grp=grp),
        grid=(grp.outer // grp.nb, ng, grp.nc),
        in_specs=specs,
        out_specs=[pl.BlockSpec((grp.nb, MIX_ROWS, width), lambda b, g, c: (b, c, g)), st_spec],
        out_shape=[jax.ShapeDtypeStruct((grp.outer, grp.per_outer, RW_DIM), BF16),
                   jax.ShapeDtypeStruct(s0.shape, F32)],
        scratch_shapes=[pltpu.VMEM((seqs, RW_SUB, RW_HP * LANES, RW_HP * LANES), F32)],
        compiler_params=_cp("parallel", "parallel", "arbitrary", flags=MIXER_FLAGS),
        name="rwkv7",
    )(*ops)
    return y.reshape(grp.B * grp.L, RW_DIM), s_out


def _ssd_body(*refs, grp):
    r_rows = MIX_ROWS
    qs, bb, nb = grp.qs, grp.bb, grp.nb
    nh, hd, gdim = SSD_HPG, SSD_HD, SSD_GDIM
    cur, halos, rest = _read_windows(grp, refs, 5, 3)
    cw_x, cw_b, cw_c, cb_x, cb_b, cb_c, dtb_ref, alog_ref, dskip_ref, nw_ref, h0_ref, y_ref, h_ref = rest
    c = pl.program_id(2)

    @pl.when(c == 0)
    def _():
        h_ref[...] = h0_ref[...]

    m = _masks(grp)
    eye_h = (_iota2((SSD_HEADS, SSD_HEADS), 0) == _iota2((SSD_HEADS, SSD_HEADS), 1)).astype(F32)
    ei, ej = _iota2((3 * SSD_HEADS, 3 * SSD_DIM), 0), _iota2((3 * SSD_HEADS, 3 * SSD_DIM), 1)
    spread = ((ei // SSD_HEADS == ej // SSD_DIM) & (ei % SSD_HEADS == (ej % SSD_DIM) // hd)).astype(F32)

    def block(s):
        x_raw, b_raw, c_raw, z, dt_blk = (cur[i][s] for i in range(5))
        xs = _silu(_conv(grp, x_raw, halos[0][s], cw_x, cb_x[...], CONV_W))
        bm = _silu(_conv(grp, b_raw, halos[1][s], cw_b, cb_b[...], CONV_W))
        cm = _silu(_conv(grp, c_raw, halos[2][s], cw_c, cb_c[...], CONV_W))

        dt = jnp.where(m.valid, _softplus(dt_blk[:, :SSD_HEADS] + dtb_ref[...]), 0.0)
        cums = _dot_hi(m.cumsum, dt * (-jnp.exp(alog_ref[...])))
        acs, acs_end = cums[:r_rows], cums[r_rows:]
        acs_row = _dot_nt_hi(eye_h, acs)
        lanes = _dot_sel(jnp.concatenate([dt, acs_end - acs, acs], axis=1), spread)
        xd = xs * lanes[:, :SSD_DIM]
        xd_dec = xd * jnp.exp(lanes[:, SSD_DIM:2 * SSD_DIM])
        acs_lanes = lanes[:, 2 * SSD_DIM:]

        def group(g):
            gl = slice(g * gdim, (g + 1) * gdim)
            bmg, cmg = bm[:, g * SSD_STATE:(g + 1) * SSD_STATE], cm[:, g * SSD_STATE:(g + 1) * SSD_STATE]
            cb = _dot_nt(cmg, bmg)
            yield
            yd = []
            for r in range(nh):
                h = g * nh + r
                diff = acs[:, h:h + 1] - acs_row[h:h + 1, :]
                lmat = jnp.where(m.incl, jnp.exp(jnp.where(m.incl, diff, 0.0)), 0.0)
                yd.append(_dot(cb * lmat, xd[:, h * hd:(h + 1) * hd]))
                yield
            y = jnp.concatenate(yd, axis=1)
            head_rows = (_iota2((gdim, SSD_HEADS), 0) // hd + g * nh == _iota2((gdim, SSD_HEADS), 1)).astype(F32)
            yoff = []
            for j in range(bb):
                rows = slice(j * qs, (j + 1) * qs)
                hj = h_ref[s * bb + j, g]
                yoff.append(_dot_nt(cmg[rows], hj))
                yield
                end_col = jnp.exp(_dot_nt_hi(head_rows, acs_end[j * qs:j * qs + SUBLANES]))[:, :1]
                yield
                h_ref[s * bb + j, g] = hj * end_col + _dot_tn(xd_dec[rows, gl], bmg[rows])
                yield
            yoff = yoff[0] if bb == 1 else jnp.concatenate(yoff, axis=0)
            y = y + yoff * jnp.exp(acs_lanes[:, gl]) + xs[:, gl] * dskip_ref[:, gl]
            yg = y * _silu(z[:, gl])
            yg = yg * lax.rsqrt(jnp.mean(yg * yg, axis=-1, keepdims=True) + EPS)
            return yg * nw_ref[:, gl]

        return [group(g) for g in range(SSD_GROUPS)]

    results = _interleave([chain for s in range(nb) for chain in block(s)])
    for s in range(nb):
        y_ref[s] = jnp.concatenate(results[s * SSD_GROUPS:(s + 1) * SSD_GROUPS], axis=1).astype(y_ref.dtype)


def _ssd(grp, proj, hist, h0, prm):
    bc = SSD_GROUPS * SSD_STATE
    toks = [
        _Tok(proj, SSD_DIM, lambda g: AB_X0 // SSD_DIM, hist, lambda g: 0),
        _Tok(proj, bc, lambda g: AB_B0 // bc, hist, lambda g: SSD_DIM // bc),
        _Tok(proj, bc, lambda g: AB_C0 // bc, hist, lambda g: SSD_DIM // bc + 1),
        _Tok(proj, SSD_DIM, lambda g: AB_Z0 // SSD_DIM),
        _Tok(proj, LANES, lambda g: AB_DT0 // LANES),
    ]
    ops, specs = _mixer_specs(grp, toks)
    cw, cbias = prm["conv_w"], prm["conv_b"].reshape(1, SSD_CONV_DIM)
    conv_cols = [(SSD_DIM, lambda g: 0), (bc, lambda g: SSD_DIM // bc), (bc, lambda g: SSD_DIM // bc + 1)]
    params = [(cw, _param_spec(wd, col)(CONV_W)) for wd, col in conv_cols]
    params += [(cbias, _param_spec(wd, col)(1)) for wd, col in conv_cols]
    full = lambda n: pl.BlockSpec((1, n), lambda b, g, c: (0, 0))
    params += [
        (prm["dt_bias"].reshape(1, SSD_HEADS), full(SSD_HEADS)),
        (prm["A_log"].reshape(1, SSD_HEADS), full(SSD_HEADS)),
        (jnp.repeat(prm["D"], SSD_HD).reshape(1, SSD_DIM), full(SSD_DIM)),
        (prm["norm_w"].reshape(1, SSD_DIM), full(SSD_DIM)),
    ]
    ops += [p for p, _ in params]
    specs += [s for _, s in params]
    st_spec = pl.BlockSpec((grp.nb * grp.bb, SSD_GROUPS, SSD_GDIM, SSD_STATE), lambda b, g, c: (b, 0, 0, 0))
    ops.append(h0)
    specs.append(st_spec)
    y, h_out = pl.pallas_call(
        functools.partial(_ssd_body, grp=grp),
        grid=(grp.outer // grp.nb, 1, grp.nc),
        in_specs=specs,
        out_specs=[pl.BlockSpec((grp.nb, MIX_ROWS, SSD_DIM), lambda b, g, c: (b, c, 0)), st_spec],
        out_shape=[jax.ShapeDtypeStruct((grp.outer, grp.per_outer, SSD_DIM), BF16),
                   jax.ShapeDtypeStruct(h0.shape, F32)],
        compiler_params=_cp("parallel", "arbitrary", "arbitrary"),
        name="ssd",
    )(*ops)
    return y.reshape(grp.B * grp.L, SSD_DIM), h_out


def _gdn_body(*refs, grp):
    r_rows = MIX_ROWS
    qs, bb, nb = grp.qs, grp.bb, grp.nb
    nh = GDN_G
    n = nh * r_rows
    cur, halos, rest = _read_windows(grp, refs, 5, 3)
    cw_q, cw_k, cw_v, alog_ref, dtb_ref, nw_ref, s0_ref, o_ref, s_ref = rest
    c = pl.program_id(2)

    @pl.when(c == 0)
    def _():
        s_ref[...] = s0_ref[...]

    m = _masks(grp, nh)
    per_step = nh * GDN_SUB
    eye_g = (_iota2((SUBLANES, nh), 0) == _iota2((SUBLANES, nh), 1)).astype(F32)
    col = lambda x: jnp.concatenate([x[:, h:h + 1] for h in range(nh)], axis=0)

    def block(s):
        q_raw, k_raw, v_raw, z, ba = (cur[i][s] for i in range(5))
        q_all = _silu(_conv(grp, q_raw, halos[0][s], cw_q, None, CONV_W))
        k_all = _silu(_conv(grp, k_raw, halos[1][s], cw_k, None, CONV_W))
        v_all = _silu(_conv(grp, v_raw, halos[2][s], cw_v, None, CONV_W))

        def stack(sub):
            lanes = slice(sub * nh * GDN_D, (sub + 1) * nh * GDN_D)
            heads = slice(sub * nh, (sub + 1) * nh)
            q = _stack_heads(q_all[:, lanes], nh, GDN_D)
            k = _stack_heads(k_all[:, lanes], nh, GDN_D)
            v = _stack_heads(v_all[:, lanes], nh, GDN_D)
            q = q * lax.rsqrt(jnp.sum(q * q, axis=-1, keepdims=True) + 1e-6) * (GDN_D ** -0.5)
            k = k * lax.rsqrt(jnp.sum(k * k, axis=-1, keepdims=True) + 1e-6)

            beta = jnp.where(m.valid, _sigmoid(ba[:, heads]), 0.0)
            a_raw = ba[:, per_step + sub * nh:per_step + (sub + 1) * nh]
            gg = jnp.where(m.valid, -jnp.exp(alog_ref[0][:, heads]) * _softplus(a_raw + dtb_ref[0][:, heads]), 0.0)
            gcs = _dot_hi(m.cumsum, gg)
            yield
            gc, gc_end = gcs[:r_rows], gcs[r_rows:]
            gc_row = _dot_nt_hi(eye_g, gc)
            yield

            bcol, gcol, gend = col(beta), col(gc), col(gc_end)
            grow = jnp.concatenate([gc_row[h:h + 1, :] for h in range(nh)], axis=1)
            dec = jnp.where(m.incl, jnp.exp(jnp.where(m.incl, gcol - grow, 0.0)), 0.0)
            kb = k * bcol
            kq = _dot_nt(jnp.concatenate([kb, q], axis=0), k)
            yield
            lm = -jnp.where(m.strict, kq[:n] * dec, 0.0)
            attn = jnp.where(m.incl, kq[n:] * dec, 0.0)
            eg = jnp.exp(gcol)
            sol = yield from _solve_unit_lower(lm, jnp.concatenate([v * bcol, kb * eg], axis=1), m.eye, qs)
            vw, kcd = sol[:, :GDN_D], sol[:, GDN_D:]
            qg = q * eg
            kg = k * jnp.exp(gend - gcol)

            def state(j):
                return jnp.concatenate([s_ref[s * bb + j, sub * nh + h] for h in range(nh)], axis=0)

            if bb == 1:
                parts = []
                for h in range(nh):
                    rows = slice(h * r_rows, (h + 1) * r_rows)
                    parts.append(_dot(jnp.concatenate([kcd[rows], qg[rows]], axis=0), s_ref[s, sub * nh + h]))
                    yield
                ks = jnp.concatenate([p[:r_rows] for p in parts], axis=0)
                qsv = jnp.concatenate([p[r_rows:] for p in parts], axis=0)
            else:
                parts = []
                for j in range(bb):
                    lhs = jnp.concatenate([_seq_rows(grp, kcd, j, nh), _seq_rows(grp, qg, j, nh)], axis=0)
                    parts.append(_dot(_expand_rows(lhs, nh, qs, GDN_D), state(j)))
                    yield
                half = nh * qs
                ks = _from_seq_rows(grp, [p[:half] for p in parts], nh)
                qsv = _from_seq_rows(grp, [p[half:] for p in parts], nh)
            v_new = vw - ks
            o = qsv + _dot(attn, v_new)
            yield
            for j in range(bb):
                kgj = _expand_rows(_seq_rows(grp, kg, j, nh), nh, qs, GDN_D)
                g_last = jnp.exp(gc_end[j * qs:j * qs + 1, :])
                g_last = jnp.concatenate([jnp.broadcast_to(g_last[:, h:h + 1], (GDN_D, 1)) for h in range(nh)], axis=0)
                s_new = state(j) * g_last + _dot_tn(kgj, _seq_rows(grp, v_new, j, nh))
                yield
                for h in range(nh):
                    s_ref[s * bb + j, sub * nh + h] = s_new[h * GDN_D:(h + 1) * GDN_D]
            o = o * lax.rsqrt(jnp.mean(o * o, axis=-1, keepdims=True) + EPS) * nw_ref[...]
            return _unstack_heads(o, nh)

        def finish(os_):
            o_ref[s] = (jnp.concatenate(os_, axis=1) * _silu(z)).astype(o_ref.dtype)

        return [stack(sub) for sub in range(GDN_SUB)], finish

    blocks = [block(s) for s in range(nb)]
    results = _interleave([chain for chains, _ in blocks for chain in chains])
    for s, (_, finish) in enumerate(blocks):
        finish(results[s * GDN_SUB:(s + 1) * GDN_SUB])


def _zeros_like_axis(x, n, axis):
    shape = list(x.shape)
    shape[axis] = n
    return jnp.zeros(shape, x.dtype)


def _ab_layout(x, axis=-1):
    cut = lambda a, b: lax.slice_in_dim(x, a, b, axis=axis)
    zeros = lambda n: _zeros_like_axis(x, n, axis)
    rw, z = cut(0, RW_PROJ), cut(RW_PROJ, RW_PROJ + SSD_DIM)
    xbc = cut(RW_PROJ + SSD_DIM, RW_PROJ + SSD_DIM + SSD_CONV_DIM)
    dt = cut(RW_PROJ + SSD_DIM + SSD_CONV_DIM, AB_PROJ)
    return jnp.concatenate([z, xbc, dt, zeros(AB_RW0 - AB_DT0 - SSD_HEADS), rw, zeros(AB_COLS - AB_RW0 - RW_PROJ)],
                           axis=axis)


def _gdn_layout(x, axis=-1):
    per_step = GDN_G * GDN_SUB
    main = GDN_CONV_DIM + GDN_V
    cut = lambda a, b: lax.slice_in_dim(x, a, b, axis=axis)
    parts = [cut(0, main)]
    for g in range(GDN_HEADS // per_step):
        h0, h1 = g * per_step, (g + 1) * per_step
        parts += [cut(main + h0, main + h1), cut(main + GDN_HEADS + h0, main + GDN_HEADS + h1),
                  _zeros_like_axis(x, LANES - 2 * per_step, axis)]
    total = main + (GDN_HEADS // per_step) * LANES
    parts.append(_zeros_like_axis(x, -total % MM_TN, axis))
    return jnp.concatenate(parts, axis=axis)


def _gdn(grp, proj, hist, s0, prm):
    per_step = GDN_G * GDN_SUB
    width = per_step * GDN_D
    ng = GDN_HEADS // per_step
    ba_blk = (GDN_CONV_DIM + GDN_V) // LANES
    toks = [
        _Tok(proj, width, lambda g: g, hist, lambda g: g),
        _Tok(proj, width, lambda g: ng + g, hist, lambda g: ng + g),
        _Tok(proj, width, lambda g: 2 * ng + g, hist, lambda g: 2 * ng + g),
        _Tok(proj, width, lambda g: 3 * ng + g),
        _Tok(proj, LANES, lambda g: ba_blk + g),
    ]
    ops, specs = _mixer_specs(grp, toks)
    cw = prm["conv_w"]
    grouped = lambda x: x.reshape(ng, 1, per_step)
    params = [
        (cw, _param_spec(width, lambda g: g)(CONV_W)),
        (cw, _param_spec(width, lambda g: ng + g)(CONV_W)),
        (cw, _param_spec(width, lambda g: 2 * ng + g)(CONV_W)),
        (grouped(prm["A_log"]), pl.BlockSpec((1, 1, per_step), lambda b, g, c: (g, 0, 0))),
        (grouped(prm["dt_bias"]), pl.BlockSpec((1, 1, per_step), lambda b, g, c: (g, 0, 0))),
        (prm["norm_w"].reshape(1, GDN_D), pl.BlockSpec((1, GDN_D), lambda b, g, c: (0, 0))),
    ]
    ops += [p for p, _ in params]
    specs += [s for _, s in params]
    st_spec = pl.BlockSpec((grp.nb * grp.bb, per_step, GDN_D, GDN_D), lambda b, g, c: (b, g, 0, 0))
    ops.append(s0)
    specs.append(st_spec)
    o, s_out = pl.pallas_call(
        functools.partial(_gdn_body, grp=grp),
        grid=(grp.outer // grp.nb, ng, grp.nc),
        in_specs=specs,
        out_specs=[pl.BlockSpec((grp.nb, MIX_ROWS, width), lambda b, g, c: (b, c, g)), st_spec],
        out_shape=[jax.ShapeDtypeStruct((grp.outer, grp.per_outer, GDN_V), BF16),
                   jax.ShapeDtypeStruct(s0.shape, F32)],
        compiler_params=_cp("parallel", "parallel", "arbitrary", flags=MIXER_FLAGS),
        name="gdn",
    )(*ops)
    return o.reshape(grp.B * grp.L, GDN_V), s_out


def _attend(q, k_ref, v_ref, nseq, lq):
    scale = XA_HD ** -0.5
    outs = []
    for j in range(nseq):
        qrows = slice(j * lq, (j + 1) * lq)
        mrows = slice(j * N_MEM, (j + 1) * N_MEM)
        heads = []
        for h in range(XA_HEADS):
            sl = slice(h * XA_HD, (h + 1) * XA_HD)
            s = _dot_nt(q[qrows, sl], k_ref[mrows, sl]) * scale
            p = jnp.exp(s - jnp.max(s, axis=-1, keepdims=True))
            heads.append(_dot(p, v_ref[mrows, sl]) / jnp.sum(p, axis=-1, keepdims=True))
        outs.append(jnp.concatenate(heads, axis=1))
    return outs[0] if nseq == 1 else jnp.concatenate(outs, axis=0)


def _attend_packed(q, k_ref, v_ref, nseq, lq):
    scale = XA_HD ** -0.5
    rows, cols = XA_HEADS * lq, N_MEM * XA_HEADS
    own_head = (_iota2((rows, cols), 0) // lq) == (_iota2((rows, cols), 1) % XA_HEADS)
    outs = []
    for j in range(nseq):
        qh = _stack_heads(q[j * lq:(j + 1) * lq], XA_HEADS, XA_HD)
        mrows = slice(j * cols, (j + 1) * cols)
        s = jnp.where(own_head, _dot_nt(qh, k_ref[mrows, :]) * scale, -1e30)
        p = jnp.where(own_head, jnp.exp(s - jnp.max(s, axis=-1, keepdims=True)), 0.0)
        o = _dot(p, v_ref[mrows, :]) / jnp.sum(p, axis=-1, keepdims=True)
        outs.append(_unstack_heads(o, XA_HEADS))
    return jnp.concatenate(outs, axis=0)


def _xattn_body(*refs, attend, nseq, lq, n_mix):
    n_lead = n_mix + 1 if n_mix else 0
    h_ref, g_ref, wq_ref, k_ref, v_ref, wo_ref, o_ref = refs[n_lead:]
    x, k0 = h_ref[...], 0
    for y_ref in refs[:n_mix]:
        k1 = k0 + y_ref.shape[1]
        x = x + jnp.dot(y_ref[...], refs[n_mix][k0:k1, :], preferred_element_type=F32)
        k0 = k1
    xn = x * lax.rsqrt(jnp.mean(x * x, axis=-1, keepdims=True) + EPS) * g_ref[...]
    q = jnp.dot(xn.astype(BF16), wq_ref[...], preferred_element_type=F32)
    o = attend(q, k_ref, v_ref, nseq, lq)
    o_ref[...] = x + jnp.dot(o.astype(BF16), wo_ref[...], preferred_element_type=F32)


def _xattn(grp, h, mix, w_mix, gain, w_q, w_o, layer, mem_k, mem_v, mem_row0):
    rows, d = h.shape
    if grp.L >= XA_TQ:
        nseq, lq, attend = 1, XA_TQ, _attend
        per = grp.L // XA_TQ
        mem_spec = pl.BlockSpec((N_MEM, XA_DIM), lambda i: (mem_row0 // N_MEM + i // per, 0))
    else:
        nseq, lq, attend = MIX_ROWS // grp.L, grp.L, _attend_packed
        blk = nseq * N_MEM * XA_HEADS
        mem_spec = pl.BlockSpec((blk, XA_HD), lambda i: (mem_row0 * XA_HEADS // blk + i, 0))
    tile = nseq * lq
    h_spec = pl.BlockSpec((tile, d), lambda i: (i, 0))
    lead_ops, lead_specs = [], []
    if mix:
        lead_ops = list(mix) + [w_mix]
        lead_specs = [pl.BlockSpec((tile, y.shape[1]), lambda i: (i, 0)) for y in mix]
        lead_specs.append(pl.BlockSpec(w_mix.shape, lambda i: (0, 0)))
    return pl.pallas_call(
        functools.partial(_xattn_body, attend=attend, nseq=nseq, lq=lq, n_mix=len(mix)),
        grid=(rows // tile,),
        in_specs=lead_specs + [h_spec, pl.BlockSpec((1, d), lambda i: (0, 0)),
                               pl.BlockSpec((None, d, XA_DIM), lambda i: (layer, 0, 0)), mem_spec, mem_spec,
                               pl.BlockSpec((None, XA_DIM, d), lambda i: (layer, 0, 0))],
        out_specs=h_spec,
        out_shape=jax.ShapeDtypeStruct((rows, d), F32),
        compiler_params=_cp("parallel"),
        name="xattn",
    )(*lead_ops, h, gain.reshape(1, d), w_q, mem_k, mem_v, w_o)


def _ffn_in_body(*refs, grp, tiles_per_seq):
    it = iter(refs)
    a_ref, g_ref, wg_ref, wu_ref, hist_ref, cw_ref, cb_ref, act_ref, tail_ref, an_ref = (next(it) for _ in range(10))
    carry_ref = None if grp.embedded else next(it)
    i, j = pl.program_id(0), pl.program_id(1)

    @pl.when(j == 0)
    def _():
        x = a_ref[...]
        y = x * lax.rsqrt(jnp.mean(x * x, axis=-1, keepdims=True) + EPS)
        an_ref[...] = (y * g_ref[...]).astype(BF16)

    a = an_ref[...]
    gate = jnp.dot(a, wg_ref[...].astype(BF16), preferred_element_type=F32)
    up = jnp.dot(a, wu_ref[...].astype(BF16), preferred_element_type=F32)
    if grp.embedded:
        halo = hist_ref[...]
        tail_ref[...] = gate
    else:
        halo = jnp.where(i % tiles_per_seq == 0, hist_ref[...], carry_ref[j])
        last = gate[gate.shape[0] - SUBLANES:]
        carry_ref[j] = last
        tail_ref[...] = last
    conv = _conv(grp, gate, halo, cw_ref, cb_ref[...], FFN_CONV_W)
    act_ref[...] = (_silu(conv) * up).astype(act_ref.dtype)


def _ffn_in(grp, h, gain, w_in, layer, hist, conv_w, conv_b, tm=MM_TM, tn=MM_TN):
    assert grp.embedded or grp.L % tm == 0
    rows, k = h.shape
    nct = D_FF // tn
    tiles_per_seq = max(grp.L // tm, 1)
    hrows = tm if grp.embedded else SUBLANES
    tile_idx = lambda i, j: (i, j)
    hist_idx = tile_idx if grp.embedded else (lambda i, j: (i // tiles_per_seq, j))
    specs = [
        pl.BlockSpec((tm, k), lambda i, j: (i, 0)),
        pl.BlockSpec((1, k), lambda i, j: (0, 0)),
        _weight_spec(w_in, layer, k, tn, lambda j: j),
        _weight_spec(w_in, layer, k, tn, lambda j: nct + j),
        pl.BlockSpec((hrows, tn), hist_idx),
        pl.BlockSpec((FFN_CONV_W, tn), lambda i, j: (0, j)),
        pl.BlockSpec((1, tn), lambda i, j: (0, j)),
    ]
    scratch = [pltpu.VMEM((tm, k), BF16)]
    if not grp.embedded:
        scratch.append(pltpu.VMEM((nct, SUBLANES, tn), F32))
    act, tails = pl.pallas_call(
        functools.partial(_ffn_in_body, grp=grp, tiles_per_seq=tiles_per_seq),
        grid=(rows // tm, nct),
        in_specs=specs,
        out_specs=[pl.BlockSpec((tm, tn), tile_idx), pl.BlockSpec((hrows, tn), tile_idx)],
        out_shape=[jax.ShapeDtypeStruct((rows, D_FF), BF16), jax.ShapeDtypeStruct((rows // tm * hrows, D_FF), F32)],
        scratch_shapes=scratch,
        compiler_params=_cp("arbitrary", "arbitrary"),
        name="ffn_in",
    )(h, gain.reshape(1, k), w_in, w_in, hist, conv_w, conv_b.reshape(1, D_FF))
    if not grp.embedded:
        tails = tails.reshape(grp.B, tiles_per_seq, SUBLANES, D_FF)[:, -1].reshape(grp.B * SUBLANES, D_FF)
    return act, tails


def _history(grp, buf, width):
    b, k, c = buf.shape
    if grp.embedded:
        h = jnp.pad(buf, ((0, 0), (grp.first - k, grp.L - grp.first), (0, 0)))
        return h.reshape(b * grp.L, c)
    return jnp.pad(buf, ((0, 0), (SUBLANES - k, 0), (0, 0))).reshape(b * SUBLANES, c)


def _tail(grp, x2d, k, cols):
    x = x2d.reshape(grp.B, grp.L, x2d.shape[1])
    return x[:, grp.L - k:, cols].astype(F32)


def _run_group(grp, x2d, mem_k, mem_v, mem_row0, st, wts, prm):
    h = x2d
    out = {}
    tm_proj = min(MM_TM_PROJ, h.shape[0])
    proj = _matmul(h, wts["in_ab"], gain=prm["norm_mix"][0], out_dtype=BF16, tm=tm_proj, w_rows_are_outputs=True)
    y_rw, s_rw = _rwkv(grp, proj, _history(grp, st["rw_shift"][:, None, :], RW_PROJ), st["rwkv"], prm["rw"])
    y_ssd, s_ssd = _ssd(grp, proj, _history(grp, st["ssd_conv"], SSD_CONV_DIM),
                        st["ssd"].reshape(grp.B, SSD_GROUPS, SSD_GDIM, SSD_STATE), prm["ssd"])
    out["rwkv"] = s_rw
    out["rw_shift"] = _tail(grp, proj, 1, slice(AB_RW0, AB_RW0 + RW_PROJ))[:, 0]
    out["ssd"] = s_ssd.reshape(st["ssd"].shape)
    out["ssd_conv"] = _tail(grp, proj, CONV_W - 1, slice(AB_X0, AB_X0 + SSD_CONV_DIM))
    mix, w_mix = [y_rw, y_ssd], wts["out_ab"]
    ffn_bufs = []
    for l in range(2):
        if l == 1:
            proj = _matmul(h, wts["in_c"], gain=prm["norm_mix"][1], out_dtype=BF16, tm=tm_proj, w_rows_are_outputs=True)
            y_c, s_gdn = _gdn(grp, proj, _history(grp, st["gdn_conv"], GDN_CONV_DIM), st["gdn"], prm["gdn"])
            out["gdn"] = s_gdn
            out["gdn_conv"] = _tail(grp, proj, CONV_W - 1, slice(0, GDN_CONV_DIM))
            mix, w_mix = [y_c], wts["out_c"]
        if grp.embedded:
            h, mix = _matmul(mix, w_mix, res=h), []
        h = _xattn(grp, h, mix, w_mix, prm["norm_xa"][l], wts["xq"], wts["xo"], l, mem_k, mem_v, mem_row0[l])
        act, gate_tail = _ffn_in(grp, h, prm["norm_ffn"][l], wts["ffn_in"], l, _history(grp, st["ffn_conv"][l], D_FF),
                                 prm["ffn_conv_w"][l], prm["ffn_conv_b"][l])
        ffn_bufs.append(gate_tail.reshape(grp.B, SUBLANES, D_FF)[:, SUBLANES - (FFN_CONV_W - 1):])
        h = _matmul(act, wts["ffn_out"], layer=l, res=h)
    out["ffn_conv"] = jnp.stack(ffn_bufs)
    out["y"] = _rmsnorm(h, prm["norm_final"])
    return out


def kernel(x_prompt, x_sample, mem_prompt, state_rwkv, state_rwkv_shift, state_ssd, state_ssd_conv, state_gdn,
           state_gdn_conv, state_ffn_conv, cache_mem_k, cache_mem_v, norm_mix, norm_xa, norm_mem, norm_ffn,
           norm_final, w_in_ab, rw_mu, rw_w0, rw_w_up, rw_a0, rw_a_up, rw_g_up, rw_k_k, rw_k_a, rw_r_k, rw_gn_w,
           rw_gn_b, ssd_conv_w, ssd_conv_b, ssd_dt_bias, ssd_A_log, ssd_D, ssd_norm_w, w_out_ab, w_in_c,
           gdn_conv_w, gdn_A_log, gdn_dt_bias, gdn_norm_w, w_out_c, w_xq, w_xk, w_xv, w_xo, ffn_w_in, ffn_conv_w,
           ffn_conv_b, ffn_w_out):
    bp, lp, _ = x_prompt.shape
    bs, ls, _ = x_sample.shape
    depth = norm_mix.shape[0]
    assert depth == 2 and w_in_ab.shape[0] == 1 and w_in_c.shape[0] == 1
    assert lp % MIX_ROWS == 0 and lp % XA_TQ == 0 and lp % MM_TM == 0
    pad_rows = SUBLANES - ls
    assert CONV_W - 1 <= pad_rows and CONV_W - 1 <= ls and MIX_ROWS % SUBLANES == 0
    gp = Group(B=bp, L=lp, first=0, bb=1, nb=MIX_NB if bp % MIX_NB == 0 else 1)
    gs = Group(B=bs, L=SUBLANES, first=pad_rows, bb=MIX_ROWS // SUBLANES)
    assert bs % gs.bb == 0 and (bs * SUBLANES) % MM_TM == 0 and (bp * lp) % MM_TM == 0

    wts = dict(
        in_ab=_ab_layout(jnp.swapaxes(w_in_ab[0], 0, 1), axis=0).astype(BF16), out_ab=w_out_ab[0].astype(BF16),
        in_c=_gdn_layout(jnp.swapaxes(w_in_c[0], 0, 1), axis=0).astype(BF16), out_c=w_out_c[0].astype(BF16),
        xq=w_xq.astype(BF16), xo=w_xo.astype(BF16), ffn_in=ffn_w_in, ffn_out=ffn_w_out.astype(BF16),
    )
    prm = dict(
        norm_mix=norm_mix, norm_xa=norm_xa, norm_ffn=norm_ffn, norm_final=norm_final,
        ffn_conv_w=ffn_conv_w, ffn_conv_b=ffn_conv_b,
        rw=dict(mu=rw_mu[0], w0=rw_w0[0], w_up=rw_w_up[0], a0=rw_a0[0], a_up=rw_a_up[0], g_up=rw_g_up[0],
                k_k=rw_k_k[0], k_a=rw_k_a[0], r_k=rw_r_k[0], gn_w=rw_gn_w[0], gn_b=rw_gn_b[0]),
        ssd=dict(conv_w=ssd_conv_w[0], conv_b=ssd_conv_b[0], dt_bias=ssd_dt_bias[0], A_log=ssd_A_log[0],
                 D=ssd_D[0], norm_w=ssd_norm_w[0]),
        gdn=dict(conv_w=gdn_conv_w[0], A_log=gdn_A_log[0], dt_bias=gdn_dt_bias[0], norm_w=gdn_norm_w[0]),
    )

    mem2d = mem_prompt.reshape(bp * N_MEM, D_MODEL)
    mem_tm = min(MM_TM, bp * N_MEM)
    mk, mv = [], []
    for l in range(depth):
        mk.append(_matmul(mem2d, w_xk[l].astype(BF16), gain=norm_mem[l], tm=mem_tm))
        mv.append(_matmul(mem2d, w_xv[l].astype(BF16), gain=norm_mem[l], tm=mem_tm))
    mem_k_p = jnp.stack(mk)
    mem_v_p = jnp.stack(mv)

    zeros = lambda *s: jnp.zeros(s, F32)
    st_p = dict(
        rwkv=zeros(bp, RW_HEADS, RW_HD, RW_HD), rw_shift=zeros(bp, RW_PROJ),
        ssd=zeros(bp, SSD_GROUPS, SSD_HPG, SSD_HD, SSD_STATE), ssd_conv=zeros(bp, CONV_W - 1, SSD_CONV_DIM),
        gdn=zeros(bp, GDN_HEADS, GDN_D, GDN_D), gdn_conv=zeros(bp, CONV_W - 1, GDN_CONV_DIM),
        ffn_conv=zeros(depth, bp, FFN_CONV_W - 1, D_FF),
    )
    rp = _run_group(gp, x_prompt.reshape(bp * lp, D_MODEL), mem_k_p.reshape(depth * bp * N_MEM, XA_DIM),
                    mem_v_p.reshape(depth * bp * N_MEM, XA_DIM), [l * bp * N_MEM for l in range(depth)],
                    st_p, wts, prm)

    st_s = dict(rwkv=state_rwkv[0], rw_shift=state_rwkv_shift[0], ssd=state_ssd[0], ssd_conv=state_ssd_conv[0],
                gdn=state_gdn[0], gdn_conv=state_gdn_conv[0], ffn_conv=state_ffn_conv)
    xs = jnp.pad(x_sample, ((0, 0), (pad_rows, 0), (0, 0))).reshape(bs * SUBLANES, D_MODEL)
    rs = _run_group(gs, xs, cache_mem_k.reshape(depth * bs * N_MEM * XA_HEADS, XA_HD),
                    cache_mem_v.reshape(depth * bs * N_MEM * XA_HEADS, XA_HD), [l * bs * N_MEM for l in range(depth)],
                    st_s, wts, prm)

    y_p = rp["y"].reshape(bp, lp, D_MODEL)
    y_s = rs["y"].reshape(bs, SUBLANES, D_MODEL)[:, pad_rows:]
    lead = lambda x: x[None]
    mem_shape = (depth, bp, N_MEM, XA_HEADS, XA_HD)
    return (y_p, y_s, lead(rp["rwkv"]), lead(rs["rwkv"]), lead(rp["rw_shift"]), lead(rs["rw_shift"]),
            lead(rp["ssd"]), lead(rs["ssd"]), lead(rp["ssd_conv"]), lead(rs["ssd_conv"]),
            lead(rp["gdn"]), lead(rs["gdn"]), lead(rp["gdn_conv"]), lead(rs["gdn_conv"]),
            rp["ffn_conv"], rs["ffn_conv"], mem_k_p.reshape(mem_shape), mem_v_p.reshape(mem_shape))
```

```python
import functools
from typing import NamedTuple

import jax
import jax.numpy as jnp
from jax import lax
from jax.experimental import pallas as pl
from jax.experimental.pallas import tpu as pltpu

F32 = jnp.float32
BF16 = jnp.bfloat16

D_MODEL = 2048
EPS = 1e-6
RW_HEADS, RW_HD = 16, 64
RW_DIM = RW_HEADS * RW_HD
RW_LORA = 256
RW_PROJ = 3 * RW_DIM + RW_LORA
RW_GN_EPS = 6.4e-4
SSD_HEADS, SSD_HD, SSD_GROUPS, SSD_STATE = 16, 64, 2, 128
SSD_HPG = SSD_HEADS // SSD_GROUPS
SSD_DIM = SSD_HEADS * SSD_HD
SSD_GDIM = SSD_HPG * SSD_HD
SSD_CONV_DIM = SSD_DIM + 2 * SSD_GROUPS * SSD_STATE
SSD_PROJ = SSD_DIM + SSD_CONV_DIM + SSD_HEADS
AB_PROJ = RW_PROJ + SSD_PROJ
GDN_HEADS, GDN_D = 16, 128
GDN_V = GDN_HEADS * GDN_D
GDN_CONV_DIM = 3 * GDN_V
GDN_PROJ = GDN_CONV_DIM + GDN_V + 2 * GDN_HEADS
D_FF = 5632
XA_HEADS, XA_HD, N_MEM = 4, 128, 256
XA_DIM = XA_HEADS * XA_HD
CONV_W = 4
FFN_CONV_W = 3

AB_Z0 = 0
AB_X0 = SSD_DIM
AB_B0 = 2 * SSD_DIM
AB_C0 = AB_B0 + SSD_GROUPS * SSD_STATE
AB_DT0 = AB_C0 + SSD_GROUPS * SSD_STATE
AB_RW0 = 3 * RW_DIM
AB_COLS = 6656
assert AB_DT0 + 128 <= AB_RW0 and AB_RW0 + RW_PROJ <= AB_COLS and AB_RW0 % RW_DIM == 0

SUBLANES = 8
LANES = 128
VMEM_LIMIT = 56 * 1024 * 1024

MIX_ROWS = 64
INV_BLOCK = 16
INV_SPLIT = 128
MIX_NB = 2
MM_TM, MM_TN = 1024, 512
MM_TM_PROJ = 2048
RW_HP = 2
RW_SUB = 4
GDN_G = 4
GDN_SUB = 4
XA_TQ = 512
NORM_TM = 512


class Group(NamedTuple):
    B: int
    L: int
    first: int
    bb: int
    nb: int = 1

    @property
    def outer(self):
        return self.B // self.bb

    @property
    def per_outer(self):
        return self.bb * self.L

    @property
    def qs(self):
        return MIX_ROWS // self.bb

    @property
    def nc(self):
        return self.L // self.qs

    @property
    def embedded(self):
        return self.first > 0


MIXER_FLAGS = None


def _cp(*sem, flags=None):
    return pltpu.CompilerParams(dimension_semantics=sem, vmem_limit_bytes=VMEM_LIMIT, flags=flags)


def _dot(a, b):
    return jnp.dot(a.astype(BF16), b.astype(BF16), preferred_element_type=F32)


def _dot_nt(a, b):
    return lax.dot_general(a.astype(BF16), b.astype(BF16), (((1,), (1,)), ((), ())), preferred_element_type=F32)


def _dot_tn(a, b):
    return lax.dot_general(a.astype(BF16), b.astype(BF16), (((0,), (0,)), ((), ())), preferred_element_type=F32)


def _split_bf16(x):
    hi = x.astype(BF16)
    return hi, (x - hi.astype(F32)).astype(BF16)


def _dot_sel(a, sel):
    ah, al = _split_bf16(a)
    sb = sel.astype(BF16)
    return jnp.dot(jnp.concatenate([ah, al], axis=1), jnp.concatenate([sb, sb], axis=0), preferred_element_type=F32)


def _sel_dot(sel, b):
    bh, bl = _split_bf16(b)
    sb = sel.astype(BF16)
    return jnp.dot(jnp.concatenate([sb, sb], axis=1), jnp.concatenate([bh, bl], axis=0), preferred_element_type=F32)


def _sel_dot_nt(sel, b):
    bh, bl = _split_bf16(b)
    sb = sel.astype(BF16)
    return lax.dot_general(jnp.concatenate([sb, sb], axis=1), jnp.concatenate([bh, bl], axis=1),
                           (((1,), (1,)), ((), ())), preferred_element_type=F32)


def _sigmoid(x):
    return 1.0 / (1.0 + jnp.exp(-x))


def _silu(x):
    return x * _sigmoid(x)


def _softplus(x):
    return jnp.maximum(x, 0.0) + jnp.log(1.0 + jnp.exp(-jnp.abs(x)))


def _iota2(shape, axis):
    return lax.broadcasted_iota(jnp.int32, shape, axis)


class _Masks(NamedTuple):
    incl: jax.Array
    strict: jax.Array
    eye: jax.Array
    cumsum: jax.Array
    valid: jax.Array


def _masks(grp, heads=1):
    r = MIX_ROWS
    n = heads * r
    ri, ci = _iota2((n, n), 0), _iota2((n, n), 1)
    same = (ri // grp.qs) == (ci // grp.qs)
    incl = same & (ci <= ri)
    strict = same & (ci < ri)
    si, sj = _iota2((2 * r, r), 0), _iota2((2 * r, r), 1)
    same_seq = ((si % r) // grp.qs) == (sj // grp.qs)
    cumsum = (same_seq & ((sj <= si) | (si >= r))).astype(F32)
    t = _iota2((r, 1), 0) % grp.qs
    return _Masks(incl, strict, (ri == ci).astype(F32), cumsum, t >= grp.first)


def _neumann(lm, rhs, nil, eye=None):
    n = lm.shape[0]
    p = None
    if rhs is None:
        t = eye + lm
        if nil > 2:
            p = _dot(lm, lm)
            yield
    elif nil > 2:
        both = _dot(lm, jnp.concatenate([lm, rhs], axis=1))
        yield
        p, t = both[:, :n], rhs + both[:, n:]
    else:
        t = rhs + _dot(lm, rhs)
        yield
    k = 2
    while k < nil:
        if 2 * k < nil:
            both = _dot(p, jnp.concatenate([p, t], axis=1))
            p, t = both[:, :n], t + both[:, n:]
        else:
            t = t + _dot(p, t)
        yield
        k *= 2
    return t


def _inv_unit_lower(lm, eye, qs):
    if qs <= INV_BLOCK:
        return (yield from _neumann(lm, None, qs, eye))
    n = lm.shape[0]
    diag = (_iota2((n, n), 0) // INV_BLOCK) == (_iota2((n, n), 1) // INV_BLOCK)
    d = jnp.where(diag, lm, 0.0)
    dinv = yield from _neumann(d, None, INV_BLOCK, eye)
    nm = _dot(dinv, lm - d)
    yield
    return (yield from _neumann(nm, dinv, qs // INV_BLOCK))


def _lockstep(gens):
    results = [None] * len(gens)
    active = list(range(len(gens)))
    while active:
        for i in list(active):
            try:
                next(gens[i])
            except StopIteration as stop:
                results[i] = stop.value
                active.remove(i)
        if active:
            yield
    return results


def _solve_unit_lower(lm, rhs, eye, qs):
    n = lm.shape[0]
    if n > INV_SPLIT:
        blocks = [slice(i, i + INV_SPLIT) for i in range(0, n, INV_SPLIT)]
        inv = yield from _lockstep([_inv_unit_lower(lm[b, b], eye[b, b], qs) for b in blocks])
        zero = jnp.zeros((INV_SPLIT, INV_SPLIT), F32)
        t = jnp.concatenate([jnp.concatenate([inv[i] if i == j else zero for j in range(len(blocks))], axis=1)
                             for i in range(len(blocks))], axis=0)
    else:
        t = yield from _inv_unit_lower(lm, eye, qs)
    x = _dot(t, rhs)
    yield
    return x


def _interleave(gens):
    results = [None] * len(gens)
    active = list(range(len(gens)))
    while active:
        for i in list(active):
            try:
                next(gens[i])
            except StopIteration as stop:
                results[i] = stop.value
                active.remove(i)
    return results


def _stack_heads(x, heads, width):
    return jnp.concatenate([x[:, h * width:(h + 1) * width] for h in range(heads)], axis=0)


def _unstack_heads(x, heads):
    r = x.shape[0] // heads
    return jnp.concatenate([x[h * r:(h + 1) * r] for h in range(heads)], axis=1)


def _seq_rows(grp, x, j, heads):
    if grp.bb == 1:
        return x
    qs = grp.qs
    return jnp.concatenate([x[h * MIX_ROWS + j * qs:h * MIX_ROWS + (j + 1) * qs] for h in range(heads)], axis=0)


def _from_seq_rows(grp, parts, heads):
    if grp.bb == 1:
        return parts[0]
    qs = grp.qs
    return jnp.concatenate([p[h * qs:(h + 1) * qs] for h in range(heads) for p in parts], axis=0)


def _expand_rows(x, heads, rows_per_head, width):
    m = x.shape[0]
    if x.shape[1] == width:
        x = jnp.concatenate([x] * heads, axis=1)
    keep = ((_iota2((m, heads * width), 0) // rows_per_head) % heads) == (_iota2((m, heads * width), 1) // width)
    return jnp.where(keep, x, 0.0)


def _taps(grp, cur, halo, width):
    if grp.embedded:
        t = _iota2((cur.shape[0], 1), 0) % grp.qs
        full = jnp.where(t < grp.first, halo, cur)
        return [full] + [pltpu.roll(full, s, axis=0) for s in range(1, width)]
    full = jnp.concatenate([halo, cur], axis=0)
    return [cur] + [pltpu.roll(full, s, axis=0)[SUBLANES:] for s in range(1, width)]


def _conv(grp, cur, halo, w_ref, bias, width):
    taps = _taps(grp, cur, halo, width)
    y = taps[0] * w_ref[width - 1:width, :]
    for s in range(1, width):
        y = y + taps[s] * w_ref[width - 1 - s:width - s, :]
    return y if bias is None else y + bias


class _Tok(NamedTuple):
    arr: jax.Array
    width: int
    col: object
    hist: jax.Array = None
    hcol: object = None


def _view3(grp, x2d):
    return x2d.reshape(grp.outer, grp.per_outer, x2d.shape[1])


def _mixer_specs(grp, toks):
    nb = grp.nb
    ops, specs = [], []
    for t in toks:
        ops.append(_view3(grp, t.arr))
        specs.append(pl.BlockSpec((nb, MIX_ROWS, t.width), lambda b, g, c, t=t: (b, c, t.col(g))))
    for t in toks:
        if t.hist is None:
            continue
        if grp.embedded:
            ops.append(_view3(grp, t.hist))
            specs.append(pl.BlockSpec((nb, MIX_ROWS, t.width), lambda b, g, c, t=t: (b, 0, t.hcol(g))))
            continue
        tile = _sublane_tile(t.arr.dtype)
        sub = MIX_ROWS // tile
        ops.append(_view3(grp, t.arr))
        specs.append(pl.BlockSpec(
            (nb, tile, t.width), lambda b, g, c, t=t, sub=sub: (b, jnp.maximum(c * sub - 1, 0), t.col(g))))
        ops.append(t.hist.reshape(grp.B, SUBLANES, t.hist.shape[1]))
        specs.append(pl.BlockSpec((nb, SUBLANES, t.width), lambda b, g, c, t=t: (b, 0, t.hcol(g))))
    return ops, specs


def _read_windows(grp, refs, n_plain, n_hist):
    c = pl.program_id(2)
    nb = grp.nb
    cur = [[r[s].astype(F32) for s in range(nb)] for r in refs[:n_plain]]
    rest = refs[n_plain:]
    halos = []
    for i in range(n_hist):
        if grp.embedded:
            halos.append([rest[i][s] for s in range(nb)])
        else:
            per = []
            for s in range(nb):
                prev = rest[2 * i][s]
                prev = prev[prev.shape[0] - SUBLANES:].astype(F32)
                per.append(jnp.where(c == 0, rest[2 * i + 1][s], prev))
            halos.append(per)
    used = n_hist if grp.embedded else 2 * n_hist
    return cur, halos, rest[used:]


def _sublane_tile(dtype):
    return SUBLANES * 4 // jnp.dtype(dtype).itemsize


def _param_spec(width, col):
    return lambda rows: pl.BlockSpec((rows, width), lambda b, g, c: (0, col(g)))


def _matmul_body(*refs, norm, has_res, n_a, w_rows_are_outputs):
    it = iter(refs)
    a_refs = [next(it) for _ in range(n_a)]
    g_ref = next(it) if norm else None
    w_ref = next(it)
    res_ref = next(it) if has_res else None
    o_ref = next(it)
    if norm:
        an_ref = next(it)

        @pl.when(pl.program_id(1) == 0)
        def _():
            x = a_refs[0][...]
            y = x * lax.rsqrt(jnp.mean(x * x, axis=-1, keepdims=True) + EPS)
            an_ref[...] = (y * g_ref[...]).astype(BF16)

        if w_rows_are_outputs:
            acc = lax.dot_general(an_ref[...], w_ref[...], (((1,), (1,)), ((), ())), preferred_element_type=F32)
        else:
            acc = jnp.dot(an_ref[...], w_ref[...], preferred_element_type=F32)
    else:
        acc, k0 = None, 0
        for a_ref in a_refs:
            k1 = k0 + a_ref.shape[1]
            part = jnp.dot(a_ref[...], w_ref[k0:k1, :], preferred_element_type=F32)
            acc = part if acc is None else acc + part
            k0 = k1
    if has_res:
        acc = acc + res_ref[...]
    o_ref[...] = acc.astype(o_ref.dtype)


def _matmul(a, w, *, layer=None, gain=None, res=None, out_dtype=F32, tm=MM_TM, tn=MM_TN, w_rows_are_outputs=False):
    a_list = list(a) if isinstance(a, (list, tuple)) else [a]
    m = a_list[0].shape[0]
    k, n = w.shape[-2:][::-1] if w_rows_are_outputs else w.shape[-2:]
    assert not w_rows_are_outputs or (gain is not None and w.ndim == 2)
    assert sum(x.shape[1] for x in a_list) == k
    norm = gain is not None
    assert not norm or len(a_list) == 1
    ops = list(a_list)
    specs = [pl.BlockSpec((tm, x.shape[1]), lambda i, j: (i, 0)) for x in a_list]
    if norm:
        ops.append(gain.reshape(1, k))
        specs.append(pl.BlockSpec((1, k), lambda i, j: (0, 0)))
    ops.append(w)
    if w_rows_are_outputs:
        specs.append(pl.BlockSpec((tn, k), lambda i, j: (j, 0)))
    else:
        specs.append(_weight_spec(w, layer, k, tn, lambda j: j))
    if res is not None:
        ops.append(res)
        specs.append(pl.BlockSpec((tm, tn), lambda i, j: (i, j)))
    return pl.pallas_call(
        functools.partial(_matmul_body, norm=norm, has_res=res is not None, n_a=len(a_list),
                          w_rows_are_outputs=w_rows_are_outputs),
        grid=(m // tm, n // tn),
        in_specs=specs,
        out_specs=pl.BlockSpec((tm, tn), lambda i, j: (i, j)),
        out_shape=jax.ShapeDtypeStruct((m, n), out_dtype),
        scratch_shapes=[pltpu.VMEM((tm, k), BF16)] if norm else [],
        compiler_params=_cp("parallel", "arbitrary"),
        name="matmul",
    )(*ops)


def _weight_spec(w, layer, k, tn, col):
    if w.ndim == 2:
        return pl.BlockSpec((k, tn), lambda i, j: (0, col(j)))
    return pl.BlockSpec((None, k, tn), lambda i, j: (layer, 0, col(j)))


def _rmsnorm_body(x_ref, g_ref, o_ref):
    x = x_ref[...]
    y = x * lax.rsqrt(jnp.mean(x * x, axis=-1, keepdims=True) + EPS)
    o_ref[...] = (y * g_ref[...]).astype(o_ref.dtype)


def _rmsnorm(x, gain, out_dtype=F32, tm=NORM_TM):
    m, k = x.shape
    return pl.pallas_call(
        _rmsnorm_body,
        grid=(m // tm,),
        in_specs=[pl.BlockSpec((tm, k), lambda i: (i, 0)), pl.BlockSpec((1, k), lambda i: (0, 0))],
        out_specs=pl.BlockSpec((tm, k), lambda i: (i, 0)),
        out_shape=jax.ShapeDtypeStruct((m, k), out_dtype),
        compiler_params=_cp("parallel"),
        name="rmsnorm",
    )(x, gain.reshape(1, k))


def _rwkv_body(*refs, grp):
    r_rows = MIX_ROWS
    qs, bb, nc, nb = grp.qs, grp.bb, grp.nc, grp.nb
    nh = 2 * RW_HP
    sw = nh * RW_HD
    width = RW_SUB * sw
    n = nh * r_rows
    cur, halos, rest = _read_windows(grp, refs, 4, 4)
    (mu_r, mu_k, mu_v, mu_lo, w0_ref, a0_ref, lora_ref, kk_ref, ka_ref, rk_ref, gnw_ref, gnb_ref,
     s0_ref, y_ref, s_ref, sbig_ref) = rest
    c = pl.program_id(2)

    @pl.when(c == 0)
    def _():
        for j in range(nb * bb):
            for sub in range(RW_SUB):
                st = jnp.concatenate([s0_ref[j, sub * nh + h] for h in range(nh)], axis=0)
                sbig_ref[j, sub] = _expand_rows(st, nh, RW_HD, RW_HD)

    m = _masks(grp, nh)
    bd = ((_iota2((sw, sw), 0) // RW_HD) == (_iota2((sw, sw), 1) // RW_HD)).astype(F32)

    def head_sums(x):
        return _unstack_heads(_dot_sel(_stack_heads(x, RW_SUB, sw), bd), RW_SUB)

    def block(s):
        def shifted(i, mu_ref):
            x = cur[i][s]
            prev = _taps(grp, x, halos[i][s], 2)[1]
            return x + (prev - x) * mu_ref[...]

        r = shifted(0, mu_r)
        k = shifted(1, mu_k)
        v = shifted(2, mu_v)
        lo = shifted(3, mu_lo)
        lora_in = jnp.concatenate([jnp.tanh(lo[:, :64]), lo[:, 64:128], _sigmoid(lo[:, 128:256])], axis=1)
        lora = _dot(lora_in, lora_ref[0])
        w = -_softplus(-(w0_ref[...] + lora[:, :width])) - 0.5
        lw = jnp.where(m.valid, -jnp.exp(w), 0.0)
        a = _sigmoid(a0_ref[...] + lora[:, width:2 * width])
        g = lora[:, 2 * width:]

        kx = k * kk_ref[...]
        kp = jnp.where(m.valid, k * (1.0 + (a - 1.0) * ka_ref[...]), 0.0)
        sums = head_sums(jnp.concatenate([kx * kx, r * kp * rk_ref[...]], axis=0))
        kkn = kx * lax.rsqrt(sums[:r_rows] + 1e-6)
        bonus = sums[r_rows:] * v
        at = jnp.where(m.valid, -kkn, 0.0)
        bt = jnp.where(m.valid, kkn * a, 0.0)

        cums = _sel_dot(m.cumsum, lw)
        cum, cum_end = cums[:r_rows], cums[r_rows:]
        e_neg = jnp.exp(-cum)
        e_end = jnp.exp(cum_end - cum)
        p_end = jnp.exp(cum_end)
        a_til, r_til = at * jnp.exp(cum - lw), r * jnp.exp(cum)
        b_til, k_til = bt * e_neg, kp * e_neg
        b_hat, k_hat = bt * e_end, kp * e_end

        def stack(sub):
            lanes = slice(sub * sw, (sub + 1) * sw)

            def tile(x):
                return _expand_rows(jnp.concatenate([x[:, lanes]] * nh, axis=0), nh, r_rows, RW_HD)

            ar = jnp.concatenate([tile(a_til), tile(r_til)], axis=0)
            bk = jnp.concatenate([tile(b_til), tile(k_til)], axis=0)
            bk_end = jnp.concatenate([tile(b_hat), tile(k_hat)], axis=0)
            v_exp = tile(v)
            gm = _dot_nt(ar, bk)
            yield
            m_ab = jnp.where(m.strict, gm[:n, :n], 0.0)
            m_ak = jnp.where(m.strict, gm[:n, n:], 0.0)
            m_r = jnp.concatenate([jnp.where(m.incl, gm[n:, :n], 0.0), jnp.where(m.incl, gm[n:, n:], 0.0)], axis=1)

            def seq2(x, j):
                if bb == 1:
                    return x
                return jnp.concatenate([_seq_rows(grp, x[:n], j, nh), _seq_rows(grp, x[n:], j, nh)], axis=0)

            parts = []
            for j in range(bb):
                parts.append(_dot_nt(seq2(ar, j), sbig_ref[s * bb + j, sub]))
                yield
            half = nh * qs
            as_a = _from_seq_rows(grp, [p[:half] for p in parts], nh)
            as_r = _from_seq_rows(grp, [p[half:] for p in parts], nh)
            rhs = as_a + _dot(m_ak, v_exp)
            yield
            u = yield from _solve_unit_lower(m_ab, rhs, m.eye, qs)
            uv = jnp.concatenate([u, v_exp], axis=0)
            y_exp = as_r + _dot(m_r, uv)
            yield
            y = y_exp[:r_rows]
            for h in range(1, nh):
                y = y + y_exp[h * r_rows:(h + 1) * r_rows]
            for j in range(bb):
                sbig_ref[s * bb + j, sub] = (sbig_ref[s * bb + j, sub] * p_end[j * qs:j * qs + 1, lanes]
                                             + _dot_tn(seq2(uv, j), seq2(bk_end, j)))
                yield
            return y

        def finish(ys):
            y = jnp.concatenate(ys, axis=1)
            mean = head_sums(y) * (1.0 / RW_HD)
            d = y - mean
            var = head_sums(d * d) * (1.0 / RW_HD)
            out = d * lax.rsqrt(var + RW_GN_EPS) * gnw_ref[...] + gnb_ref[...] + bonus
            y_ref[s] = (out * g).astype(y_ref.dtype)

        return [stack(sub) for sub in range(RW_SUB)], finish

    blocks = [block(s) for s in range(nb)]
    results = _interleave([chain for chains, _ in blocks for chain in chains])
    for s, (_, finish) in enumerate(blocks):
        finish(results[s * RW_SUB:(s + 1) * RW_SUB])

    @pl.when(c == nc - 1)
    def _():
        for j in range(nb * bb):
            for sub in range(RW_SUB):
                sb = sbig_ref[j, sub]
                for h in range(nh):
                    s_ref[j, sub * nh + h] = sb[h * RW_HD:(h + 1) * RW_HD, h * RW_HD:(h + 1) * RW_HD]


def _rwkv(grp, proj, hist, s0, prm):
    width = RW_SUB * RW_HP * LANES
    ng = RW_DIM // width
    lora_blk = 3 * RW_DIM // RW_LORA
    p0 = AB_RW0 // width
    toks = [
        _Tok(proj, width, lambda g: p0 + g, hist, lambda g: g),
        _Tok(proj, width, lambda g: p0 + ng + g, hist, lambda g: ng + g),
        _Tok(proj, width, lambda g: p0 + 2 * ng + g, hist, lambda g: 2 * ng + g),
        _Tok(proj, RW_LORA, lambda g: AB_RW0 // RW_LORA + lora_blk, hist, lambda g: lora_blk),
    ]
    ops, specs = _mixer_specs(grp, toks)
    head = _param_spec(width, lambda g: g)
    lora_w = jnp.zeros((ng, RW_LORA, 3 * width), F32)
    for i, (name, r0, r1) in enumerate((("w_up", 0, 64), ("a_up", 64, 128), ("g_up", 128, 256))):
        blk = prm[name].reshape(r1 - r0, ng, width).transpose(1, 0, 2)
        lora_w = lora_w.at[:, r0:r1, i * width:(i + 1) * width].set(blk)
    mu = prm["mu"].reshape(1, RW_PROJ)
    flat = lambda x: x.reshape(1, RW_DIM)
    params = [
        (mu, pl.BlockSpec((1, width), lambda b, g, c: (0, g))),
        (mu, pl.BlockSpec((1, width), lambda b, g, c: (0, ng + g))),
        (mu, pl.BlockSpec((1, width), lambda b, g, c: (0, 2 * ng + g))),
        (mu, pl.BlockSpec((1, RW_LORA), lambda b, g, c: (0, lora_blk))),
        (flat(prm["w0"]), head(1)), (flat(prm["a0"]), head(1)),
        (lora_w, pl.BlockSpec((1, RW_LORA, 3 * width), lambda b, g, c: (g, 0, 0))),
        (flat(prm["k_k"]), head(1)), (flat(prm["k_a"]), head(1)), (flat(prm["r_k"]), head(1)),
        (flat(prm["gn_w"]), head(1)), (flat(prm["gn_b"]), head(1)),
    ]
    ops += [p for p, _ in params]
    specs += [s for _, s in params]
    seqs = grp.nb * grp.bb
    st_spec = pl.BlockSpec((seqs, RW_SUB * 2 * RW_HP, RW_HD, RW_HD), lambda b, g, c: (b, g, 0, 0))
    ops.append(s0)
    specs.append(st_spec)
    y, s_out = pl.pallas_call(
        functools.partial(_rwkv_body, grp=grp),
        grid=(grp.outer // grp.nb, ng, grp.nc),
        in_specs=specs,
        out_specs=[pl.BlockSpec((grp.nb, MIX_ROWS, width), lambda b, g, c: (b, c, g)), st_spec],
        out_shape=[jax.ShapeDtypeStruct((grp.outer, grp.per_outer, RW_DIM), BF16),
                   jax.ShapeDtypeStruct(s0.shape, F32)],
        scratch_shapes=[pltpu.VMEM((seqs, RW_SUB, RW_HP * LANES, RW_HP * LANES), F32)],
        compiler_params=_cp("parallel", "parallel", "arbitrary", flags=MIXER_FLAGS),
        name="rwkv7",
    )(*ops)
    return y.reshape(grp.B * grp.L, RW_DIM), s_out


def _ssd_body(*refs, grp):
    r_rows = MIX_ROWS
    qs, bb, nb = grp.qs, grp.bb, grp.nb
    nh, hd, gdim = SSD_HPG, SSD_HD, SSD_GDIM
    cur, halos, rest = _read_windows(grp, refs, 5, 3)
    cw_x, cw_b, cw_c, cb_x, cb_b, cb_c, dtb_ref, alog_ref, dskip_ref, nw_ref, h0_ref, y_ref, h_ref = rest
    c = pl.program_id(2)

    @pl.when(c == 0)
    def _():
        h_ref[...] = h0_ref[...]

    m = _masks(grp)
    eye_h = (_iota2((SSD_HEADS, SSD_HEADS), 0) == _iota2((SSD_HEADS, SSD_HEADS), 1)).astype(F32)
    ei, ej = _iota2((3 * SSD_HEADS, 3 * SSD_DIM), 0), _iota2((3 * SSD_HEADS, 3 * SSD_DIM), 1)
    spread = ((ei // SSD_HEADS == ej // SSD_DIM) & (ei % SSD_HEADS == (ej % SSD_DIM) // hd)).astype(F32)

    def block(s):
        x_raw, b_raw, c_raw, z, dt_blk = (cur[i][s] for i in range(5))
        xs = _silu(_conv(grp, x_raw, halos[0][s], cw_x, cb_x[...], CONV_W))
        bm = _silu(_conv(grp, b_raw, halos[1][s], cw_b, cb_b[...], CONV_W))
        cm = _silu(_conv(grp, c_raw, halos[2][s], cw_c, cb_c[...], CONV_W))

        dt = jnp.where(m.valid, _softplus(dt_blk[:, :SSD_HEADS] + dtb_ref[...]), 0.0)
        cums = _sel_dot(m.cumsum, dt * (-jnp.exp(alog_ref[...])))
        acs, acs_end = cums[:r_rows], cums[r_rows:]
        acs_row = _sel_dot_nt(eye_h, acs)
        lanes = _dot_sel(jnp.concatenate([dt, acs_end - acs, acs], axis=1), spread)
        xd = xs * lanes[:, :SSD_DIM]
        xd_dec = xd * jnp.exp(lanes[:, SSD_DIM:2 * SSD_DIM])
        acs_lanes = lanes[:, 2 * SSD_DIM:]

        def group(g):
            gl = slice(g * gdim, (g + 1) * gdim)
            bmg, cmg = bm[:, g * SSD_STATE:(g + 1) * SSD_STATE], cm[:, g * SSD_STATE:(g + 1) * SSD_STATE]
            cb = _dot_nt(cmg, bmg)
            yield
            yd = []
            for r in range(nh):
                h = g * nh + r
                diff = acs[:, h:h + 1] - acs_row[h:h + 1, :]
                lmat = jnp.where(m.incl, jnp.exp(jnp.where(m.incl, diff, 0.0)), 0.0)
                yd.append(_dot(cb * lmat, xd[:, h * hd:(h + 1) * hd]))
                yield
            y = jnp.concatenate(yd, axis=1)
            head_rows = (_iota2((gdim, SSD_HEADS), 0) // hd + g * nh == _iota2((gdim, SSD_HEADS), 1)).astype(F32)
            yoff = []
            for j in range(bb):
                rows = slice(j * qs, (j + 1) * qs)
                hj = h_ref[s * bb + j, g]
                yoff.append(_dot_nt(cmg[rows], hj))
                yield
                end_col = jnp.exp(_sel_dot_nt(head_rows, acs_end[j * qs:j * qs + SUBLANES]))[:, :1]
                yield
                h_ref[s * bb + j, g] = hj * end_col + _dot_tn(xd_dec[rows, gl], bmg[rows])
                yield
            yoff = yoff[0] if bb == 1 else jnp.concatenate(yoff, axis=0)
            y = y + yoff * jnp.exp(acs_lanes[:, gl]) + xs[:, gl] * dskip_ref[:, gl]
            yg = y * _silu(z[:, gl])
            yg = yg * lax.rsqrt(jnp.mean(yg * yg, axis=-1, keepdims=True) + EPS)
            return yg * nw_ref[:, gl]

        return [group(g) for g in range(SSD_GROUPS)]

    results = _interleave([chain for s in range(nb) for chain in block(s)])
    for s in range(nb):
        y_ref[s] = jnp.concatenate(results[s * SSD_GROUPS:(s + 1) * SSD_GROUPS], axis=1).astype(y_ref.dtype)


def _ssd(grp, proj, hist, h0, prm):
    bc = SSD_GROUPS * SSD_STATE
    toks = [
        _Tok(proj, SSD_DIM, lambda g: AB_X0 // SSD_DIM, hist, lambda g: 0),
        _Tok(proj, bc, lambda g: AB_B0 // bc, hist, lambda g: SSD_DIM // bc),
        _Tok(proj, bc, lambda g: AB_C0 // bc, hist, lambda g: SSD_DIM // bc + 1),
        _Tok(proj, SSD_DIM, lambda g: AB_Z0 // SSD_DIM),
        _Tok(proj, LANES, lambda g: AB_DT0 // LANES),
    ]
    ops, specs = _mixer_specs(grp, toks)
    cw, cbias = prm["conv_w"], prm["conv_b"].reshape(1, SSD_CONV_DIM)
    conv_cols = [(SSD_DIM, lambda g: 0), (bc, lambda g: SSD_DIM // bc), (bc, lambda g: SSD_DIM // bc + 1)]
    params = [(cw, _param_spec(wd, col)(CONV_W)) for wd, col in conv_cols]
    params += [(cbias, _param_spec(wd, col)(1)) for wd, col in conv_cols]
    full = lambda n: pl.BlockSpec((1, n), lambda b, g, c: (0, 0))
    params += [
        (prm["dt_bias"].reshape(1, SSD_HEADS), full(SSD_HEADS)),
        (prm["A_log"].reshape(1, SSD_HEADS), full(SSD_HEADS)),
        (jnp.repeat(prm["D"], SSD_HD).reshape(1, SSD_DIM), full(SSD_DIM)),
        (prm["norm_w"].reshape(1, SSD_DIM), full(SSD_DIM)),
    ]
    ops += [p for p, _ in params]
    specs += [s for _, s in params]
    st_spec = pl.BlockSpec((grp.nb * grp.bb, SSD_GROUPS, SSD_GDIM, SSD_STATE), lambda b, g, c: (b, 0, 0, 0))
    ops.append(h0)
    specs.append(st_spec)
    y, h_out = pl.pallas_call(
        functools.partial(_ssd_body, grp=grp),
        grid=(grp.outer // grp.nb, 1, grp.nc),
        in_specs=specs,
        out_specs=[pl.BlockSpec((grp.nb, MIX_ROWS, SSD_DIM), lambda b, g, c: (b, c, 0)), st_spec],
        out_shape=[jax.ShapeDtypeStruct((grp.outer, grp.per_outer, SSD_DIM), BF16),
                   jax.ShapeDtypeStruct(h0.shape, F32)],
        compiler_params=_cp("parallel", "arbitrary", "arbitrary"),
        name="ssd",
    )(*ops)
    return y.reshape(grp.B * grp.L, SSD_DIM), h_out


def _gdn_body(*refs, grp):
    r_rows = MIX_ROWS
    qs, bb, nb = grp.qs, grp.bb, grp.nb
    nh = GDN_G
    n = nh * r_rows
    cur, halos, rest = _read_windows(grp, refs, 5, 3)
    cw_q, cw_k, cw_v, alog_ref, dtb_ref, nw_ref, s0_ref, o_ref, s_ref = rest
    c = pl.program_id(2)

    @pl.when(c == 0)
    def _():
        s_ref[...] = s0_ref[...]

    m = _masks(grp, nh)
    per_step = nh * GDN_SUB
    eye_g = (_iota2((SUBLANES, nh), 0) == _iota2((SUBLANES, nh), 1)).astype(F32)
    col = lambda x: jnp.concatenate([x[:, h:h + 1] for h in range(nh)], axis=0)

    def block(s):
        q_raw, k_raw, v_raw, z, ba = (cur[i][s] for i in range(5))
        q_all = _silu(_conv(grp, q_raw, halos[0][s], cw_q, None, CONV_W))
        k_all = _silu(_conv(grp, k_raw, halos[1][s], cw_k, None, CONV_W))
        v_all = _silu(_conv(grp, v_raw, halos[2][s], cw_v, None, CONV_W))

        def stack(sub):
            lanes = slice(sub * nh * GDN_D, (sub + 1) * nh * GDN_D)
            heads = slice(sub * nh, (sub + 1) * nh)
            q = _stack_heads(q_all[:, lanes], nh, GDN_D)
            k = _stack_heads(k_all[:, lanes], nh, GDN_D)
            v = _stack_heads(v_all[:, lanes], nh, GDN_D)
            q = q * lax.rsqrt(jnp.sum(q * q, axis=-1, keepdims=True) + 1e-6) * (GDN_D ** -0.5)
            k = k * lax.rsqrt(jnp.sum(k * k, axis=-1, keepdims=True) + 1e-6)

            beta = jnp.where(m.valid, _sigmoid(ba[:, heads]), 0.0)
            a_raw = ba[:, per_step + sub * nh:per_step + (sub + 1) * nh]
            gg = jnp.where(m.valid, -jnp.exp(alog_ref[0][:, heads]) * _softplus(a_raw + dtb_ref[0][:, heads]), 0.0)
            gcs = _sel_dot(m.cumsum, gg)
            yield
            gc, gc_end = gcs[:r_rows], gcs[r_rows:]
            gc_row = _sel_dot_nt(eye_g, gc)
            yield

            bcol, gcol, gend = col(beta), col(gc), col(gc_end)
            grow = jnp.concatenate([gc_row[h:h + 1, :] for h in range(nh)], axis=1)
            dec = jnp.where(m.incl, jnp.exp(jnp.where(m.incl, gcol - grow, 0.0)), 0.0)
            kb = k * bcol
            kq = _dot_nt(jnp.concatenate([kb, q], axis=0), k)
            yield
            lm = -jnp.where(m.strict, kq[:n] * dec, 0.0)
            attn = jnp.where(m.incl, kq[n:] * dec, 0.0)
            eg = jnp.exp(gcol)
            sol = yield from _solve_unit_lower(lm, jnp.concatenate([v * bcol, kb * eg], axis=1), m.eye, qs)
            vw, kcd = sol[:, :GDN_D], sol[:, GDN_D:]
            qg = q * eg
            kg = k * jnp.exp(gend - gcol)

            def state(j):
                return jnp.concatenate([s_ref[s * bb + j, sub * nh + h] for h in range(nh)], axis=0)

            if bb == 1:
                parts = []
                for h in range(nh):
                    rows = slice(h * r_rows, (h + 1) * r_rows)
                    parts.append(_dot(jnp.concatenate([kcd[rows], qg[rows]], axis=0), s_ref[s, sub * nh + h]))
                    yield
                ks = jnp.concatenate([p[:r_rows] for p in parts], axis=0)
                qsv = jnp.concatenate([p[r_rows:] for p in parts], axis=0)
            else:
                parts = []
                for j in range(bb):
                    lhs = jnp.concatenate([_seq_rows(grp, kcd, j, nh), _seq_rows(grp, qg, j, nh)], axis=0)
                    parts.append(_dot(_expand_rows(lhs, nh, qs, GDN_D), state(j)))
                    yield
                half = nh * qs
                ks = _from_seq_rows(grp, [p[:half] for p in parts], nh)
                qsv = _from_seq_rows(grp, [p[half:] for p in parts], nh)
            v_new = vw - ks
            o = qsv + _dot(attn, v_new)
            yield
            for j in range(bb):
                kgj = _expand_rows(_seq_rows(grp, kg, j, nh), nh, qs, GDN_D)
                g_last = jnp.exp(gc_end[j * qs:j * qs + 1, :])
                g_last = jnp.concatenate([jnp.broadcast_to(g_last[:, h:h + 1], (GDN_D, 1)) for h in range(nh)], axis=0)
                s_new = state(j) * g_last + _dot_tn(kgj, _seq_rows(grp, v_new, j, nh))
                yield
                for h in range(nh):
                    s_ref[s * bb + j, sub * nh + h] = s_new[h * GDN_D:(h + 1) * GDN_D]
            o = o * lax.rsqrt(jnp.mean(o * o, axis=-1, keepdims=True) + EPS) * nw_ref[...]
            return _unstack_heads(o, nh)

        def finish(os_):
            o_ref[s] = (jnp.concatenate(os_, axis=1) * _silu(z)).astype(o_ref.dtype)

        return [stack(sub) for sub in range(GDN_SUB)], finish

    blocks = [block(s) for s in range(nb)]
    results = _interleave([chain for chains, _ in blocks for chain in chains])
    for s, (_, finish) in enumerate(blocks):
        finish(results[s * GDN_SUB:(s + 1) * GDN_SUB])


def _zeros_like_axis(x, n, axis):
    shape = list(x.shape)
    shape[axis] = n
    return jnp.zeros(shape, x.dtype)


def _ab_layout(x, axis=-1):
    cut = lambda a, b: lax.slice_in_dim(x, a, b, axis=axis)
    zeros = lambda n: _zeros_like_axis(x, n, axis)
    rw, z = cut(0, RW_PROJ), cut(RW_PROJ, RW_PROJ + SSD_DIM)
    xbc = cut(RW_PROJ + SSD_DIM, RW_PROJ + SSD_DIM + SSD_CONV_DIM)
    dt = cut(RW_PROJ + SSD_DIM + SSD_CONV_DIM, AB_PROJ)
    return jnp.concatenate([z, xbc, dt, zeros(AB_RW0 - AB_DT0 - SSD_HEADS), rw, zeros(AB_COLS - AB_RW0 - RW_PROJ)],
                           axis=axis)


def _gdn_layout(x, axis=-1):
    per_step = GDN_G * GDN_SUB
    main = GDN_CONV_DIM + GDN_V
    cut = lambda a, b: lax.slice_in_dim(x, a, b, axis=axis)
    parts = [cut(0, main)]
    for g in range(GDN_HEADS // per_step):
        h0, h1 = g * per_step, (g + 1) * per_step
        parts += [cut(main + h0, main + h1), cut(main + GDN_HEADS + h0, main + GDN_HEADS + h1),
                  _zeros_like_axis(x, LANES - 2 * per_step, axis)]
    total = main + (GDN_HEADS // per_step) * LANES
    parts.append(_zeros_like_axis(x, -total % MM_TN, axis))
    return jnp.concatenate(parts, axis=axis)


def _gdn(grp, proj, hist, s0, prm):
    per_step = GDN_G * GDN_SUB
    width = per_step * GDN_D
    ng = GDN_HEADS // per_step
    ba_blk = (GDN_CONV_DIM + GDN_V) // LANES
    toks = [
        _Tok(proj, width, lambda g: g, hist, lambda g: g),
        _Tok(proj, width, lambda g: ng + g, hist, lambda g: ng + g),
        _Tok(proj, width, lambda g: 2 * ng + g, hist, lambda g: 2 * ng + g),
        _Tok(proj, width, lambda g: 3 * ng + g),
        _Tok(proj, LANES, lambda g: ba_blk + g),
    ]
    ops, specs = _mixer_specs(grp, toks)
    cw = prm["conv_w"]
    grouped = lambda x: x.reshape(ng, 1, per_step)
    params = [
        (cw, _param_spec(width, lambda g: g)(CONV_W)),
        (cw, _param_spec(width, lambda g: ng + g)(CONV_W)),
        (cw, _param_spec(width, lambda g: 2 * ng + g)(CONV_W)),
        (grouped(prm["A_log"]), pl.BlockSpec((1, 1, per_step), lambda b, g, c: (g, 0, 0))),
        (grouped(prm["dt_bias"]), pl.BlockSpec((1, 1, per_step), lambda b, g, c: (g, 0, 0))),
        (prm["norm_w"].reshape(1, GDN_D), pl.BlockSpec((1, GDN_D), lambda b, g, c: (0, 0))),
    ]
    ops += [p for p, _ in params]
    specs += [s for _, s in params]
    st_spec = pl.BlockSpec((grp.nb * grp.bb, per_step, GDN_D, GDN_D), lambda b, g, c: (b, g, 0, 0))
    ops.append(s0)
    specs.append(st_spec)
    o, s_out = pl.pallas_call(
        functools.partial(_gdn_body, grp=grp),
        grid=(grp.outer // grp.nb, ng, grp.nc),
        in_specs=specs,
        out_specs=[pl.BlockSpec((grp.nb, MIX_ROWS, width), lambda b, g, c: (b, c, g)), st_spec],
        out_shape=[jax.ShapeDtypeStruct((grp.outer, grp.per_outer, GDN_V), BF16),
                   jax.ShapeDtypeStruct(s0.shape, F32)],
        compiler_params=_cp("parallel", "parallel", "arbitrary", flags=MIXER_FLAGS),
        name="gdn",
    )(*ops)
    return o.reshape(grp.B * grp.L, GDN_V), s_out


def _attend(q, k_ref, v_ref, nseq, lq):
    scale = XA_HD ** -0.5
    outs = []
    for j in range(nseq):
        qrows = slice(j * lq, (j + 1) * lq)
        mrows = slice(j * N_MEM, (j + 1) * N_MEM)
        heads = []
        for h in range(XA_HEADS):
            sl = slice(h * XA_HD, (h + 1) * XA_HD)
            s = _dot_nt(q[qrows, sl], k_ref[mrows, sl]) * scale
            p = jnp.exp(s - jnp.max(s, axis=-1, keepdims=True))
            heads.append(_dot(p, v_ref[mrows, sl]) / jnp.sum(p, axis=-1, keepdims=True))
        outs.append(jnp.concatenate(heads, axis=1))
    return outs[0] if nseq == 1 else jnp.concatenate(outs, axis=0)


def _attend_packed(q, k_ref, v_ref, nseq, lq):
    scale = XA_HD ** -0.5
    rows, cols = XA_HEADS * lq, N_MEM * XA_HEADS
    own_head = (_iota2((rows, cols), 0) // lq) == (_iota2((rows, cols), 1) % XA_HEADS)
    outs = []
    for j in range(nseq):
        qh = _stack_heads(q[j * lq:(j + 1) * lq], XA_HEADS, XA_HD)
        mrows = slice(j * cols, (j + 1) * cols)
        s = jnp.where(own_head, _dot_nt(qh, k_ref[mrows, :]) * scale, -1e30)
        p = jnp.where(own_head, jnp.exp(s - jnp.max(s, axis=-1, keepdims=True)), 0.0)
        o = _dot(p, v_ref[mrows, :]) / jnp.sum(p, axis=-1, keepdims=True)
        outs.append(_unstack_heads(o, XA_HEADS))
    return jnp.concatenate(outs, axis=0)


def _xattn_body(*refs, attend, nseq, lq, n_mix):
    n_lead = n_mix + 1 if n_mix else 0
    h_ref, g_ref, wq_ref, k_ref, v_ref, wo_ref, o_ref = refs[n_lead:]
    x, k0 = h_ref[...], 0
    for y_ref in refs[:n_mix]:
        k1 = k0 + y_ref.shape[1]
        x = x + jnp.dot(y_ref[...], refs[n_mix][k0:k1, :], preferred_element_type=F32)
        k0 = k1
    xn = x * lax.rsqrt(jnp.mean(x * x, axis=-1, keepdims=True) + EPS) * g_ref[...]
    q = jnp.dot(xn.astype(BF16), wq_ref[...], preferred_element_type=F32)
    o = attend(q, k_ref, v_ref, nseq, lq)
    o_ref[...] = x + jnp.dot(o.astype(BF16), wo_ref[...], preferred_element_type=F32)


def _xattn(grp, h, mix, w_mix, gain, w_q, w_o, layer, mem_k, mem_v, mem_row0):
    rows, d = h.shape
    if grp.L >= XA_TQ:
        nseq, lq, attend = 1, XA_TQ, _attend
        per = grp.L // XA_TQ
        mem_spec = pl.BlockSpec((N_MEM, XA_DIM), lambda i: (mem_row0 // N_MEM + i // per, 0))
    else:
        nseq, lq, attend = MIX_ROWS // grp.L, grp.L, _attend_packed
        blk = nseq * N_MEM * XA_HEADS
        mem_spec = pl.BlockSpec((blk, XA_HD), lambda i: (mem_row0 * XA_HEADS // blk + i, 0))
    tile = nseq * lq
    h_spec = pl.BlockSpec((tile, d), lambda i: (i, 0))
    lead_ops, lead_specs = [], []
    if mix:
        lead_ops = list(mix) + [w_mix]
        lead_specs = [pl.BlockSpec((tile, y.shape[1]), lambda i: (i, 0)) for y in mix]
        lead_specs.append(pl.BlockSpec(w_mix.shape, lambda i: (0, 0)))
    return pl.pallas_call(
        functools.partial(_xattn_body, attend=attend, nseq=nseq, lq=lq, n_mix=len(mix)),
        grid=(rows // tile,),
        in_specs=lead_specs + [h_spec, pl.BlockSpec((1, d), lambda i: (0, 0)),
                               pl.BlockSpec((None, d, XA_DIM), lambda i: (layer, 0, 0)), mem_spec, mem_spec,
                               pl.BlockSpec((None, XA_DIM, d), lambda i: (layer, 0, 0))],
        out_specs=h_spec,
        out_shape=jax.ShapeDtypeStruct((rows, d), F32),
        compiler_params=_cp("parallel"),
        name="xattn",
    )(*lead_ops, h, gain.reshape(1, d), w_q, mem_k, mem_v, w_o)


def _ffn_in_body(*refs, grp, tiles_per_seq):
    it = iter(refs)
    a_ref, g_ref, wg_ref, wu_ref, hist_ref, cw_ref, cb_ref, act_ref, tail_ref, an_ref = (next(it) for _ in range(10))
    carry_ref = None if grp.embedded else next(it)
    i, j = pl.program_id(0), pl.program_id(1)

    @pl.when(j == 0)
    def _():
        x = a_ref[...]
        y = x * lax.rsqrt(jnp.mean(x * x, axis=-1, keepdims=True) + EPS)
        an_ref[...] = (y * g_ref[...]).astype(BF16)

    a = an_ref[...]
    gate = jnp.dot(a, wg_ref[...].astype(BF16), preferred_element_type=F32)
    up = jnp.dot(a, wu_ref[...].astype(BF16), preferred_element_type=F32)
    if grp.embedded:
        halo = hist_ref[...]
        tail_ref[...] = gate
    else:
        halo = jnp.where(i % tiles_per_seq == 0, hist_ref[...], carry_ref[j])
        last = gate[gate.shape[0] - SUBLANES:]
        carry_ref[j] = last
        tail_ref[...] = last
    conv = _conv(grp, gate, halo, cw_ref, cb_ref[...], FFN_CONV_W)
    act_ref[...] = (_silu(conv) * up).astype(act_ref.dtype)


def _ffn_in(grp, h, gain, w_in, layer, hist, conv_w, conv_b, tm=MM_TM, tn=MM_TN):
    assert grp.embedded or grp.L % tm == 0
    rows, k = h.shape
    nct = D_FF // tn
    tiles_per_seq = max(grp.L // tm, 1)
    hrows = tm if grp.embedded else SUBLANES
    tile_idx = lambda i, j: (i, j)
    hist_idx = tile_idx if grp.embedded else (lambda i, j: (i // tiles_per_seq, j))
    specs = [
        pl.BlockSpec((tm, k), lambda i, j: (i, 0)),
        pl.BlockSpec((1, k), lambda i, j: (0, 0)),
        _weight_spec(w_in, layer, k, tn, lambda j: j),
        _weight_spec(w_in, layer, k, tn, lambda j: nct + j),
        pl.BlockSpec((hrows, tn), hist_idx),
        pl.BlockSpec((FFN_CONV_W, tn), lambda i, j: (0, j)),
        pl.BlockSpec((1, tn), lambda i, j: (0, j)),
    ]
    scratch = [pltpu.VMEM((tm, k), BF16)]
    if not grp.embedded:
        scratch.append(pltpu.VMEM((nct, SUBLANES, tn), F32))
    act, tails = pl.pallas_call(
        functools.partial(_ffn_in_body, grp=grp, tiles_per_seq=tiles_per_seq),
        grid=(rows // tm, nct),
        in_specs=specs,
        out_specs=[pl.BlockSpec((tm, tn), tile_idx), pl.BlockSpec((hrows, tn), tile_idx)],
        out_shape=[jax.ShapeDtypeStruct((rows, D_FF), BF16), jax.ShapeDtypeStruct((rows // tm * hrows, D_FF), F32)],
        scratch_shapes=scratch,
        compiler_params=_cp("arbitrary", "arbitrary"),
        name="ffn_in",
    )(h, gain.reshape(1, k), w_in, w_in, hist, conv_w, conv_b.reshape(1, D_FF))
    if not grp.embedded:
        tails = tails.reshape(grp.B, tiles_per_seq, SUBLANES, D_FF)[:, -1].reshape(grp.B * SUBLANES, D_FF)
    return act, tails


def _history(grp, buf, width):
    b, k, c = buf.shape
    if grp.embedded:
        h = jnp.pad(buf, ((0, 0), (grp.first - k, grp.L - grp.first), (0, 0)))
        return h.reshape(b * grp.L, c)
    return jnp.pad(buf, ((0, 0), (SUBLANES - k, 0), (0, 0))).reshape(b * SUBLANES, c)


def _tail(grp, x2d, k, cols):
    x = x2d.reshape(grp.B, grp.L, x2d.shape[1])
    return x[:, grp.L - k:, cols].astype(F32)


def _run_group(grp, x2d, mem_k, mem_v, mem_row0, st, wts, prm):
    h = x2d
    out = {}
    tm_proj = min(MM_TM_PROJ, h.shape[0])
    proj = _matmul(h, wts["in_ab"], gain=prm["norm_mix"][0], out_dtype=BF16, tm=tm_proj, w_rows_are_outputs=True)
    y_rw, s_rw = _rwkv(grp, proj, _history(grp, st["rw_shift"][:, None, :], RW_PROJ), st["rwkv"], prm["rw"])
    y_ssd, s_ssd = _ssd(grp, proj, _history(grp, st["ssd_conv"], SSD_CONV_DIM),
                        st["ssd"].reshape(grp.B, SSD_GROUPS, SSD_GDIM, SSD_STATE), prm["ssd"])
    out["rwkv"] = s_rw
    out["rw_shift"] = _tail(grp, proj, 1, slice(AB_RW0, AB_RW0 + RW_PROJ))[:, 0]
    out["ssd"] = s_ssd.reshape(st["ssd"].shape)
    out["ssd_conv"] = _tail(grp, proj, CONV_W - 1, slice(AB_X0, AB_X0 + SSD_CONV_DIM))
    mix, w_mix = [y_rw, y_ssd], wts["out_ab"]
    ffn_bufs = []
    for l in range(2):
        if l == 1:
            proj = _matmul(h, wts["in_c"], gain=prm["norm_mix"][1], out_dtype=BF16, tm=tm_proj, w_rows_are_outputs=True)
            y_c, s_gdn = _gdn(grp, proj, _history(grp, st["gdn_conv"], GDN_CONV_DIM), st["gdn"], prm["gdn"])
            out["gdn"] = s_gdn
            out["gdn_conv"] = _tail(grp, proj, CONV_W - 1, slice(0, GDN_CONV_DIM))
            mix, w_mix = [y_c], wts["out_c"]
        if grp.embedded:
            h, mix = _matmul(mix, w_mix, res=h), []
        h = _xattn(grp, h, mix, w_mix, prm["norm_xa"][l], wts["xq"], wts["xo"], l, mem_k, mem_v, mem_row0[l])
        act, gate_tail = _ffn_in(grp, h, prm["norm_ffn"][l], wts["ffn_in"], l, _history(grp, st["ffn_conv"][l], D_FF),
                                 prm["ffn_conv_w"][l], prm["ffn_conv_b"][l])
        ffn_bufs.append(gate_tail.reshape(grp.B, SUBLANES, D_FF)[:, SUBLANES - (FFN_CONV_W - 1):])
        h = _matmul(act, wts["ffn_out"], layer=l, res=h)
    out["ffn_conv"] = jnp.stack(ffn_bufs)
    out["y"] = _rmsnorm(h, prm["norm_final"])
    return out


def kernel(x_prompt, x_sample, mem_prompt, state_rwkv, state_rwkv_shift, state_ssd, state_ssd_conv, state_gdn,
           state_gdn_conv, state_ffn_conv, cache_mem_k, cache_mem_v, norm_mix, norm_xa, norm_mem, norm_ffn,
           norm_final, w_in_ab, rw_mu, rw_w0, rw_w_up, rw_a0, rw_a_up, rw_g_up, rw_k_k, rw_k_a, rw_r_k, rw_gn_w,
           rw_gn_b, ssd_conv_w, ssd_conv_b, ssd_dt_bias, ssd_A_log, ssd_D, ssd_norm_w, w_out_ab, w_in_c,
           gdn_conv_w, gdn_A_log, gdn_dt_bias, gdn_norm_w, w_out_c, w_xq, w_xk, w_xv, w_xo, ffn_w_in, ffn_conv_w,
           ffn_conv_b, ffn_w_out):
    bp, lp, _ = x_prompt.shape
    bs, ls, _ = x_sample.shape
    depth = norm_mix.shape[0]
    assert depth == 2 and w_in_ab.shape[0] == 1 and w_in_c.shape[0] == 1
    assert lp % MIX_ROWS == 0 and lp % XA_TQ == 0 and lp % MM_TM == 0
    pad_rows = SUBLANES - ls
    assert CONV_W - 1 <= pad_rows and CONV_W - 1 <= ls and MIX_ROWS % SUBLANES == 0
    gp = Group(B=bp, L=lp, first=0, bb=1, nb=MIX_NB if bp % MIX_NB == 0 else 1)
    gs = Group(B=bs, L=SUBLANES, first=pad_rows, bb=MIX_ROWS // SUBLANES)
    assert bs % gs.bb == 0 and (bs * SUBLANES) % MM_TM == 0 and (bp * lp) % MM_TM == 0

    wts = dict(
        in_ab=_ab_layout(jnp.swapaxes(w_in_ab[0], 0, 1), axis=0).astype(BF16), out_ab=w_out_ab[0].astype(BF16),
        in_c=_gdn_layout(jnp.swapaxes(w_in_c[0], 0, 1), axis=0).astype(BF16), out_c=w_out_c[0].astype(BF16),
        xq=w_xq.astype(BF16), xo=w_xo.astype(BF16), ffn_in=ffn_w_in, ffn_out=ffn_w_out.astype(BF16),
    )
    prm = dict(
        norm_mix=norm_mix, norm_xa=norm_xa, norm_ffn=norm_ffn, norm_final=norm_final,
        ffn_conv_w=ffn_conv_w, ffn_conv_b=ffn_conv_b,
        rw=dict(mu=rw_mu[0], w0=rw_w0[0], w_up=rw_w_up[0], a0=rw_a0[0], a_up=rw_a_up[0], g_up=rw_g_up[0],
                k_k=rw_k_k[0], k_a=rw_k_a[0], r_k=rw_r_k[0], gn_w=rw_gn_w[0], gn_b=rw_gn_b[0]),
        ssd=dict(conv_w=ssd_conv_w[0], conv_b=ssd_conv_b[0], dt_bias=ssd_dt_bias[0], A_log=ssd_A_log[0],
                 D=ssd_D[0], norm_w=ssd_norm_w[0]),
        gdn=dict(conv_w=gdn_conv_w[0], A_log=gdn_A_log[0], dt_bias=gdn_dt_bias[0], norm_w=gdn_norm_w[0]),
    )

    mem2d = mem_prompt.reshape(bp * N_MEM, D_MODEL)
    mem_tm = min(MM_TM, bp * N_MEM)
    mk, mv = [], []
    for l in range(depth):
        mk.append(_matmul(mem2d, w_xk[l].astype(BF16), gain=norm_mem[l], tm=mem_tm))
        mv.append(_matmul(mem2d, w_xv[l].astype(BF16), gain=norm_mem[l], tm=mem_tm))
    mem_k_p = jnp.stack(mk)
    mem_v_p = jnp.stack(mv)

    zeros = lambda *s: jnp.zeros(s, F32)
    st_p = dict(
        rwkv=zeros(bp, RW_HEADS, RW_HD, RW_HD), rw_shift=zeros(bp, RW_PROJ),
        ssd=zeros(bp, SSD_GROUPS, SSD_HPG, SSD_HD, SSD_STATE), ssd_conv=zeros(bp, CONV_W - 1, SSD_CONV_DIM),
        gdn=zeros(bp, GDN_HEADS, GDN_D, GDN_D), gdn_conv=zeros(bp, CONV_W - 1, GDN_CONV_DIM),
        ffn_conv=zeros(depth, bp, FFN_CONV_W - 1, D_FF),
    )
    rp = _run_group(gp, x_prompt.reshape(bp * lp, D_MODEL), mem_k_p.reshape(depth * bp * N_MEM, XA_DIM),
                    mem_v_p.reshape(depth * bp * N_MEM, XA_DIM), [l * bp * N_MEM for l in range(depth)],
                    st_p, wts, prm)

    st_s = dict(rwkv=state_rwkv[0], rw_shift=state_rwkv_shift[0], ssd=state_ssd[0], ssd_conv=state_ssd_conv[0],
                gdn=state_gdn[0], gdn_conv=state_gdn_conv[0], ffn_conv=state_ffn_conv)
    xs = jnp.pad(x_sample, ((0, 0), (pad_rows, 0), (0, 0))).reshape(bs * SUBLANES, D_MODEL)
    rs = _run_group(gs, xs, cache_mem_k.reshape(depth * bs * N_MEM * XA_HEADS, XA_HD),
                    cache_mem_v.reshape(depth * bs * N_MEM * XA_HEADS, XA_HD), [l * bs * N_MEM for l in range(depth)],
                    st_s, wts, prm)

    y_p = rp["y"].reshape(bp, lp, D_MODEL)
    y_s = rs["y"].reshape(bs, SUBLANES, D_MODEL)[:, pad_rows:]
    lead = lambda x: x[None]
    mem_shape = (depth, bp, N_MEM, XA_HEADS, XA_HD)
    return (y_p, y_s, lead(rp["rwkv"]), lead(rs["rwkv"]), lead(rp["rw_shift"]), lead(rs["rw_shift"]),
            lead(rp["ssd"]), lead(rs["ssd"]), lead(rp["ssd_conv"]), lead(rs["ssd_conv"]),
            lead(rp["gdn"]), lead(rs["gdn"]), lead(rp["gdn_conv"]), lead(rs["gdn_conv"]),
            rp["ffn_conv"], rs["ffn_conv"], mem_k_p.reshape(mem_shape), mem_v_p.reshape(mem_shape))
```
